```python
import math
import jax
import jax.numpy as jnp
from jax import lax
import numpy as np

D_MODEL = 1024
BATCH = 4
SEQ = 4096
DEPTH = 1
DEC_BATCH = 32
DEC_SEQ = 8
PAST_LEN = 8192
PAGE_SIZE = 128

N_ATT_HEADS = 16
ATT_HEAD_DIM = 64
N_KV_HEADS = 4
GQA = N_ATT_HEADS // N_KV_HEADS
ATT_WIDTH = N_ATT_HEADS * ATT_HEAD_DIM
KV_WIDTH = 2 * N_KV_HEADS * ATT_HEAD_DIM
BLK = 64
N_SEL = 16
WINDOW = 512
N_BRANCH = 3
Q_BLOCK = 128
N_BUCKETS = 32
MAX_DISTANCE = 128
SSD_HEADS = 16
SSD_HEAD_DIM = 64
SSD_INNER = SSD_HEADS * SSD_HEAD_DIM
SSD_GROUPS = 2
SSD_STATE = 128
CONV_W = 4
CONV_DIM = SSD_INNER + 2 * SSD_GROUPS * SSD_STATE
CHUNK = 128
D_MIX = ATT_WIDTH + SSD_INNER
D_FF = 2816
IN_SPLITS = (ATT_WIDTH, KV_WIDTH, KV_WIDTH, KV_WIDTH, N_BRANCH * N_ATT_HEADS, SSD_INNER, CONV_DIM, SSD_HEADS)
D_IN = ATT_WIDTH + 3 * KV_WIDTH + N_BRANCH * N_ATT_HEADS + SSD_INNER + CONV_DIM + SSD_HEADS
EPS = 1e-6

kernel_name = 'hymba_nsa_ssd_macaron_step'


def rmsnorm(x, g):
    xf = x.astype(jnp.float32)
    y = xf * lax.rsqrt(jnp.mean(xf * xf, axis=-1, keepdims=True) + EPS)
    return (y * g.astype(jnp.float32)).astype(x.dtype)


def swiglu(x, w_in, w_out):
    gate, up = jnp.split(x @ w_in, 2, axis=-1)
    return (jax.nn.silu(gate) * up) @ w_out


def ffn_sublayer(x, pre_g, post_g, w_in, w_out):
    return x + 0.5 * rmsnorm(swiglu(rmsnorm(x, pre_g), w_in, w_out), post_g)


def masked_softmax(s, mask):
    s = jnp.where(mask, s, -jnp.inf)
    m = jnp.max(s, axis=-1, keepdims=True)
    m = jnp.where(jnp.isfinite(m), m, 0.0)
    p = jnp.exp(s - m)
    return p / jnp.maximum(jnp.sum(p, axis=-1, keepdims=True), 1e-30)


def rel_bucket(dist):
    n = jnp.maximum(dist, 0)
    max_exact = N_BUCKETS // 2
    nf = jnp.maximum(n, 1).astype(jnp.float32)
    log_b = max_exact + (jnp.log(nf / max_exact) / math.log(MAX_DISTANCE / max_exact)
                         * (N_BUCKETS - max_exact)).astype(jnp.int32)
    return jnp.where(n < max_exact, n, jnp.minimum(log_b, N_BUCKETS - 1))


def split_in_proj(u):
    offs, acc = [], 0
    for s in IN_SPLITS[:-1]:
        acc += s
        offs.append(acc)
    return jnp.split(u, offs, axis=-1)


def mixer_inputs(x, pre_g, w_in):
    b, t = x.shape[0], x.shape[1]
    q, ckv, skv, wkv, gates, z, xbc, dt = split_in_proj(rmsnorm(x, pre_g) @ w_in)
    kv_shape = (b, t, N_KV_HEADS, 2, ATT_HEAD_DIM)
    return (q.reshape(b, t, N_KV_HEADS, GQA, ATT_HEAD_DIM), ckv.reshape(kv_shape), skv.reshape(kv_shape),
            wkv.reshape(kv_shape), gates, z, xbc, dt)


def mixer_output(x, o_att, o_ssd, att_out_norm, post_g, w_out):
    o = jnp.concatenate([rmsnorm(o_att, att_out_norm), o_ssd.astype(o_att.dtype)], axis=-1) @ w_out
    return x + rmsnorm(o, post_g)


def compress_blocks(kv_blocks, cmp_pe, w_cmp):
    out = jnp.einsum('bnlhcd,lcde->bnhce', kv_blocks, w_cmp, preferred_element_type=jnp.float32)
    return out + jnp.einsum('lcd,lcde->ce', cmp_pe, w_cmp, preferred_element_type=jnp.float32)


def nsa_query_block(q, pos, gates, kc, vc, fetch_sel, wkv, wpos, rel_table):
    b, t = q.shape[0], q.shape[1]
    nb = kc.shape[1]
    f32 = jnp.float32
    scale = ATT_HEAD_DIM ** -0.5
    table = rel_table.astype(f32)

    def head_bias(dist):
        bias = table[rel_bucket(dist)]
        return bias.reshape(dist.shape + (N_KV_HEADS, GQA)).transpose(2, 3, 0, 1)

    blk = jnp.arange(nb, dtype=jnp.int32)
    c_dist = pos[:, None] - ((blk + 1) * BLK - 1)[None, :]
    s_c = jnp.einsum('btkgd,bnkd->bkgtn', q, kc, preferred_element_type=f32) * scale + head_bias(c_dist)
    p_c = masked_softmax(s_c, c_dist >= 0)
    o_c = jnp.einsum('bkgtn,bnkd->btkgd', p_c, vc.astype(f32))

    cur = (pos // BLK)[:, None]
    forced = (blk[None] == 0) | (blk[None] == cur) | (blk[None] == cur - 1)
    imp = jnp.sum(p_c, axis=2)
    score = jnp.where(forced, jnp.inf, jnp.where(blk[None] <= cur, imp, -jnp.inf))
    top_s, idx = lax.top_k(score, min(N_SEL, nb))
    idx = idx.transpose(0, 2, 1, 3)
    sel_ok = (top_s > -jnp.inf).transpose(0, 2, 1, 3)
    k_sel, v_sel = fetch_sel(idx)
    n = idx.shape[-1]
    kpos = idx[..., None] * BLK + jnp.arange(BLK, dtype=jnp.int32)
    s_dist = pos[None, :, None, None, None] - kpos
    s_mask = sel_ok[..., None] & (s_dist >= 0)
    kv_ix = jnp.arange(N_KV_HEADS)[None, None, :, None, None]
    s_bias = table.reshape(N_BUCKETS, N_KV_HEADS, GQA)[rel_bucket(s_dist), kv_ix]
    s_s = jnp.einsum('btkgd,btknld->btkgnl', q, k_sel, preferred_element_type=f32) * scale \
        + jnp.moveaxis(s_bias, -1, 3)
    p_s = masked_softmax(s_s.reshape(b, t, N_KV_HEADS, GQA, n * BLK),
                         s_mask.reshape(b, t, N_KV_HEADS, 1, n * BLK))
    o_s = jnp.einsum('btkgm,btkmd->btkgd', p_s,
                     v_sel.reshape(b, t, N_KV_HEADS, n * BLK, ATT_HEAD_DIM).astype(f32))

    w_dist = pos[:, None] - wpos[None, :]
    w_mask = (w_dist >= 0) & (w_dist < WINDOW) & (wpos[None, :] >= 0)
    s_w = jnp.einsum('btkgd,bskd->bkgts', q, wkv[..., 0, :], preferred_element_type=f32) * scale + head_bias(w_dist)
    p_w = masked_softmax(s_w, w_mask)
    o_w = jnp.einsum('bkgts,bskd->btkgd', p_w, wkv[..., 1, :].astype(f32))

    g = jax.nn.sigmoid(gates.astype(f32)).reshape(b, t, N_KV_HEADS, GQA, N_BRANCH)
    o = g[..., 0:1] * o_c + g[..., 1:2] * o_s + g[..., 2:3] * o_w
    return o.reshape(b, t, ATT_WIDTH).astype(q.dtype)


def nsa_prompt(q, gates, cmp_kv, slc_kv, win_kv, cmp_pe, w_cmp, rel_table):
    b, s = q.shape[0], q.shape[1]
    nb = s // BLK
    ckv = compress_blocks(cmp_kv.reshape(b, nb, BLK, N_KV_HEADS, 2, ATT_HEAD_DIM), cmp_pe, w_cmp)
    kc, vc = ckv[..., 0, :], ckv[..., 1, :]
    slc_blocks = slc_kv.reshape(b, nb, BLK, N_KV_HEADS, 2, ATT_HEAD_DIM)
    b_ix = jnp.arange(b)[:, None, None, None]
    h_ix = jnp.arange(N_KV_HEADS)[None, None, :, None]

    def fetch(idx):
        rows = slc_blocks[b_ix, idx, :, h_ix]
        return rows[..., 0, :], rows[..., 1, :]

    win_pad = jnp.pad(win_kv, ((0, 0), (WINDOW, 0), (0, 0), (0, 0), (0, 0)))
    nq = s // Q_BLOCK
    qb = jnp.moveaxis(q.reshape(b, nq, Q_BLOCK, N_KV_HEADS, GQA, ATT_HEAD_DIM), 1, 0)
    gb = jnp.moveaxis(gates.reshape(b, nq, Q_BLOCK, N_BRANCH * N_ATT_HEADS), 1, 0)

    def one_block(args):
        i, q_i, g_i = args
        s0 = i * Q_BLOCK
        pos = s0 + jnp.arange(Q_BLOCK, dtype=jnp.int32)
        wkv = lax.dynamic_slice_in_dim(win_pad, s0, WINDOW + Q_BLOCK, axis=1)
        wpos = s0 - WINDOW + jnp.arange(WINDOW + Q_BLOCK, dtype=jnp.int32)
        return nsa_query_block(q_i, pos, g_i, kc, vc, fetch, wkv, wpos, rel_table)

    o = lax.map(one_block, (jnp.arange(nq, dtype=jnp.int32), qb, gb))
    return jnp.moveaxis(o, 0, 1).reshape(b, s, ATT_WIDTH)


def nsa_sample(q, gates, cmp_new, slc_new, win_new, cache_cmp, cache_slc, cache_win, page_table,
               cmp_pe, w_cmp, rel_table):
    b, t = q.shape[0], q.shape[1]
    npb = PAST_LEN // BLK
    nbn = -(-t // BLK)
    bpp = PAGE_SIZE // BLK
    pad = ((0, 0), (0, nbn * BLK - t), (0, 0), (0, 0), (0, 0))
    new_shape = (b, nbn, BLK, N_KV_HEADS, 2, ATT_HEAD_DIM)
    past_cmp = cache_cmp[page_table].reshape(b, npb, BLK, N_KV_HEADS, 2, ATT_HEAD_DIM)
    ckv = jnp.concatenate([compress_blocks(past_cmp, cmp_pe, w_cmp),
                           compress_blocks(jnp.pad(cmp_new, pad).reshape(new_shape), cmp_pe, w_cmp)], axis=1)
    kc, vc = ckv[..., 0, :], ckv[..., 1, :]
    pool = cache_slc.reshape((-1, bpp, BLK, N_KV_HEADS, 2, ATT_HEAD_DIM))
    new_blocks = jnp.pad(slc_new, pad).reshape(new_shape)
    b_ix = jnp.arange(b)[:, None, None, None]
    h_ix = jnp.arange(N_KV_HEADS)[None, None, :, None]

    def fetch(idx):
        in_past = (idx < npb)[..., None, None, None]
        pidx = jnp.minimum(idx, npb - 1)
        phys = page_table[b_ix, pidx // bpp]
        from_pool = pool[phys, pidx % bpp, :, h_ix]
        from_new = new_blocks[b_ix, jnp.clip(idx - npb, 0, nbn - 1), :, h_ix]
        rows = jnp.where(in_past, from_pool, from_new)
        return rows[..., 0, :], rows[..., 1, :]

    w_buf = cache_win.shape[1]
    wkv = jnp.concatenate([cache_win.astype(win_new.dtype), win_new], axis=1)
    wpos = PAST_LEN - w_buf + jnp.arange(w_buf + t, dtype=jnp.int32)
    pos = PAST_LEN + jnp.arange(t, dtype=jnp.int32)
    o = nsa_query_block(q, pos, gates, kc, vc, fetch, wkv, wpos, rel_table)
    return o, wkv[:, -min(WINDOW, w_buf + t):]


def causal_dwconv(xpad, w, bias):
    out = lax.conv_general_dilated(xpad, w[:, None, :].astype(xpad.dtype), window_strides=(1,), padding='VALID',
                                   dimension_numbers=('NWC', 'WIO', 'NWC'), feature_group_count=xpad.shape[-1])
    return out + bias


def ssd_scan(x, dt, a, b_in, c_in, h0):
    bsz, seq, nh, hp = x.shape
    r = nh // SSD_GROUPS
    cl = min(CHUNK, seq)
    nc = seq // cl
    f32 = jnp.float32
    xdt = (x.astype(f32) * dt[..., None]).reshape(bsz, nc, cl, SSD_GROUPS, r, hp)
    la = jnp.moveaxis((dt * a).reshape(bsz, nc, cl, SSD_GROUPS, r), 2, -1)
    bc = b_in.astype(f32).reshape(bsz, nc, cl, SSD_GROUPS, SSD_STATE)
    cc = c_in.astype(f32).reshape(bsz, nc, cl, SSD_GROUPS, SSD_STATE)
    cs = jnp.cumsum(la, axis=-1)
    causal = jnp.tril(jnp.ones((cl, cl), bool))
    decay = jnp.where(causal, jnp.exp(jnp.where(causal, cs[..., :, None] - cs[..., None, :], 0.0)), 0.0)
    cb = jnp.einsum('bclgn,bcsgn->bcgls', cc, bc)
    y_diag = jnp.einsum('bcgrls,bcsgrp->bclgrp', cb[:, :, :, None] * decay, xdt)
    to_end = jnp.exp(cs[..., -1:] - cs)
    chunk_states = jnp.einsum('bclgn,bcgrl,bclgrp->bcgrpn', bc, to_end, xdt)
    chunk_decay = jnp.exp(cs[..., -1])

    def step(h, inp):
        st, dc = inp
        return h * dc[..., None, None] + st, h

    h_last, h_start = lax.scan(step, h0.astype(f32).reshape(bsz, SSD_GROUPS, r, hp, SSD_STATE),
                               (jnp.moveaxis(chunk_states, 1, 0), jnp.moveaxis(chunk_decay, 1, 0)))
    h_start = jnp.moveaxis(h_start, 0, 1)
    y_off = jnp.einsum('bclgn,bcgrpn,bcgrl->bclgrp', cc, h_start, jnp.exp(cs))
    y = (y_diag + y_off).reshape(bsz, seq, nh, hp)
    return y, h_last.reshape(bsz, nh, hp, SSD_STATE)


def ssd_mixer(z, xbc, dt_raw, conv_prev, h0, conv_w, conv_b, dt_bias, a_log, d_skip, ssd_norm):
    bsz, seq = xbc.shape[0], xbc.shape[1]
    f32 = jnp.float32
    xpad = jnp.concatenate([conv_prev.astype(xbc.dtype), xbc], axis=1)
    new_conv = xpad[:, -(CONV_W - 1):]
    u = jax.nn.silu(causal_dwconv(xpad, conv_w, conv_b).astype(f32))
    xs = u[..., :SSD_INNER].reshape(bsz, seq, SSD_HEADS, SSD_HEAD_DIM)
    bm = u[..., SSD_INNER:SSD_INNER + SSD_GROUPS * SSD_STATE].reshape(bsz, seq, SSD_GROUPS, SSD_STATE)
    cm = u[..., SSD_INNER + SSD_GROUPS * SSD_STATE:].reshape(bsz, seq, SSD_GROUPS, SSD_STATE)
    dt = jax.nn.softplus(dt_raw.astype(f32) + dt_bias.astype(f32))
    a = -jnp.exp(a_log.astype(f32))
    y, h_last = ssd_scan(xs, dt, a, bm, cm, h0)
    y = (y + d_skip.astype(f32)[:, None] * xs).reshape(bsz, seq, SSD_INNER) * jax.nn.silu(z.astype(f32))
    yg = y.reshape(bsz, seq, SSD_GROUPS, SSD_INNER // SSD_GROUPS)
    yg = yg * lax.rsqrt(jnp.mean(yg * yg, axis=-1, keepdims=True) + EPS)
    y = yg.reshape(bsz, seq, SSD_INNER) * ssd_norm.astype(f32)
    return y.astype(z.dtype), h_last.astype(z.dtype), new_conv


def setup_inputs(seed: int = 0) -> dict:
    key = jax.random.key(seed)
    ks = list(jax.random.split(key, 32))

    def nrm(i, shape, scale):
        return jax.random.normal(ks[i], shape, jnp.float32) * scale

    n_pages = PAST_LEN // PAGE_SIZE
    n_used = DEC_BATCH * n_pages
    n_pool = n_used + max(1, n_used // 4)
    page_table = jax.random.permutation(ks[0], n_pool)[:n_used].reshape(DEC_BATCH, n_pages).astype(jnp.int32)
    w_buf = min(WINDOW, PAST_LEN)
    kv = (N_KV_HEADS, 2, ATT_HEAD_DIM)
    dt0 = jnp.exp(jax.random.uniform(ks[1], (DEPTH, SSD_HEADS), jnp.float32, math.log(1e-3), math.log(1e-1)))
    return {
        'x_prompt': nrm(2, (BATCH, SEQ, D_MODEL), 1.0),
        'x_sample': nrm(3, (DEC_BATCH, DEC_SEQ, D_MODEL), 1.0),
        'cache_cmp_kv': nrm(4, (DEPTH, n_pool, PAGE_SIZE) + kv, 1.0),
        'cache_slc_kv': nrm(5, (DEPTH, n_pool, PAGE_SIZE) + kv, 1.0),
        'cache_win_kv': nrm(6, (DEPTH, DEC_BATCH, w_buf) + kv, 1.0),
        'state_ssm': nrm(7, (DEPTH, DEC_BATCH, SSD_HEADS, SSD_HEAD_DIM, SSD_STATE), 0.3),
        'state_conv': nrm(8, (DEPTH, DEC_BATCH, CONV_W - 1, CONV_DIM), 1.0),
        'page_table': page_table,
        'rel_table': nrm(9, (N_BUCKETS, N_ATT_HEADS), 0.5),
        'ffn1_pre': 1.0 + nrm(10, (DEPTH, D_MODEL), 0.05),
        'ffn1_post': 1.0 + nrm(11, (DEPTH, D_MODEL), 0.05),
        'ffn1_w_in': nrm(12, (DEPTH, D_MODEL, 2 * D_FF), D_MODEL ** -0.5),
        'ffn1_w_out': nrm(13, (DEPTH, D_FF, D_MODEL), D_FF ** -0.5),
        'mix_pre': 1.0 + nrm(14, (DEPTH, D_MODEL), 0.05),
        'mix_post': 1.0 + nrm(15, (DEPTH, D_MODEL), 0.05),
        'w_in': nrm(16, (DEPTH, D_MODEL, D_IN), D_MODEL ** -0.5),
        'w_out': nrm(17, (DEPTH, D_MIX, D_MODEL), D_MIX ** -0.5),
        'att_out_norm': 1.0 + nrm(18, (DEPTH, ATT_WIDTH), 0.05),
        'cmp_pe': nrm(19, (DEPTH, BLK, 2, ATT_HEAD_DIM), 0.5),
        'w_cmp': nrm(20, (DEPTH, BLK, 2, ATT_HEAD_DIM, ATT_HEAD_DIM), (BLK * ATT_HEAD_DIM) ** -0.5),
        'conv_w': nrm(21, (DEPTH, CONV_W, CONV_DIM), CONV_W ** -0.5),
        'conv_b': nrm(22, (DEPTH, CONV_DIM), 0.02),
        'dt_bias': dt0 + jnp.log(-jnp.expm1(-dt0)),
        'a_log': jnp.log(jax.random.uniform(ks[23], (DEPTH, SSD_HEADS), jnp.float32, 1.0, 16.0)),
        'd_skip': 1.0 + nrm(24, (DEPTH, SSD_HEADS), 0.1),
        'ssd_norm': 1.0 + nrm(25, (DEPTH, SSD_INNER), 0.05),
        'ffn2_pre': 1.0 + nrm(26, (DEPTH, D_MODEL), 0.05),
        'ffn2_post': 1.0 + nrm(27, (DEPTH, D_MODEL), 0.05),
        'ffn2_w_in': nrm(28, (DEPTH, D_MODEL, 2 * D_FF), D_MODEL ** -0.5),
        'ffn2_w_out': nrm(29, (DEPTH, D_FF, D_MODEL), D_FF ** -0.5),
    }


def reference(x_prompt, x_sample, cache_cmp_kv, cache_slc_kv, cache_win_kv, state_ssm, state_conv, page_table,
              rel_table, ffn1_pre, ffn1_post, ffn1_w_in, ffn1_w_out, mix_pre, mix_post, w_in, w_out, att_out_norm,
              cmp_pe, w_cmp, conv_w, conv_b, dt_bias, a_log, d_skip, ssd_norm,
              ffn2_pre, ffn2_post, ffn2_w_in, ffn2_w_out):
    yp, ys = x_prompt, x_sample
    pc, psl, pw, pssm, pconv = [], [], [], [], []
    sc, ssl, sw, sssm, sconv = [], [], [], [], []
    for l in range(DEPTH):
        yp = ffn_sublayer(yp, ffn1_pre[l], ffn1_post[l], ffn1_w_in[l], ffn1_w_out[l])
        ys = ffn_sublayer(ys, ffn1_pre[l], ffn1_post[l], ffn1_w_in[l], ffn1_w_out[l])

        q, ckv, skv, wkv, gates, z, xbc, dt = mixer_inputs(yp, mix_pre[l], w_in[l])
        o_att = nsa_prompt(q, gates, ckv, skv, wkv, cmp_pe[l], w_cmp[l], rel_table)
        conv0 = jnp.zeros((yp.shape[0], CONV_W - 1, CONV_DIM), xbc.dtype)
        h0 = jnp.zeros((yp.shape[0], SSD_HEADS, SSD_HEAD_DIM, SSD_STATE), jnp.float32)
        o_ssd, h_new, conv_new = ssd_mixer(z, xbc, dt, conv0, h0, conv_w[l], conv_b[l], dt_bias[l], a_log[l],
                                           d_skip[l], ssd_norm[l])
        yp = mixer_output(yp, o_att, o_ssd, att_out_norm[l], mix_post[l], w_out[l])
        pc.append(ckv)
        psl.append(skv)
        pw.append(wkv[:, -min(WINDOW, wkv.shape[1]):])
        pssm.append(h_new)
        pconv.append(conv_new)

        q, ckv, skv, wkv, gates, z, xbc, dt = mixer_inputs(ys, mix_pre[l], w_in[l])
        o_att, win_state = nsa_sample(q, gates, ckv, skv, wkv, cache_cmp_kv[l], cache_slc_kv[l], cache_win_kv[l],
                                      page_table, cmp_pe[l], w_cmp[l], rel_table)
        o_ssd, h_new, conv_new = ssd_mixer(z, xbc, dt, state_conv[l], state_ssm[l], conv_w[l], conv_b[l],
                                           dt_bias[l], a_log[l], d_skip[l], ssd_norm[l])
        ys = mixer_output(ys, o_att, o_ssd, att_out_norm[l], mix_post[l], w_out[l])
        sc.append(ckv)
        ssl.append(skv)
        sw.append(win_state)
        sssm.append(h_new)
        sconv.append(conv_new)

        yp = ffn_sublayer(yp, ffn2_pre[l], ffn2_post[l], ffn2_w_in[l], ffn2_w_out[l])
        ys = ffn_sublayer(ys, ffn2_pre[l], ffn2_post[l], ffn2_w_in[l], ffn2_w_out[l])

    p_cmp_kv = jnp.stack(pc)
    p_slc_kv = jnp.stack(psl)
    p_win_kv = jnp.stack(pw)
    p_ssm = jnp.stack(pssm)
    p_conv = jnp.stack(pconv)
    s_cmp_kv = jnp.stack(sc)
    s_slc_kv = jnp.stack(ssl)
    s_win_kv = jnp.stack(sw)
    s_ssm = jnp.stack(sssm)
    s_conv = jnp.stack(sconv)
    return (yp, ys, p_cmp_kv, p_slc_kv, p_win_kv, p_ssm, p_conv, s_cmp_kv, s_slc_kv, s_win_kv, s_ssm, s_conv)
```

```python
import functools
import math

import jax
import jax.numpy as jnp
from jax import lax
from jax.experimental import pallas as pl
from jax.experimental.pallas import tpu as pltpu

F32 = jnp.float32
BF16 = jnp.bfloat16

D_MODEL = 1024
N_ATT_HEADS = 16
HEAD_DIM = 64
N_KV_HEADS = 4
GQA = N_ATT_HEADS // N_KV_HEADS
ATT_WIDTH = N_ATT_HEADS * HEAD_DIM
KV_WIDTH = 2 * N_KV_HEADS * HEAD_DIM
KV_GROUP = 2 * HEAD_DIM
BLK = 64
N_SEL = 16
WINDOW = 512
N_BRANCH = 3
N_BUCKETS = 32
MAX_DISTANCE = 128
SSD_HEADS = 16
SSD_HEAD_DIM = 64
SSD_INNER = SSD_HEADS * SSD_HEAD_DIM
SSD_GROUPS = 2
SSD_STATE = 128
CONV_W = 4
CONV_DIM = SSD_INNER + 2 * SSD_GROUPS * SSD_STATE
CHUNK = 128
PAGE_SIZE = 128
EPS = 1e-6

TILE = 128
SUBLANES = 8
MASK_VALUE = -(2.0 ** 100)
M_INIT = -1e30
VMEM_LIMIT = 56 * 1024 * 1024
PAGES_PER_STEP = 8

NT_DIMS = (((1,), (1,)), ((), ()))
TN_DIMS = (((0,), (0,)), ((), ()))


def _params(n_grid_dims):
    return pltpu.CompilerParams(dimension_semantics=("arbitrary",) * n_grid_dims,
                                vmem_limit_bytes=VMEM_LIMIT)


def _rms(x, g):
    return x * lax.rsqrt(jnp.mean(x * x, axis=-1, keepdims=True) + EPS) * g


def _const_spec(shape):
    zeros = (0,) * len(shape)
    return pl.BlockSpec(shape, lambda *_: zeros, pipeline_mode=pl.Buffered(1))


def _ffn_body(x_ref, pre_ref, post_ref, wg_ref, wu_ref, wo_ref, o_ref):
    x = x_ref[...]
    h = _rms(x, pre_ref[...]).astype(BF16)
    gate = jnp.dot(h, wg_ref[...], preferred_element_type=F32)
    up = jnp.dot(h, wu_ref[...], preferred_element_type=F32)
    a = (gate * jax.nn.sigmoid(gate) * up).astype(BF16)
    y = jnp.dot(a, wo_ref[...], preferred_element_type=F32)
    o_ref[...] = x + 0.5 * _rms(y, post_ref[...])


def _ffn(x, pre, post, wg, wu, wo):
    t, d = x.shape
    f = wg.shape[1]
    tm = min(256, t)
    row = pl.BlockSpec((tm, d), lambda i: (i, 0))
    return pl.pallas_call(
        _ffn_body, grid=(t // tm,),
        in_specs=[row, _const_spec((1, d)), _const_spec((1, d)), _const_spec((d, f)), _const_spec((d, f)),
                  _const_spec((f, d))],
        out_specs=row, out_shape=jax.ShapeDtypeStruct((t, d), F32),
        compiler_params=_params(1), name="ffn")(x, pre, post, wg, wu, wo)


_SEG = {}
_off = 0
for _name, _w in (("q", ATT_WIDTH), ("ckv", KV_WIDTH), ("skv", KV_WIDTH), ("wkv", KV_WIDTH),
                  ("gates", N_KV_HEADS * TILE), ("z", SSD_INNER), ("xbc", CONV_DIM), ("dt", TILE)):
    _SEG[_name] = (_off, _off + _w)
    _off += _w
PROJ_WIDTH = _off


def _inproj_body(x_ref, pre_ref, w_ref, wdt_t_ref, q_ref, ckv_ref, skv_ref, wkv_ref, skvb_ref, wkvb_ref,
                 gates_ref, z_ref, xbc_ref, dt_ref, dtt_ref):
    h = _rms(x_ref[...], pre_ref[...]).astype(BF16)

    def proj(name):
        lo, hi = _SEG[name]
        return jnp.dot(h, w_ref[:, lo:hi], preferred_element_type=F32)

    q_ref[...] = proj("q")
    ckv_ref[...] = proj("ckv")
    skv = proj("skv")
    skv_ref[...] = skv
    skvb_ref[...] = skv.astype(BF16)
    wkv = proj("wkv")
    wkv_ref[...] = wkv
    wkvb_ref[...] = wkv.astype(BF16)
    gates_ref[...] = proj("gates")
    z_ref[...] = proj("z")
    xbc_ref[...] = proj("xbc")
    dt_ref[...] = proj("dt")
    dtt_ref[...] = lax.dot_general(wdt_t_ref[...], h, NT_DIMS, preferred_element_type=F32)


def _inproj(x, pre, w, wdt_t):
    t, d = x.shape
    tm = min(256, t)

    def rows(width):
        return pl.BlockSpec((tm, width), lambda i: (i, 0))

    widths = (ATT_WIDTH, KV_WIDTH, KV_WIDTH, KV_WIDTH, KV_WIDTH, KV_WIDTH, N_KV_HEADS * TILE, SSD_INNER,
              CONV_DIM, TILE)
    dtypes = (F32, F32, F32, F32, BF16, BF16, F32, F32, F32, F32)
    out_shape = [jax.ShapeDtypeStruct((t, wd), dt) for wd, dt in zip(widths, dtypes)]
    out_shape.append(jax.ShapeDtypeStruct((TILE, t), F32))
    out_specs = [rows(wd) for wd in widths] + [pl.BlockSpec((TILE, tm), lambda i: (0, i))]
    return pl.pallas_call(
        _inproj_body, grid=(t // tm,),
        in_specs=[rows(d), _const_spec((1, d)), _const_spec((d, PROJ_WIDTH)), _const_spec((TILE, d))],
        out_specs=out_specs, out_shape=out_shape,
        compiler_params=_params(1), name="inproj")(x, pre, w, wdt_t)


def _mixout_body(x_ref, oa_ref, os_ref, an_ref, post_ref, wa_ref, ws_ref, o_ref):
    a = _rms(oa_ref[...], an_ref[...]).astype(BF16)
    y = jnp.dot(a, wa_ref[...], preferred_element_type=F32)
    y = y + jnp.dot(os_ref[...].astype(BF16), ws_ref[...], preferred_element_type=F32)
    o_ref[...] = x_ref[...] + _rms(y, post_ref[...])


def _mixout(x, o_att, o_ssd, att_norm, post, wa, ws):
    t, d = x.shape
    tm = min(256, t)
    row = pl.BlockSpec((tm, d), lambda i: (i, 0))
    return pl.pallas_call(
        _mixout_body, grid=(t // tm,),
        in_specs=[row, row, row, _const_spec((1, d)), _const_spec((1, d)), _const_spec((d, d)),
                  _const_spec((d, d))],
        out_specs=row, out_shape=jax.ShapeDtypeStruct((t, d), F32),
        compiler_params=_params(1), name="mixout")(x, o_att, o_ssd, att_norm, post, wa, ws)


def _compress_rows(x_refs, pe_ref, bd_ref, n_blocks):
    acc = jnp.zeros((N_KV_HEADS * n_blocks, KV_GROUP), F32)
    for l in range(BLK):
        pe = pe_ref[l:l + 1, :]
        xs = jnp.concatenate([x_ref[pl.ds(l, n_blocks, stride=BLK), :] + pe for x_ref in x_refs], axis=0)
        acc = acc + jnp.dot(xs.astype(BF16), bd_ref[l], preferred_element_type=F32)
    return acc


def _store_summaries(acc, o_ref, n_blocks):
    for k in range(N_KV_HEADS):
        o_ref[0:n_blocks, k * KV_GROUP:(k + 1) * KV_GROUP] = acc[k * n_blocks:(k + 1) * n_blocks].astype(BF16)


def _compress_body(*refs, n_blocks, n_out):
    x_refs = refs[:N_KV_HEADS]
    pe_ref, bd_ref, o_ref = refs[N_KV_HEADS:]
    _store_summaries(_compress_rows(x_refs, pe_ref, bd_ref, n_blocks), o_ref, n_blocks)
    if n_out > n_blocks:
        o_ref[n_blocks:n_out, :] = jnp.zeros((n_out - n_blocks, KV_WIDTH), BF16)


def _compress(kv, pe, bd, n_out):
    b, s, _ = kv.shape
    n_blocks = s // BLK

    def head_spec(k):
        return pl.BlockSpec((None, s, KV_GROUP), lambda i: (i, 0, k))

    return pl.pallas_call(
        functools.partial(_compress_body, n_blocks=n_blocks, n_out=n_out), grid=(b,),
        in_specs=[head_spec(k) for k in range(N_KV_HEADS)]
        + [_const_spec((BLK, KV_GROUP)), _const_spec((BLK, KV_GROUP, KV_GROUP))],
        out_specs=pl.BlockSpec((None, n_out, KV_WIDTH), lambda i: (i, 0, 0)),
        out_shape=jax.ShapeDtypeStruct((b, n_out, KV_WIDTH), BF16),
        compiler_params=_params(1), name="compress")(*([kv] * N_KV_HEADS), pe, bd)


def _compress_paged_body(pt_ref, *refs, n_pages):
    page_refs = refs[:PAGES_PER_STEP]
    pe_ref, bd_ref, o_ref, stage_ref = refs[PAGES_PER_STEP:]
    j = pl.program_id(1)
    for p in range(PAGES_PER_STEP):
        row = pl.multiple_of((j * PAGES_PER_STEP + p) * PAGE_SIZE, PAGE_SIZE)
        for k in range(N_KV_HEADS):
            stage_ref[k, pl.ds(row, PAGE_SIZE), :] = page_refs[p][:, k * KV_GROUP:(k + 1) * KV_GROUP]

    @pl.when(j == pl.num_programs(1) - 1)
    def _():
        n_blocks = n_pages * (PAGE_SIZE // BLK)
        x_refs = [stage_ref.at[k] for k in range(N_KV_HEADS)]
        _store_summaries(_compress_rows(x_refs, pe_ref, bd_ref, n_blocks), o_ref, n_blocks)


def _compress_paged(cache, page_table, pe, bd):
    b, n_pages = page_table.shape
    steps = n_pages // PAGES_PER_STEP
    n_blocks = n_pages * (PAGE_SIZE // BLK)

    def page_spec(p):
        return pl.BlockSpec((None, PAGE_SIZE, KV_WIDTH), lambda i, j, pt: (pt[i, j * PAGES_PER_STEP + p], 0, 0))

    grid_spec = pltpu.PrefetchScalarGridSpec(
        num_scalar_prefetch=1, grid=(b, steps),
        in_specs=[page_spec(p) for p in range(PAGES_PER_STEP)]
        + [pl.BlockSpec((BLK, KV_GROUP), lambda i, j, pt: (0, 0)),
           pl.BlockSpec((BLK, KV_GROUP, KV_GROUP), lambda i, j, pt: (0, 0, 0))],
        out_specs=pl.BlockSpec((None, n_blocks, KV_WIDTH), lambda i, j, pt: (i, 0, 0)),
        scratch_shapes=[pltpu.VMEM((N_KV_HEADS, n_pages * PAGE_SIZE, KV_GROUP), F32)])
    return pl.pallas_call(
        functools.partial(_compress_paged_body, n_pages=n_pages), grid_spec=grid_spec,
        out_shape=jax.ShapeDtypeStruct((b, n_blocks, KV_WIDTH), BF16),
        compiler_params=_params(2), name="compress_paged")(page_table, *([cache] * PAGES_PER_STEP), pe, bd)


def _rel_bucket(dist):
    n = jnp.maximum(dist, 0)
    max_exact = N_BUCKETS // 2
    nf = jnp.maximum(n, 1).astype(F32)
    log_b = max_exact + (jnp.log(nf / max_exact) / math.log(MAX_DISTANCE / max_exact)
                         * (N_BUCKETS - max_exact)).astype(jnp.int32)
    return jnp.where(n < max_exact, n, jnp.minimum(log_b, N_BUCKETS - 1))


def _bias_of(rel_table, dist, visible):
    bias = jnp.moveaxis(rel_table.astype(F32)[_rel_bucket(dist)], -1, 0)
    return jnp.where(visible[None], bias, MASK_VALUE)


def _prompt_bias_tables(rel_table):
    r = jnp.arange(TILE, dtype=jnp.int32)[:, None]
    c = jnp.arange(TILE, dtype=jnp.int32)[None, :]
    true = jnp.ones((TILE, TILE), bool)
    far = jnp.full((TILE, TILE), 4 * TILE, jnp.int32)
    tiles = jnp.stack([_bias_of(rel_table, r - c, r >= c), _bias_of(rel_table, TILE + r - c, true),
                       _bias_of(rel_table, far, true), _bias_of(rel_table, far, r < c)])
    c_dist = r - (BLK - 1) + BLK * (BLK - 1 - c)
    cmp_tab = _bias_of(rel_table, c_dist, c_dist >= 0)
    return tiles, cmp_tab


def _selection_mask(score, blk, n_rows):
    n_groups = score.shape[0] // SUBLANES
    groups = [score[SUBLANES * j:SUBLANES * (j + 1)] for j in range(n_groups)]
    ranks = [jnp.zeros(groups[0].shape, jnp.int32) for _ in range(n_groups)]
    row8 = blk[0:SUBLANES]
    for n in range(n_rows):
        other = jnp.broadcast_to(score[n:n + 1, :], groups[0].shape)
        for j in range(n_groups):
            lo = SUBLANES * j
            if lo > n:
                inc = jnp.where(other >= groups[j], 1, 0)
            elif lo + SUBLANES - 1 <= n:
                inc = jnp.where(other > groups[j], 1, 0)
            else:
                inc = jnp.where(row8 + lo > n, jnp.where(other >= groups[j], 1, 0),
                                jnp.where(other > groups[j], 1, 0))
            ranks[j] = ranks[j] + inc
    rank = jnp.concatenate(ranks, axis=0)
    return jnp.where(rank < N_SEL, jnp.where(score > -jnp.inf, 0.0, MASK_VALUE), MASK_VALUE)


def _softmax_step(s, kv, m_ref, l_ref, acc_ref, g):
    m_prev = m_ref[g]
    m_new = jnp.maximum(m_prev, jnp.max(s, axis=1, keepdims=True))
    alpha = jnp.exp(m_prev - m_new)
    p = jnp.exp(s - m_new)
    l_ref[g] = alpha * l_ref[g] + jnp.sum(p, axis=1, keepdims=True)
    acc_ref[g] = alpha * acc_ref[g] + jnp.dot(p.astype(BF16), kv, preferred_element_type=F32)
    m_ref[g] = m_new


def _nsa_prompt_body(q_ref, gates_ref, kvc_ref, skv_ref, wkv_ref, tb_ref, ctab_ref, o_ref,
                     m_ref, l_ref, acc_ref, *, n_blocks):
    i = pl.program_id(2)
    lane = lax.broadcasted_iota(jnp.int32, (TILE, TILE), 1)
    sub = lax.broadcasted_iota(jnp.int32, (TILE, TILE), 0)
    low = lane < HEAD_DIM

    qf = q_ref[...]
    q_pad = []
    for g in range(GQA):
        pair = qf[:, (g // 2) * TILE:(g // 2 + 1) * TILE]
        q_pad.append(pair if g % 2 == 0 else pltpu.roll(pair, HEAD_DIM, 1))
    lhs_plain = [jnp.where(low, q_pad[g], 0.0).astype(BF16) for g in range(GQA)]

    def reset():
        m_ref[...] = jnp.full(m_ref.shape, M_INIT, F32)
        l_ref[...] = jnp.zeros(l_ref.shape, F32)
        acc_ref[...] = jnp.zeros(acc_ref.shape, F32)

    def result(g):
        return acc_ref[g] / jnp.maximum(l_ref[g], 1e-30)

    kvc = kvc_ref[...]
    shift = (2 * i + (TILE - (BLK - 1))) % TILE
    imp = jnp.zeros((TILE, TILE), F32)
    o_cmp = []
    for g in range(GQA):
        s = lax.dot_general(lhs_plain[g], kvc, NT_DIMS, preferred_element_type=F32)
        bias = pltpu.roll(ctab_ref[g], shift, 1)
        s = jnp.where(lane < n_blocks, s + bias, MASK_VALUE)
        m = jnp.maximum(jnp.max(s, axis=1, keepdims=True), M_INIT)
        p = jnp.exp(s - m)
        p = p / jnp.maximum(jnp.sum(p, axis=1, keepdims=True), 1e-30)
        imp = imp + p
        o_cmp.append(jnp.dot(p.astype(BF16), kvc, preferred_element_type=F32))

    blk = lax.broadcasted_iota(jnp.int32, (n_blocks, TILE), 0)
    query = lax.broadcasted_iota(jnp.int32, (n_blocks, TILE), 1)
    cur = 2 * i + jnp.where(query >= BLK, 1, 0)
    forced = (blk == 0) | (blk == cur) | (blk == cur - 1)
    score = jnp.where(forced, jnp.inf, jnp.where(blk <= cur, imp.T[0:n_blocks], -jnp.inf))
    neg_t = _selection_mask(score, blk, n_blocks)
    pieces = [jnp.zeros((HEAD_DIM, TILE), F32), neg_t]
    if n_blocks < TILE - HEAD_DIM:
        pieces.append(jnp.zeros((TILE - HEAD_DIM - n_blocks, TILE), F32))
    neg = jnp.concatenate(pieces, axis=0).T
    lhs_sel = [jnp.where(low, q_pad[g], neg).astype(BF16) for g in range(GQA)]

    reset()

    def sel_tile(t, carry):
        kv = skv_ref[pl.ds(pl.multiple_of(t * TILE, TILE), TILE), :]
        onehot = jnp.where(lane - HEAD_DIM == 2 * t + jnp.where(sub >= BLK, 1, 0), 1.0, 0.0).astype(BF16)
        rhs = jnp.where(low, kv, onehot)
        kind = jnp.minimum(i - t, 2)
        for g in range(GQA):
            s = lax.dot_general(lhs_sel[g], rhs, NT_DIMS, preferred_element_type=F32) + tb_ref[kind, g]
            _softmax_step(s, kv, m_ref, l_ref, acc_ref, g)
        return carry

    lax.fori_loop(0, i + 1, sel_tile, 0)
    o_sel = [result(g) for g in range(GQA)]

    reset()

    def win_tile(t, carry):
        kv = wkv_ref[pl.ds(pl.multiple_of(t * TILE, TILE), TILE), :]
        d = i - t
        kind = jnp.where(d == WINDOW // TILE, 3, jnp.minimum(d, 2))
        for g in range(GQA):
            s = lax.dot_general(lhs_plain[g], kv, NT_DIMS, preferred_element_type=F32) + tb_ref[kind, g]
            _softmax_step(s, kv, m_ref, l_ref, acc_ref, g)
        return carry

    lax.fori_loop(jnp.maximum(i - WINDOW // TILE, 0), i + 1, win_tile, 0)

    gate = jax.nn.sigmoid(gates_ref[...])
    outs = []
    for g in range(GQA):
        def gcol(br):
            c = g * N_BRANCH + br
            return gate[:, c:c + 1]
        outs.append(gcol(0) * o_cmp[g] + gcol(1) * o_sel[g] + gcol(2) * result(g))
    for j in range(GQA // 2):
        o_ref[:, j * TILE:(j + 1) * TILE] = jnp.where(low, pltpu.roll(outs[2 * j], HEAD_DIM, 1), outs[2 * j + 1])


def _nsa_prompt(q, gates, kvc, skv_b, wkv_b, tiles, cmp_tab):
    b, s, _ = q.shape
    nq = s // TILE
    n_blocks = s // BLK
    assert n_blocks <= TILE - HEAD_DIM and kvc.shape[1] == TILE
    gw = GQA * HEAD_DIM
    return pl.pallas_call(
        functools.partial(_nsa_prompt_body, n_blocks=n_blocks), grid=(b, N_KV_HEADS, nq),
        in_specs=[pl.BlockSpec((None, TILE, gw), lambda bi, k, i: (bi, i, k)),
                  pl.BlockSpec((None, TILE, TILE), lambda bi, k, i: (bi, i, k)),
                  pl.BlockSpec((None, TILE, KV_GROUP), lambda bi, k, i: (bi, 0, k)),
                  pl.BlockSpec((None, s, KV_GROUP), lambda bi, k, i: (bi, 0, k)),
                  pl.BlockSpec((None, s, KV_GROUP), lambda bi, k, i: (bi, 0, k)),
                  pl.BlockSpec((4, GQA, TILE, TILE), lambda bi, k, i: (0, k, 0, 0)),
                  pl.BlockSpec((GQA, TILE, TILE), lambda bi, k, i: (k, 0, 0))],
        out_specs=pl.BlockSpec((None, TILE, gw), lambda bi, k, i: (bi, i, k)),
        out_shape=jax.ShapeDtypeStruct((b, s, ATT_WIDTH), F32),
        scratch_shapes=[pltpu.VMEM((GQA, TILE, TILE), F32), pltpu.VMEM((GQA, TILE, TILE), F32),
                        pltpu.VMEM((GQA, TILE, KV_GROUP), F32)],
        compiler_params=_params(3), name="nsa_prompt")(q, gates, kvc, skv_b, wkv_b, tiles, cmp_tab)


def _t_softmax_step(kv, qbd, bias, m_ref, l_ref, acc_ref):
    s = jnp.dot(kv, qbd, preferred_element_type=F32) + bias
    m_prev = m_ref[...]
    m_new = jnp.maximum(m_prev, jnp.max(s, axis=0, keepdims=True))
    alpha = jnp.exp(m_prev - m_new)
    p = jnp.exp(s - m_new)
    l_ref[...] = alpha * l_ref[...] + jnp.sum(p, axis=0, keepdims=True)
    acc_ref[...] = alpha * acc_ref[...] + lax.dot_general(kv, p.astype(BF16), TN_DIMS,
                                                          preferred_element_type=F32)
    m_ref[...] = m_new


def _nsa_sample_body(pt_ref, *refs, n_pages, dec_seq, past_len):
    page_refs = refs[:PAGES_PER_STEP]
    (qbd_ref, gates_ref, kc_ref, ctab_ref, gsum_ref, snew_ref, cwin_ref, wnew_ref, last_ref, new_ref, far_ref,
     win0_ref, o_ref, neg_ref, m_ref, l_ref, acc_ref, oc_ref, ow_ref) = refs[PAGES_PER_STEP:]
    j = pl.program_id(1)
    n_past = n_pages * (PAGE_SIZE // BLK)
    n_rows = kc_ref.shape[0]
    sub = lax.broadcasted_iota(jnp.int32, (TILE, TILE), 0)
    qbd = qbd_ref[...]
    far = far_ref[0:1, :]

    def reset():
        m_ref[...] = jnp.full(m_ref.shape, M_INIT, F32)
        l_ref[...] = jnp.zeros(l_ref.shape, F32)
        acc_ref[...] = jnp.zeros(acc_ref.shape, F32)

    def result():
        return acc_ref[...] / jnp.maximum(l_ref[...], 1e-30)

    @pl.when(j == 0)
    def _():
        kc = kc_ref[...]
        s = jnp.dot(kc, qbd, preferred_element_type=F32) + ctab_ref[...]
        m = jnp.maximum(jnp.max(s, axis=0, keepdims=True), M_INIT)
        p = jnp.exp(s - m)
        p = p / jnp.maximum(jnp.sum(p, axis=0, keepdims=True), 1e-30)
        oc_ref[...] = lax.dot_general(kc, p.astype(BF16), TN_DIMS, preferred_element_type=F32)
        imp = jnp.dot(p, gsum_ref[...], preferred_element_type=F32, precision=lax.Precision.HIGHEST)
        blk = lax.broadcasted_iota(jnp.int32, (n_rows, TILE), 0)
        tok = lax.broadcasted_iota(jnp.int32, (n_rows, TILE), 1) % dec_seq
        cur = (past_len + tok) // BLK
        forced = (blk == 0) | (blk == cur) | (blk == cur - 1)
        score = jnp.where(forced, jnp.inf, jnp.where(blk <= cur, imp, -jnp.inf))
        neg_ref[...] = _selection_mask(score, blk, n_past + 1)

        reset()
        n_win = cwin_ref.shape[0] // TILE
        for t in range(n_win):
            kv = cwin_ref[t * TILE:(t + 1) * TILE, :].astype(BF16)
            bias = win0_ref[...] if t == 0 else (last_ref[...] if t == n_win - 1 else far)
            _t_softmax_step(kv, qbd, bias, m_ref, l_ref, acc_ref)
        _t_softmax_step(wnew_ref[...], qbd, new_ref[...], m_ref, l_ref, acc_ref)
        ow_ref[...] = result()
        reset()

    blocks_per_step = PAGES_PER_STEP * (PAGE_SIZE // BLK)
    neg_rows = neg_ref[pl.ds(pl.multiple_of(j * blocks_per_step, blocks_per_step), blocks_per_step), :]
    for p in range(PAGES_PER_STEP):
        page = j * PAGES_PER_STEP + p
        kv = page_refs[p][...].astype(BF16)
        first = neg_rows[2 * p:2 * p + 1, :]
        second = neg_rows[2 * p + 1:2 * p + 2, :]
        bias = jnp.where(sub < BLK, first, second) + jnp.where(page == n_pages - 1, last_ref[...], far)
        _t_softmax_step(kv, qbd, bias, m_ref, l_ref, acc_ref)

    @pl.when(j == pl.num_programs(1) - 1)
    def _():
        _t_softmax_step(snew_ref[...], qbd, new_ref[...] + neg_ref[n_past:n_past + 1, :], m_ref, l_ref, acc_ref)
        gate = jax.nn.sigmoid(gates_ref[...])
        o_ref[...] = gate[0:1, :] * oc_ref[...] + gate[1:2, :] * result() + gate[2:3, :] * ow_ref[...]


def _nsa_sample(page_table, cache_slc, qbd, gates_t, kc, ctab, gsum, slc_new, cache_win, win_new, last_tab,
                new_tab, far_row, win0_tab, *, dec_seq, past_len):
    b, n_pages = page_table.shape
    steps = n_pages // PAGES_PER_STEP
    n_rows = kc.shape[1]
    w_buf = cache_win.shape[1]
    n_new = slc_new.shape[1]

    def page_spec(p):
        return pl.BlockSpec((None, PAGE_SIZE, KV_WIDTH), lambda i, j, pt: (pt[i, j * PAGES_PER_STEP + p], 0, 0))

    def per_batch(*shape):
        zeros = (0,) * len(shape)
        return pl.BlockSpec((None,) + shape, lambda i, j, pt: (i,) + zeros)

    def const(*shape):
        zeros = (0,) * len(shape)
        return pl.BlockSpec(shape, lambda i, j, pt: zeros)

    grid_spec = pltpu.PrefetchScalarGridSpec(
        num_scalar_prefetch=1, grid=(b, steps),
        in_specs=[page_spec(p) for p in range(PAGES_PER_STEP)]
        + [per_batch(KV_WIDTH, TILE), per_batch(SUBLANES, TILE), per_batch(n_rows, KV_WIDTH), const(n_rows, TILE),
           const(TILE, TILE), per_batch(n_new, KV_WIDTH), per_batch(w_buf, KV_WIDTH), per_batch(n_new, KV_WIDTH),
           const(TILE, TILE), const(n_new, TILE), const(SUBLANES, TILE), const(TILE, TILE)],
        out_specs=per_batch(KV_WIDTH, TILE),
        scratch_shapes=[pltpu.VMEM((n_rows, TILE), F32), pltpu.VMEM((1, TILE), F32), pltpu.VMEM((1, TILE), F32),
                        pltpu.VMEM((KV_WIDTH, TILE), F32), pltpu.VMEM((KV_WIDTH, TILE), F32),
                        pltpu.VMEM((KV_WIDTH, TILE), F32)])
    return pl.pallas_call(
        functools.partial(_nsa_sample_body, n_pages=n_pages, dec_seq=dec_seq, past_len=past_len),
        grid_spec=grid_spec, out_shape=jax.ShapeDtypeStruct((b, KV_WIDTH, TILE), F32),
        compiler_params=_params(2), name="nsa_sample")(
            page_table, *([cache_slc] * PAGES_PER_STEP), qbd, gates_t, kc, ctab, gsum, slc_new, cache_win, win_new,
            last_tab, new_tab, far_row, win0_tab)


def _softplus(x):
    return jnp.maximum(x, 0.0) + jnp.log(1.0 + jnp.exp(-jnp.abs(x)))


def _ssd_body(z_ref, xbc_ref, dt_ref, dtt_ref, convp_ref, h0_ref, cw_ref, cb_ref, dtb_ref, dtbt_ref, al_ref,
              alt_ref, dskip_ref, norm_ref, o_ref, hout_ref, convout_ref, xp_ref, *, cl):
    c = pl.program_id(1)
    pad = SUBLANES

    @pl.when(c == 0)
    def _():
        xp_ref[0:pad, :] = convp_ref[...]
        hout_ref[...] = h0_ref[...]

    xp_ref[pad:pad + cl, :] = xbc_ref[...]
    conv = cb_ref[...]
    for k in range(CONV_W):
        lo = pad - (CONV_W - 1) + k
        conv = conv + cw_ref[k:k + 1, :] * xp_ref[lo:lo + cl, :]
    u = conv * jax.nn.sigmoid(conv)
    tail = xp_ref[pad + cl - (CONV_W - 1):pad + cl, :]
    convout_ref[...] = tail
    xp_ref[pad - (CONV_W - 1):pad, :] = tail

    xs = u[:, :SSD_INNER]
    dt = _softplus(dt_ref[:, 0:SSD_HEADS] + dtb_ref[...])
    dt_t = _softplus(dtt_ref[0:SSD_HEADS, :] + dtbt_ref[...])
    la = dt * -jnp.exp(al_ref[...])
    la_t = dt_t * -jnp.exp(alt_ref[...])
    row = lax.broadcasted_iota(jnp.int32, (cl, cl), 0)
    col = lax.broadcasted_iota(jnp.int32, (cl, cl), 1)
    causal = row >= col
    hi = lax.Precision.HIGHEST
    cs = jnp.dot(jnp.where(causal, 1.0, 0.0), la, preferred_element_type=F32, precision=hi)
    cs_t = jnp.dot(la_t, jnp.where(row <= col, 1.0, 0.0), preferred_element_type=F32, precision=hi)

    heads_per_group = SSD_HEADS // SSD_GROUPS
    ys = []
    for g in range(SSD_GROUPS):
        b_lo = SSD_INNER + g * SSD_STATE
        c_lo = SSD_INNER + SSD_GROUPS * SSD_STATE + g * SSD_STATE
        bm = u[:, b_lo:b_lo + SSD_STATE].astype(BF16)
        cm = u[:, c_lo:c_lo + SSD_STATE].astype(BF16)
        cb = lax.dot_general(cm, bm, NT_DIMS, preferred_element_type=F32)
        for r in range(heads_per_group):
            h = g * heads_per_group + r
            cs_col = cs[:, h:h + 1]
            diff = cs_col - cs_t[h:h + 1, :]
            decay = jnp.where(causal, jnp.exp(jnp.where(causal, diff, 0.0)), 0.0)
            xs_h = xs[:, h * SSD_HEAD_DIM:(h + 1) * SSD_HEAD_DIM]
            xdt = xs_h * dt[:, h:h + 1]
            y = jnp.dot((cb * decay).astype(BF16), xdt.astype(BF16), preferred_element_type=F32)
            last = cs[cl - 1:cl, h:h + 1]
            w = (xdt * jnp.exp(last - cs_col)).astype(BF16)
            state_in = lax.dot_general(w, bm, TN_DIMS, preferred_element_type=F32)
            h_prev = hout_ref[h]
            y = y + lax.dot_general(cm, h_prev.astype(BF16), NT_DIMS, preferred_element_type=F32) * jnp.exp(cs_col)
            hout_ref[h] = h_prev * jnp.exp(last) + state_in
            ys.append(y + dskip_ref[:, h * SSD_HEAD_DIM:(h + 1) * SSD_HEAD_DIM] * xs_h)
    y = jnp.concatenate(ys, axis=1)
    z = z_ref[...]
    y = y * (z * jax.nn.sigmoid(z))
    gw = SSD_INNER // SSD_GROUPS
    normed = []
    for g in range(SSD_GROUPS):
        yg = y[:, g * gw:(g + 1) * gw]
        normed.append(yg * lax.rsqrt(jnp.mean(yg * yg, axis=-1, keepdims=True) + EPS))
    o_ref[...] = jnp.concatenate(normed, axis=1) * norm_ref[...]


def _ssd(z, xbc, dt, dt_t, conv_prev, h0, consts, cl):
    b, seq, _ = z.shape
    nc = seq // cl
    cw, cbias, dtb, dtb_t, al, al_t, dskip, norm = consts

    def rows(width):
        return pl.BlockSpec((None, cl, width), lambda i, c: (i, c, 0))

    state = pl.BlockSpec((None, SSD_HEADS, SSD_HEAD_DIM, SSD_STATE), lambda i, c: (i, 0, 0, 0))

    def const(x):
        zeros = (0,) * x.ndim
        return pl.BlockSpec(x.shape, lambda i, c: zeros)

    return pl.pallas_call(
        functools.partial(_ssd_body, cl=cl), grid=(b, nc),
        in_specs=[rows(SSD_INNER), rows(CONV_DIM), rows(TILE),
                  pl.BlockSpec((None, TILE, cl), lambda i, c: (i, 0, c)),
                  pl.BlockSpec((None, SUBLANES, CONV_DIM), lambda i, c: (i, 0, 0)), state,
                  const(cw), const(cbias), const(dtb), const(dtb_t), const(al), const(al_t), const(dskip),
                  const(norm)],
        out_specs=[rows(SSD_INNER), state, pl.BlockSpec((None, CONV_W - 1, CONV_DIM), lambda i, c: (i, 0, 0))],
        out_shape=[jax.ShapeDtypeStruct((b, seq, SSD_INNER), F32),
                   jax.ShapeDtypeStruct((b, SSD_HEADS, SSD_HEAD_DIM, SSD_STATE), F32),
                   jax.ShapeDtypeStruct((b, CONV_W - 1, CONV_DIM), F32)],
        scratch_shapes=[pltpu.VMEM((SUBLANES + cl, CONV_DIM), F32)],
        compiler_params=_params(2), name="ssd")(z, xbc, dt, dt_t, conv_prev, h0, cw, cbias, dtb, dtb_t, al, al_t,
                                                 dskip, norm)


def _pack_inproj(w_in):
    splits = (ATT_WIDTH, KV_WIDTH, KV_WIDTH, KV_WIDTH, N_BRANCH * N_ATT_HEADS, SSD_INNER, CONV_DIM, SSD_HEADS)
    offs = [0]
    for s in splits:
        offs.append(offs[-1] + s)
    wq, wc, ws, ww, wg, wz, wx, wd = [w_in[:, offs[i]:offs[i + 1]] for i in range(len(splits))]
    d = w_in.shape[0]
    per_kv = GQA * N_BRANCH
    wg4 = jnp.pad(wg.reshape(d, N_KV_HEADS, per_kv), ((0, 0), (0, 0), (0, TILE - per_kv)))
    wd_pad = jnp.pad(wd, ((0, 0), (0, TILE - SSD_HEADS)))
    w = jnp.concatenate([wq * HEAD_DIM ** -0.5, wc, ws, ww, wg4.reshape(d, N_KV_HEADS * TILE), wz, wx, wd_pad],
                        axis=1)
    assert w.shape[1] == PROJ_WIDTH
    return w.astype(BF16), wd_pad.T.astype(BF16)


def _pack_compress(cmp_pe, w_cmp):
    zero = jnp.zeros((BLK, HEAD_DIM, HEAD_DIM), w_cmp.dtype)
    bd = jnp.concatenate([jnp.concatenate([w_cmp[:, 0], zero], axis=2),
                          jnp.concatenate([zero, w_cmp[:, 1]], axis=2)], axis=1)
    pe = cmp_pe.reshape(BLK, KV_GROUP)
    return pe.astype(F32), bd.astype(BF16)


def _sample_tables(rel_table, dec_seq, past_len, n_rows, n_new, w_buf):
    col = jnp.arange(TILE, dtype=jnp.int32)
    head = (col // (GQA * dec_seq)) * GQA + (col // dec_seq) % GQA
    tok = (col % dec_seq)[None, :]
    table = rel_table.astype(F32)

    def bias(dist, visible):
        return jnp.where(visible, table[_rel_bucket(dist), head[None, :]], MASK_VALUE)

    row = jnp.arange(TILE, dtype=jnp.int32)[:, None]
    blk = jnp.arange(n_rows, dtype=jnp.int32)[:, None]
    c_dist = past_len + tok - ((blk + 1) * BLK - 1)
    ctab = bias(c_dist, c_dist >= 0)
    last_tab = bias(TILE + tok - row, jnp.ones((TILE, TILE), bool))
    new_row = jnp.arange(n_new, dtype=jnp.int32)[:, None]
    new_tab = bias(tok - new_row, (tok >= new_row) & (new_row < dec_seq))
    far = jnp.full((SUBLANES, TILE), w_buf, jnp.int32)
    far_row = bias(far, jnp.ones((SUBLANES, TILE), bool))
    win0_tab = bias(jnp.full((TILE, TILE), w_buf, jnp.int32), row > tok)
    same = (col[:, None] // (GQA * dec_seq) == col[None, :] // (GQA * dec_seq)) & \
           (col[:, None] % dec_seq == col[None, :] % dec_seq)
    return ctab, last_tab, new_tab, far_row, win0_tab, same.astype(F32)


def _layer_weights(ffn_pre, ffn_post, ffn_w_in, ffn_w_out):
    f = ffn_w_out.shape[0]
    return (ffn_pre[None, :], ffn_post[None, :], ffn_w_in[:, :f].astype(BF16), ffn_w_in[:, f:].astype(BF16),
            ffn_w_out.astype(BF16))


def kernel(x_prompt, x_sample, cache_cmp_kv, cache_slc_kv, cache_win_kv, state_ssm, state_conv, page_table,
           rel_table, ffn1_pre, ffn1_post, ffn1_w_in, ffn1_w_out, mix_pre, mix_post, w_in, w_out, att_out_norm,
           cmp_pe, w_cmp, conv_w, conv_b, dt_bias, a_log, d_skip, ssd_norm,
           ffn2_pre, ffn2_post, ffn2_w_in, ffn2_w_out):
    depth = ffn1_pre.shape[0]
    bp, seq, d = x_prompt.shape
    bs, dec_seq, _ = x_sample.shape
    n_pool = cache_cmp_kv.shape[1]
    n_pages = page_table.shape[1]
    past_len = n_pages * PAGE_SIZE
    w_buf = cache_win_kv.shape[2]
    kv_shape = (N_KV_HEADS, 2, HEAD_DIM)
    assert GQA * dec_seq * N_KV_HEADS == TILE and w_buf == WINDOW and seq % TILE == 0

    yp = x_prompt.reshape(bp * seq, d)
    ys = x_sample.reshape(bs * dec_seq, d)
    outs = [[] for _ in range(10)]
    for l in range(depth):
        ffn1 = _layer_weights(ffn1_pre[l], ffn1_post[l], ffn1_w_in[l], ffn1_w_out[l])
        ffn2 = _layer_weights(ffn2_pre[l], ffn2_post[l], ffn2_w_in[l], ffn2_w_out[l])
        w_proj, wdt_t = _pack_inproj(w_in[l])
        pe, bd = _pack_compress(cmp_pe[l], w_cmp[l])
        wo_att = w_out[l][:ATT_WIDTH].astype(BF16)
        wo_ssd = w_out[l][ATT_WIDTH:].astype(BF16)
        ssd_consts = (conv_w[l], conv_b[l][None, :], dt_bias[l][None, :], dt_bias[l][:, None], a_log[l][None, :],
                      a_log[l][:, None], jnp.repeat(d_skip[l], SSD_HEAD_DIM)[None, :], ssd_norm[l][None, :])
        mix_pre_l, mix_post_l, att_norm_l = mix_pre[l][None, :], mix_post[l][None, :], att_out_norm[l][None, :]

        yp = _ffn(yp, *ffn1)
        q, ckv, skv, wkv, skv_b, wkv_b, gates, z, xbc, dt, dt_t = _inproj(yp, mix_pre_l, w_proj, wdt_t)
        kvc = _compress(ckv.reshape(bp, seq, KV_WIDTH), pe, bd, TILE)
        tiles, cmp_tab = _prompt_bias_tables(rel_table)
        o_att = _nsa_prompt(q.reshape(bp, seq, ATT_WIDTH), gates.reshape(bp, seq, N_KV_HEADS * TILE), kvc,
                            skv_b.reshape(bp, seq, KV_WIDTH), wkv_b.reshape(bp, seq, KV_WIDTH), tiles, cmp_tab)
        cl = min(CHUNK, seq)
        o_ssd, h_new, conv_new = _ssd(
            z.reshape(bp, seq, SSD_INNER), xbc.reshape(bp, seq, CONV_DIM), dt.reshape(bp, seq, TILE),
            dt_t.reshape(TILE, bp, seq).transpose(1, 0, 2), jnp.zeros((bp, SUBLANES, CONV_DIM), F32),
            jnp.zeros((bp, SSD_HEADS, SSD_HEAD_DIM, SSD_STATE), F32), ssd_consts, cl)
        yp = _mixout(yp, o_att.reshape(bp * seq, ATT_WIDTH), o_ssd.reshape(bp * seq, SSD_INNER), att_norm_l,
                     mix_post_l, wo_att, wo_ssd)
        outs[0].append(ckv.reshape((bp, seq) + kv_shape))
        outs[1].append(skv.reshape((bp, seq) + kv_shape))
        outs[2].append(wkv.reshape((bp, seq) + kv_shape)[:, -min(WINDOW, seq):])
        outs[3].append(h_new)
        outs[4].append(conv_new)

        ys = _ffn(ys, *ffn1)
        q, ckv, skv, wkv, skv_b, wkv_b, gates, z, xbc, dt, dt_t = _inproj(ys, mix_pre_l, w_proj, wdt_t)
        kc_past = _compress_paged(cache_cmp_kv[l].reshape(n_pool, PAGE_SIZE, KV_WIDTH), page_table, pe, bd)
        cmp_new = jnp.pad(ckv.reshape(bs, dec_seq, KV_WIDTH), ((0, 0), (0, BLK - dec_seq), (0, 0)))
        kc_new = _compress(cmp_new.reshape(1, bs * BLK, KV_WIDTH), pe, bd, bs).reshape(bs, 1, KV_WIDTH)
        n_past = kc_past.shape[1]
        n_rows = -(-(n_past + 1) // 16) * 16
        kc = jnp.concatenate([kc_past, kc_new, jnp.zeros((bs, n_rows - n_past - 1, KV_WIDTH), BF16)], axis=1)
        n_new = 16
        ctab, last_tab, new_tab, far_row, win0_tab, gsum = _sample_tables(rel_table, dec_seq, past_len, n_rows,
                                                                          n_new, w_buf)
        q5 = q.reshape(bs, dec_seq, N_KV_HEADS, GQA, HEAD_DIM).transpose(0, 2, 4, 3, 1)
        q5 = q5.reshape(bs, N_KV_HEADS, HEAD_DIM, GQA * dec_seq)
        eye = jnp.eye(N_KV_HEADS, dtype=F32)[None, :, None, :, None]
        qbd = jnp.pad(q5, ((0, 0), (0, 0), (0, HEAD_DIM), (0, 0)))[:, :, :, None, :] * eye
        qbd = qbd.reshape(bs, KV_WIDTH, TILE).astype(BF16)
        g5 = gates.reshape(bs, dec_seq, N_KV_HEADS, TILE)[..., :GQA * N_BRANCH]
        g5 = g5.reshape(bs, dec_seq, N_KV_HEADS, GQA, N_BRANCH).transpose(0, 4, 2, 3, 1).reshape(bs, N_BRANCH, TILE)
        gates_t = jnp.pad(g5, ((0, 0), (0, SUBLANES - N_BRANCH), (0, 0)))

        def pad_new(a):
            return jnp.pad(a.reshape(bs, dec_seq, KV_WIDTH), ((0, 0), (0, n_new - dec_seq), (0, 0)))

        o_t = _nsa_sample(page_table, cache_slc_kv[l].reshape(n_pool, PAGE_SIZE, KV_WIDTH), qbd, gates_t, kc, ctab,
                          gsum, pad_new(skv_b), cache_win_kv[l].reshape(bs, w_buf, KV_WIDTH), pad_new(wkv_b),
                          last_tab, new_tab, far_row, win0_tab, dec_seq=dec_seq, past_len=past_len)
        o7 = o_t.reshape(bs, N_KV_HEADS, 2, HEAD_DIM, N_KV_HEADS, GQA, dec_seq)
        o_att = jnp.stack([o7[:, k, 1, :, k] for k in range(N_KV_HEADS)], axis=1).transpose(0, 4, 1, 3, 2)
        o_att = o_att.reshape(bs * dec_seq, ATT_WIDTH)
        conv_prev = jnp.pad(state_conv[l], ((0, 0), (SUBLANES - (CONV_W - 1), 0), (0, 0)))
        o_ssd, h_new, conv_new = _ssd(
            z.reshape(bs, dec_seq, SSD_INNER), xbc.reshape(bs, dec_seq, CONV_DIM), dt.reshape(bs, dec_seq, TILE),
            dt_t.reshape(TILE, bs, dec_seq).transpose(1, 0, 2), conv_prev, state_ssm[l], ssd_consts,
            min(CHUNK, dec_seq))
        ys = _mixout(ys, o_att, o_ssd.reshape(bs * dec_seq, SSD_INNER), att_norm_l, mix_post_l, wo_att, wo_ssd)
        win_all = jnp.concatenate([cache_win_kv[l], wkv.reshape((bs, dec_seq) + kv_shape)], axis=1)
        outs[5].append(ckv.reshape((bs, dec_seq) + kv_shape))
        outs[6].append(skv.reshape((bs, dec_seq) + kv_shape))
        outs[7].append(win_all[:, -min(WINDOW, w_buf + dec_seq):])
        outs[8].append(h_new)
        outs[9].append(conv_new)

        yp = _ffn(yp, *ffn2)
        ys = _ffn(ys, *ffn2)

    return (yp.reshape(bp, seq, d), ys.reshape(bs, dec_seq, d)) + tuple(jnp.stack(o) for o in outs)
```

```python
import functools
import math

import jax
import jax.numpy as jnp
from jax import lax
from jax.experimental import pallas as pl
from jax.experimental.pallas import tpu as pltpu

F32 = jnp.float32
BF16 = jnp.bfloat16

D_MODEL = 1024
N_ATT_HEADS = 16
HEAD_DIM = 64
N_KV_HEADS = 4
GQA = N_ATT_HEADS // N_KV_HEADS
ATT_WIDTH = N_ATT_HEADS * HEAD_DIM
KV_WIDTH = 2 * N_KV_HEADS * HEAD_DIM
KV_GROUP = 2 * HEAD_DIM
BLK = 64
N_SEL = 16
WINDOW = 512
N_BRANCH = 3
N_BUCKETS = 32
MAX_DISTANCE = 128
SSD_HEADS = 16
SSD_HEAD_DIM = 64
SSD_INNER = SSD_HEADS * SSD_HEAD_DIM
SSD_GROUPS = 2
SSD_STATE = 128
CONV_W = 4
CONV_DIM = SSD_INNER + 2 * SSD_GROUPS * SSD_STATE
CHUNK = 128
PAGE_SIZE = 128
EPS = 1e-6

TILE = 128
SUBLANES = 8
MASK_VALUE = -(2.0 ** 100)
M_INIT = -1e30
VMEM_LIMIT = 56 * 1024 * 1024
PAGES_PER_STEP = 8

NT_DIMS = (((1,), (1,)), ((), ()))
TN_DIMS = (((0,), (0,)), ((), ()))


def _params(n_grid_dims):
    return pltpu.CompilerParams(dimension_semantics=("arbitrary",) * n_grid_dims,
                                vmem_limit_bytes=VMEM_LIMIT)


def _rms(x, g):
    return x * lax.rsqrt(jnp.mean(x * x, axis=-1, keepdims=True) + EPS) * g


def _const_spec(shape):
    zeros = (0,) * len(shape)
    return pl.BlockSpec(shape, lambda *_: zeros, pipeline_mode=pl.Buffered(1))


def _ffn_body(x_ref, pre_ref, post_ref, wg_ref, wu_ref, wo_ref, o_ref):
    x = x_ref[...]
    h = _rms(x, pre_ref[...]).astype(BF16)
    gate = jnp.dot(h, wg_ref[...], preferred_element_type=F32)
    up = jnp.dot(h, wu_ref[...], preferred_element_type=F32)
    a = (gate * jax.nn.sigmoid(gate) * up).astype(BF16)
    y = jnp.dot(a, wo_ref[...], preferred_element_type=F32)
    o_ref[...] = x + 0.5 * _rms(y, post_ref[...])


def _ffn(x, pre, post, wg, wu, wo):
    t, d = x.shape
    f = wg.shape[1]
    tm = min(256, t)
    row = pl.BlockSpec((tm, d), lambda i: (i, 0))
    return pl.pallas_call(
        _ffn_body, grid=(t // tm,),
        in_specs=[row, _const_spec((1, d)), _const_spec((1, d)), _const_spec((d, f)), _const_spec((d, f)),
                  _const_spec((f, d))],
        out_specs=row, out_shape=jax.ShapeDtypeStruct((t, d), F32),
        compiler_params=_params(1), name="ffn")(x, pre, post, wg, wu, wo)


_SEG = {}
_off = 0
for _name, _w in (("q", ATT_WIDTH), ("ckv", KV_WIDTH), ("skv", KV_WIDTH), ("wkv", KV_WIDTH),
                  ("gates", N_KV_HEADS * TILE), ("z", SSD_INNER), ("xbc", CONV_DIM), ("dt", TILE)):
    _SEG[_name] = (_off, _off + _w)
    _off += _w
PROJ_WIDTH = _off


def _inproj_body(x_ref, pre_ref, w_ref, wdt_t_ref, q_ref, ckv_ref, skv_ref, wkv_ref, skvb_ref, wkvb_ref,
                 gates_ref, z_ref, xbc_ref, dt_ref, dtt_ref):
    h = _rms(x_ref[...], pre_ref[...]).astype(BF16)

    def proj(name):
        lo, hi = _SEG[name]
        return jnp.dot(h, w_ref[:, lo:hi], preferred_element_type=F32)

    q_ref[...] = proj("q")
    ckv_ref[...] = proj("ckv")
    skv = proj("skv")
    skv_ref[...] = skv
    skvb_ref[...] = skv.astype(BF16)
    wkv = proj("wkv")
    wkv_ref[...] = wkv
    wkvb_ref[...] = wkv.astype(BF16)
    gates_ref[...] = proj("gates")
    z_ref[...] = proj("z")
    xbc_ref[...] = proj("xbc")
    dt_ref[...] = proj("dt")
    dtt_ref[...] = lax.dot_general(wdt_t_ref[...], h, NT_DIMS, preferred_element_type=F32)


def _inproj(x, pre, w, wdt_t):
    t, d = x.shape
    tm = min(256, t)

    def rows(width):
        return pl.BlockSpec((tm, width), lambda i: (i, 0))

    widths = (ATT_WIDTH, KV_WIDTH, KV_WIDTH, KV_WIDTH, KV_WIDTH, KV_WIDTH, N_KV_HEADS * TILE, SSD_INNER,
              CONV_DIM, TILE)
    dtypes = (F32, F32, F32, F32, BF16, BF16, F32, F32, F32, F32)
    out_shape = [jax.ShapeDtypeStruct((t, wd), dt) for wd, dt in zip(widths, dtypes)]
    out_shape.append(jax.ShapeDtypeStruct((TILE, t), F32))
    out_specs = [rows(wd) for wd in widths] + [pl.BlockSpec((TILE, tm), lambda i: (0, i))]
    return pl.pallas_call(
        _inproj_body, grid=(t // tm,),
        in_specs=[rows(d), _const_spec((1, d)), _const_spec((d, PROJ_WIDTH)), _const_spec((TILE, d))],
        out_specs=out_specs, out_shape=out_shape,
        compiler_params=_params(1), name="inproj")(x, pre, w, wdt_t)


def _mixout_body(x_ref, oa_ref, os_ref, an_ref, post_ref, wa_ref, ws_ref, o_ref):
    a = _rms(oa_ref[...], an_ref[...]).astype(BF16)
    y = jnp.dot(a, wa_ref[...], preferred_element_type=F32)
    y = y + jnp.dot(os_ref[...].astype(BF16), ws_ref[...], preferred_element_type=F32)
    o_ref[...] = x_ref[...] + _rms(y, post_ref[...])


def _mixout(x, o_att, o_ssd, att_norm, post, wa, ws):
    t, d = x.shape
    tm = min(256, t)
    row = pl.BlockSpec((tm, d), lambda i: (i, 0))
    return pl.pallas_call(
        _mixout_body, grid=(t // tm,),
        in_specs=[row, row, row, _const_spec((1, d)), _const_spec((1, d)), _const_spec((d, d)),
                  _const_spec((d, d))],
        out_specs=row, out_shape=jax.ShapeDtypeStruct((t, d), F32),
        compiler_params=_params(1), name="mixout")(x, o_att, o_ssd, att_norm, post, wa, ws)


def _compress_rows(x_refs, pe_ref, bd_ref, n_blocks):
    acc = jnp.zeros((N_KV_HEADS * n_blocks, KV_GROUP), F32)
    for l in range(BLK):
        pe = pe_ref[l:l + 1, :]
        xs = jnp.concatenate([x_ref[pl.ds(l, n_blocks, stride=BLK), :] + pe for x_ref in x_refs], axis=0)
        acc = acc + jnp.dot(xs.astype(BF16), bd_ref[l], preferred_element_type=F32)
    return acc


def _store_summaries(acc, o_ref, n_blocks):
    for k in range(N_KV_HEADS):
        o_ref[0:n_blocks, k * KV_GROUP:(k + 1) * KV_GROUP] = acc[k * n_blocks:(k + 1) * n_blocks].astype(BF16)


def _compress_body(*refs, n_blocks, n_out):
    x_refs = refs[:N_KV_HEADS]
    pe_ref, bd_ref, o_ref = refs[N_KV_HEADS:]
    _store_summaries(_compress_rows(x_refs, pe_ref, bd_ref, n_blocks), o_ref, n_blocks)
    if n_out > n_blocks:
        o_ref[n_blocks:n_out, :] = jnp.zeros((n_out - n_blocks, KV_WIDTH), BF16)


def _compress(kv, pe, bd, n_out):
    b, s, _ = kv.shape
    n_blocks = s // BLK

    def head_spec(k):
        return pl.BlockSpec((None, s, KV_GROUP), lambda i: (i, 0, k))

    return pl.pallas_call(
        functools.partial(_compress_body, n_blocks=n_blocks, n_out=n_out), grid=(b,),
        in_specs=[head_spec(k) for k in range(N_KV_HEADS)]
        + [_const_spec((BLK, KV_GROUP)), _const_spec((BLK, KV_GROUP, KV_GROUP))],
        out_specs=pl.BlockSpec((None, n_out, KV_WIDTH), lambda i: (i, 0, 0)),
        out_shape=jax.ShapeDtypeStruct((b, n_out, KV_WIDTH), BF16),
        compiler_params=_params(1), name="compress")(*([kv] * N_KV_HEADS), pe, bd)


def _compress_paged_body(pt_ref, *refs, n_pages):
    page_refs = refs[:PAGES_PER_STEP]
    pe_ref, bd_ref, o_ref, stage_ref = refs[PAGES_PER_STEP:]
    j = pl.program_id(1)
    for p in range(PAGES_PER_STEP):
        row = pl.multiple_of((j * PAGES_PER_STEP + p) * PAGE_SIZE, PAGE_SIZE)
        for k in range(N_KV_HEADS):
            stage_ref[k, pl.ds(row, PAGE_SIZE), :] = page_refs[p][:, k * KV_GROUP:(k + 1) * KV_GROUP]

    @pl.when(j == pl.num_programs(1) - 1)
    def _():
        n_blocks = n_pages * (PAGE_SIZE // BLK)
        x_refs = [stage_ref.at[k] for k in range(N_KV_HEADS)]
        _store_summaries(_compress_rows(x_refs, pe_ref, bd_ref, n_blocks), o_ref, n_blocks)


def _compress_paged(cache, page_table, pe, bd):
    b, n_pages = page_table.shape
    steps = n_pages // PAGES_PER_STEP
    n_blocks = n_pages * (PAGE_SIZE // BLK)

    def page_spec(p):
        return pl.BlockSpec((None, PAGE_SIZE, KV_WIDTH), lambda i, j, pt: (pt[i, j * PAGES_PER_STEP + p], 0, 0))

    grid_spec = pltpu.PrefetchScalarGridSpec(
        num_scalar_prefetch=1, grid=(b, steps),
        in_specs=[page_spec(p) for p in range(PAGES_PER_STEP)]
        + [pl.BlockSpec((BLK, KV_GROUP), lambda i, j, pt: (0, 0)),
           pl.BlockSpec((BLK, KV_GROUP, KV_GROUP), lambda i, j, pt: (0, 0, 0))],
        out_specs=pl.BlockSpec((None, n_blocks, KV_WIDTH), lambda i, j, pt: (i, 0, 0)),
        scratch_shapes=[pltpu.VMEM((N_KV_HEADS, n_pages * PAGE_SIZE, KV_GROUP), F32)])
    return pl.pallas_call(
        functools.partial(_compress_paged_body, n_pages=n_pages), grid_spec=grid_spec,
        out_shape=jax.ShapeDtypeStruct((b, n_blocks, KV_WIDTH), BF16),
        compiler_params=_params(2), name="compress_paged")(page_table, *([cache] * PAGES_PER_STEP), pe, bd)


def _rel_bucket(dist):
    n = jnp.maximum(dist, 0)
    max_exact = N_BUCKETS // 2
    nf = jnp.maximum(n, 1).astype(F32)
    log_b = max_exact + (jnp.log(nf / max_exact) / math.log(MAX_DISTANCE / max_exact)
                         * (N_BUCKETS - max_exact)).astype(jnp.int32)
    return jnp.where(n < max_exact, n, jnp.minimum(log_b, N_BUCKETS - 1))


def _lookup(table, bucket):
    onehot = (bucket[..., None] == jnp.arange(table.shape[0], dtype=jnp.int32)).astype(F32)
    return jnp.einsum('...b,b...->...', onehot, table, precision=lax.Precision.HIGHEST)


def _bias_of(rel_table, dist, visible):
    onehot = (_rel_bucket(dist)[..., None] == jnp.arange(N_BUCKETS, dtype=jnp.int32)).astype(F32)
    bias = jnp.einsum('rcb,bh->hrc', onehot, rel_table.astype(F32), precision=lax.Precision.HIGHEST)
    return jnp.where(visible[None], bias, MASK_VALUE).reshape(-1, dist.shape[-1])


KIND_ZERO, KIND_DIAG, KIND_PREV, KIND_MASKED, KIND_OLDEST = range(5)


def _prompt_bias_tables(rel_table):
    r = jnp.arange(TILE, dtype=jnp.int32)[:, None]
    c = jnp.arange(TILE, dtype=jnp.int32)[None, :]
    true = jnp.ones((TILE, TILE), bool)
    far = _bias_of(rel_table, jnp.full((TILE, TILE), MAX_DISTANCE, jnp.int32), true)
    tiles = jnp.stack([
        jnp.zeros_like(far),
        _bias_of(rel_table, r - c, r >= c) - far,
        _bias_of(rel_table, TILE + r - c, true) - far,
        jnp.full_like(far, MASK_VALUE),
        jnp.where(jnp.tile(r < c, (N_ATT_HEADS, 1)), 0.0, MASK_VALUE),
    ])
    c_dist = r - (BLK - 1) + BLK * (BLK - 1 - c)
    cmp_tab = _bias_of(rel_table, c_dist, c_dist >= 0)
    return tiles, cmp_tab


def _selection_mask(score, blk, n_rows):
    n_groups = score.shape[0] // SUBLANES
    groups = [score[SUBLANES * j:SUBLANES * (j + 1)] for j in range(n_groups)]
    ranks = [jnp.zeros(groups[0].shape, jnp.int32) for _ in range(n_groups)]
    row8 = blk[0:SUBLANES]
    for n in range(n_rows):
        other = jnp.broadcast_to(score[n:n + 1, :], groups[0].shape)
        for j in range(n_groups):
            lo = SUBLANES * j
            if lo > n:
                inc = jnp.where(other >= groups[j], 1, 0)
            elif lo + SUBLANES - 1 <= n:
                inc = jnp.where(other > groups[j], 1, 0)
            else:
                inc = jnp.where(row8 + lo > n, jnp.where(other >= groups[j], 1, 0),
                                jnp.where(other > groups[j], 1, 0))
            ranks[j] = ranks[j] + inc
    rank = jnp.concatenate(ranks, axis=0)
    return jnp.where(rank < N_SEL, jnp.where(score > -jnp.inf, 0.0, MASK_VALUE), MASK_VALUE)


PAIR = 2 * TILE


def _nsa_prompt_body(q_ref, gates_ref, kvc_ref, skv_ref, wkv_ref, tb_ref, ctab_ref, o_ref,
                     s_ref, sw_ref, mx_ref, mb_ref, l_ref, acc_ref, *, n_blocks):
    i = pl.program_id(2)
    a = i // 2
    odd = i % 2
    rows = GQA * TILE
    lane = lax.broadcasted_iota(jnp.int32, (rows, TILE), 1)
    low = lane < HEAD_DIM
    key_lane = lax.broadcasted_iota(jnp.int32, (PAIR, TILE), 1)
    key_blk = lax.broadcasted_iota(jnp.int32, (PAIR, TILE), 0) // BLK

    qf = q_ref[...]
    q_pad = []
    for g in range(GQA):
        pair = qf[:, (g // 2) * TILE:(g // 2 + 1) * TILE]
        q_pad.append(pair if g % 2 == 0 else pltpu.roll(pair, HEAD_DIM, 1))
    q_all = jnp.concatenate(q_pad, axis=0)
    lhs_plain = jnp.where(low, q_all, 0.0).astype(BF16)

    kvc = kvc_ref[...]
    shift = (2 * i + (TILE - (BLK - 1))) % TILE
    bias = jnp.concatenate([pltpu.roll(ctab_ref[g * TILE:(g + 1) * TILE, :], shift, 1) for g in range(GQA)], axis=0)
    s = lax.dot_general(lhs_plain, kvc, NT_DIMS, preferred_element_type=F32)
    s = jnp.where(lane < n_blocks, s + bias, MASK_VALUE)
    m = jnp.maximum(jnp.max(s, axis=1, keepdims=True), M_INIT)
    p = jnp.exp(s - m)
    p = p / jnp.maximum(jnp.sum(p, axis=1, keepdims=True), 1e-30)
    o_cmp = jnp.dot(p.astype(BF16), kvc, preferred_element_type=F32)
    imp = p[0:TILE]
    for g in range(1, GQA):
        imp = imp + p[g * TILE:(g + 1) * TILE]

    blk = lax.broadcasted_iota(jnp.int32, (n_blocks, TILE), 0)
    query = lax.broadcasted_iota(jnp.int32, (n_blocks, TILE), 1)
    cur = 2 * i + jnp.where(query >= BLK, 1, 0)
    forced = (blk == 0) | (blk == cur) | (blk == cur - 1)
    score = jnp.where(forced, jnp.inf, jnp.where(blk <= cur, imp.T[0:n_blocks], -jnp.inf))
    neg_t = _selection_mask(score, blk, n_blocks)
    pieces = [jnp.zeros((HEAD_DIM, TILE), F32), neg_t]
    if n_blocks < TILE - HEAD_DIM:
        pieces.append(jnp.zeros((TILE - HEAD_DIM - n_blocks, TILE), F32))
    neg = jnp.concatenate(pieces, axis=0).T
    lhs_sel = jnp.where(low, q_all, jnp.concatenate([neg] * GQA, axis=0)).astype(BF16)

    def sel_kinds(j):
        last = j == a
        left = jnp.where(last, jnp.where(odd == 1, KIND_PREV, KIND_DIAG), KIND_ZERO)
        right = jnp.where(last, jnp.where(odd == 1, KIND_DIAG, KIND_MASKED),
                          jnp.where((j == a - 1) & (odd == 0), KIND_PREV, KIND_ZERO))
        return left, right

    def win_kinds(j):
        left, right = sel_kinds(j)
        first = j == a - WINDOW // PAIR
        return (jnp.where(first, jnp.where(odd == 1, KIND_MASKED, KIND_OLDEST), left),
                jnp.where(first, jnp.where(odd == 1, KIND_OLDEST, KIND_ZERO), right))

    def sweep(lhs, kv_ref, s_buf, j_lo, kinds, block_columns):
        def keys(j):
            return kv_ref[pl.ds(pl.multiple_of(j * PAIR, PAIR), PAIR), :]

        mx_ref[...] = jnp.full(mx_ref.shape, M_INIT, F32)

        def score_pair(j, carry):
            rhs = keys(j)
            if block_columns:
                onehot = jnp.where(key_lane - HEAD_DIM == (PAIR // BLK) * j + key_blk, 1.0, 0.0).astype(BF16)
                rhs = jnp.where(key_lane < HEAD_DIM, rhs, onehot)
            sc = lax.dot_general(lhs, rhs, NT_DIMS, preferred_element_type=F32)
            left, right = kinds(j)
            sc = sc + jnp.concatenate([tb_ref[left], tb_ref[right]], axis=1)
            s_buf[j - j_lo] = sc
            mx_ref[...] = jnp.maximum(mx_ref[...], jnp.maximum(sc[:, :TILE], sc[:, TILE:]))
            return carry

        lax.fori_loop(j_lo, a + 1, score_pair, 0)
        mb_ref[...] = jnp.broadcast_to(jnp.max(mx_ref[...], axis=1, keepdims=True), mb_ref.shape)
        l_ref[...] = jnp.zeros(l_ref.shape, F32)
        acc_ref[...] = jnp.zeros(acc_ref.shape, F32)

        def weigh_pair(j, carry):
            mb = mb_ref[...]
            pr = jnp.exp(s_buf[j - j_lo] - jnp.concatenate([mb, mb], axis=1))
            l_ref[...] += pr[:, :TILE] + pr[:, TILE:]
            acc_ref[...] += jnp.dot(pr.astype(BF16), keys(j), preferred_element_type=F32)
            return carry

        lax.fori_loop(j_lo, a + 1, weigh_pair, 0)
        return acc_ref[...] / jnp.maximum(jnp.sum(l_ref[...], axis=1, keepdims=True), 1e-30)

    o_sel = sweep(lhs_sel, skv_ref, s_ref, 0, sel_kinds, True)
    o_win = sweep(lhs_plain, wkv_ref, sw_ref, jnp.maximum(a - WINDOW // PAIR, 0), win_kinds, False)

    gate = jax.nn.sigmoid(gates_ref[...])
    outs = []
    for g in range(GQA):
        def gcol(br):
            c = g * N_BRANCH + br
            return gate[:, c:c + 1]
        head = slice(g * TILE, (g + 1) * TILE)
        outs.append(gcol(0) * o_cmp[head] + gcol(1) * o_sel[head] + gcol(2) * o_win[head])
    out_low = lax.broadcasted_iota(jnp.int32, (TILE, TILE), 1) < HEAD_DIM
    for j in range(GQA // 2):
        o_ref[:, j * TILE:(j + 1) * TILE] = jnp.where(out_low, pltpu.roll(outs[2 * j], HEAD_DIM, 1), outs[2 * j + 1])


def _nsa_prompt(q, gates, kvc, skv_b, wkv_b, tiles, cmp_tab):
    b, s, _ = q.shape
    nq = s // TILE
    n_blocks = s // BLK
    assert n_blocks <= TILE - HEAD_DIM and kvc.shape[1] == TILE and s % PAIR == 0
    gw = GQA * HEAD_DIM
    rows = GQA * TILE
    return pl.pallas_call(
        functools.partial(_nsa_prompt_body, n_blocks=n_blocks), grid=(b, N_KV_HEADS, nq),
        in_specs=[pl.BlockSpec((None, TILE, gw), lambda bi, k, i: (bi, i, k)),
                  pl.BlockSpec((None, TILE, TILE), lambda bi, k, i: (bi, i, k)),
                  pl.BlockSpec((None, TILE, KV_GROUP), lambda bi, k, i: (bi, 0, k)),
                  pl.BlockSpec((None, s, KV_GROUP), lambda bi, k, i: (bi, 0, k)),
                  pl.BlockSpec((None, s, KV_GROUP), lambda bi, k, i: (bi, 0, k)),
                  pl.BlockSpec((len(tiles), rows, TILE), lambda bi, k, i: (0, k, 0)),
                  pl.BlockSpec((rows, TILE), lambda bi, k, i: (k, 0))],
        out_specs=pl.BlockSpec((None, TILE, gw), lambda bi, k, i: (bi, i, k)),
        out_shape=jax.ShapeDtypeStruct((b, s, ATT_WIDTH), F32),
        scratch_shapes=[pltpu.VMEM((s // PAIR, rows, PAIR), F32),
                        pltpu.VMEM((WINDOW // PAIR + 1, rows, PAIR), F32),
                        pltpu.VMEM((rows, TILE), F32), pltpu.VMEM((rows, TILE), F32),
                        pltpu.VMEM((rows, TILE), F32), pltpu.VMEM((rows, KV_GROUP), F32)],
        compiler_params=_params(3), name="nsa_prompt")(q, gates, kvc, skv_b, wkv_b, tiles, cmp_tab)


def _t_softmax_step(kv, qbd, bias, m_ref, l_ref, acc_ref):
    s = jnp.dot(kv, qbd, preferred_element_type=F32) + bias
    m_prev = m_ref[...]
    m_new = jnp.maximum(m_prev, jnp.max(s, axis=0, keepdims=True))
    alpha = jnp.exp(m_prev - m_new)
    p = jnp.exp(s - m_new)
    l_ref[...] = alpha * l_ref[...] + jnp.sum(p, axis=0, keepdims=True)
    acc_ref[...] = alpha * acc_ref[...] + lax.dot_general(kv, p.astype(BF16), TN_DIMS,
                                                          preferred_element_type=F32)
    m_ref[...] = m_new


def _nsa_sample_body(pt_ref, *refs, n_pages, dec_seq, past_len):
    page_refs = refs[:PAGES_PER_STEP]
    (qbd_ref, gates_ref, kc_ref, ctab_ref, gsum_ref, snew_ref, cwin_ref, wnew_ref, last_ref, new_ref, far_ref,
     win0_ref, o_ref, neg_ref, m_ref, l_ref, acc_ref, oc_ref, ow_ref) = refs[PAGES_PER_STEP:]
    j = pl.program_id(1)
    n_past = n_pages * (PAGE_SIZE // BLK)
    n_rows = kc_ref.shape[0]
    sub = lax.broadcasted_iota(jnp.int32, (TILE, TILE), 0)
    qbd = qbd_ref[...]
    far = far_ref[0:1, :]

    def reset():
        m_ref[...] = jnp.full(m_ref.shape, M_INIT, F32)
        l_ref[...] = jnp.zeros(l_ref.shape, F32)
        acc_ref[...] = jnp.zeros(acc_ref.shape, F32)

    def result():
        return acc_ref[...] / jnp.maximum(l_ref[...], 1e-30)

    @pl.when(j == 0)
    def _():
        kc = kc_ref[...]
        s = jnp.dot(kc, qbd, preferred_element_type=F32) + ctab_ref[...]
        m = jnp.maximum(jnp.max(s, axis=0, keepdims=True), M_INIT)
        p = jnp.exp(s - m)
        p = p / jnp.maximum(jnp.sum(p, axis=0, keepdims=True), 1e-30)
        oc_ref[...] = lax.dot_general(kc, p.astype(BF16), TN_DIMS, preferred_element_type=F32)
        imp = jnp.dot(p, gsum_ref[...], preferred_element_type=F32, precision=lax.Precision.HIGHEST)
        blk = lax.broadcasted_iota(jnp.int32, (n_rows, TILE), 0)
        tok = lax.broadcasted_iota(jnp.int32, (n_rows, TILE), 1) % dec_seq
        cur = (past_len + tok) // BLK
        forced = (blk == 0) | (blk == cur) | (blk == cur - 1)
        score = jnp.where(forced, jnp.inf, jnp.where(blk <= cur, imp, -jnp.inf))
        neg_ref[...] = _selection_mask(score, blk, n_past + 1)

        reset()
        n_win = cwin_ref.shape[0] // TILE
        for t in range(n_win):
            kv = cwin_ref[t * TILE:(t + 1) * TILE, :].astype(BF16)
            bias = win0_ref[...] if t == 0 else (last_ref[...] if t == n_win - 1 else far)
            _t_softmax_step(kv, qbd, bias, m_ref, l_ref, acc_ref)
        _t_softmax_step(wnew_ref[...], qbd, new_ref[...], m_ref, l_ref, acc_ref)
        ow_ref[...] = result()
        reset()

    blocks_per_step = PAGES_PER_STEP * (PAGE_SIZE // BLK)
    neg_rows = neg_ref[pl.ds(pl.multiple_of(j * blocks_per_step, blocks_per_step), blocks_per_step), :]
    for p in range(PAGES_PER_STEP):
        page = j * PAGES_PER_STEP + p
        kv = page_refs[p][...].astype(BF16)
        first = neg_rows[2 * p:2 * p + 1, :]
        second = neg_rows[2 * p + 1:2 * p + 2, :]
        bias = jnp.where(sub < BLK, first, second) + jnp.where(page == n_pages - 1, last_ref[...], far)
        _t_softmax_step(kv, qbd, bias, m_ref, l_ref, acc_ref)

    @pl.when(j == pl.num_programs(1) - 1)
    def _():
        _t_softmax_step(snew_ref[...], qbd, new_ref[...] + neg_ref[n_past:n_past + 1, :], m_ref, l_ref, acc_ref)
        gate = jax.nn.sigmoid(gates_ref[...])
        o_ref[...] = gate[0:1, :] * oc_ref[...] + gate[1:2, :] * result() + gate[2:3, :] * ow_ref[...]


def _nsa_sample(page_table, cache_slc, qbd, gates_t, kc, ctab, gsum, slc_new, cache_win, win_new, last_tab,
                new_tab, far_row, win0_tab, *, dec_seq, past_len):
    b, n_pages = page_table.shape
    steps = n_pages // PAGES_PER_STEP
    n_rows = kc.shape[1]
    w_buf = cache_win.shape[1]
    n_new = slc_new.shape[1]

    def page_spec(p):
        return pl.BlockSpec((None, PAGE_SIZE, KV_WIDTH), lambda i, j, pt: (pt[i, j * PAGES_PER_STEP + p], 0, 0))

    def per_batch(*shape):
        zeros = (0,) * len(shape)
        return pl.BlockSpec((None,) + shape, lambda i, j, pt: (i,) + zeros)

    def const(*shape):
        zeros = (0,) * len(shape)
        return pl.BlockSpec(shape, lambda i, j, pt: zeros)

    grid_spec = pltpu.PrefetchScalarGridSpec(
        num_scalar_prefetch=1, grid=(b, steps),
        in_specs=[page_spec(p) for p in range(PAGES_PER_STEP)]
        + [per_batch(KV_WIDTH, TILE), per_batch(SUBLANES, TILE), per_batch(n_rows, KV_WIDTH), const(n_rows, TILE),
           const(TILE, TILE), per_batch(n_new, KV_WIDTH), per_batch(w_buf, KV_WIDTH), per_batch(n_new, KV_WIDTH),
           const(TILE, TILE), const(n_new, TILE), const(SUBLANES, TILE), const(TILE, TILE)],
        out_specs=per_batch(KV_WIDTH, TILE),
        scratch_shapes=[pltpu.VMEM((n_rows, TILE), F32), pltpu.VMEM((1, TILE), F32), pltpu.VMEM((1, TILE), F32),
                        pltpu.VMEM((KV_WIDTH, TILE), F32), pltpu.VMEM((KV_WIDTH, TILE), F32),
                        pltpu.VMEM((KV_WIDTH, TILE), F32)])
    return pl.pallas_call(
        functools.partial(_nsa_sample_body, n_pages=n_pages, dec_seq=dec_seq, past_len=past_len),
        grid_spec=grid_spec, out_shape=jax.ShapeDtypeStruct((b, KV_WIDTH, TILE), F32),
        compiler_params=_params(2), name="nsa_sample")(
            page_table, *([cache_slc] * PAGES_PER_STEP), qbd, gates_t, kc, ctab, gsum, slc_new, cache_win, win_new,
            last_tab, new_tab, far_row, win0_tab)


def _softplus(x):
    return jnp.maximum(x, 0.0) + jnp.log(1.0 + jnp.exp(-jnp.abs(x)))


def _ssd_body(z_ref, xbc_ref, dt_ref, dtt_ref, convp_ref, h0_ref, cw_ref, cb_ref, dtb_ref, dtbt_ref, al_ref,
              alt_ref, dskip_ref, norm_ref, o_ref, hout_ref, convout_ref, xp_ref, *, cl):
    c = pl.program_id(1)
    pad = SUBLANES

    @pl.when(c == 0)
    def _():
        xp_ref[0:pad, :] = convp_ref[...]
        hout_ref[...] = h0_ref[...]

    xp_ref[pad:pad + cl, :] = xbc_ref[...]
    conv = cb_ref[...]
    for k in range(CONV_W):
        lo = pad - (CONV_W - 1) + k
        conv = conv + cw_ref[k:k + 1, :] * xp_ref[lo:lo + cl, :]
    u = conv * jax.nn.sigmoid(conv)
    tail = xp_ref[pad + cl - (CONV_W - 1):pad + cl, :]
    convout_ref[...] = tail
    xp_ref[pad - (CONV_W - 1):pad, :] = tail

    xs = u[:, :SSD_INNER]
    dt = _softplus(dt_ref[:, 0:SSD_HEADS] + dtb_ref[...])
    dt_t = _softplus(dtt_ref[0:SSD_HEADS, :] + dtbt_ref[...])
    la = dt * -jnp.exp(al_ref[...])
    la_t = dt_t * -jnp.exp(alt_ref[...])
    row = lax.broadcasted_iota(jnp.int32, (cl, cl), 0)
    col = lax.broadcasted_iota(jnp.int32, (cl, cl), 1)
    causal = row >= col
    hi = lax.Precision.HIGHEST
    cs = jnp.dot(jnp.where(causal, 1.0, 0.0), la, preferred_element_type=F32, precision=hi)
    cs_t = jnp.dot(la_t, jnp.where(row <= col, 1.0, 0.0), preferred_element_type=F32, precision=hi)

    heads_per_group = SSD_HEADS // SSD_GROUPS
    ys = []
    for g in range(SSD_GROUPS):
        b_lo = SSD_INNER + g * SSD_STATE
        c_lo = SSD_INNER + SSD_GROUPS * SSD_STATE + g * SSD_STATE
        bm = u[:, b_lo:b_lo + SSD_STATE].astype(BF16)
        cm = u[:, c_lo:c_lo + SSD_STATE].astype(BF16)
        cb = lax.dot_general(cm, bm, NT_DIMS, preferred_element_type=F32)
        for r in range(heads_per_group):
            h = g * heads_per_group + r
            cs_col = cs[:, h:h + 1]
            diff = cs_col - cs_t[h:h + 1, :]
            decay = jnp.where(causal, jnp.exp(jnp.where(causal, diff, 0.0)), 0.0)
            xs_h = xs[:, h * SSD_HEAD_DIM:(h + 1) * SSD_HEAD_DIM]
            xdt = xs_h * dt[:, h:h + 1]
            y = jnp.dot((cb * decay).astype(BF16), xdt.astype(BF16), preferred_element_type=F32)
            last = cs[cl - 1:cl, h:h + 1]
            w = (xdt * jnp.exp(last - cs_col)).astype(BF16)
            state_in = lax.dot_general(w, bm, TN_DIMS, preferred_element_type=F32)
            h_prev = hout_ref[h]
            y = y + lax.dot_general(cm, h_prev.astype(BF16), NT_DIMS, preferred_element_type=F32) * jnp.exp(cs_col)
            hout_ref[h] = h_prev * jnp.exp(last) + state_in
            ys.append(y + dskip_ref[:, h * SSD_HEAD_DIM:(h + 1) * SSD_HEAD_DIM] * xs_h)
    y = jnp.concatenate(ys, axis=1)
    z = z_ref[...]
    y = y * (z * jax.nn.sigmoid(z))
    gw = SSD_INNER // SSD_GROUPS
    normed = []
    for g in range(SSD_GROUPS):
        yg = y[:, g * gw:(g + 1) * gw]
        normed.append(yg * lax.rsqrt(jnp.mean(yg * yg, axis=-1, keepdims=True) + EPS))
    o_ref[...] = jnp.concatenate(normed, axis=1) * norm_ref[...]


def _ssd(z, xbc, dt, dt_t, conv_prev, h0, consts, cl):
    b, seq, _ = z.shape
    nc = seq // cl
    cw, cbias, dtb, dtb_t, al, al_t, dskip, norm = consts

    def rows(width):
        return pl.BlockSpec((None, cl, width), lambda i, c: (i, c, 0))

    state = pl.BlockSpec((None, SSD_HEADS, SSD_HEAD_DIM, SSD_STATE), lambda i, c: (i, 0, 0, 0))

    def const(x):
        zeros = (0,) * x.ndim
        return pl.BlockSpec(x.shape, lambda i, c: zeros)

    return pl.pallas_call(
        functools.partial(_ssd_body, cl=cl), grid=(b, nc),
        in_specs=[rows(SSD_INNER), rows(CONV_DIM), rows(TILE),
                  pl.BlockSpec((None, TILE, cl), lambda i, c: (i, 0, c)),
                  pl.BlockSpec((None, SUBLANES, CONV_DIM), lambda i, c: (i, 0, 0)), state,
                  const(cw), const(cbias), const(dtb), const(dtb_t), const(al), const(al_t), const(dskip),
                  const(norm)],
        out_specs=[rows(SSD_INNER), state, pl.BlockSpec((None, CONV_W - 1, CONV_DIM), lambda i, c: (i, 0, 0))],
        out_shape=[jax.ShapeDtypeStruct((b, seq, SSD_INNER), F32),
                   jax.ShapeDtypeStruct((b, SSD_HEADS, SSD_HEAD_DIM, SSD_STATE), F32),
                   jax.ShapeDtypeStruct((b, CONV_W - 1, CONV_DIM), F32)],
        scratch_shapes=[pltpu.VMEM((SUBLANES + cl, CONV_DIM), F32)],
        compiler_params=_params(2), name="ssd")(z, xbc, dt, dt_t, conv_prev, h0, cw, cbias, dtb, dtb_t, al, al_t,
                                                 dskip, norm)


def _pack_inproj(w_in):
    splits = (ATT_WIDTH, KV_WIDTH, KV_WIDTH, KV_WIDTH, N_BRANCH * N_ATT_HEADS, SSD_INNER, CONV_DIM, SSD_HEADS)
    offs = [0]
    for s in splits:
        offs.append(offs[-1] + s)
    wq, wc, ws, ww, wg, wz, wx, wd = [w_in[:, offs[i]:offs[i + 1]] for i in range(len(splits))]
    d = w_in.shape[0]
    per_kv = GQA * N_BRANCH
    wg4 = jnp.pad(wg.reshape(d, N_KV_HEADS, per_kv), ((0, 0), (0, 0), (0, TILE - per_kv)))
    wd_pad = jnp.pad(wd, ((0, 0), (0, TILE - SSD_HEADS)))
    w = jnp.concatenate([wq * HEAD_DIM ** -0.5, wc, ws, ww, wg4.reshape(d, N_KV_HEADS * TILE), wz, wx, wd_pad],
                        axis=1)
    assert w.shape[1] == PROJ_WIDTH
    return w.astype(BF16), wd_pad.T.astype(BF16)


def _pack_compress(cmp_pe, w_cmp):
    zero = jnp.zeros((BLK, HEAD_DIM, HEAD_DIM), w_cmp.dtype)
    bd = jnp.concatenate([jnp.concatenate([w_cmp[:, 0], zero], axis=2),
                          jnp.concatenate([zero, w_cmp[:, 1]], axis=2)], axis=1)
    pe = cmp_pe.reshape(BLK, KV_GROUP)
    return pe.astype(F32), bd.astype(BF16)


def _sample_tables(rel_table, dec_seq, past_len, n_rows, n_new, w_buf):
    col = jnp.arange(TILE, dtype=jnp.int32)
    head = (col // (GQA * dec_seq)) * GQA + (col // dec_seq) % GQA
    tok = (col % dec_seq)[None, :]
    table = rel_table.astype(F32)

    def bias(dist, visible):
        return jnp.where(visible, _lookup(table[:, head][:, None, :], _rel_bucket(dist)), MASK_VALUE)

    row = jnp.arange(TILE, dtype=jnp.int32)[:, None]
    blk = jnp.arange(n_rows, dtype=jnp.int32)[:, None]
    c_dist = past_len + tok - ((blk + 1) * BLK - 1)
    ctab = bias(c_dist, c_dist >= 0)
    last_tab = bias(TILE + tok - row, jnp.ones((TILE, TILE), bool))
    new_row = jnp.arange(n_new, dtype=jnp.int32)[:, None]
    new_tab = bias(tok - new_row, (tok >= new_row) & (new_row < dec_seq))
    far = jnp.full((SUBLANES, TILE), w_buf, jnp.int32)
    far_row = bias(far, jnp.ones((SUBLANES, TILE), bool))
    win0_tab = bias(jnp.full((TILE, TILE), w_buf, jnp.int32), row > tok)
    same = (col[:, None] // (GQA * dec_seq) == col[None, :] // (GQA * dec_seq)) & \
           (col[:, None] % dec_seq == col[None, :] % dec_seq)
    return ctab, last_tab, new_tab, far_row, win0_tab, same.astype(F32)


def _layer_weights(ffn_pre, ffn_post, ffn_w_in, ffn_w_out):
    f = ffn_w_out.shape[0]
    return (ffn_pre[None, :], ffn_post[None, :], ffn_w_in[:, :f].astype(BF16), ffn_w_in[:, f:].astype(BF16),
            ffn_w_out.astype(BF16))


def kernel(x_prompt, x_sample, cache_cmp_kv, cache_slc_kv, cache_win_kv, state_ssm, state_conv, page_table,
           rel_table, ffn1_pre, ffn1_post, ffn1_w_in, ffn1_w_out, mix_pre, mix_post, w_in, w_out, att_out_norm,
           cmp_pe, w_cmp, conv_w, conv_b, dt_bias, a_log, d_skip, ssd_norm,
           ffn2_pre, ffn2_post, ffn2_w_in, ffn2_w_out):
    depth = ffn1_pre.shape[0]
    bp, seq, d = x_prompt.shape
    bs, dec_seq, _ = x_sample.shape
    n_pool = cache_cmp_kv.shape[1]
    n_pages = page_table.shape[1]
    past_len = n_pages * PAGE_SIZE
    w_buf = cache_win_kv.shape[2]
    kv_shape = (N_KV_HEADS, 2, HEAD_DIM)
    assert GQA * dec_seq * N_KV_HEADS == TILE and w_buf == WINDOW and seq % TILE == 0

    yp = x_prompt.reshape(bp * seq, d)
    ys = x_sample.reshape(bs * dec_seq, d)
    outs = [[] for _ in range(10)]
    for l in range(depth):
        ffn1 = _layer_weights(ffn1_pre[l], ffn1_post[l], ffn1_w_in[l], ffn1_w_out[l])
        ffn2 = _layer_weights(ffn2_pre[l], ffn2_post[l], ffn2_w_in[l], ffn2_w_out[l])
        w_proj, wdt_t = _pack_inproj(w_in[l])
        pe, bd = _pack_compress(cmp_pe[l], w_cmp[l])
        wo_att = w_out[l][:ATT_WIDTH].astype(BF16)
        wo_ssd = w_out[l][ATT_WIDTH:].astype(BF16)
        ssd_consts = (conv_w[l], conv_b[l][None, :], dt_bias[l][None, :], dt_bias[l][:, None], a_log[l][None, :],
                      a_log[l][:, None], jnp.repeat(d_skip[l], SSD_HEAD_DIM)[None, :], ssd_norm[l][None, :])
        mix_pre_l, mix_post_l, att_norm_l = mix_pre[l][None, :], mix_post[l][None, :], att_out_norm[l][None, :]

        yp = _ffn(yp, *ffn1)
        q, ckv, skv, wkv, skv_b, wkv_b, gates, z, xbc, dt, dt_t = _inproj(yp, mix_pre_l, w_proj, wdt_t)
        kvc = _compress(ckv.reshape(bp, seq, KV_WIDTH), pe, bd, TILE)
        tiles, cmp_tab = _prompt_bias_tables(rel_table)
        o_att = _nsa_prompt(q.reshape(bp, seq, ATT_WIDTH), gates.reshape(bp, seq, N_KV_HEADS * TILE), kvc,
                            skv_b.reshape(bp, seq, KV_WIDTH), wkv_b.reshape(bp, seq, KV_WIDTH), tiles, cmp_tab)
        cl = min(CHUNK, seq)
        o_ssd, h_new, conv_new = _ssd(
            z.reshape(bp, seq, SSD_INNER), xbc.reshape(bp, seq, CONV_DIM), dt.reshape(bp, seq, TILE),
            dt_t.reshape(TILE, bp, seq).transpose(1, 0, 2), jnp.zeros((bp, SUBLANES, CONV_DIM), F32),
            jnp.zeros((bp, SSD_HEADS, SSD_HEAD_DIM, SSD_STATE), F32), ssd_consts, cl)
        yp = _mixout(yp, o_att.reshape(bp * seq, ATT_WIDTH), o_ssd.reshape(bp * seq, SSD_INNER), att_norm_l,
                     mix_post_l, wo_att, wo_ssd)
        outs[0].append(ckv.reshape((bp, seq) + kv_shape))
        outs[1].append(skv.reshape((bp, seq) + kv_shape))
        outs[2].append(wkv.reshape((bp, seq) + kv_shape)[:, -min(WINDOW, seq):])
        outs[3].append(h_new)
        outs[4].append(conv_new)

        ys = _ffn(ys, *ffn1)
        q, ckv, skv, wkv, skv_b, wkv_b, gates, z, xbc, dt, dt_t = _inproj(ys, mix_pre_l, w_proj, wdt_t)
        kc_past = _compress_paged(cache_cmp_kv[l].reshape(n_pool, PAGE_SIZE, KV_WIDTH), page_table, pe, bd)
        cmp_new = jnp.pad(ckv.reshape(bs, dec_seq, KV_WIDTH), ((0, 0), (0, BLK - dec_seq), (0, 0)))
        kc_new = _compress(cmp_new.reshape(1, bs * BLK, KV_WIDTH), pe, bd, bs).reshape(bs, 1, KV_WIDTH)
        n_past = kc_past.shape[1]
        n_rows = -(-(n_past + 1) // 16) * 16
        kc = jnp.concatenate([kc_past, kc_new, jnp.zeros((bs, n_rows - n_past - 1, KV_WIDTH), BF16)], axis=1)
        n_new = 16
        ctab, last_tab, new_tab, far_row, win0_tab, gsum = _sample_tables(rel_table, dec_seq, past_len, n_rows,
                                                                          n_new, w_buf)
        q5 = q.reshape(bs, dec_seq, N_KV_HEADS, GQA, HEAD_DIM).transpose(0, 2, 4, 3, 1)
        q5 = q5.reshape(bs, N_KV_HEADS, HEAD_DIM, GQA * dec_seq)
        eye = jnp.eye(N_KV_HEADS, dtype=F32)[None, :, None, :, None]
        qbd = jnp.pad(q5, ((0, 0), (0, 0), (0, HEAD_DIM), (0, 0)))[:, :, :, None, :] * eye
        qbd = qbd.reshape(bs, KV_WIDTH, TILE).astype(BF16)
        g5 = gates.reshape(bs, dec_seq, N_KV_HEADS, TILE)[..., :GQA * N_BRANCH]
        g5 = g5.reshape(bs, dec_seq, N_KV_HEADS, GQA, N_BRANCH).transpose(0, 4, 2, 3, 1).reshape(bs, N_BRANCH, TILE)
        gates_t = jnp.pad(g5, ((0, 0), (0, SUBLANES - N_BRANCH), (0, 0)))

        def pad_new(a):
            return jnp.pad(a.reshape(bs, dec_seq, KV_WIDTH), ((0, 0), (0, n_new - dec_seq), (0, 0)))

        o_t = _nsa_sample(page_table, cache_slc_kv[l].reshape(n_pool, PAGE_SIZE, KV_WIDTH), qbd, gates_t, kc, ctab,
                          gsum, pad_new(skv_b), cache_win_kv[l].reshape(bs, w_buf, KV_WIDTH), pad_new(wkv_b),
                          last_tab, new_tab, far_row, win0_tab, dec_seq=dec_seq, past_len=past_len)
        o7 = o_t.reshape(bs, N_KV_HEADS, 2, HEAD_DIM, N_KV_HEADS, GQA, dec_seq)
        o_att = jnp.stack([o7[:, k, 1, :, k] for k in range(N_KV_HEADS)], axis=1).transpose(0, 4, 1, 3, 2)
        o_att = o_att.reshape(bs * dec_seq, ATT_WIDTH)
        conv_prev = jnp.pad(state_conv[l], ((0, 0), (SUBLANES - (CONV_W - 1), 0), (0, 0)))
        o_ssd, h_new, conv_new = _ssd(
            z.reshape(bs, dec_seq, SSD_INNER), xbc.reshape(bs, dec_seq, CONV_DIM), dt.reshape(bs, dec_seq, TILE),
            dt_t.reshape(TILE, bs, dec_seq).transpose(1, 0, 2), conv_prev, state_ssm[l], ssd_consts,
            min(CHUNK, dec_seq))
        ys = _mixout(ys, o_att, o_ssd.reshape(bs * dec_seq, SSD_INNER), att_norm_l, mix_post_l, wo_att, wo_ssd)
        win_all = jnp.concatenate([cache_win_kv[l], wkv.reshape((bs, dec_seq) + kv_shape)], axis=1)
        outs[5].append(ckv.reshape((bs, dec_seq) + kv_shape))
        outs[6].append(skv.reshape((bs, dec_seq) + kv_shape))
        outs[7].append(win_all[:, -min(WINDOW, w_buf + dec_seq):])
        outs[8].append(h_new)
        outs[9].append(conv_new)

        yp = _ffn(yp, *ffn2)
        ys = _ffn(ys, *ffn2)

    return (yp.reshape(bp, seq, d), ys.reshape(bs, dec_seq, d)) + tuple(jnp.stack(o) for o in outs)
```

```python
import functools
import math

import jax
import jax.numpy as jnp
from jax import lax
from jax.experimental import pallas as pl
from jax.experimental.pallas import tpu as pltpu

F32 = jnp.float32
BF16 = jnp.bfloat16

D_MODEL = 1024
N_ATT_HEADS = 16
HEAD_DIM = 64
N_KV_HEADS = 4
GQA = N_ATT_HEADS // N_KV_HEADS
ATT_WIDTH = N_ATT_HEADS * HEAD_DIM
KV_WIDTH = 2 * N_KV_HEADS * HEAD_DIM
KV_GROUP = 2 * HEAD_DIM
BLK = 64
N_SEL = 16
WINDOW = 512
N_BRANCH = 3
N_BUCKETS = 32
MAX_DISTANCE = 128
SSD_HEADS = 16
SSD_HEAD_DIM = 64
SSD_INNER = SSD_HEADS * SSD_HEAD_DIM
SSD_GROUPS = 2
SSD_STATE = 128
CONV_W = 4
CONV_DIM = SSD_INNER + 2 * SSD_GROUPS * SSD_STATE
CHUNK = 128
PAGE_SIZE = 128
EPS = 1e-6

TILE = 128
SUBLANES = 8
MASK_VALUE = -(2.0 ** 100)
M_INIT = -1e30
LOG2E = math.log2(math.e)
VMEM_LIMIT = 56 * 1024 * 1024
PAGES_PER_STEP = 8

NT_DIMS = (((1,), (1,)), ((), ()))
TN_DIMS = (((0,), (0,)), ((), ()))


def _params(n_grid_dims):
    return pltpu.CompilerParams(dimension_semantics=("arbitrary",) * n_grid_dims,
                                vmem_limit_bytes=VMEM_LIMIT)


def _rms(x, g):
    return x * lax.rsqrt(jnp.mean(x * x, axis=-1, keepdims=True) + EPS) * g


def _const_spec(shape):
    zeros = (0,) * len(shape)
    return pl.BlockSpec(shape, lambda *_: zeros, pipeline_mode=pl.Buffered(1))


def _ffn_body(x_ref, pre_ref, post_ref, wg_ref, wu_ref, wo_ref, o_ref):
    x = x_ref[...]
    h = _rms(x, pre_ref[...]).astype(BF16)
    gate = jnp.dot(h, wg_ref[...], preferred_element_type=F32)
    up = jnp.dot(h, wu_ref[...], preferred_element_type=F32)
    a = (gate * jax.nn.sigmoid(gate) * up).astype(BF16)
    y = jnp.dot(a, wo_ref[...], preferred_element_type=F32)
    o_ref[...] = x + 0.5 * _rms(y, post_ref[...])


def _ffn(x, pre, post, wg, wu, wo):
    t, d = x.shape
    f = wg.shape[1]
    tm = min(256, t)
    row = pl.BlockSpec((tm, d), lambda i: (i, 0))
    return pl.pallas_call(
        _ffn_body, grid=(t // tm,),
        in_specs=[row, _const_spec((1, d)), _const_spec((1, d)), _const_spec((d, f)), _const_spec((d, f)),
                  _const_spec((f, d))],
        out_specs=row, out_shape=jax.ShapeDtypeStruct((t, d), F32),
        compiler_params=_params(1), name="ffn")(x, pre, post, wg, wu, wo)


_SEG = {}
_off = 0
for _name, _w in (("q", ATT_WIDTH), ("ckv", KV_WIDTH), ("skv", KV_WIDTH), ("wkv", KV_WIDTH),
                  ("gates", N_KV_HEADS * TILE), ("z", SSD_INNER), ("xbc", CONV_DIM), ("dt", TILE)):
    _SEG[_name] = (_off, _off + _w)
    _off += _w
PROJ_WIDTH = _off


def _inproj_body(x_ref, pre_ref, w_ref, wdt_t_ref, q_ref, ckv_ref, skv_ref, wkv_ref, skvb_ref, wkvb_ref,
                 gates_ref, z_ref, xbc_ref, dt_ref, dtt_ref):
    h = _rms(x_ref[...], pre_ref[...]).astype(BF16)

    def proj(name):
        lo, hi = _SEG[name]
        return jnp.dot(h, w_ref[:, lo:hi], preferred_element_type=F32)

    q_ref[...] = proj("q")
    ckv_ref[...] = proj("ckv")
    skv = proj("skv")
    skv_ref[...] = skv
    skvb_ref[...] = skv.astype(BF16)
    wkv = proj("wkv")
    wkv_ref[...] = wkv
    wkvb_ref[...] = wkv.astype(BF16)
    gates_ref[...] = proj("gates")
    z_ref[...] = proj("z")
    xbc_ref[...] = proj("xbc")
    dt_ref[...] = proj("dt")
    dtt_ref[...] = lax.dot_general(wdt_t_ref[...], h, NT_DIMS, preferred_element_type=F32)


def _inproj(x, pre, w, wdt_t):
    t, d = x.shape
    tm = min(256, t)

    def rows(width):
        return pl.BlockSpec((tm, width), lambda i: (i, 0))

    widths = (ATT_WIDTH, KV_WIDTH, KV_WIDTH, KV_WIDTH, KV_WIDTH, KV_WIDTH, N_KV_HEADS * TILE, SSD_INNER,
              CONV_DIM, TILE)
    dtypes = (F32, F32, F32, F32, BF16, BF16, F32, F32, F32, F32)
    out_shape = [jax.ShapeDtypeStruct((t, wd), dt) for wd, dt in zip(widths, dtypes)]
    out_shape.append(jax.ShapeDtypeStruct((TILE, t), F32))
    out_specs = [rows(wd) for wd in widths] + [pl.BlockSpec((TILE, tm), lambda i: (0, i))]
    return pl.pallas_call(
        _inproj_body, grid=(t // tm,),
        in_specs=[rows(d), _const_spec((1, d)), _const_spec((d, PROJ_WIDTH)), _const_spec((TILE, d))],
        out_specs=out_specs, out_shape=out_shape,
        compiler_params=_params(1), name="inproj")(x, pre, w, wdt_t)


def _mixout_body(x_ref, oa_ref, os_ref, an_ref, post_ref, wa_ref, ws_ref, o_ref):
    a = _rms(oa_ref[...], an_ref[...]).astype(BF16)
    y = jnp.dot(a, wa_ref[...], preferred_element_type=F32)
    y = y + jnp.dot(os_ref[...].astype(BF16), ws_ref[...], preferred_element_type=F32)
    o_ref[...] = x_ref[...] + _rms(y, post_ref[...])


def _mixout(x, o_att, o_ssd, att_norm, post, wa, ws):
    t, d = x.shape
    tm = min(256, t)
    row = pl.BlockSpec((tm, d), lambda i: (i, 0))
    return pl.pallas_call(
        _mixout_body, grid=(t // tm,),
        in_specs=[row, row, row, _const_spec((1, d)), _const_spec((1, d)), _const_spec((d, d)),
                  _const_spec((d, d))],
        out_specs=row, out_shape=jax.ShapeDtypeStruct((t, d), F32),
        compiler_params=_params(1), name="mixout")(x, o_att, o_ssd, att_norm, post, wa, ws)


def _compress_rows(x_refs, pe_ref, bd_ref, n_blocks):
    acc = jnp.zeros((N_KV_HEADS * n_blocks, KV_GROUP), F32)
    for l in range(BLK):
        pe = pe_ref[l:l + 1, :]
        xs = jnp.concatenate([x_ref[pl.ds(l, n_blocks, stride=BLK), :] + pe for x_ref in x_refs], axis=0)
        acc = acc + jnp.dot(xs.astype(BF16), bd_ref[l], preferred_element_type=F32)
    return acc


def _store_summaries(acc, o_ref, n_blocks):
    for k in range(N_KV_HEADS):
        o_ref[0:n_blocks, k * KV_GROUP:(k + 1) * KV_GROUP] = acc[k * n_blocks:(k + 1) * n_blocks].astype(BF16)


def _compress_body(*refs, n_blocks, n_out):
    x_refs = refs[:N_KV_HEADS]
    pe_ref, bd_ref, o_ref = refs[N_KV_HEADS:]
    _store_summaries(_compress_rows(x_refs, pe_ref, bd_ref, n_blocks), o_ref, n_blocks)
    if n_out > n_blocks:
        o_ref[n_blocks:n_out, :] = jnp.zeros((n_out - n_blocks, KV_WIDTH), BF16)


def _compress(kv, pe, bd, n_out):
    b, s, _ = kv.shape
    n_blocks = s // BLK

    def head_spec(k):
        return pl.BlockSpec((None, s, KV_GROUP), lambda i: (i, 0, k))

    return pl.pallas_call(
        functools.partial(_compress_body, n_blocks=n_blocks, n_out=n_out), grid=(b,),
        in_specs=[head_spec(k) for k in range(N_KV_HEADS)]
        + [_const_spec((BLK, KV_GROUP)), _const_spec((BLK, KV_GROUP, KV_GROUP))],
        out_specs=pl.BlockSpec((None, n_out, KV_WIDTH), lambda i: (i, 0, 0)),
        out_shape=jax.ShapeDtypeStruct((b, n_out, KV_WIDTH), BF16),
        compiler_params=_params(1), name="compress")(*([kv] * N_KV_HEADS), pe, bd)


def _compress_paged_body(pt_ref, *refs, n_pages):
    page_refs = refs[:PAGES_PER_STEP]
    pe_ref, bd_ref, o_ref, stage_ref = refs[PAGES_PER_STEP:]
    j = pl.program_id(1)
    for p in range(PAGES_PER_STEP):
        row = pl.multiple_of((j * PAGES_PER_STEP + p) * PAGE_SIZE, PAGE_SIZE)
        for k in range(N_KV_HEADS):
            stage_ref[k, pl.ds(row, PAGE_SIZE), :] = page_refs[p][:, k * KV_GROUP:(k + 1) * KV_GROUP]

    @pl.when(j == pl.num_programs(1) - 1)
    def _():
        n_blocks = n_pages * (PAGE_SIZE // BLK)
        x_refs = [stage_ref.at[k] for k in range(N_KV_HEADS)]
        _store_summaries(_compress_rows(x_refs, pe_ref, bd_ref, n_blocks), o_ref, n_blocks)


def _compress_paged(cache, page_table, pe, bd):
    b, n_pages = page_table.shape
    steps = n_pages // PAGES_PER_STEP
    n_blocks = n_pages * (PAGE_SIZE // BLK)

    def page_spec(p):
        return pl.BlockSpec((None, PAGE_SIZE, KV_WIDTH), lambda i, j, pt: (pt[i, j * PAGES_PER_STEP + p], 0, 0))

    grid_spec = pltpu.PrefetchScalarGridSpec(
        num_scalar_prefetch=1, grid=(b, steps),
        in_specs=[page_spec(p) for p in range(PAGES_PER_STEP)]
        + [pl.BlockSpec((BLK, KV_GROUP), lambda i, j, pt: (0, 0)),
           pl.BlockSpec((BLK, KV_GROUP, KV_GROUP), lambda i, j, pt: (0, 0, 0))],
        out_specs=pl.BlockSpec((None, n_blocks, KV_WIDTH), lambda i, j, pt: (i, 0, 0)),
        scratch_shapes=[pltpu.VMEM((N_KV_HEADS, n_pages * PAGE_SIZE, KV_GROUP), F32)])
    return pl.pallas_call(
        functools.partial(_compress_paged_body, n_pages=n_pages), grid_spec=grid_spec,
        out_shape=jax.ShapeDtypeStruct((b, n_blocks, KV_WIDTH), BF16),
        compiler_params=_params(2), name="compress_paged")(page_table, *([cache] * PAGES_PER_STEP), pe, bd)


def _rel_bucket(dist):
    n = jnp.maximum(dist, 0)
    max_exact = N_BUCKETS // 2
    nf = jnp.maximum(n, 1).astype(F32)
    log_b = max_exact + (jnp.log(nf / max_exact) / math.log(MAX_DISTANCE / max_exact)
                         * (N_BUCKETS - max_exact)).astype(jnp.int32)
    return jnp.where(n < max_exact, n, jnp.minimum(log_b, N_BUCKETS - 1))


def _lookup(table, bucket):
    onehot = (bucket[..., None] == jnp.arange(table.shape[0], dtype=jnp.int32)).astype(F32)
    return jnp.einsum('...b,b...->...', onehot, table, precision=lax.Precision.HIGHEST)


def _bias_of(rel_table, dist, visible):
    onehot = (_rel_bucket(dist)[..., None] == jnp.arange(N_BUCKETS, dtype=jnp.int32)).astype(F32)
    bias = jnp.einsum('rcb,bh->hrc', onehot, rel_table.astype(F32) * LOG2E, precision=lax.Precision.HIGHEST)
    return jnp.where(visible[None], bias, MASK_VALUE).reshape(-1, dist.shape[-1])


KIND_ZERO, KIND_DIAG, KIND_PREV, KIND_MASKED, KIND_OLDEST = range(5)


def _prompt_bias_tables(rel_table):
    r = jnp.arange(TILE, dtype=jnp.int32)[:, None]
    c = jnp.arange(TILE, dtype=jnp.int32)[None, :]
    true = jnp.ones((TILE, TILE), bool)
    far = _bias_of(rel_table, jnp.full((TILE, TILE), MAX_DISTANCE, jnp.int32), true)
    tiles = jnp.stack([
        jnp.zeros_like(far),
        _bias_of(rel_table, r - c, r >= c) - far,
        _bias_of(rel_table, TILE + r - c, true) - far,
        jnp.full_like(far, MASK_VALUE),
        jnp.where(jnp.tile(r < c, (N_ATT_HEADS, 1)), 0.0, MASK_VALUE),
    ])
    c_dist = r - (BLK - 1) + BLK * (BLK - 1 - c)
    cmp_tab = _bias_of(rel_table, c_dist, c_dist >= 0)
    return tiles, cmp_tab


def _selection_mask(score, blk, n_rows):
    n_groups = score.shape[0] // SUBLANES
    groups = [score[SUBLANES * j:SUBLANES * (j + 1)] for j in range(n_groups)]
    ranks = [jnp.zeros(groups[0].shape, jnp.int32) for _ in range(n_groups)]
    row8 = blk[0:SUBLANES]
    for n in range(n_rows):
        other = jnp.broadcast_to(score[n:n + 1, :], groups[0].shape)
        for j in range(n_groups):
            lo = SUBLANES * j
            if lo > n:
                inc = jnp.where(other >= groups[j], 1, 0)
            elif lo + SUBLANES - 1 <= n:
                inc = jnp.where(other > groups[j], 1, 0)
            else:
                inc = jnp.where(row8 + lo > n, jnp.where(other >= groups[j], 1, 0),
                                jnp.where(other > groups[j], 1, 0))
            ranks[j] = ranks[j] + inc
    rank = jnp.concatenate(ranks, axis=0)
    return jnp.where(rank < N_SEL, jnp.where(score > -jnp.inf, 0.0, MASK_VALUE), MASK_VALUE)


PAIR = 2 * TILE
SEL_UNROLL = 4


def _nsa_prompt_body(q_ref, gates_ref, kvc_ref, skv_ref, wkv_ref, tb_ref, ctab_ref, o_ref,
                     s_ref, sw_ref, mx_ref, mb_ref, acc_ref, *, n_blocks):
    i = pl.program_id(2)
    a = i // 2
    odd = i % 2
    rows = GQA * TILE
    lane = lax.broadcasted_iota(jnp.int32, (rows, TILE), 1)
    low = lane < HEAD_DIM
    key_lane = lax.broadcasted_iota(jnp.int32, (PAIR, TILE), 1)
    key_blk = lax.broadcasted_iota(jnp.int32, (PAIR, TILE), 0) // BLK
    n_pairs = skv_ref.shape[0] // PAIR

    qf = q_ref[...]
    q_pad = []
    for g in range(GQA):
        pair = qf[:, (g // 2) * TILE:(g // 2 + 1) * TILE]
        q_pad.append(pair if g % 2 == 0 else pltpu.roll(pair, HEAD_DIM, 1))
    q_all = jnp.concatenate(q_pad, axis=0)
    lhs_plain = jnp.where(low, q_all, 0.0).astype(BF16)

    def ones_and_values(kv):
        kv_lane = lax.broadcasted_iota(jnp.int32, kv.shape, 1)
        return jnp.where(kv_lane < HEAD_DIM, jnp.ones_like(kv), kv)

    def normalise(acc):
        return acc / jnp.maximum(pltpu.roll(acc, HEAD_DIM, 1), 1e-30)

    kvc = kvc_ref[...]
    shift = (2 * i + (TILE - (BLK - 1))) % TILE
    bias = jnp.concatenate([pltpu.roll(ctab_ref[g * TILE:(g + 1) * TILE, :], shift, 1) for g in range(GQA)], axis=0)
    s = lax.dot_general(lhs_plain, kvc, NT_DIMS, preferred_element_type=F32)
    s = jnp.where(lane < n_blocks, s + bias, MASK_VALUE)
    m = jnp.maximum(jnp.max(s, axis=1, keepdims=True), M_INIT)
    p = jnp.exp2(s - m)
    c_acc = jnp.dot(p.astype(BF16), ones_and_values(kvc), preferred_element_type=F32)
    p = p / jnp.maximum(c_acc, 1e-30)
    o_cmp = normalise(c_acc)
    imp = p[0:TILE]
    for g in range(1, GQA):
        imp = imp + p[g * TILE:(g + 1) * TILE]

    blk = lax.broadcasted_iota(jnp.int32, (n_blocks, TILE), 0)
    query = lax.broadcasted_iota(jnp.int32, (n_blocks, TILE), 1)
    cur = 2 * i + jnp.where(query >= BLK, 1, 0)
    forced = (blk == 0) | (blk == cur) | (blk == cur - 1)
    score = jnp.where(forced, jnp.inf, jnp.where(blk <= cur, imp.T[0:n_blocks], -jnp.inf))
    neg_t = _selection_mask(score, blk, n_blocks)
    pieces = [jnp.zeros((HEAD_DIM, TILE), F32), neg_t]
    if n_blocks < TILE - HEAD_DIM:
        pieces.append(jnp.zeros((TILE - HEAD_DIM - n_blocks, TILE), F32))
    neg = jnp.concatenate(pieces, axis=0).T
    lhs_sel = jnp.where(low, q_all, jnp.concatenate([neg] * GQA, axis=0)).astype(BF16)

    def sel_kinds(j):
        last = j == a
        left = jnp.where(last, jnp.where(odd == 1, KIND_PREV, KIND_DIAG), KIND_ZERO)
        right = jnp.where(last, jnp.where(odd == 1, KIND_DIAG, KIND_MASKED),
                          jnp.where((j == a - 1) & (odd == 0), KIND_PREV, KIND_ZERO))
        return left, right

    def win_kinds(j):
        left, right = sel_kinds(j)
        first = j == a - WINDOW // PAIR
        return (jnp.where(first, jnp.where(odd == 1, KIND_MASKED, KIND_OLDEST), left),
                jnp.where(first, jnp.where(odd == 1, KIND_OLDEST, KIND_ZERO), right))

    def keys_of(kv_ref, j):
        jc = jnp.clip(j, 0, n_pairs - 1)
        return kv_ref[pl.ds(pl.multiple_of(jc * PAIR, PAIR), PAIR), :]

    def score_pair(lhs, kv_ref, j, valid, kinds, block_columns):
        rhs = keys_of(kv_ref, j)
        if block_columns:
            onehot = jnp.where(key_lane - HEAD_DIM == (PAIR // BLK) * j + key_blk, 1.0, 0.0).astype(BF16)
            rhs = jnp.where(key_lane < HEAD_DIM, rhs, onehot)
        sc = lax.dot_general(lhs, rhs, NT_DIMS, preferred_element_type=F32)
        left, right = kinds(j)
        left = jnp.where(valid, left, KIND_MASKED)
        right = jnp.where(valid, right, KIND_MASKED)
        return sc + jnp.concatenate([tb_ref[left], tb_ref[right]], axis=1)

    def half_max(sc):
        return jnp.maximum(sc[:, :TILE], sc[:, TILE:])

    def weigh_pair(sc, mb, kv_ref, j):
        pr = jnp.exp2(sc - jnp.concatenate([mb, mb], axis=1))
        return jnp.dot(pr.astype(BF16), ones_and_values(keys_of(kv_ref, j)), preferred_element_type=F32)

    def row_max(mx):
        return jnp.broadcast_to(jnp.max(mx, axis=1, keepdims=True), (rows, TILE))

    w_pairs = [a - WINDOW // PAIR + w for w in range(WINDOW // PAIR + 1)]
    w_mx = None
    for w, j in enumerate(w_pairs):
        sc = score_pair(lhs_plain, wkv_ref, j, j >= 0, win_kinds, False)
        sw_ref[w] = sc
        w_mx = half_max(sc) if w_mx is None else jnp.maximum(w_mx, half_max(sc))
    w_mb = row_max(jnp.maximum(w_mx, M_INIT))
    w_acc = None
    for w, j in enumerate(w_pairs):
        dacc = weigh_pair(sw_ref[w], w_mb, wkv_ref, j)
        w_acc = dacc if w_acc is None else w_acc + dacc
    o_win = normalise(w_acc)

    n_trips = (a + SEL_UNROLL) // SEL_UNROLL
    mx_ref[...] = jnp.full(mx_ref.shape, M_INIT, F32)

    def score_trip(t, carry):
        mx = None
        for u in range(SEL_UNROLL):
            j = t * SEL_UNROLL + u
            sc = score_pair(lhs_sel, skv_ref, j, j <= a, sel_kinds, True)
            s_ref[j] = sc
            mx = half_max(sc) if mx is None else jnp.maximum(mx, half_max(sc))
        mx_ref[...] = jnp.maximum(mx_ref[...], mx)
        return carry

    lax.fori_loop(0, n_trips, score_trip, 0)
    mb_ref[...] = row_max(mx_ref[...])
    acc_ref[...] = jnp.zeros(acc_ref.shape, F32)

    def weigh_trip(t, carry):
        mb = mb_ref[...]
        acc = None
        for u in range(SEL_UNROLL):
            j = t * SEL_UNROLL + u
            dacc = weigh_pair(s_ref[j], mb, skv_ref, j)
            acc = dacc if acc is None else acc + dacc
        acc_ref[...] += acc
        return carry

    lax.fori_loop(0, n_trips, weigh_trip, 0)
    o_sel = normalise(acc_ref[...])

    gate = jax.nn.sigmoid(gates_ref[...])
    outs = []
    for g in range(GQA):
        def gcol(br):
            c = g * N_BRANCH + br
            return gate[:, c:c + 1]
        head = slice(g * TILE, (g + 1) * TILE)
        outs.append(gcol(0) * o_cmp[head] + gcol(1) * o_sel[head] + gcol(2) * o_win[head])
    out_low = lax.broadcasted_iota(jnp.int32, (TILE, TILE), 1) < HEAD_DIM
    for j in range(GQA // 2):
        o_ref[:, j * TILE:(j + 1) * TILE] = jnp.where(out_low, pltpu.roll(outs[2 * j], HEAD_DIM, 1), outs[2 * j + 1])


def _nsa_prompt(q, gates, kvc, skv_b, wkv_b, tiles, cmp_tab):
    b, s, _ = q.shape
    nq = s // TILE
    n_blocks = s // BLK
    assert n_blocks <= TILE - HEAD_DIM and kvc.shape[1] == TILE and s % PAIR == 0
    gw = GQA * HEAD_DIM
    rows = GQA * TILE
    return pl.pallas_call(
        functools.partial(_nsa_prompt_body, n_blocks=n_blocks), grid=(b, N_KV_HEADS, nq),
        in_specs=[pl.BlockSpec((None, TILE, gw), lambda bi, k, i: (bi, i, k)),
                  pl.BlockSpec((None, TILE, TILE), lambda bi, k, i: (bi, i, k)),
                  pl.BlockSpec((None, TILE, KV_GROUP), lambda bi, k, i: (bi, 0, k)),
                  pl.BlockSpec((None, s, KV_GROUP), lambda bi, k, i: (bi, 0, k)),
                  pl.BlockSpec((None, s, KV_GROUP), lambda bi, k, i: (bi, 0, k)),
                  pl.BlockSpec((len(tiles), rows, TILE), lambda bi, k, i: (0, k, 0)),
                  pl.BlockSpec((rows, TILE), lambda bi, k, i: (k, 0))],
        out_specs=pl.BlockSpec((None, TILE, gw), lambda bi, k, i: (bi, i, k)),
        out_shape=jax.ShapeDtypeStruct((b, s, ATT_WIDTH), F32),
        scratch_shapes=[pltpu.VMEM((s // PAIR + SEL_UNROLL - 1, rows, PAIR), F32),
                        pltpu.VMEM((WINDOW // PAIR + 1, rows, PAIR), F32),
                        pltpu.VMEM((rows, TILE), F32), pltpu.VMEM((rows, TILE), F32),
                        pltpu.VMEM((rows, KV_GROUP), F32)],
        compiler_params=_params(3), name="nsa_prompt")(q, gates, kvc, skv_b, wkv_b, tiles, cmp_tab)


def _t_softmax_step(kv, qbd, bias, m_ref, l_ref, acc_ref):
    s = jnp.dot(kv, qbd, preferred_element_type=F32) + bias
    m_prev = m_ref[...]
    m_new = jnp.maximum(m_prev, jnp.max(s, axis=0, keepdims=True))
    alpha = jnp.exp2(m_prev - m_new)
    p = jnp.exp2(s - m_new)
    l_ref[...] = alpha * l_ref[...] + jnp.sum(p, axis=0, keepdims=True)
    acc_ref[...] = alpha * acc_ref[...] + lax.dot_general(kv, p.astype(BF16), TN_DIMS,
                                                          preferred_element_type=F32)
    m_ref[...] = m_new


def _nsa_sample_body(pt_ref, *refs, n_pages, dec_seq, past_len):
    page_refs = refs[:PAGES_PER_STEP]
    (qbd_ref, gates_ref, kc_ref, ctab_ref, gsum_ref, snew_ref, cwin_ref, wnew_ref, last_ref, new_ref, far_ref,
     win0_ref, o_ref, neg_ref, m_ref, l_ref, acc_ref, oc_ref, ow_ref) = refs[PAGES_PER_STEP:]
    j = pl.program_id(1)
    n_past = n_pages * (PAGE_SIZE // BLK)
    n_rows = kc_ref.shape[0]
    sub = lax.broadcasted_iota(jnp.int32, (TILE, TILE), 0)
    qbd = qbd_ref[...]
    far = far_ref[0:1, :]

    def reset():
        m_ref[...] = jnp.full(m_ref.shape, M_INIT, F32)
        l_ref[...] = jnp.zeros(l_ref.shape, F32)
        acc_ref[...] = jnp.zeros(acc_ref.shape, F32)

    def result():
        return acc_ref[...] / jnp.maximum(l_ref[...], 1e-30)

    @pl.when(j == 0)
    def _():
        kc = kc_ref[...]
        s = jnp.dot(kc, qbd, preferred_element_type=F32) + ctab_ref[...]
        m = jnp.maximum(jnp.max(s, axis=0, keepdims=True), M_INIT)
        p = jnp.exp2(s - m)
        p = p / jnp.maximum(jnp.sum(p, axis=0, keepdims=True), 1e-30)
        oc_ref[...] = lax.dot_general(kc, p.astype(BF16), TN_DIMS, preferred_element_type=F32)
        imp = jnp.dot(p, gsum_ref[...], preferred_element_type=F32, precision=lax.Precision.HIGHEST)
        blk = lax.broadcasted_iota(jnp.int32, (n_rows, TILE), 0)
        tok = lax.broadcasted_iota(jnp.int32, (n_rows, TILE), 1) % dec_seq
        cur = (past_len + tok) // BLK
        forced = (blk == 0) | (blk == cur) | (blk == cur - 1)
        score = jnp.where(forced, jnp.inf, jnp.where(blk <= cur, imp, -jnp.inf))
        neg_ref[...] = _selection_mask(score, blk, n_past + 1)

        reset()
        n_win = cwin_ref.shape[0] // TILE
        for t in range(n_win):
            kv = cwin_ref[t * TILE:(t + 1) * TILE, :].astype(BF16)
            bias = win0_ref[...] if t == 0 else (last_ref[...] if t == n_win - 1 else far)
            _t_softmax_step(kv, qbd, bias, m_ref, l_ref, acc_ref)
        _t_softmax_step(wnew_ref[...], qbd, new_ref[...], m_ref, l_ref, acc_ref)
        ow_ref[...] = result()
        reset()

    blocks_per_step = PAGES_PER_STEP * (PAGE_SIZE // BLK)
    neg_rows = neg_ref[pl.ds(pl.multiple_of(j * blocks_per_step, blocks_per_step), blocks_per_step), :]
    for p in range(PAGES_PER_STEP):
        page = j * PAGES_PER_STEP + p
        kv = page_refs[p][...].astype(BF16)
        first = neg_rows[2 * p:2 * p + 1, :]
        second = neg_rows[2 * p + 1:2 * p + 2, :]
        bias = jnp.where(sub < BLK, first, second) + jnp.where(page == n_pages - 1, last_ref[...], far)
        _t_softmax_step(kv, qbd, bias, m_ref, l_ref, acc_ref)

    @pl.when(j == pl.num_programs(1) - 1)
    def _():
        _t_softmax_step(snew_ref[...], qbd, new_ref[...] + neg_ref[n_past:n_past + 1, :], m_ref, l_ref, acc_ref)
        gate = jax.nn.sigmoid(gates_ref[...])
        o_ref[...] = gate[0:1, :] * oc_ref[...] + gate[1:2, :] * result() + gate[2:3, :] * ow_ref[...]


def _nsa_sample(page_table, cache_slc, qbd, gates_t, kc, ctab, gsum, slc_new, cache_win, win_new, last_tab,
                new_tab, far_row, win0_tab, *, dec_seq, past_len):
    b, n_pages = page_table.shape
    steps = n_pages // PAGES_PER_STEP
    n_rows = kc.shape[1]
    w_buf = cache_win.shape[1]
    n_new = slc_new.shape[1]

    def page_spec(p):
        return pl.BlockSpec((None, PAGE_SIZE, KV_WIDTH), lambda i, j, pt: (pt[i, j * PAGES_PER_STEP + p], 0, 0))

    def per_batch(*shape):
        zeros = (0,) * len(shape)
        return pl.BlockSpec((None,) + shape, lambda i, j, pt: (i,) + zeros)

    def const(*shape):
        zeros = (0,) * len(shape)
        return pl.BlockSpec(shape, lambda i, j, pt: zeros)

    grid_spec = pltpu.PrefetchScalarGridSpec(
        num_scalar_prefetch=1, grid=(b, steps),
        in_specs=[page_spec(p) for p in range(PAGES_PER_STEP)]
        + [per_batch(KV_WIDTH, TILE), per_batch(SUBLANES, TILE), per_batch(n_rows, KV_WIDTH), const(n_rows, TILE),
           const(TILE, TILE), per_batch(n_new, KV_WIDTH), per_batch(w_buf, KV_WIDTH), per_batch(n_new, KV_WIDTH),
           const(TILE, TILE), const(n_new, TILE), const(SUBLANES, TILE), const(TILE, TILE)],
        out_specs=per_batch(KV_WIDTH, TILE),
        scratch_shapes=[pltpu.VMEM((n_rows, TILE), F32), pltpu.VMEM((1, TILE), F32), pltpu.VMEM((1, TILE), F32),
                        pltpu.VMEM((KV_WIDTH, TILE), F32), pltpu.VMEM((KV_WIDTH, TILE), F32),
                        pltpu.VMEM((KV_WIDTH, TILE), F32)])
    return pl.pallas_call(
        functools.partial(_nsa_sample_body, n_pages=n_pages, dec_seq=dec_seq, past_len=past_len),
        grid_spec=grid_spec, out_shape=jax.ShapeDtypeStruct((b, KV_WIDTH, TILE), F32),
        compiler_params=_params(2), name="nsa_sample")(
            page_table, *([cache_slc] * PAGES_PER_STEP), qbd, gates_t, kc, ctab, gsum, slc_new, cache_win, win_new,
            last_tab, new_tab, far_row, win0_tab)


def _softplus(x):
    return jnp.maximum(x, 0.0) + jnp.log(1.0 + jnp.exp(-jnp.abs(x)))


def _ssd_body(z_ref, xbc_ref, dt_ref, dtt_ref, convp_ref, h0_ref, cw_ref, cb_ref, dtb_ref, dtbt_ref, al_ref,
              alt_ref, dskip_ref, norm_ref, o_ref, hout_ref, convout_ref, xp_ref, *, cl):
    c = pl.program_id(1)
    pad = SUBLANES

    @pl.when(c == 0)
    def _():
        xp_ref[0:pad, :] = convp_ref[...]
        hout_ref[...] = h0_ref[...]

    xp_ref[pad:pad + cl, :] = xbc_ref[...]
    conv = cb_ref[...]
    for k in range(CONV_W):
        lo = pad - (CONV_W - 1) + k
        conv = conv + cw_ref[k:k + 1, :] * xp_ref[lo:lo + cl, :]
    u = conv * jax.nn.sigmoid(conv)
    tail = xp_ref[pad + cl - (CONV_W - 1):pad + cl, :]
    convout_ref[...] = tail
    xp_ref[pad - (CONV_W - 1):pad, :] = tail

    xs = u[:, :SSD_INNER]
    dt = _softplus(dt_ref[:, 0:SSD_HEADS] + dtb_ref[...])
    dt_t = _softplus(dtt_ref[0:SSD_HEADS, :] + dtbt_ref[...])
    la = dt * -jnp.exp(al_ref[...])
    la_t = dt_t * -jnp.exp(alt_ref[...])
    row = lax.broadcasted_iota(jnp.int32, (cl, cl), 0)
    col = lax.broadcasted_iota(jnp.int32, (cl, cl), 1)
    causal = row >= col
    hi = lax.Precision.HIGHEST
    cs = jnp.dot(jnp.where(causal, 1.0, 0.0), la, preferred_element_type=F32, precision=hi)
    cs_t = jnp.dot(la_t, jnp.where(row <= col, 1.0, 0.0), preferred_element_type=F32, precision=hi)

    heads_per_group = SSD_HEADS // SSD_GROUPS
    ys = []
    for g in range(SSD_GROUPS):
        b_lo = SSD_INNER + g * SSD_STATE
        c_lo = SSD_INNER + SSD_GROUPS * SSD_STATE + g * SSD_STATE
        bm = u[:, b_lo:b_lo + SSD_STATE].astype(BF16)
        cm = u[:, c_lo:c_lo + SSD_STATE].astype(BF16)
        cb = lax.dot_general(cm, bm, NT_DIMS, preferred_element_type=F32)
        for r in range(heads_per_group):
            h = g * heads_per_group + r
            cs_col = cs[:, h:h + 1]
            diff = cs_col - cs_t[h:h + 1, :]
            decay = jnp.where(causal, jnp.exp(jnp.where(causal, diff, 0.0)), 0.0)
            xs_h = xs[:, h * SSD_HEAD_DIM:(h + 1) * SSD_HEAD_DIM]
            xdt = xs_h * dt[:, h:h + 1]
            y = jnp.dot((cb * decay).astype(BF16), xdt.astype(BF16), preferred_element_type=F32)
            last = cs[cl - 1:cl, h:h + 1]
            w = (xdt * jnp.exp(last - cs_col)).astype(BF16)
            state_in = lax.dot_general(w, bm, TN_DIMS, preferred_element_type=F32)
            h_prev = hout_ref[h]
            y = y + lax.dot_general(cm, h_prev.astype(BF16), NT_DIMS, preferred_element_type=F32) * jnp.exp(cs_col)
            hout_ref[h] = h_prev * jnp.exp(last) + state_in
            ys.append(y + dskip_ref[:, h * SSD_HEAD_DIM:(h + 1) * SSD_HEAD_DIM] * xs_h)
    y = jnp.concatenate(ys, axis=1)
    z = z_ref[...]
    y = y * (z * jax.nn.sigmoid(z))
    gw = SSD_INNER // SSD_GROUPS
    normed = []
    for g in range(SSD_GROUPS):
        yg = y[:, g * gw:(g + 1) * gw]
        normed.append(yg * lax.rsqrt(jnp.mean(yg * yg, axis=-1, keepdims=True) + EPS))
    o_ref[...] = jnp.concatenate(normed, axis=1) * norm_ref[...]


def _ssd(z, xbc, dt, dt_t, conv_prev, h0, consts, cl):
    b, seq, _ = z.shape
    nc = seq // cl
    cw, cbias, dtb, dtb_t, al, al_t, dskip, norm = consts

    def rows(width):
        return pl.BlockSpec((None, cl, width), lambda i, c: (i, c, 0))

    state = pl.BlockSpec((None, SSD_HEADS, SSD_HEAD_DIM, SSD_STATE), lambda i, c: (i, 0, 0, 0))

    def const(x):
        zeros = (0,) * x.ndim
        return pl.BlockSpec(x.shape, lambda i, c: zeros)

    return pl.pallas_call(
        functools.partial(_ssd_body, cl=cl), grid=(b, nc),
        in_specs=[rows(SSD_INNER), rows(CONV_DIM), rows(TILE),
                  pl.BlockSpec((None, TILE, cl), lambda i, c: (i, 0, c)),
                  pl.BlockSpec((None, SUBLANES, CONV_DIM), lambda i, c: (i, 0, 0)), state,
                  const(cw), const(cbias), const(dtb), const(dtb_t), const(al), const(al_t), const(dskip),
                  const(norm)],
        out_specs=[rows(SSD_INNER), state, pl.BlockSpec((None, CONV_W - 1, CONV_DIM), lambda i, c: (i, 0, 0))],
        out_shape=[jax.ShapeDtypeStruct((b, seq, SSD_INNER), F32),
                   jax.ShapeDtypeStruct((b, SSD_HEADS, SSD_HEAD_DIM, SSD_STATE), F32),
                   jax.ShapeDtypeStruct((b, CONV_W - 1, CONV_DIM), F32)],
        scratch_shapes=[pltpu.VMEM((SUBLANES + cl, CONV_DIM), F32)],
        compiler_params=_params(2), name="ssd")(z, xbc, dt, dt_t, conv_prev, h0, cw, cbias, dtb, dtb_t, al, al_t,
                                                 dskip, norm)


def _pack_inproj(w_in):
    splits = (ATT_WIDTH, KV_WIDTH, KV_WIDTH, KV_WIDTH, N_BRANCH * N_ATT_HEADS, SSD_INNER, CONV_DIM, SSD_HEADS)
    offs = [0]
    for s in splits:
        offs.append(offs[-1] + s)
    wq, wc, ws, ww, wg, wz, wx, wd = [w_in[:, offs[i]:offs[i + 1]] for i in range(len(splits))]
    d = w_in.shape[0]
    per_kv = GQA * N_BRANCH
    wg4 = jnp.pad(wg.reshape(d, N_KV_HEADS, per_kv), ((0, 0), (0, 0), (0, TILE - per_kv)))
    wd_pad = jnp.pad(wd, ((0, 0), (0, TILE - SSD_HEADS)))
    w = jnp.concatenate([wq * (HEAD_DIM ** -0.5 * LOG2E), wc, ws, ww, wg4.reshape(d, N_KV_HEADS * TILE), wz, wx, wd_pad],
                        axis=1)
    assert w.shape[1] == PROJ_WIDTH
    return w.astype(BF16), wd_pad.T.astype(BF16)


def _pack_compress(cmp_pe, w_cmp):
    zero = jnp.zeros((BLK, HEAD_DIM, HEAD_DIM), w_cmp.dtype)
    bd = jnp.concatenate([jnp.concatenate([w_cmp[:, 0], zero], axis=2),
                          jnp.concatenate([zero, w_cmp[:, 1]], axis=2)], axis=1)
    pe = cmp_pe.reshape(BLK, KV_GROUP)
    return pe.astype(F32), bd.astype(BF16)


def _sample_tables(rel_table, dec_seq, past_len, n_rows, n_new, w_buf):
    col = jnp.arange(TILE, dtype=jnp.int32)
    head = (col // (GQA * dec_seq)) * GQA + (col // dec_seq) % GQA
    tok = (col % dec_seq)[None, :]
    table = rel_table.astype(F32) * LOG2E

    def bias(dist, visible):
        return jnp.where(visible, _lookup(table[:, head][:, None, :], _rel_bucket(dist)), MASK_VALUE)

    row = jnp.arange(TILE, dtype=jnp.int32)[:, None]
    blk = jnp.arange(n_rows, dtype=jnp.int32)[:, None]
    c_dist = past_len + tok - ((blk + 1) * BLK - 1)
    ctab = bias(c_dist, c_dist >= 0)
    last_tab = bias(TILE + tok - row, jnp.ones((TILE, TILE), bool))
    new_row = jnp.arange(n_new, dtype=jnp.int32)[:, None]
    new_tab = bias(tok - new_row, (tok >= new_row) & (new_row < dec_seq))
    far = jnp.full((SUBLANES, TILE), w_buf, jnp.int32)
    far_row = bias(far, jnp.ones((SUBLANES, TILE), bool))
    win0_tab = bias(jnp.full((TILE, TILE), w_buf, jnp.int32), row > tok)
    same = (col[:, None] // (GQA * dec_seq) == col[None, :] // (GQA * dec_seq)) & \
           (col[:, None] % dec_seq == col[None, :] % dec_seq)
    return ctab, last_tab, new_tab, far_row, win0_tab, same.astype(F32)


def _layer_weights(ffn_pre, ffn_post, ffn_w_in, ffn_w_out):
    f = ffn_w_out.shape[0]
    return (ffn_pre[None, :], ffn_post[None, :], ffn_w_in[:, :f].astype(BF16), ffn_w_in[:, f:].astype(BF16),
            ffn_w_out.astype(BF16))


def kernel(x_prompt, x_sample, cache_cmp_kv, cache_slc_kv, cache_win_kv, state_ssm, state_conv, page_table,
           rel_table, ffn1_pre, ffn1_post, ffn1_w_in, ffn1_w_out, mix_pre, mix_post, w_in, w_out, att_out_norm,
           cmp_pe, w_cmp, conv_w, conv_b, dt_bias, a_log, d_skip, ssd_norm,
           ffn2_pre, ffn2_post, ffn2_w_in, ffn2_w_out):
    depth = ffn1_pre.shape[0]
    bp, seq, d = x_prompt.shape
    bs, dec_seq, _ = x_sample.shape
    n_pool = cache_cmp_kv.shape[1]
    n_pages = page_table.shape[1]
    past_len = n_pages * PAGE_SIZE
    w_buf = cache_win_kv.shape[2]
    kv_shape = (N_KV_HEADS, 2, HEAD_DIM)
    assert GQA * dec_seq * N_KV_HEADS == TILE and w_buf == WINDOW and seq % TILE == 0

    yp = x_prompt.reshape(bp * seq, d)
    ys = x_sample.reshape(bs * dec_seq, d)
    outs = [[] for _ in range(10)]
    for l in range(depth):
        ffn1 = _layer_weights(ffn1_pre[l], ffn1_post[l], ffn1_w_in[l], ffn1_w_out[l])
        ffn2 = _layer_weights(ffn2_pre[l], ffn2_post[l], ffn2_w_in[l], ffn2_w_out[l])
        w_proj, wdt_t = _pack_inproj(w_in[l])
        pe, bd = _pack_compress(cmp_pe[l], w_cmp[l])
        wo_att = w_out[l][:ATT_WIDTH].astype(BF16)
        wo_ssd = w_out[l][ATT_WIDTH:].astype(BF16)
        ssd_consts = (conv_w[l], conv_b[l][None, :], dt_bias[l][None, :], dt_bias[l][:, None], a_log[l][None, :],
                      a_log[l][:, None], jnp.repeat(d_skip[l], SSD_HEAD_DIM)[None, :], ssd_norm[l][None, :])
        mix_pre_l, mix_post_l, att_norm_l = mix_pre[l][None, :], mix_post[l][None, :], att_out_norm[l][None, :]

        yp = _ffn(yp, *ffn1)
        q, ckv, skv, wkv, skv_b, wkv_b, gates, z, xbc, dt, dt_t = _inproj(yp, mix_pre_l, w_proj, wdt_t)
        kvc = _compress(ckv.reshape(bp, seq, KV_WIDTH), pe, bd, TILE)
        tiles, cmp_tab = _prompt_bias_tables(rel_table)
        o_att = _nsa_prompt(q.reshape(bp, seq, ATT_WIDTH), gates.reshape(bp, seq, N_KV_HEADS * TILE), kvc,
                            skv_b.reshape(bp, seq, KV_WIDTH), wkv_b.reshape(bp, seq, KV_WIDTH), tiles, cmp_tab)
        cl = min(CHUNK, seq)
        o_ssd, h_new, conv_new = _ssd(
            z.reshape(bp, seq, SSD_INNER), xbc.reshape(bp, seq, CONV_DIM), dt.reshape(bp, seq, TILE),
            dt_t.reshape(TILE, bp, seq).transpose(1, 0, 2), jnp.zeros((bp, SUBLANES, CONV_DIM), F32),
            jnp.zeros((bp, SSD_HEADS, SSD_HEAD_DIM, SSD_STATE), F32), ssd_consts, cl)
        yp = _mixout(yp, o_att.reshape(bp * seq, ATT_WIDTH), o_ssd.reshape(bp * seq, SSD_INNER), att_norm_l,
                     mix_post_l, wo_att, wo_ssd)
        outs[0].append(ckv.reshape((bp, seq) + kv_shape))
        outs[1].append(skv.reshape((bp, seq) + kv_shape))
        outs[2].append(wkv.reshape((bp, seq) + kv_shape)[:, -min(WINDOW, seq):])
        outs[3].append(h_new)
        outs[4].append(conv_new)

        ys = _ffn(ys, *ffn1)
        q, ckv, skv, wkv, skv_b, wkv_b, gates, z, xbc, dt, dt_t = _inproj(ys, mix_pre_l, w_proj, wdt_t)
        kc_past = _compress_paged(cache_cmp_kv[l].reshape(n_pool, PAGE_SIZE, KV_WIDTH), page_table, pe, bd)
        cmp_new = jnp.pad(ckv.reshape(bs, dec_seq, KV_WIDTH), ((0, 0), (0, BLK - dec_seq), (0, 0)))
        kc_new = _compress(cmp_new.reshape(1, bs * BLK, KV_WIDTH), pe, bd, bs).reshape(bs, 1, KV_WIDTH)
        n_past = kc_past.shape[1]
        n_rows = -(-(n_past + 1) // 16) * 16
        kc = jnp.concatenate([kc_past, kc_new, jnp.zeros((bs, n_rows - n_past - 1, KV_WIDTH), BF16)], axis=1)
        n_new = 16
        ctab, last_tab, new_tab, far_row, win0_tab, gsum = _sample_tables(rel_table, dec_seq, past_len, n_rows,
                                                                          n_new, w_buf)
        q5 = q.reshape(bs, dec_seq, N_KV_HEADS, GQA, HEAD_DIM).transpose(0, 2, 4, 3, 1)
        q5 = q5.reshape(bs, N_KV_HEADS, HEAD_DIM, GQA * dec_seq)
        eye = jnp.eye(N_KV_HEADS, dtype=F32)[None, :, None, :, None]
        qbd = jnp.pad(q5, ((0, 0), (0, 0), (0, HEAD_DIM), (0, 0)))[:, :, :, None, :] * eye
        qbd = qbd.reshape(bs, KV_WIDTH, TILE).astype(BF16)
        g5 = gates.reshape(bs, dec_seq, N_KV_HEADS, TILE)[..., :GQA * N_BRANCH]
        g5 = g5.reshape(bs, dec_seq, N_KV_HEADS, GQA, N_BRANCH).transpose(0, 4, 2, 3, 1).reshape(bs, N_BRANCH, TILE)
        gates_t = jnp.pad(g5, ((0, 0), (0, SUBLANES - N_BRANCH), (0, 0)))

        def pad_new(a):
            return jnp.pad(a.reshape(bs, dec_seq, KV_WIDTH), ((0, 0), (0, n_new - dec_seq), (0, 0)))

        o_t = _nsa_sample(page_table, cache_slc_kv[l].reshape(n_pool, PAGE_SIZE, KV_WIDTH), qbd, gates_t, kc, ctab,
                          gsum, pad_new(skv_b), cache_win_kv[l].reshape(bs, w_buf, KV_WIDTH), pad_new(wkv_b),
                          last_tab, new_tab, far_row, win0_tab, dec_seq=dec_seq, past_len=past_len)
        o7 = o_t.reshape(bs, N_KV_HEADS, 2, HEAD_DIM, N_KV_HEADS, GQA, dec_seq)
        o_att = jnp.stack([o7[:, k, 1, :, k] for k in range(N_KV_HEADS)], axis=1).transpose(0, 4, 1, 3, 2)
        o_att = o_att.reshape(bs * dec_seq, ATT_WIDTH)
        conv_prev = jnp.pad(state_conv[l], ((0, 0), (SUBLANES - (CONV_W - 1), 0), (0, 0)))
        o_ssd, h_new, conv_new = _ssd(
            z.reshape(bs, dec_seq, SSD_INNER), xbc.reshape(bs, dec_seq, CONV_DIM), dt.reshape(bs, dec_seq, TILE),
            dt_t.reshape(TILE, bs, dec_seq).transpose(1, 0, 2), conv_prev, state_ssm[l], ssd_consts,
            min(CHUNK, dec_seq))
        ys = _mixout(ys, o_att, o_ssd.reshape(bs * dec_seq, SSD_INNER), att_norm_l, mix_post_l, wo_att, wo_ssd)
        win_all = jnp.concatenate([cache_win_kv[l], wkv.reshape((bs, dec_seq) + kv_shape)], axis=1)
        outs[5].append(ckv.reshape((bs, dec_seq) + kv_shape))
        outs[6].append(skv.reshape((bs, dec_seq) + kv_shape))
        outs[7].append(win_all[:, -min(WINDOW, w_buf + dec_seq):])
        outs[8].append(h_new)
        outs[9].append(conv_new)

        yp = _ffn(yp, *ffn2)
        ys = _ffn(ys, *ffn2)

    return (yp.reshape(bp, seq, d), ys.reshape(bs, dec_seq, d)) + tuple(jnp.stack(o) for o in outs)
```

```python
import functools
import math

import jax
import jax.numpy as jnp
from jax import lax
from jax.experimental import pallas as pl
from jax.experimental.pallas import tpu as pltpu

F32 = jnp.float32
BF16 = jnp.bfloat16

D_MODEL = 1024
N_ATT_HEADS = 16
HEAD_DIM = 64
N_KV_HEADS = 4
GQA = N_ATT_HEADS // N_KV_HEADS
ATT_WIDTH = N_ATT_HEADS * HEAD_DIM
KV_WIDTH = 2 * N_KV_HEADS * HEAD_DIM
KV_GROUP = 2 * HEAD_DIM
BLK = 64
N_SEL = 16
WINDOW = 512
N_BRANCH = 3
N_BUCKETS = 32
MAX_DISTANCE = 128
SSD_HEADS = 16
SSD_HEAD_DIM = 64
SSD_INNER = SSD_HEADS * SSD_HEAD_DIM
SSD_GROUPS = 2
SSD_STATE = 128
CONV_W = 4
CONV_DIM = SSD_INNER + 2 * SSD_GROUPS * SSD_STATE
CHUNK = 128
PAGE_SIZE = 128
EPS = 1e-6

TILE = 128
SUBLANES = 8
MASK_VALUE = -(2.0 ** 100)
M_INIT = -1e30
LOG2E = math.log2(math.e)
VMEM_LIMIT = 56 * 1024 * 1024
PAGES_PER_STEP = 8

NT_DIMS = (((1,), (1,)), ((), ()))
TN_DIMS = (((0,), (0,)), ((), ()))


def _params(n_grid_dims):
    return pltpu.CompilerParams(dimension_semantics=("arbitrary",) * n_grid_dims,
                                vmem_limit_bytes=VMEM_LIMIT)


def _rms(x, g):
    return x * lax.rsqrt(jnp.mean(x * x, axis=-1, keepdims=True) + EPS) * g


def _const_spec(shape):
    zeros = (0,) * len(shape)
    return pl.BlockSpec(shape, lambda *_: zeros, pipeline_mode=pl.Buffered(1))


def _ffn_body(x_ref, pre_ref, post_ref, wg_ref, wu_ref, wo_ref, o_ref):
    x = x_ref[...]
    h = _rms(x, pre_ref[...]).astype(BF16)
    gate = jnp.dot(h, wg_ref[...], preferred_element_type=F32)
    up = jnp.dot(h, wu_ref[...], preferred_element_type=F32)
    a = (gate * jax.nn.sigmoid(gate) * up).astype(BF16)
    y = jnp.dot(a, wo_ref[...], preferred_element_type=F32)
    o_ref[...] = x + 0.5 * _rms(y, post_ref[...])


def _ffn(x, pre, post, wg, wu, wo):
    t, d = x.shape
    f = wg.shape[1]
    tm = min(256, t)
    row = pl.BlockSpec((tm, d), lambda i: (i, 0))
    return pl.pallas_call(
        _ffn_body, grid=(t // tm,),
        in_specs=[row, _const_spec((1, d)), _const_spec((1, d)), _const_spec((d, f)), _const_spec((d, f)),
                  _const_spec((f, d))],
        out_specs=row, out_shape=jax.ShapeDtypeStruct((t, d), F32),
        compiler_params=_params(1), name="ffn")(x, pre, post, wg, wu, wo)


_SEG = {}
_off = 0
for _name, _w in (("q", ATT_WIDTH), ("ckv", KV_WIDTH), ("skv", KV_WIDTH), ("wkv", KV_WIDTH),
                  ("gates", N_KV_HEADS * TILE), ("z", SSD_INNER), ("xbc", CONV_DIM), ("dt", TILE)):
    _SEG[_name] = (_off, _off + _w)
    _off += _w
PROJ_WIDTH = _off


def _inproj_body(x_ref, pre_ref, w_ref, wt_ref, q_ref, ckv_ref, skv_ref, wkv_ref, skvb_ref, wkvb_ref,
                 gates_ref, z_ref, xbc_ref, dt_ref, dtt_ref, ckvt_ref, skvt_ref, wkvt_ref):
    h = _rms(x_ref[...], pre_ref[...]).astype(BF16)

    def proj(name):
        lo, hi = _SEG[name]
        return jnp.dot(h, w_ref[:, lo:hi], preferred_element_type=F32)

    q_ref[...] = proj("q")
    ckv_ref[...] = proj("ckv")
    skv = proj("skv")
    skv_ref[...] = skv
    skvb_ref[...] = skv.astype(BF16)
    wkv = proj("wkv")
    wkv_ref[...] = wkv
    wkvb_ref[...] = wkv.astype(BF16)
    gates_ref[...] = proj("gates")
    z_ref[...] = proj("z")
    xbc_ref[...] = proj("xbc")
    dt_ref[...] = proj("dt")
    dtt_ref[...] = lax.dot_general(wt_ref[0:TILE, :], h, NT_DIMS, preferred_element_type=F32)
    for n, ref in enumerate((ckvt_ref, skvt_ref, wkvt_ref)):
        lo = TILE + n * KV_WIDTH
        ref[...] = lax.dot_general(wt_ref[lo:lo + KV_WIDTH, :], h, NT_DIMS, preferred_element_type=F32)


def _inproj(x, pre, w, w_t, n_seq):
    t, d = x.shape
    tm = min(256, t)
    seq = t // n_seq
    per_seq = seq // tm
    assert seq % tm == 0

    def cols(height):
        return pl.BlockSpec((None, height, tm), lambda i: (i // per_seq, 0, i % per_seq))

    def rows(width):
        return pl.BlockSpec((tm, width), lambda i: (i, 0))

    widths = (ATT_WIDTH, KV_WIDTH, KV_WIDTH, KV_WIDTH, KV_WIDTH, KV_WIDTH, N_KV_HEADS * TILE, SSD_INNER,
              CONV_DIM, TILE)
    dtypes = (F32, F32, F32, F32, BF16, BF16, F32, F32, F32, F32)
    out_shape = [jax.ShapeDtypeStruct((t, wd), dt) for wd, dt in zip(widths, dtypes)]
    t_heights = (TILE, KV_WIDTH, KV_WIDTH, KV_WIDTH)
    out_shape += [jax.ShapeDtypeStruct((n_seq, ht, seq), F32) for ht in t_heights]
    out_specs = [rows(wd) for wd in widths] + [cols(ht) for ht in t_heights]
    return pl.pallas_call(
        _inproj_body, grid=(t // tm,),
        in_specs=[rows(d), _const_spec((1, d)), _const_spec((d, PROJ_WIDTH)), _const_spec(w_t.shape)],
        out_specs=out_specs, out_shape=out_shape,
        compiler_params=_params(1), name="inproj")(x, pre, w, w_t)


def _mixout_body(x_ref, oa_ref, os_ref, an_ref, post_ref, wa_ref, ws_ref, o_ref):
    a = _rms(oa_ref[...], an_ref[...]).astype(BF16)
    y = jnp.dot(a, wa_ref[...], preferred_element_type=F32)
    y = y + jnp.dot(os_ref[...].astype(BF16), ws_ref[...], preferred_element_type=F32)
    o_ref[...] = x_ref[...] + _rms(y, post_ref[...])


def _mixout(x, o_att, o_ssd, att_norm, post, wa, ws):
    t, d = x.shape
    tm = min(256, t)
    row = pl.BlockSpec((tm, d), lambda i: (i, 0))
    return pl.pallas_call(
        _mixout_body, grid=(t // tm,),
        in_specs=[row, row, row, _const_spec((1, d)), _const_spec((1, d)), _const_spec((d, d)),
                  _const_spec((d, d))],
        out_specs=row, out_shape=jax.ShapeDtypeStruct((t, d), F32),
        compiler_params=_params(1), name="mixout")(x, o_att, o_ssd, att_norm, post, wa, ws)


def _token_rows(ref, k, lo=0, n=None):
    n = ref.shape[-1] if n is None else n
    return ref[k, :, :, lo:lo + n].reshape(KV_GROUP, n).T


def _compress_rows(x_refs, pe_ref, bd_ref, n_blocks):
    acc = jnp.zeros((N_KV_HEADS * n_blocks, KV_GROUP), F32)
    for l in range(BLK):
        pe = pe_ref[l:l + 1, :]
        xs = jnp.concatenate([x_ref[pl.ds(l, n_blocks, stride=BLK), :] + pe for x_ref in x_refs], axis=0)
        acc = acc + jnp.dot(xs.astype(BF16), bd_ref[l], preferred_element_type=F32)
    return acc


def _store_summaries(acc, o_ref, n_blocks):
    for k in range(N_KV_HEADS):
        o_ref[0:n_blocks, k * KV_GROUP:(k + 1) * KV_GROUP] = acc[k * n_blocks:(k + 1) * n_blocks].astype(BF16)


def _compress_body(*refs, n_blocks, n_out):
    x_refs = refs[:N_KV_HEADS]
    pe_ref, bd_ref, o_ref = refs[N_KV_HEADS:]
    _store_summaries(_compress_rows(x_refs, pe_ref, bd_ref, n_blocks), o_ref, n_blocks)
    if n_out > n_blocks:
        o_ref[n_blocks:n_out, :] = jnp.zeros((n_out - n_blocks, KV_WIDTH), BF16)


def _compress(kv, pe, bd, n_out):
    b, s, _ = kv.shape
    n_blocks = s // BLK

    def head_spec(k):
        return pl.BlockSpec((None, s, KV_GROUP), lambda i: (i, 0, k))

    return pl.pallas_call(
        functools.partial(_compress_body, n_blocks=n_blocks, n_out=n_out), grid=(b,),
        in_specs=[head_spec(k) for k in range(N_KV_HEADS)]
        + [_const_spec((BLK, KV_GROUP)), _const_spec((BLK, KV_GROUP, KV_GROUP))],
        out_specs=pl.BlockSpec((None, n_out, KV_WIDTH), lambda i: (i, 0, 0)),
        out_shape=jax.ShapeDtypeStruct((b, n_out, KV_WIDTH), BF16),
        compiler_params=_params(1), name="compress")(*([kv] * N_KV_HEADS), pe, bd)


def _compress_paged_body(pt_ref, *refs, n_pages):
    page_refs = refs[:PAGES_PER_STEP]
    pe_ref, bd_ref, o_ref, stage_ref = refs[PAGES_PER_STEP:]
    j = pl.program_id(1)
    for p in range(PAGES_PER_STEP):
        row = pl.multiple_of((j * PAGES_PER_STEP + p) * PAGE_SIZE, PAGE_SIZE)
        for k in range(N_KV_HEADS):
            stage_ref[k, pl.ds(row, PAGE_SIZE), :] = _token_rows(page_refs[p], k)

    @pl.when(j == pl.num_programs(1) - 1)
    def _():
        n_blocks = n_pages * (PAGE_SIZE // BLK)
        x_refs = [stage_ref.at[k] for k in range(N_KV_HEADS)]
        _store_summaries(_compress_rows(x_refs, pe_ref, bd_ref, n_blocks), o_ref, n_blocks)


def _compress_paged(cache, page_table, pe, bd):
    b, n_pages = page_table.shape
    steps = n_pages // PAGES_PER_STEP
    n_blocks = n_pages * (PAGE_SIZE // BLK)

    def page_spec(p):
        return pl.BlockSpec((None, N_KV_HEADS, 2, HEAD_DIM, PAGE_SIZE),
                            lambda i, j, pt: (pt[i, j * PAGES_PER_STEP + p], 0, 0, 0, 0))

    grid_spec = pltpu.PrefetchScalarGridSpec(
        num_scalar_prefetch=1, grid=(b, steps),
        in_specs=[page_spec(p) for p in range(PAGES_PER_STEP)]
        + [pl.BlockSpec((BLK, KV_GROUP), lambda i, j, pt: (0, 0)),
           pl.BlockSpec((BLK, KV_GROUP, KV_GROUP), lambda i, j, pt: (0, 0, 0))],
        out_specs=pl.BlockSpec((None, n_blocks, KV_WIDTH), lambda i, j, pt: (i, 0, 0)),
        scratch_shapes=[pltpu.VMEM((N_KV_HEADS, n_pages * PAGE_SIZE, KV_GROUP), F32)])
    return pl.pallas_call(
        functools.partial(_compress_paged_body, n_pages=n_pages), grid_spec=grid_spec,
        out_shape=jax.ShapeDtypeStruct((b, n_blocks, KV_WIDTH), BF16),
        compiler_params=_params(2), name="compress_paged")(page_table, *([cache] * PAGES_PER_STEP), pe, bd)


def _rel_bucket(dist):
    n = jnp.maximum(dist, 0)
    max_exact = N_BUCKETS // 2
    nf = jnp.maximum(n, 1).astype(F32)
    log_b = max_exact + (jnp.log(nf / max_exact) / math.log(MAX_DISTANCE / max_exact)
                         * (N_BUCKETS - max_exact)).astype(jnp.int32)
    return jnp.where(n < max_exact, n, jnp.minimum(log_b, N_BUCKETS - 1))


def _lookup(table, bucket):
    onehot = (bucket[..., None] == jnp.arange(table.shape[0], dtype=jnp.int32)).astype(F32)
    return jnp.einsum('...b,b...->...', onehot, table, precision=lax.Precision.HIGHEST)


def _bias_of(rel_table, dist, visible):
    onehot = (_rel_bucket(dist)[..., None] == jnp.arange(N_BUCKETS, dtype=jnp.int32)).astype(F32)
    bias = jnp.einsum('rcb,bh->hrc', onehot, rel_table.astype(F32) * LOG2E, precision=lax.Precision.HIGHEST)
    return jnp.where(visible[None], bias, MASK_VALUE).reshape(-1, dist.shape[-1])


KIND_ZERO, KIND_DIAG, KIND_PREV, KIND_MASKED, KIND_OLDEST = range(5)


def _prompt_bias_tables(rel_table):
    r = jnp.arange(TILE, dtype=jnp.int32)[:, None]
    c = jnp.arange(TILE, dtype=jnp.int32)[None, :]
    true = jnp.ones((TILE, TILE), bool)
    far = _bias_of(rel_table, jnp.full((TILE, TILE), MAX_DISTANCE, jnp.int32), true)
    tiles = jnp.stack([
        jnp.zeros_like(far),
        _bias_of(rel_table, r - c, r >= c) - far,
        _bias_of(rel_table, TILE + r - c, true) - far,
        jnp.full_like(far, MASK_VALUE),
        jnp.where(jnp.tile(r < c, (N_ATT_HEADS, 1)), 0.0, MASK_VALUE),
    ])
    c_dist = r - (BLK - 1) + BLK * (BLK - 1 - c)
    cmp_tab = _bias_of(rel_table, c_dist, c_dist >= 0)
    return tiles, cmp_tab


def _selection_mask(score, blk, n_rows):
    n_groups = score.shape[0] // SUBLANES
    groups = [score[SUBLANES * j:SUBLANES * (j + 1)] for j in range(n_groups)]
    ranks = [jnp.zeros(groups[0].shape, jnp.int32) for _ in range(n_groups)]
    row8 = blk[0:SUBLANES]
    for n in range(n_rows):
        other = jnp.broadcast_to(score[n:n + 1, :], groups[0].shape)
        for j in range(n_groups):
            lo = SUBLANES * j
            if lo > n:
                inc = jnp.where(other >= groups[j], 1, 0)
            elif lo + SUBLANES - 1 <= n:
                inc = jnp.where(other > groups[j], 1, 0)
            else:
                inc = jnp.where(row8 + lo > n, jnp.where(other >= groups[j], 1, 0),
                                jnp.where(other > groups[j], 1, 0))
            ranks[j] = ranks[j] + inc
    rank = jnp.concatenate(ranks, axis=0)
    return jnp.where(rank < N_SEL, jnp.where(score > -jnp.inf, 0.0, MASK_VALUE), MASK_VALUE)


PAIR = 2 * TILE
SEL_UNROLL = 4


def _nsa_prompt_body(q_ref, gates_ref, kvc_ref, skv_ref, wkv_ref, tb_ref, ctab_ref, o_ref,
                     s_ref, sw_ref, mx_ref, mb_ref, acc_ref, *, n_blocks):
    i = pl.program_id(2)
    a = i // 2
    odd = i % 2
    rows = GQA * TILE
    lane = lax.broadcasted_iota(jnp.int32, (rows, TILE), 1)
    low = lane < HEAD_DIM
    key_lane = lax.broadcasted_iota(jnp.int32, (PAIR, TILE), 1)
    key_blk = lax.broadcasted_iota(jnp.int32, (PAIR, TILE), 0) // BLK
    n_pairs = skv_ref.shape[0] // PAIR

    qf = q_ref[...]
    q_pad = []
    for g in range(GQA):
        pair = qf[:, (g // 2) * TILE:(g // 2 + 1) * TILE]
        q_pad.append(pair if g % 2 == 0 else pltpu.roll(pair, HEAD_DIM, 1))
    q_all = jnp.concatenate(q_pad, axis=0)
    lhs_plain = jnp.where(low, q_all, 0.0).astype(BF16)

    def ones_and_values(kv):
        kv_lane = lax.broadcasted_iota(jnp.int32, kv.shape, 1)
        return jnp.where(kv_lane < HEAD_DIM, jnp.ones_like(kv), kv)

    def normalise(acc):
        return acc / jnp.maximum(pltpu.roll(acc, HEAD_DIM, 1), 1e-30)

    kvc = kvc_ref[...]
    shift = (2 * i + (TILE - (BLK - 1))) % TILE
    bias = jnp.concatenate([pltpu.roll(ctab_ref[g * TILE:(g + 1) * TILE, :], shift, 1) for g in range(GQA)], axis=0)
    s = lax.dot_general(lhs_plain, kvc, NT_DIMS, preferred_element_type=F32)
    s = jnp.where(lane < n_blocks, s + bias, MASK_VALUE)
    m = jnp.maximum(jnp.max(s, axis=1, keepdims=True), M_INIT)
    p = jnp.exp2(s - m)
    c_acc = jnp.dot(p.astype(BF16), ones_and_values(kvc), preferred_element_type=F32)
    p = p / jnp.maximum(c_acc, 1e-30)
    o_cmp = normalise(c_acc)
    imp = p[0:TILE]
    for g in range(1, GQA):
        imp = imp + p[g * TILE:(g + 1) * TILE]

    blk = lax.broadcasted_iota(jnp.int32, (n_blocks, TILE), 0)
    query = lax.broadcasted_iota(jnp.int32, (n_blocks, TILE), 1)
    cur = 2 * i + jnp.where(query >= BLK, 1, 0)
    forced = (blk == 0) | (blk == cur) | (blk == cur - 1)
    score = jnp.where(forced, jnp.inf, jnp.where(blk <= cur, imp.T[0:n_blocks], -jnp.inf))
    neg_t = _selection_mask(score, blk, n_blocks)
    pieces = [jnp.zeros((HEAD_DIM, TILE), F32), neg_t]
    if n_blocks < TILE - HEAD_DIM:
        pieces.append(jnp.zeros((TILE - HEAD_DIM - n_blocks, TILE), F32))
    neg = jnp.concatenate(pieces, axis=0).T
    lhs_sel = jnp.where(low, q_all, jnp.concatenate([neg] * GQA, axis=0)).astype(BF16)

    def sel_kinds(j):
        last = j == a
        left = jnp.where(last, jnp.where(odd == 1, KIND_PREV, KIND_DIAG), KIND_ZERO)
        right = jnp.where(last, jnp.where(odd == 1, KIND_DIAG, KIND_MASKED),
                          jnp.where((j == a - 1) & (odd == 0), KIND_PREV, KIND_ZERO))
        return left, right

    def win_kinds(j):
        left, right = sel_kinds(j)
        first = j == a - WINDOW // PAIR
        return (jnp.where(first, jnp.where(odd == 1, KIND_MASKED, KIND_OLDEST), left),
                jnp.where(first, jnp.where(odd == 1, KIND_OLDEST, KIND_ZERO), right))

    def keys_of(kv_ref, j):
        jc = jnp.clip(j, 0, n_pairs - 1)
        return kv_ref[pl.ds(pl.multiple_of(jc * PAIR, PAIR), PAIR), :]

    def score_pair(lhs, kv_ref, j, valid, kinds, block_columns):
        rhs = keys_of(kv_ref, j)
        if block_columns:
            onehot = jnp.where(key_lane - HEAD_DIM == (PAIR // BLK) * j + key_blk, 1.0, 0.0).astype(BF16)
            rhs = jnp.where(key_lane < HEAD_DIM, rhs, onehot)
        sc = lax.dot_general(lhs, rhs, NT_DIMS, preferred_element_type=F32)
        left, right = kinds(j)
        left = jnp.where(valid, left, KIND_MASKED)
        right = jnp.where(valid, right, KIND_MASKED)
        return sc + jnp.concatenate([tb_ref[left], tb_ref[right]], axis=1)

    def half_max(sc):
        return jnp.maximum(sc[:, :TILE], sc[:, TILE:])

    def weigh_pair(sc, mb, kv_ref, j):
        pr = jnp.exp2(sc - jnp.concatenate([mb, mb], axis=1))
        return jnp.dot(pr.astype(BF16), ones_and_values(keys_of(kv_ref, j)), preferred_element_type=F32)

    def row_max(mx):
        return jnp.broadcast_to(jnp.max(mx, axis=1, keepdims=True), (rows, TILE))

    w_pairs = [a - WINDOW // PAIR + w for w in range(WINDOW // PAIR + 1)]
    w_mx = None
    for w, j in enumerate(w_pairs):
        sc = score_pair(lhs_plain, wkv_ref, j, j >= 0, win_kinds, False)
        sw_ref[w] = sc
        w_mx = half_max(sc) if w_mx is None else jnp.maximum(w_mx, half_max(sc))
    w_mb = row_max(jnp.maximum(w_mx, M_INIT))
    w_acc = None
    for w, j in enumerate(w_pairs):
        dacc = weigh_pair(sw_ref[w], w_mb, wkv_ref, j)
        w_acc = dacc if w_acc is None else w_acc + dacc
    o_win = normalise(w_acc)

    n_trips = (a + SEL_UNROLL) // SEL_UNROLL
    mx_ref[...] = jnp.full(mx_ref.shape, M_INIT, F32)

    def score_trip(t, carry):
        mx = None
        for u in range(SEL_UNROLL):
            j = t * SEL_UNROLL + u
            sc = score_pair(lhs_sel, skv_ref, j, j <= a, sel_kinds, True)
            s_ref[j] = sc
            mx = half_max(sc) if mx is None else jnp.maximum(mx, half_max(sc))
        mx_ref[...] = jnp.maximum(mx_ref[...], mx)
        return carry

    lax.fori_loop(0, n_trips, score_trip, 0)
    mb_ref[...] = row_max(mx_ref[...])
    acc_ref[...] = jnp.zeros(acc_ref.shape, F32)

    def weigh_trip(t, carry):
        mb = mb_ref[...]
        acc = None
        for u in range(SEL_UNROLL):
            j = t * SEL_UNROLL + u
            dacc = weigh_pair(s_ref[j], mb, skv_ref, j)
            acc = dacc if acc is None else acc + dacc
        acc_ref[...] += acc
        return carry

    lax.fori_loop(0, n_trips, weigh_trip, 0)
    o_sel = normalise(acc_ref[...])

    gate = jax.nn.sigmoid(gates_ref[...])
    outs = []
    for g in range(GQA):
        def gcol(br):
            c = g * N_BRANCH + br
            return gate[:, c:c + 1]
        head = slice(g * TILE, (g + 1) * TILE)
        outs.append(gcol(0) * o_cmp[head] + gcol(1) * o_sel[head] + gcol(2) * o_win[head])
    out_low = lax.broadcasted_iota(jnp.int32, (TILE, TILE), 1) < HEAD_DIM
    for j in range(GQA // 2):
        o_ref[:, j * TILE:(j + 1) * TILE] = jnp.where(out_low, pltpu.roll(outs[2 * j], HEAD_DIM, 1), outs[2 * j + 1])


def _nsa_prompt(q, gates, kvc, skv_b, wkv_b, tiles, cmp_tab):
    b, s, _ = q.shape
    nq = s // TILE
    n_blocks = s // BLK
    assert n_blocks <= TILE - HEAD_DIM and kvc.shape[1] == TILE and s % PAIR == 0
    gw = GQA * HEAD_DIM
    rows = GQA * TILE
    return pl.pallas_call(
        functools.partial(_nsa_prompt_body, n_blocks=n_blocks), grid=(b, N_KV_HEADS, nq),
        in_specs=[pl.BlockSpec((None, TILE, gw), lambda bi, k, i: (bi, i, k)),
                  pl.BlockSpec((None, TILE, TILE), lambda bi, k, i: (bi, i, k)),
                  pl.BlockSpec((None, TILE, KV_GROUP), lambda bi, k, i: (bi, 0, k)),
                  pl.BlockSpec((None, s, KV_GROUP), lambda bi, k, i: (bi, 0, k)),
                  pl.BlockSpec((None, s, KV_GROUP), lambda bi, k, i: (bi, 0, k)),
                  pl.BlockSpec((len(tiles), rows, TILE), lambda bi, k, i: (0, k, 0)),
                  pl.BlockSpec((rows, TILE), lambda bi, k, i: (k, 0))],
        out_specs=pl.BlockSpec((None, TILE, gw), lambda bi, k, i: (bi, i, k)),
        out_shape=jax.ShapeDtypeStruct((b, s, ATT_WIDTH), F32),
        scratch_shapes=[pltpu.VMEM((s // PAIR + SEL_UNROLL - 1, rows, PAIR), F32),
                        pltpu.VMEM((WINDOW // PAIR + 1, rows, PAIR), F32),
                        pltpu.VMEM((rows, TILE), F32), pltpu.VMEM((rows, TILE), F32),
                        pltpu.VMEM((rows, KV_GROUP), F32)],
        compiler_params=_params(3), name="nsa_prompt")(q, gates, kvc, skv_b, wkv_b, tiles, cmp_tab)


def _t_softmax_step(kv, qbd, bias, m_ref, l_ref, acc_ref):
    s = jnp.dot(kv, qbd, preferred_element_type=F32) + bias
    m_prev = m_ref[...]
    m_new = jnp.maximum(m_prev, jnp.max(s, axis=0, keepdims=True))
    alpha = jnp.exp2(m_prev - m_new)
    p = jnp.exp2(s - m_new)
    l_ref[...] = alpha * l_ref[...] + jnp.sum(p, axis=0, keepdims=True)
    acc_ref[...] = alpha * acc_ref[...] + lax.dot_general(kv, p.astype(BF16), TN_DIMS,
                                                          preferred_element_type=F32)
    m_ref[...] = m_new


def _nsa_sample_body(pt_ref, *refs, n_pages, dec_seq, past_len):
    page_refs = refs[:PAGES_PER_STEP]
    (qbd_ref, gates_ref, kc_ref, ctab_ref, gsum_ref, snew_ref, cwin_ref, wnew_ref, last_ref, new_ref, far_ref,
     win0_ref, o_ref, neg_ref, m_ref, l_ref, acc_ref, oc_ref, ow_ref) = refs[PAGES_PER_STEP:]
    j = pl.program_id(1)
    n_past = n_pages * (PAGE_SIZE // BLK)
    n_rows = kc_ref.shape[0]
    sub = lax.broadcasted_iota(jnp.int32, (TILE, TILE), 0)
    qbd = qbd_ref[...]
    far = far_ref[0:1, :]

    def reset():
        m_ref[...] = jnp.full(m_ref.shape, M_INIT, F32)
        l_ref[...] = jnp.zeros(l_ref.shape, F32)
        acc_ref[...] = jnp.zeros(acc_ref.shape, F32)

    def result():
        return acc_ref[...] / jnp.maximum(l_ref[...], 1e-30)

    @pl.when(j == 0)
    def _():
        kc = kc_ref[...]
        s = jnp.dot(kc, qbd, preferred_element_type=F32) + ctab_ref[...]
        m = jnp.maximum(jnp.max(s, axis=0, keepdims=True), M_INIT)
        p = jnp.exp2(s - m)
        p = p / jnp.maximum(jnp.sum(p, axis=0, keepdims=True), 1e-30)
        oc_ref[...] = lax.dot_general(kc, p.astype(BF16), TN_DIMS, preferred_element_type=F32)
        imp = jnp.dot(p, gsum_ref[...], preferred_element_type=F32, precision=lax.Precision.HIGHEST)
        blk = lax.broadcasted_iota(jnp.int32, (n_rows, TILE), 0)
        tok = lax.broadcasted_iota(jnp.int32, (n_rows, TILE), 1) % dec_seq
        cur = (past_len + tok) // BLK
        forced = (blk == 0) | (blk == cur) | (blk == cur - 1)
        score = jnp.where(forced, jnp.inf, jnp.where(blk <= cur, imp, -jnp.inf))
        neg_ref[...] = _selection_mask(score, blk, n_past + 1)

        reset()
        n_win = cwin_ref.shape[-1] // TILE
        for t in range(n_win):
            kv = jnp.concatenate([_token_rows(cwin_ref, k, t * TILE, TILE) for k in range(N_KV_HEADS)],
                                 axis=1).astype(BF16)
            bias = win0_ref[...] if t == 0 else (last_ref[...] if t == n_win - 1 else far)
            _t_softmax_step(kv, qbd, bias, m_ref, l_ref, acc_ref)
        _t_softmax_step(wnew_ref[...], qbd, new_ref[...], m_ref, l_ref, acc_ref)
        ow_ref[...] = result()
        reset()

    blocks_per_step = PAGES_PER_STEP * (PAGE_SIZE // BLK)
    neg_rows = neg_ref[pl.ds(pl.multiple_of(j * blocks_per_step, blocks_per_step), blocks_per_step), :]
    for p in range(PAGES_PER_STEP):
        page = j * PAGES_PER_STEP + p
        kv = jnp.concatenate([_token_rows(page_refs[p], k) for k in range(N_KV_HEADS)], axis=1).astype(BF16)
        first = neg_rows[2 * p:2 * p + 1, :]
        second = neg_rows[2 * p + 1:2 * p + 2, :]
        bias = jnp.where(sub < BLK, first, second) + jnp.where(page == n_pages - 1, last_ref[...], far)
        _t_softmax_step(kv, qbd, bias, m_ref, l_ref, acc_ref)

    @pl.when(j == pl.num_programs(1) - 1)
    def _():
        _t_softmax_step(snew_ref[...], qbd, new_ref[...] + neg_ref[n_past:n_past + 1, :], m_ref, l_ref, acc_ref)
        gate = jax.nn.sigmoid(gates_ref[...])
        o_ref[...] = gate[0:1, :] * oc_ref[...] + gate[1:2, :] * result() + gate[2:3, :] * ow_ref[...]


def _nsa_sample(page_table, cache_slc, qbd, gates_t, kc, ctab, gsum, slc_new, cache_win, win_new, last_tab,
                new_tab, far_row, win0_tab, *, dec_seq, past_len):
    b, n_pages = page_table.shape
    steps = n_pages // PAGES_PER_STEP
    n_rows = kc.shape[1]
    w_buf = cache_win.shape[-1]
    n_new = slc_new.shape[1]

    def page_spec(p):
        return pl.BlockSpec((None, N_KV_HEADS, 2, HEAD_DIM, PAGE_SIZE),
                            lambda i, j, pt: (pt[i, j * PAGES_PER_STEP + p], 0, 0, 0, 0))

    def per_batch(*shape):
        zeros = (0,) * len(shape)
        return pl.BlockSpec((None,) + shape, lambda i, j, pt: (i,) + zeros)

    def const(*shape):
        zeros = (0,) * len(shape)
        return pl.BlockSpec(shape, lambda i, j, pt: zeros)

    grid_spec = pltpu.PrefetchScalarGridSpec(
        num_scalar_prefetch=1, grid=(b, steps),
        in_specs=[page_spec(p) for p in range(PAGES_PER_STEP)]
        + [per_batch(KV_WIDTH, TILE), per_batch(SUBLANES, TILE), per_batch(n_rows, KV_WIDTH), const(n_rows, TILE),
           const(TILE, TILE), per_batch(n_new, KV_WIDTH), per_batch(N_KV_HEADS, 2, HEAD_DIM, w_buf), per_batch(n_new, KV_WIDTH),
           const(TILE, TILE), const(n_new, TILE), const(SUBLANES, TILE), const(TILE, TILE)],
        out_specs=per_batch(KV_WIDTH, TILE),
        scratch_shapes=[pltpu.VMEM((n_rows, TILE), F32), pltpu.VMEM((1, TILE), F32), pltpu.VMEM((1, TILE), F32),
                        pltpu.VMEM((KV_WIDTH, TILE), F32), pltpu.VMEM((KV_WIDTH, TILE), F32),
                        pltpu.VMEM((KV_WIDTH, TILE), F32)])
    return pl.pallas_call(
        functools.partial(_nsa_sample_body, n_pages=n_pages, dec_seq=dec_seq, past_len=past_len),
        grid_spec=grid_spec, out_shape=jax.ShapeDtypeStruct((b, KV_WIDTH, TILE), F32),
        compiler_params=_params(2), name="nsa_sample")(
            page_table, *([cache_slc] * PAGES_PER_STEP), qbd, gates_t, kc, ctab, gsum, slc_new, cache_win, win_new,
            last_tab, new_tab, far_row, win0_tab)


def _softplus(x):
    return jnp.maximum(x, 0.0) + jnp.log(1.0 + jnp.exp(-jnp.abs(x)))


def _ssd_body(z_ref, xbc_ref, dt_ref, dtt_ref, convp_ref, h0_ref, cw_ref, cb_ref, dtb_ref, dtbt_ref, al_ref,
              alt_ref, dskip_ref, norm_ref, o_ref, hout_ref, convout_ref, xp_ref, *, cl):
    c = pl.program_id(1)
    pad = SUBLANES

    @pl.when(c == 0)
    def _():
        xp_ref[0:pad, :] = convp_ref[...]
        hout_ref[...] = h0_ref[...]

    xp_ref[pad:pad + cl, :] = xbc_ref[...]
    conv = cb_ref[...]
    for k in range(CONV_W):
        lo = pad - (CONV_W - 1) + k
        conv = conv + cw_ref[k:k + 1, :] * xp_ref[lo:lo + cl, :]
    u = conv * jax.nn.sigmoid(conv)
    tail = xp_ref[pad + cl - (CONV_W - 1):pad + cl, :]
    convout_ref[...] = tail
    xp_ref[pad - (CONV_W - 1):pad, :] = tail

    xs = u[:, :SSD_INNER]
    dt = _softplus(dt_ref[:, 0:SSD_HEADS] + dtb_ref[...])
    dt_t = _softplus(dtt_ref[0:SSD_HEADS, :] + dtbt_ref[...])
    la = dt * -jnp.exp(al_ref[...])
    la_t = dt_t * -jnp.exp(alt_ref[...])
    row = lax.broadcasted_iota(jnp.int32, (cl, cl), 0)
    col = lax.broadcasted_iota(jnp.int32, (cl, cl), 1)
    causal = row >= col
    hi = lax.Precision.HIGHEST
    cs = jnp.dot(jnp.where(causal, 1.0, 0.0), la, preferred_element_type=F32, precision=hi)
    cs_t = jnp.dot(la_t, jnp.where(row <= col, 1.0, 0.0), preferred_element_type=F32, precision=hi)

    heads_per_group = SSD_HEADS // SSD_GROUPS
    ys = []
    for g in range(SSD_GROUPS):
        b_lo = SSD_INNER + g * SSD_STATE
        c_lo = SSD_INNER + SSD_GROUPS * SSD_STATE + g * SSD_STATE
        bm = u[:, b_lo:b_lo + SSD_STATE].astype(BF16)
        cm = u[:, c_lo:c_lo + SSD_STATE].astype(BF16)
        cb = lax.dot_general(cm, bm, NT_DIMS, preferred_element_type=F32)
        for r in range(heads_per_group):
            h = g * heads_per_group + r
            cs_col = cs[:, h:h + 1]
            diff = cs_col - cs_t[h:h + 1, :]
            decay = jnp.where(causal, jnp.exp(jnp.where(causal, diff, 0.0)), 0.0)
            xs_h = xs[:, h * SSD_HEAD_DIM:(h + 1) * SSD_HEAD_DIM]
            xdt = xs_h * dt[:, h:h + 1]
            y = jnp.dot((cb * decay).astype(BF16), xdt.astype(BF16), preferred_element_type=F32)
            last = cs[cl - 1:cl, h:h + 1]
            w = (xdt * jnp.exp(last - cs_col)).astype(BF16)
            state_in = lax.dot_general(w, bm, TN_DIMS, preferred_element_type=F32)
            h_prev = hout_ref[h]
            y = y + lax.dot_general(cm, h_prev.astype(BF16), NT_DIMS, preferred_element_type=F32) * jnp.exp(cs_col)
            hout_ref[h] = h_prev * jnp.exp(last) + state_in
            ys.append(y + dskip_ref[:, h * SSD_HEAD_DIM:(h + 1) * SSD_HEAD_DIM] * xs_h)
    y = jnp.concatenate(ys, axis=1)
    z = z_ref[...]
    y = y * (z * jax.nn.sigmoid(z))
    gw = SSD_INNER // SSD_GROUPS
    normed = []
    for g in range(SSD_GROUPS):
        yg = y[:, g * gw:(g + 1) * gw]
        normed.append(yg * lax.rsqrt(jnp.mean(yg * yg, axis=-1, keepdims=True) + EPS))
    o_ref[...] = jnp.concatenate(normed, axis=1) * norm_ref[...]


def _ssd(z, xbc, dt, dt_t, conv_prev, h0, consts, cl):
    b, seq, _ = z.shape
    nc = seq // cl
    cw, cbias, dtb, dtb_t, al, al_t, dskip, norm = consts

    def rows(width):
        return pl.BlockSpec((None, cl, width), lambda i, c: (i, c, 0))

    state = pl.BlockSpec((None, SSD_HEADS, SSD_HEAD_DIM, SSD_STATE), lambda i, c: (i, 0, 0, 0))

    def const(x):
        zeros = (0,) * x.ndim
        return pl.BlockSpec(x.shape, lambda i, c: zeros)

    return pl.pallas_call(
        functools.partial(_ssd_body, cl=cl), grid=(b, nc),
        in_specs=[rows(SSD_INNER), rows(CONV_DIM), rows(TILE),
                  pl.BlockSpec((None, TILE, cl), lambda i, c: (i, 0, c)),
                  pl.BlockSpec((None, SUBLANES, CONV_DIM), lambda i, c: (i, 0, 0)), state,
                  const(cw), const(cbias), const(dtb), const(dtb_t), const(al), const(al_t), const(dskip),
                  const(norm)],
        out_specs=[rows(SSD_INNER), state, pl.BlockSpec((None, CONV_W - 1, CONV_DIM), lambda i, c: (i, 0, 0))],
        out_shape=[jax.ShapeDtypeStruct((b, seq, SSD_INNER), F32),
                   jax.ShapeDtypeStruct((b, SSD_HEADS, SSD_HEAD_DIM, SSD_STATE), F32),
                   jax.ShapeDtypeStruct((b, CONV_W - 1, CONV_DIM), F32)],
        scratch_shapes=[pltpu.VMEM((SUBLANES + cl, CONV_DIM), F32)],
        compiler_params=_params(2), name="ssd")(z, xbc, dt, dt_t, conv_prev, h0, cw, cbias, dtb, dtb_t, al, al_t,
                                                 dskip, norm)


def _pack_inproj(w_in):
    splits = (ATT_WIDTH, KV_WIDTH, KV_WIDTH, KV_WIDTH, N_BRANCH * N_ATT_HEADS, SSD_INNER, CONV_DIM, SSD_HEADS)
    offs = [0]
    for s in splits:
        offs.append(offs[-1] + s)
    wq, wc, ws, ww, wg, wz, wx, wd = [w_in[:, offs[i]:offs[i + 1]] for i in range(len(splits))]
    d = w_in.shape[0]
    per_kv = GQA * N_BRANCH
    wg4 = jnp.pad(wg.reshape(d, N_KV_HEADS, per_kv), ((0, 0), (0, 0), (0, TILE - per_kv)))
    wd_pad = jnp.pad(wd, ((0, 0), (0, TILE - SSD_HEADS)))
    w = jnp.concatenate([wq * (HEAD_DIM ** -0.5 * LOG2E), wc, ws, ww, wg4.reshape(d, N_KV_HEADS * TILE), wz, wx, wd_pad],
                        axis=1)
    assert w.shape[1] == PROJ_WIDTH
    w_t = jnp.concatenate([wd_pad, wc, ws, ww], axis=1).T
    return w.astype(BF16), w_t.astype(BF16)


def _pack_compress(cmp_pe, w_cmp):
    zero = jnp.zeros((BLK, HEAD_DIM, HEAD_DIM), w_cmp.dtype)
    bd = jnp.concatenate([jnp.concatenate([w_cmp[:, 0], zero], axis=2),
                          jnp.concatenate([zero, w_cmp[:, 1]], axis=2)], axis=1)
    pe = cmp_pe.reshape(BLK, KV_GROUP)
    return pe.astype(F32), bd.astype(BF16)


def _sample_tables(rel_table, dec_seq, past_len, n_rows, n_new, w_buf):
    col = jnp.arange(TILE, dtype=jnp.int32)
    head = (col // (GQA * dec_seq)) * GQA + (col // dec_seq) % GQA
    tok = (col % dec_seq)[None, :]
    table = rel_table.astype(F32) * LOG2E

    def bias(dist, visible):
        return jnp.where(visible, _lookup(table[:, head][:, None, :], _rel_bucket(dist)), MASK_VALUE)

    row = jnp.arange(TILE, dtype=jnp.int32)[:, None]
    blk = jnp.arange(n_rows, dtype=jnp.int32)[:, None]
    c_dist = past_len + tok - ((blk + 1) * BLK - 1)
    ctab = bias(c_dist, c_dist >= 0)
    last_tab = bias(TILE + tok - row, jnp.ones((TILE, TILE), bool))
    new_row = jnp.arange(n_new, dtype=jnp.int32)[:, None]
    new_tab = bias(tok - new_row, (tok >= new_row) & (new_row < dec_seq))
    far = jnp.full((SUBLANES, TILE), w_buf, jnp.int32)
    far_row = bias(far, jnp.ones((SUBLANES, TILE), bool))
    win0_tab = bias(jnp.full((TILE, TILE), w_buf, jnp.int32), row > tok)
    same = (col[:, None] // (GQA * dec_seq) == col[None, :] // (GQA * dec_seq)) & \
           (col[:, None] % dec_seq == col[None, :] % dec_seq)
    return ctab, last_tab, new_tab, far_row, win0_tab, same.astype(F32)


def _layer_weights(ffn_pre, ffn_post, ffn_w_in, ffn_w_out):
    f = ffn_w_out.shape[0]
    return (ffn_pre[None, :], ffn_post[None, :], ffn_w_in[:, :f].astype(BF16), ffn_w_in[:, f:].astype(BF16),
            ffn_w_out.astype(BF16))


def kernel(x_prompt, x_sample, cache_cmp_kv, cache_slc_kv, cache_win_kv, state_ssm, state_conv, page_table,
           rel_table, ffn1_pre, ffn1_post, ffn1_w_in, ffn1_w_out, mix_pre, mix_post, w_in, w_out, att_out_norm,
           cmp_pe, w_cmp, conv_w, conv_b, dt_bias, a_log, d_skip, ssd_norm,
           ffn2_pre, ffn2_post, ffn2_w_in, ffn2_w_out):
    depth = ffn1_pre.shape[0]
    bp, seq, d = x_prompt.shape
    bs, dec_seq, _ = x_sample.shape
    n_pool = cache_cmp_kv.shape[1]
    n_pages = page_table.shape[1]
    past_len = n_pages * PAGE_SIZE
    w_buf = cache_win_kv.shape[2]
    kv_shape = (N_KV_HEADS, 2, HEAD_DIM)
    assert GQA * dec_seq * N_KV_HEADS == TILE and w_buf == WINDOW and seq % TILE == 0

    yp = x_prompt.reshape(bp * seq, d)
    ys = x_sample.reshape(bs * dec_seq, d)
    outs = [[] for _ in range(10)]
    for l in range(depth):
        ffn1 = _layer_weights(ffn1_pre[l], ffn1_post[l], ffn1_w_in[l], ffn1_w_out[l])
        ffn2 = _layer_weights(ffn2_pre[l], ffn2_post[l], ffn2_w_in[l], ffn2_w_out[l])
        w_proj, w_proj_t = _pack_inproj(w_in[l])
        pe, bd = _pack_compress(cmp_pe[l], w_cmp[l])
        wo_att = w_out[l][:ATT_WIDTH].astype(BF16)
        wo_ssd = w_out[l][ATT_WIDTH:].astype(BF16)
        ssd_consts = (conv_w[l], conv_b[l][None, :], dt_bias[l][None, :], dt_bias[l][:, None], a_log[l][None, :],
                      a_log[l][:, None], jnp.repeat(d_skip[l], SSD_HEAD_DIM)[None, :], ssd_norm[l][None, :])
        mix_pre_l, mix_post_l, att_norm_l = mix_pre[l][None, :], mix_post[l][None, :], att_out_norm[l][None, :]

        yp = _ffn(yp, *ffn1)
        (q, ckv, skv, wkv, skv_b, wkv_b, gates, z, xbc, dt, dt_t, ckv_t, skv_t, wkv_t) = _inproj(
            yp, mix_pre_l, w_proj, w_proj_t, bp)
        kvc = _compress(ckv.reshape(bp, seq, KV_WIDTH), pe, bd, TILE)
        tiles, cmp_tab = _prompt_bias_tables(rel_table)
        o_att = _nsa_prompt(q.reshape(bp, seq, ATT_WIDTH), gates.reshape(bp, seq, N_KV_HEADS * TILE), kvc,
                            skv_b.reshape(bp, seq, KV_WIDTH), wkv_b.reshape(bp, seq, KV_WIDTH), tiles, cmp_tab)
        cl = min(CHUNK, seq)
        o_ssd, h_new, conv_new = _ssd(
            z.reshape(bp, seq, SSD_INNER), xbc.reshape(bp, seq, CONV_DIM), dt.reshape(bp, seq, TILE),
            dt_t, jnp.zeros((bp, SUBLANES, CONV_DIM), F32),
            jnp.zeros((bp, SSD_HEADS, SSD_HEAD_DIM, SSD_STATE), F32), ssd_consts, cl)
        yp = _mixout(yp, o_att.reshape(bp * seq, ATT_WIDTH), o_ssd.reshape(bp * seq, SSD_INNER), att_norm_l,
                     mix_post_l, wo_att, wo_ssd)
        def token_major(a_t):
            return a_t.reshape((bp,) + kv_shape + (a_t.shape[-1],)).transpose(0, 4, 1, 2, 3)

        outs[0].append(token_major(ckv_t))
        outs[1].append(token_major(skv_t))
        outs[2].append(token_major(wkv_t[:, :, -min(WINDOW, seq):]))
        outs[3].append(h_new)
        outs[4].append(conv_new)

        ys = _ffn(ys, *ffn1)
        q, ckv, skv, wkv, skv_b, wkv_b, gates, z, xbc, dt, dt_t, _, _, _ = _inproj(ys, mix_pre_l, w_proj, w_proj_t, 1)
        native = (0, 2, 3, 4, 1)
        kc_past = _compress_paged(cache_cmp_kv[l].transpose(native), page_table, pe, bd)
        cmp_new = jnp.pad(ckv.reshape(bs, dec_seq, KV_WIDTH), ((0, 0), (0, BLK - dec_seq), (0, 0)))
        kc_new = _compress(cmp_new.reshape(1, bs * BLK, KV_WIDTH), pe, bd, bs).reshape(bs, 1, KV_WIDTH)
        n_past = kc_past.shape[1]
        n_rows = -(-(n_past + 1) // 16) * 16
        kc = jnp.concatenate([kc_past, kc_new, jnp.zeros((bs, n_rows - n_past - 1, KV_WIDTH), BF16)], axis=1)
        n_new = 16
        ctab, last_tab, new_tab, far_row, win0_tab, gsum = _sample_tables(rel_table, dec_seq, past_len, n_rows,
                                                                          n_new, w_buf)
        q5 = q.reshape(bs, dec_seq, N_KV_HEADS, GQA, HEAD_DIM).transpose(0, 2, 4, 3, 1)
        q5 = q5.reshape(bs, N_KV_HEADS, HEAD_DIM, GQA * dec_seq)
        eye = jnp.eye(N_KV_HEADS, dtype=F32)[None, :, None, :, None]
        qbd = jnp.pad(q5, ((0, 0), (0, 0), (0, HEAD_DIM), (0, 0)))[:, :, :, None, :] * eye
        qbd = qbd.reshape(bs, KV_WIDTH, TILE).astype(BF16)
        g5 = gates.reshape(bs, dec_seq, N_KV_HEADS, TILE)[..., :GQA * N_BRANCH]
        g5 = g5.reshape(bs, dec_seq, N_KV_HEADS, GQA, N_BRANCH).transpose(0, 4, 2, 3, 1).reshape(bs, N_BRANCH, TILE)
        gates_t = jnp.pad(g5, ((0, 0), (0, SUBLANES - N_BRANCH), (0, 0)))

        def pad_new(a):
            return jnp.pad(a.reshape(bs, dec_seq, KV_WIDTH), ((0, 0), (0, n_new - dec_seq), (0, 0)))

        o_t = _nsa_sample(page_table, cache_slc_kv[l].transpose(native), qbd, gates_t, kc, ctab,
                          gsum, pad_new(skv_b), cache_win_kv[l].transpose(native), pad_new(wkv_b),
                          last_tab, new_tab, far_row, win0_tab, dec_seq=dec_seq, past_len=past_len)
        o7 = o_t.reshape(bs, N_KV_HEADS, 2, HEAD_DIM, N_KV_HEADS, GQA, dec_seq)
        o_att = jnp.stack([o7[:, k, 1, :, k] for k in range(N_KV_HEADS)], axis=1).transpose(0, 4, 1, 3, 2)
        o_att = o_att.reshape(bs * dec_seq, ATT_WIDTH)
        conv_prev = jnp.pad(state_conv[l], ((0, 0), (SUBLANES - (CONV_W - 1), 0), (0, 0)))
        o_ssd, h_new, conv_new = _ssd(
            z.reshape(bs, dec_seq, SSD_INNER), xbc.reshape(bs, dec_seq, CONV_DIM), dt.reshape(bs, dec_seq, TILE),
            dt_t.reshape(TILE, bs, dec_seq).transpose(1, 0, 2), conv_prev, state_ssm[l], ssd_consts,
            min(CHUNK, dec_seq))
        ys = _mixout(ys, o_att, o_ssd.reshape(bs * dec_seq, SSD_INNER), att_norm_l, mix_post_l, wo_att, wo_ssd)
        win_all = jnp.concatenate([cache_win_kv[l], wkv.reshape((bs, dec_seq) + kv_shape)], axis=1)
        outs[5].append(ckv.reshape((bs, dec_seq) + kv_shape))
        outs[6].append(skv.reshape((bs, dec_seq) + kv_shape))
        outs[7].append(win_all[:, -min(WINDOW, w_buf + dec_seq):])
        outs[8].append(h_new)
        outs[9].append(conv_new)

        yp = _ffn(yp, *ffn2)
        ys = _ffn(ys, *ffn2)

    return (yp.reshape(bp, seq, d), ys.reshape(bs, dec_seq, d)) + tuple(jnp.stack(o) for o in outs)
```

```python
import functools
import math

import jax
import jax.numpy as jnp
from jax import lax
from jax.experimental import pallas as pl
from jax.experimental.pallas import tpu as pltpu

F32 = jnp.float32
BF16 = jnp.bfloat16

D_MODEL = 1024
N_ATT_HEADS = 16
HEAD_DIM = 64
N_KV_HEADS = 4
GQA = N_ATT_HEADS // N_KV_HEADS
ATT_WIDTH = N_ATT_HEADS * HEAD_DIM
KV_WIDTH = 2 * N_KV_HEADS * HEAD_DIM
KV_GROUP = 2 * HEAD_DIM
BLK = 64
N_SEL = 16
WINDOW = 512
N_BRANCH = 3
N_BUCKETS = 32
MAX_DISTANCE = 128
SSD_HEADS = 16
SSD_HEAD_DIM = 64
SSD_INNER = SSD_HEADS * SSD_HEAD_DIM
SSD_GROUPS = 2
SSD_STATE = 128
CONV_W = 4
CONV_DIM = SSD_INNER + 2 * SSD_GROUPS * SSD_STATE
CHUNK = 128
PAGE_SIZE = 128
EPS = 1e-6

TILE = 128
SUBLANES = 8
MASK_VALUE = -(2.0 ** 100)
M_INIT = -1e30
LOG2E = math.log2(math.e)
VMEM_LIMIT = 56 * 1024 * 1024
PAGES_PER_STEP = 8
COMPRESS_ROWS_PER_DOT = 16
STAGE_PITCH = 72

NT_DIMS = (((1,), (1,)), ((), ()))
TN_DIMS = (((0,), (0,)), ((), ()))


def _params(n_grid_dims):
    return pltpu.CompilerParams(dimension_semantics=("arbitrary",) * n_grid_dims,
                                vmem_limit_bytes=VMEM_LIMIT)


def _rms(x, g):
    return x * lax.rsqrt(jnp.mean(x * x, axis=-1, keepdims=True) + EPS) * g


def _const_spec(shape):
    zeros = (0,) * len(shape)
    return pl.BlockSpec(shape, lambda *_: zeros, pipeline_mode=pl.Buffered(1))


def _ffn_body(x_ref, pre_ref, post_ref, wg_ref, wu_ref, wo_ref, o_ref):
    x = x_ref[...]
    h = _rms(x, pre_ref[...]).astype(BF16)
    gate = jnp.dot(h, wg_ref[...], preferred_element_type=F32)
    up = jnp.dot(h, wu_ref[...], preferred_element_type=F32)
    a = (gate * jax.nn.sigmoid(gate) * up).astype(BF16)
    y = jnp.dot(a, wo_ref[...], preferred_element_type=F32)
    o_ref[...] = x + 0.5 * _rms(y, post_ref[...])


def _ffn(x, pre, post, wg, wu, wo):
    t, d = x.shape
    f = wg.shape[1]
    tm = min(256, t)
    row = pl.BlockSpec((tm, d), lambda i: (i, 0))
    return pl.pallas_call(
        _ffn_body, grid=(t // tm,),
        in_specs=[row, _const_spec((1, d)), _const_spec((1, d)), _const_spec((d, f)), _const_spec((d, f)),
                  _const_spec((f, d))],
        out_specs=row, out_shape=jax.ShapeDtypeStruct((t, d), F32),
        compiler_params=_params(1), name="ffn")(x, pre, post, wg, wu, wo)


_SEG = {}
_off = 0
for _name, _w in (("q", ATT_WIDTH), ("ckv", KV_WIDTH), ("skv", KV_WIDTH), ("wkv", KV_WIDTH),
                  ("gates", N_KV_HEADS * TILE), ("z", SSD_INNER), ("xbc", CONV_DIM), ("dt", TILE)):
    _SEG[_name] = (_off, _off + _w)
    _off += _w
PROJ_WIDTH = _off


def _inproj_body(x_ref, pre_ref, w_ref, wt_ref, q_ref, ckv_ref, skv_ref, wkv_ref, skvb_ref, wkvb_ref,
                 gates_ref, z_ref, xbc_ref, dt_ref, dtt_ref, ckvt_ref, skvt_ref, wkvt_ref):
    h = _rms(x_ref[...], pre_ref[...]).astype(BF16)

    def proj(name):
        lo, hi = _SEG[name]
        return jnp.dot(h, w_ref[:, lo:hi], preferred_element_type=F32)

    q_ref[...] = proj("q")
    ckv_ref[...] = proj("ckv")
    skv = proj("skv")
    skv_ref[...] = skv
    skvb_ref[...] = skv.astype(BF16)
    wkv = proj("wkv")
    wkv_ref[...] = wkv
    wkvb_ref[...] = wkv.astype(BF16)
    gates_ref[...] = proj("gates")
    z_ref[...] = proj("z")
    xbc_ref[...] = proj("xbc")
    dt_ref[...] = proj("dt")
    dtt_ref[...] = lax.dot_general(wt_ref[0:TILE, :], h, NT_DIMS, preferred_element_type=F32)
    for n, ref in enumerate((ckvt_ref, skvt_ref, wkvt_ref)):
        lo = TILE + n * KV_WIDTH
        ref[...] = lax.dot_general(wt_ref[lo:lo + KV_WIDTH, :], h, NT_DIMS, preferred_element_type=F32)


def _inproj(x, pre, w, w_t, n_seq):
    t, d = x.shape
    tm = min(256, t)
    seq = t // n_seq
    per_seq = seq // tm
    assert seq % tm == 0

    def cols(height):
        return pl.BlockSpec((None, height, tm), lambda i: (i // per_seq, 0, i % per_seq))

    def rows(width):
        return pl.BlockSpec((tm, width), lambda i: (i, 0))

    widths = (ATT_WIDTH, KV_WIDTH, KV_WIDTH, KV_WIDTH, KV_WIDTH, KV_WIDTH, N_KV_HEADS * TILE, SSD_INNER,
              CONV_DIM, TILE)
    dtypes = (F32, F32, F32, F32, BF16, BF16, F32, F32, F32, F32)
    out_shape = [jax.ShapeDtypeStruct((t, wd), dt) for wd, dt in zip(widths, dtypes)]
    t_heights = (TILE, KV_WIDTH, KV_WIDTH, KV_WIDTH)
    out_shape += [jax.ShapeDtypeStruct((n_seq, ht, seq), F32) for ht in t_heights]
    out_specs = [rows(wd) for wd in widths] + [cols(ht) for ht in t_heights]
    return pl.pallas_call(
        _inproj_body, grid=(t // tm,),
        in_specs=[rows(d), _const_spec((1, d)), _const_spec((d, PROJ_WIDTH)), _const_spec(w_t.shape)],
        out_specs=out_specs, out_shape=out_shape,
        compiler_params=_params(1), name="inproj")(x, pre, w, w_t)


def _mixout_body(x_ref, oa_ref, os_ref, an_ref, post_ref, wa_ref, ws_ref, o_ref):
    a = _rms(oa_ref[...], an_ref[...]).astype(BF16)
    y = jnp.dot(a, wa_ref[...], preferred_element_type=F32)
    y = y + jnp.dot(os_ref[...].astype(BF16), ws_ref[...], preferred_element_type=F32)
    o_ref[...] = x_ref[...] + _rms(y, post_ref[...])


def _mixout(x, o_att, o_ssd, att_norm, post, wa, ws):
    t, d = x.shape
    tm = min(256, t)
    row = pl.BlockSpec((tm, d), lambda i: (i, 0))
    return pl.pallas_call(
        _mixout_body, grid=(t // tm,),
        in_specs=[row, row, row, _const_spec((1, d)), _const_spec((1, d)), _const_spec((d, d)),
                  _const_spec((d, d))],
        out_specs=row, out_shape=jax.ShapeDtypeStruct((t, d), F32),
        compiler_params=_params(1), name="mixout")(x, o_att, o_ssd, att_norm, post, wa, ws)


def _token_rows(ref, k, lo=0, n=None):
    n = ref.shape[-1] if n is None else n
    return ref[k, :, :, lo:lo + n].reshape(KV_GROUP, n).T


def _compress_rows(x_refs, pe_ref, bd_ref, n_blocks, pitch=BLK):
    acc = None
    for l0 in range(0, BLK, COMPRESS_ROWS_PER_DOT):
        pieces = []
        for l in range(l0, l0 + COMPRESS_ROWS_PER_DOT):
            pe = pe_ref[l:l + 1, :]
            xs = jnp.concatenate([x_ref[pl.ds(l, n_blocks, stride=pitch), :] + pe for x_ref in x_refs], axis=0)
            pieces.append(xs.astype(BF16))
        w = bd_ref[l0:l0 + COMPRESS_ROWS_PER_DOT].reshape(COMPRESS_ROWS_PER_DOT * KV_GROUP, KV_GROUP)
        part = jnp.dot(jnp.concatenate(pieces, axis=1), w, preferred_element_type=F32)
        acc = part if acc is None else acc + part
    return acc


def _store_summaries(acc, o_ref, n_blocks):
    for k in range(N_KV_HEADS):
        o_ref[0:n_blocks, k * KV_GROUP:(k + 1) * KV_GROUP] = acc[k * n_blocks:(k + 1) * n_blocks].astype(BF16)


def _compress_body(*refs, n_blocks, n_out):
    x_refs = refs[:N_KV_HEADS]
    pe_ref, bd_ref, o_ref = refs[N_KV_HEADS:]
    _store_summaries(_compress_rows(x_refs, pe_ref, bd_ref, n_blocks), o_ref, n_blocks)
    if n_out > n_blocks:
        o_ref[n_blocks:n_out, :] = jnp.zeros((n_out - n_blocks, KV_WIDTH), BF16)


def _compress(kv, pe, bd, n_out):
    b, s, _ = kv.shape
    n_blocks = s // BLK

    def head_spec(k):
        return pl.BlockSpec((None, s, KV_GROUP), lambda i: (i, 0, k))

    return pl.pallas_call(
        functools.partial(_compress_body, n_blocks=n_blocks, n_out=n_out), grid=(b,),
        in_specs=[head_spec(k) for k in range(N_KV_HEADS)]
        + [_const_spec((BLK, KV_GROUP)), _const_spec((BLK, KV_GROUP, KV_GROUP))],
        out_specs=pl.BlockSpec((None, n_out, KV_WIDTH), lambda i: (i, 0, 0)),
        out_shape=jax.ShapeDtypeStruct((b, n_out, KV_WIDTH), BF16),
        compiler_params=_params(1), name="compress")(*([kv] * N_KV_HEADS), pe, bd)


def _compress_paged_body(pt_ref, *refs, n_pages):
    page_refs = refs[:PAGES_PER_STEP]
    pe_ref, bd_ref, o_ref, stage_ref = refs[PAGES_PER_STEP:]
    j = pl.program_id(1)
    blocks_per_page = PAGE_SIZE // BLK
    for p in range(PAGES_PER_STEP):
        first_block = (j * PAGES_PER_STEP + p) * blocks_per_page
        for k in range(N_KV_HEADS):
            rows = _token_rows(page_refs[p], k)
            for h in range(blocks_per_page):
                at = pl.multiple_of((first_block + h) * STAGE_PITCH, SUBLANES)
                stage_ref[k, pl.ds(at, BLK), :] = rows[h * BLK:(h + 1) * BLK]

    @pl.when(j == pl.num_programs(1) - 1)
    def _():
        n_blocks = n_pages * blocks_per_page
        x_refs = [stage_ref.at[k] for k in range(N_KV_HEADS)]
        _store_summaries(_compress_rows(x_refs, pe_ref, bd_ref, n_blocks, STAGE_PITCH), o_ref, n_blocks)


def _compress_paged(cache, page_table, pe, bd):
    b, n_pages = page_table.shape
    steps = n_pages // PAGES_PER_STEP
    n_blocks = n_pages * (PAGE_SIZE // BLK)

    def page_spec(p):
        return pl.BlockSpec((None, N_KV_HEADS, 2, HEAD_DIM, PAGE_SIZE),
                            lambda i, j, pt: (pt[i, j * PAGES_PER_STEP + p], 0, 0, 0, 0))

    grid_spec = pltpu.PrefetchScalarGridSpec(
        num_scalar_prefetch=1, grid=(b, steps),
        in_specs=[page_spec(p) for p in range(PAGES_PER_STEP)]
        + [pl.BlockSpec((BLK, KV_GROUP), lambda i, j, pt: (0, 0)),
           pl.BlockSpec((BLK, KV_GROUP, KV_GROUP), lambda i, j, pt: (0, 0, 0))],
        out_specs=pl.BlockSpec((None, n_blocks, KV_WIDTH), lambda i, j, pt: (i, 0, 0)),
        scratch_shapes=[pltpu.VMEM((N_KV_HEADS, n_blocks * STAGE_PITCH, KV_GROUP), F32)])
    return pl.pallas_call(
        functools.partial(_compress_paged_body, n_pages=n_pages), grid_spec=grid_spec,
        out_shape=jax.ShapeDtypeStruct((b, n_blocks, KV_WIDTH), BF16),
        compiler_params=_params(2), name="compress_paged")(page_table, *([cache] * PAGES_PER_STEP), pe, bd)


def _rel_bucket(dist):
    n = jnp.maximum(dist, 0)
    max_exact = N_BUCKETS // 2
    nf = jnp.maximum(n, 1).astype(F32)
    log_b = max_exact + (jnp.log(nf / max_exact) / math.log(MAX_DISTANCE / max_exact)
                         * (N_BUCKETS - max_exact)).astype(jnp.int32)
    return jnp.where(n < max_exact, n, jnp.minimum(log_b, N_BUCKETS - 1))


def _lookup(table, bucket):
    onehot = (bucket[..., None] == jnp.arange(table.shape[0], dtype=jnp.int32)).astype(F32)
    return jnp.einsum('...b,b...->...', onehot, table, precision=lax.Precision.HIGHEST)


def _bias_of(rel_table, dist, visible):
    onehot = (_rel_bucket(dist)[..., None] == jnp.arange(N_BUCKETS, dtype=jnp.int32)).astype(F32)
    bias = jnp.einsum('rcb,bh->hrc', onehot, rel_table.astype(F32) * LOG2E, precision=lax.Precision.HIGHEST)
    return jnp.where(visible[None], bias, MASK_VALUE).reshape(-1, dist.shape[-1])


KIND_ZERO, KIND_DIAG, KIND_PREV, KIND_MASKED, KIND_OLDEST = range(5)


def _prompt_bias_tables(rel_table):
    r = jnp.arange(TILE, dtype=jnp.int32)[:, None]
    c = jnp.arange(TILE, dtype=jnp.int32)[None, :]
    true = jnp.ones((TILE, TILE), bool)
    far = _bias_of(rel_table, jnp.full((TILE, TILE), MAX_DISTANCE, jnp.int32), true)
    tiles = jnp.stack([
        jnp.zeros_like(far),
        _bias_of(rel_table, r - c, r >= c) - far,
        _bias_of(rel_table, TILE + r - c, true) - far,
        jnp.full_like(far, MASK_VALUE),
        jnp.where(jnp.tile(r < c, (N_ATT_HEADS, 1)), 0.0, MASK_VALUE),
    ])
    c_dist = r - (BLK - 1) + BLK * (BLK - 1 - c)
    cmp_tab = _bias_of(rel_table, c_dist, c_dist >= 0)
    return tiles, cmp_tab


def _selection_mask(score, blk, n_rows):
    n_groups = score.shape[0] // SUBLANES
    groups = [score[SUBLANES * j:SUBLANES * (j + 1)] for j in range(n_groups)]
    ranks = [jnp.zeros(groups[0].shape, jnp.int32) for _ in range(n_groups)]
    row8 = blk[0:SUBLANES]
    for n in range(n_rows):
        other = jnp.broadcast_to(score[n:n + 1, :], groups[0].shape)
        for j in range(n_groups):
            lo = SUBLANES * j
            if lo > n:
                inc = jnp.where(other >= groups[j], 1, 0)
            elif lo + SUBLANES - 1 <= n:
                inc = jnp.where(other > groups[j], 1, 0)
            else:
                inc = jnp.where(row8 + lo > n, jnp.where(other >= groups[j], 1, 0),
                                jnp.where(other > groups[j], 1, 0))
            ranks[j] = ranks[j] + inc
    rank = jnp.concatenate(ranks, axis=0)
    return jnp.where(rank < N_SEL, jnp.where(score > -jnp.inf, 0.0, MASK_VALUE), MASK_VALUE)


PAIR = 2 * TILE
SEL_UNROLL = 4


def _nsa_prompt_body(q_ref, gates_ref, kvc_ref, skv_ref, wkv_ref, tb_ref, ctab_ref, o_ref,
                     s_ref, sw_ref, mx_ref, mb_ref, acc_ref, *, n_blocks):
    i = pl.program_id(2)
    a = i // 2
    odd = i % 2
    rows = GQA * TILE
    lane = lax.broadcasted_iota(jnp.int32, (rows, TILE), 1)
    low = lane < HEAD_DIM
    key_lane = lax.broadcasted_iota(jnp.int32, (PAIR, TILE), 1)
    key_blk = lax.broadcasted_iota(jnp.int32, (PAIR, TILE), 0) // BLK
    n_pairs = skv_ref.shape[0] // PAIR

    qf = q_ref[...]
    q_pad = []
    for g in range(GQA):
        pair = qf[:, (g // 2) * TILE:(g // 2 + 1) * TILE]
        q_pad.append(pair if g % 2 == 0 else pltpu.roll(pair, HEAD_DIM, 1))
    q_all = jnp.concatenate(q_pad, axis=0)
    lhs_plain = jnp.where(low, q_all, 0.0).astype(BF16)

    def ones_and_values(kv):
        kv_lane = lax.broadcasted_iota(jnp.int32, kv.shape, 1)
        return jnp.where(kv_lane < HEAD_DIM, jnp.ones_like(kv), kv)

    def normalise(acc):
        return acc / jnp.maximum(pltpu.roll(acc, HEAD_DIM, 1), 1e-30)

    kvc = kvc_ref[...]
    shift = (2 * i + (TILE - (BLK - 1))) % TILE
    bias = jnp.concatenate([pltpu.roll(ctab_ref[g * TILE:(g + 1) * TILE, :], shift, 1) for g in range(GQA)], axis=0)
    s = lax.dot_general(lhs_plain, kvc, NT_DIMS, preferred_element_type=F32)
    s = jnp.where(lane < n_blocks, s + bias, MASK_VALUE)
    m = jnp.maximum(jnp.max(s, axis=1, keepdims=True), M_INIT)
    p = jnp.exp2(s - m)
    c_acc = jnp.dot(p.astype(BF16), ones_and_values(kvc), preferred_element_type=F32)
    p = p / jnp.maximum(c_acc, 1e-30)
    o_cmp = normalise(c_acc)
    imp = p[0:TILE]
    for g in range(1, GQA):
        imp = imp + p[g * TILE:(g + 1) * TILE]

    blk = lax.broadcasted_iota(jnp.int32, (n_blocks, TILE), 0)
    query = lax.broadcasted_iota(jnp.int32, (n_blocks, TILE), 1)
    cur = 2 * i + jnp.where(query >= BLK, 1, 0)
    forced = (blk == 0) | (blk == cur) | (blk == cur - 1)
    score = jnp.where(forced, jnp.inf, jnp.where(blk <= cur, imp.T[0:n_blocks], -jnp.inf))
    neg_t = _selection_mask(score, blk, n_blocks)
    pieces = [jnp.zeros((HEAD_DIM, TILE), F32), neg_t]
    if n_blocks < TILE - HEAD_DIM:
        pieces.append(jnp.zeros((TILE - HEAD_DIM - n_blocks, TILE), F32))
    neg = jnp.concatenate(pieces, axis=0).T
    lhs_sel = jnp.where(low, q_all, jnp.concatenate([neg] * GQA, axis=0)).astype(BF16)

    def sel_kinds(j):
        last = j == a
        left = jnp.where(last, jnp.where(odd == 1, KIND_PREV, KIND_DIAG), KIND_ZERO)
        right = jnp.where(last, jnp.where(odd == 1, KIND_DIAG, KIND_MASKED),
                          jnp.where((j == a - 1) & (odd == 0), KIND_PREV, KIND_ZERO))
        return left, right

    def win_kinds(j):
        left, right = sel_kinds(j)
        first = j == a - WINDOW // PAIR
        return (jnp.where(first, jnp.where(odd == 1, KIND_MASKED, KIND_OLDEST), left),
                jnp.where(first, jnp.where(odd == 1, KIND_OLDEST, KIND_ZERO), right))

    def keys_of(kv_ref, j):
        jc = jnp.clip(j, 0, n_pairs - 1)
        return kv_ref[pl.ds(pl.multiple_of(jc * PAIR, PAIR), PAIR), :]

    def score_pair(lhs, kv_ref, j, valid, kinds, block_columns):
        rhs = keys_of(kv_ref, j)
        if block_columns:
            onehot = jnp.where(key_lane - HEAD_DIM == (PAIR // BLK) * j + key_blk, 1.0, 0.0).astype(BF16)
            rhs = jnp.where(key_lane < HEAD_DIM, rhs, onehot)
        sc = lax.dot_general(lhs, rhs, NT_DIMS, preferred_element_type=F32)
        left, right = kinds(j)
        left = jnp.where(valid, left, KIND_MASKED)
        right = jnp.where(valid, right, KIND_MASKED)
        return sc + jnp.concatenate([tb_ref[left], tb_ref[right]], axis=1)

    def half_max(sc):
        return jnp.maximum(sc[:, :TILE], sc[:, TILE:])

    def weigh_pair(sc, mb, kv_ref, j):
        pr = jnp.exp2(sc - jnp.concatenate([mb, mb], axis=1))
        return jnp.dot(pr.astype(BF16), ones_and_values(keys_of(kv_ref, j)), preferred_element_type=F32)

    def row_max(mx):
        return jnp.broadcast_to(jnp.max(mx, axis=1, keepdims=True), (rows, TILE))

    w_pairs = [a - WINDOW // PAIR + w for w in range(WINDOW // PAIR + 1)]
    w_mx = None
    for w, j in enumerate(w_pairs):
        sc = score_pair(lhs_plain, wkv_ref, j, j >= 0, win_kinds, False)
        sw_ref[w] = sc
        w_mx = half_max(sc) if w_mx is None else jnp.maximum(w_mx, half_max(sc))
    w_mb = row_max(jnp.maximum(w_mx, M_INIT))
    w_acc = None
    for w, j in enumerate(w_pairs):
        dacc = weigh_pair(sw_ref[w], w_mb, wkv_ref, j)
        w_acc = dacc if w_acc is None else w_acc + dacc
    o_win = normalise(w_acc)

    n_trips = (a + SEL_UNROLL) // SEL_UNROLL
    mx_ref[...] = jnp.full(mx_ref.shape, M_INIT, F32)

    def score_trip(t, carry):
        mx = None
        for u in range(SEL_UNROLL):
            j = t * SEL_UNROLL + u
            sc = score_pair(lhs_sel, skv_ref, j, j <= a, sel_kinds, True)
            s_ref[j] = sc
            mx = half_max(sc) if mx is None else jnp.maximum(mx, half_max(sc))
        mx_ref[...] = jnp.maximum(mx_ref[...], mx)
        return carry

    lax.fori_loop(0, n_trips, score_trip, 0)
    mb_ref[...] = row_max(mx_ref[...])
    acc_ref[...] = jnp.zeros(acc_ref.shape, F32)

    def weigh_trip(t, carry):
        mb = mb_ref[...]
        acc = None
        for u in range(SEL_UNROLL):
            j = t * SEL_UNROLL + u
            dacc = weigh_pair(s_ref[j], mb, skv_ref, j)
            acc = dacc if acc is None else acc + dacc
        acc_ref[...] += acc
        return carry

    lax.fori_loop(0, n_trips, weigh_trip, 0)
    o_sel = normalise(acc_ref[...])

    gate = jax.nn.sigmoid(gates_ref[...])
    outs = []
    for g in range(GQA):
        def gcol(br):
            c = g * N_BRANCH + br
            return gate[:, c:c + 1]
        head = slice(g * TILE, (g + 1) * TILE)
        outs.append(gcol(0) * o_cmp[head] + gcol(1) * o_sel[head] + gcol(2) * o_win[head])
    out_low = lax.broadcasted_iota(jnp.int32, (TILE, TILE), 1) < HEAD_DIM
    for j in range(GQA // 2):
        o_ref[:, j * TILE:(j + 1) * TILE] = jnp.where(out_low, pltpu.roll(outs[2 * j], HEAD_DIM, 1), outs[2 * j + 1])


def _nsa_prompt(q, gates, kvc, skv_b, wkv_b, tiles, cmp_tab):
    b, s, _ = q.shape
    nq = s // TILE
    n_blocks = s // BLK
    assert n_blocks <= TILE - HEAD_DIM and kvc.shape[1] == TILE and s % PAIR == 0
    gw = GQA * HEAD_DIM
    rows = GQA * TILE
    return pl.pallas_call(
        functools.partial(_nsa_prompt_body, n_blocks=n_blocks), grid=(b, N_KV_HEADS, nq),
        in_specs=[pl.BlockSpec((None, TILE, gw), lambda bi, k, i: (bi, i, k)),
                  pl.BlockSpec((None, TILE, TILE), lambda bi, k, i: (bi, i, k)),
                  pl.BlockSpec((None, TILE, KV_GROUP), lambda bi, k, i: (bi, 0, k)),
                  pl.BlockSpec((None, s, KV_GROUP), lambda bi, k, i: (bi, 0, k)),
                  pl.BlockSpec((None, s, KV_GROUP), lambda bi, k, i: (bi, 0, k)),
                  pl.BlockSpec((len(tiles), rows, TILE), lambda bi, k, i: (0, k, 0)),
                  pl.BlockSpec((rows, TILE), lambda bi, k, i: (k, 0))],
        out_specs=pl.BlockSpec((None, TILE, gw), lambda bi, k, i: (bi, i, k)),
        out_shape=jax.ShapeDtypeStruct((b, s, ATT_WIDTH), F32),
        scratch_shapes=[pltpu.VMEM((s // PAIR + SEL_UNROLL - 1, rows, PAIR), F32),
                        pltpu.VMEM((WINDOW // PAIR + 1, rows, PAIR), F32),
                        pltpu.VMEM((rows, TILE), F32), pltpu.VMEM((rows, TILE), F32),
                        pltpu.VMEM((rows, KV_GROUP), F32)],
        compiler_params=_params(3), name="nsa_prompt")(q, gates, kvc, skv_b, wkv_b, tiles, cmp_tab)


def _t_softmax_step(kv, qbd, bias, m_ref, l_ref, acc_ref):
    s = jnp.dot(kv, qbd, preferred_element_type=F32) + bias
    m_prev = m_ref[...]
    m_new = jnp.maximum(m_prev, jnp.max(s, axis=0, keepdims=True))
    alpha = jnp.exp2(m_prev - m_new)
    p = jnp.exp2(s - m_new)
    l_ref[...] = alpha * l_ref[...] + jnp.sum(p, axis=0, keepdims=True)
    acc_ref[...] = alpha * acc_ref[...] + lax.dot_general(kv, p.astype(BF16), TN_DIMS,
                                                          preferred_element_type=F32)
    m_ref[...] = m_new


def _nsa_sample_body(pt_ref, *refs, n_pages, dec_seq, past_len):
    page_refs = refs[:PAGES_PER_STEP]
    (qbd_ref, gates_ref, kc_ref, ctab_ref, gsum_ref, snew_ref, cwin_ref, wnew_ref, last_ref, new_ref, far_ref,
     win0_ref, o_ref, neg_ref, m_ref, l_ref, acc_ref, oc_ref, ow_ref) = refs[PAGES_PER_STEP:]
    j = pl.program_id(1)
    n_past = n_pages * (PAGE_SIZE // BLK)
    n_rows = kc_ref.shape[0]
    sub = lax.broadcasted_iota(jnp.int32, (TILE, TILE), 0)
    qbd = qbd_ref[...]
    far = far_ref[0:1, :]

    def reset():
        m_ref[...] = jnp.full(m_ref.shape, M_INIT, F32)
        l_ref[...] = jnp.zeros(l_ref.shape, F32)
        acc_ref[...] = jnp.zeros(acc_ref.shape, F32)

    def result():
        return acc_ref[...] / jnp.maximum(l_ref[...], 1e-30)

    @pl.when(j == 0)
    def _():
        kc = kc_ref[...]
        s = jnp.dot(kc, qbd, preferred_element_type=F32) + ctab_ref[...]
        m = jnp.maximum(jnp.max(s, axis=0, keepdims=True), M_INIT)
        p = jnp.exp2(s - m)
        p = p / jnp.maximum(jnp.sum(p, axis=0, keepdims=True), 1e-30)
        oc_ref[...] = lax.dot_general(kc, p.astype(BF16), TN_DIMS, preferred_element_type=F32)
        imp = jnp.dot(p, gsum_ref[...], preferred_element_type=F32, precision=lax.Precision.HIGHEST)
        blk = lax.broadcasted_iota(jnp.int32, (n_rows, TILE), 0)
        tok = lax.broadcasted_iota(jnp.int32, (n_rows, TILE), 1) % dec_seq
        cur = (past_len + tok) // BLK
        forced = (blk == 0) | (blk == cur) | (blk == cur - 1)
        score = jnp.where(forced, jnp.inf, jnp.where(blk <= cur, imp, -jnp.inf))
        neg_ref[...] = _selection_mask(score, blk, n_past + 1)

        reset()
        n_win = cwin_ref.shape[-1] // TILE
        for t in range(n_win):
            kv = jnp.concatenate([_token_rows(cwin_ref, k, t * TILE, TILE) for k in range(N_KV_HEADS)],
                                 axis=1).astype(BF16)
            bias = win0_ref[...] if t == 0 else (last_ref[...] if t == n_win - 1 else far)
            _t_softmax_step(kv, qbd, bias, m_ref, l_ref, acc_ref)
        _t_softmax_step(wnew_ref[...], qbd, new_ref[...], m_ref, l_ref, acc_ref)
        ow_ref[...] = result()
        reset()

    blocks_per_step = PAGES_PER_STEP * (PAGE_SIZE // BLK)
    neg_rows = neg_ref[pl.ds(pl.multiple_of(j * blocks_per_step, blocks_per_step), blocks_per_step), :]
    for p in range(PAGES_PER_STEP):
        page = j * PAGES_PER_STEP + p
        kv = jnp.concatenate([_token_rows(page_refs[p], k) for k in range(N_KV_HEADS)], axis=1).astype(BF16)
        first = neg_rows[2 * p:2 * p + 1, :]
        second = neg_rows[2 * p + 1:2 * p + 2, :]
        bias = jnp.where(sub < BLK, first, second) + jnp.where(page == n_pages - 1, last_ref[...], far)
        _t_softmax_step(kv, qbd, bias, m_ref, l_ref, acc_ref)

    @pl.when(j == pl.num_programs(1) - 1)
    def _():
        _t_softmax_step(snew_ref[...], qbd, new_ref[...] + neg_ref[n_past:n_past + 1, :], m_ref, l_ref, acc_ref)
        gate = jax.nn.sigmoid(gates_ref[...])
        o_ref[...] = gate[0:1, :] * oc_ref[...] + gate[1:2, :] * result() + gate[2:3, :] * ow_ref[...]


def _nsa_sample(page_table, cache_slc, qbd, gates_t, kc, ctab, gsum, slc_new, cache_win, win_new, last_tab,
                new_tab, far_row, win0_tab, *, dec_seq, past_len):
    b, n_pages = page_table.shape
    steps = n_pages // PAGES_PER_STEP
    n_rows = kc.shape[1]
    w_buf = cache_win.shape[-1]
    n_new = slc_new.shape[1]

    def page_spec(p):
        return pl.BlockSpec((None, N_KV_HEADS, 2, HEAD_DIM, PAGE_SIZE),
                            lambda i, j, pt: (pt[i, j * PAGES_PER_STEP + p], 0, 0, 0, 0))

    def per_batch(*shape):
        zeros = (0,) * len(shape)
        return pl.BlockSpec((None,) + shape, lambda i, j, pt: (i,) + zeros)

    def const(*shape):
        zeros = (0,) * len(shape)
        return pl.BlockSpec(shape, lambda i, j, pt: zeros)

    grid_spec = pltpu.PrefetchScalarGridSpec(
        num_scalar_prefetch=1, grid=(b, steps),
        in_specs=[page_spec(p) for p in range(PAGES_PER_STEP)]
        + [per_batch(KV_WIDTH, TILE), per_batch(SUBLANES, TILE), per_batch(n_rows, KV_WIDTH), const(n_rows, TILE),
           const(TILE, TILE), per_batch(n_new, KV_WIDTH), per_batch(N_KV_HEADS, 2, HEAD_DIM, w_buf), per_batch(n_new, KV_WIDTH),
           const(TILE, TILE), const(n_new, TILE), const(SUBLANES, TILE), const(TILE, TILE)],
        out_specs=per_batch(KV_WIDTH, TILE),
        scratch_shapes=[pltpu.VMEM((n_rows, TILE), F32), pltpu.VMEM((1, TILE), F32), pltpu.VMEM((1, TILE), F32),
                        pltpu.VMEM((KV_WIDTH, TILE), F32), pltpu.VMEM((KV_WIDTH, TILE), F32),
                        pltpu.VMEM((KV_WIDTH, TILE), F32)])
    return pl.pallas_call(
        functools.partial(_nsa_sample_body, n_pages=n_pages, dec_seq=dec_seq, past_len=past_len),
        grid_spec=grid_spec, out_shape=jax.ShapeDtypeStruct((b, KV_WIDTH, TILE), F32),
        compiler_params=_params(2), name="nsa_sample")(
            page_table, *([cache_slc] * PAGES_PER_STEP), qbd, gates_t, kc, ctab, gsum, slc_new, cache_win, win_new,
            last_tab, new_tab, far_row, win0_tab)


def _softplus(x):
    return jnp.maximum(x, 0.0) + jnp.log(1.0 + jnp.exp(-jnp.abs(x)))


def _ssd_body(z_ref, xbc_ref, dt_ref, dtt_ref, convp_ref, h0_ref, cw_ref, cb_ref, dtb_ref, dtbt_ref, al_ref,
              alt_ref, dskip_ref, norm_ref, o_ref, hout_ref, convout_ref, xp_ref, *, cl):
    c = pl.program_id(1)
    pad = SUBLANES

    @pl.when(c == 0)
    def _():
        xp_ref[0:pad, :] = convp_ref[...]
        hout_ref[...] = h0_ref[...]

    xp_ref[pad:pad + cl, :] = xbc_ref[...]
    conv = cb_ref[...]
    for k in range(CONV_W):
        lo = pad - (CONV_W - 1) + k
        conv = conv + cw_ref[k:k + 1, :] * xp_ref[lo:lo + cl, :]
    u = conv * jax.nn.sigmoid(conv)
    tail = xp_ref[pad + cl - (CONV_W - 1):pad + cl, :]
    convout_ref[...] = tail
    xp_ref[pad - (CONV_W - 1):pad, :] = tail

    xs = u[:, :SSD_INNER]
    dt = _softplus(dt_ref[:, 0:SSD_HEADS] + dtb_ref[...])
    dt_t = _softplus(dtt_ref[0:SSD_HEADS, :] + dtbt_ref[...])
    la = dt * -jnp.exp(al_ref[...])
    la_t = dt_t * -jnp.exp(alt_ref[...])
    row = lax.broadcasted_iota(jnp.int32, (cl, cl), 0)
    col = lax.broadcasted_iota(jnp.int32, (cl, cl), 1)
    causal = row >= col
    hi = lax.Precision.HIGHEST
    cs = jnp.dot(jnp.where(causal, 1.0, 0.0), la, preferred_element_type=F32, precision=hi)
    cs_t = jnp.dot(la_t, jnp.where(row <= col, 1.0, 0.0), preferred_element_type=F32, precision=hi)

    heads_per_group = SSD_HEADS // SSD_GROUPS
    ys = []
    for g in range(SSD_GROUPS):
        b_lo = SSD_INNER + g * SSD_STATE
        c_lo = SSD_INNER + SSD_GROUPS * SSD_STATE + g * SSD_STATE
        bm = u[:, b_lo:b_lo + SSD_STATE].astype(BF16)
        cm = u[:, c_lo:c_lo + SSD_STATE].astype(BF16)
        cb = lax.dot_general(cm, bm, NT_DIMS, preferred_element_type=F32)
        for r in range(heads_per_group):
            h = g * heads_per_group + r
            cs_col = cs[:, h:h + 1]
            diff = cs_col - cs_t[h:h + 1, :]
            decay = jnp.where(causal, jnp.exp(jnp.where(causal, diff, 0.0)), 0.0)
            xs_h = xs[:, h * SSD_HEAD_DIM:(h + 1) * SSD_HEAD_DIM]
            xdt = xs_h * dt[:, h:h + 1]
            y = jnp.dot((cb * decay).astype(BF16), xdt.astype(BF16), preferred_element_type=F32)
            last = cs[cl - 1:cl, h:h + 1]
            w = (xdt * jnp.exp(last - cs_col)).astype(BF16)
            state_in = lax.dot_general(w, bm, TN_DIMS, preferred_element_type=F32)
            h_prev = hout_ref[h]
            y = y + lax.dot_general(cm, h_prev.astype(BF16), NT_DIMS, preferred_element_type=F32) * jnp.exp(cs_col)
            hout_ref[h] = h_prev * jnp.exp(last) + state_in
            ys.append(y + dskip_ref[:, h * SSD_HEAD_DIM:(h + 1) * SSD_HEAD_DIM] * xs_h)
    y = jnp.concatenate(ys, axis=1)
    z = z_ref[...]
    y = y * (z * jax.nn.sigmoid(z))
    gw = SSD_INNER // SSD_GROUPS
    normed = []
    for g in range(SSD_GROUPS):
        yg = y[:, g * gw:(g + 1) * gw]
        normed.append(yg * lax.rsqrt(jnp.mean(yg * yg, axis=-1, keepdims=True) + EPS))
    o_ref[...] = jnp.concatenate(normed, axis=1) * norm_ref[...]


def _ssd(z, xbc, dt, dt_t, conv_prev, h0, consts, cl):
    b, seq, _ = z.shape
    nc = seq // cl
    cw, cbias, dtb, dtb_t, al, al_t, dskip, norm = consts

    def rows(width):
        return pl.BlockSpec((None, cl, width), lambda i, c: (i, c, 0))

    state = pl.BlockSpec((None, SSD_HEADS, SSD_HEAD_DIM, SSD_STATE), lambda i, c: (i, 0, 0, 0))

    def const(x):
        zeros = (0,) * x.ndim
        return pl.BlockSpec(x.shape, lambda i, c: zeros)

    return pl.pallas_call(
        functools.partial(_ssd_body, cl=cl), grid=(b, nc),
        in_specs=[rows(SSD_INNER), rows(CONV_DIM), rows(TILE),
                  pl.BlockSpec((None, TILE, cl), lambda i, c: (i, 0, c)),
                  pl.BlockSpec((None, SUBLANES, CONV_DIM), lambda i, c: (i, 0, 0)), state,
                  const(cw), const(cbias), const(dtb), const(dtb_t), const(al), const(al_t), const(dskip),
                  const(norm)],
        out_specs=[rows(SSD_INNER), state, pl.BlockSpec((None, CONV_W - 1, CONV_DIM), lambda i, c: (i, 0, 0))],
        out_shape=[jax.ShapeDtypeStruct((b, seq, SSD_INNER), F32),
                   jax.ShapeDtypeStruct((b, SSD_HEADS, SSD_HEAD_DIM, SSD_STATE), F32),
                   jax.ShapeDtypeStruct((b, CONV_W - 1, CONV_DIM), F32)],
        scratch_shapes=[pltpu.VMEM((SUBLANES + cl, CONV_DIM), F32)],
        compiler_params=_params(2), name="ssd")(z, xbc, dt, dt_t, conv_prev, h0, cw, cbias, dtb, dtb_t, al, al_t,
                                                 dskip, norm)


def _pack_inproj(w_in):
    splits = (ATT_WIDTH, KV_WIDTH, KV_WIDTH, KV_WIDTH, N_BRANCH * N_ATT_HEADS, SSD_INNER, CONV_DIM, SSD_HEADS)
    offs = [0]
    for s in splits:
        offs.append(offs[-1] + s)
    wq, wc, ws, ww, wg, wz, wx, wd = [w_in[:, offs[i]:offs[i + 1]] for i in range(len(splits))]
    d = w_in.shape[0]
    per_kv = GQA * N_BRANCH
    wg4 = jnp.pad(wg.reshape(d, N_KV_HEADS, per_kv), ((0, 0), (0, 0), (0, TILE - per_kv)))
    wd_pad = jnp.pad(wd, ((0, 0), (0, TILE - SSD_HEADS)))
    w = jnp.concatenate([wq * (HEAD_DIM ** -0.5 * LOG2E), wc, ws, ww, wg4.reshape(d, N_KV_HEADS * TILE), wz, wx, wd_pad],
                        axis=1)
    assert w.shape[1] == PROJ_WIDTH
    w_t = jnp.concatenate([wd_pad, wc, ws, ww], axis=1).T
    return w.astype(BF16), w_t.astype(BF16)


def _pack_compress(cmp_pe, w_cmp):
    zero = jnp.zeros((BLK, HEAD_DIM, HEAD_DIM), w_cmp.dtype)
    bd = jnp.concatenate([jnp.concatenate([w_cmp[:, 0], zero], axis=2),
                          jnp.concatenate([zero, w_cmp[:, 1]], axis=2)], axis=1)
    pe = cmp_pe.reshape(BLK, KV_GROUP)
    return pe.astype(F32), bd.astype(BF16)


def _sample_tables(rel_table, dec_seq, past_len, n_rows, n_new, w_buf):
    col = jnp.arange(TILE, dtype=jnp.int32)
    head = (col // (GQA * dec_seq)) * GQA + (col // dec_seq) % GQA
    tok = (col % dec_seq)[None, :]
    table = rel_table.astype(F32) * LOG2E

    def bias(dist, visible):
        return jnp.where(visible, _lookup(table[:, head][:, None, :], _rel_bucket(dist)), MASK_VALUE)

    row = jnp.arange(TILE, dtype=jnp.int32)[:, None]
    blk = jnp.arange(n_rows, dtype=jnp.int32)[:, None]
    c_dist = past_len + tok - ((blk + 1) * BLK - 1)
    ctab = bias(c_dist, c_dist >= 0)
    last_tab = bias(TILE + tok - row, jnp.ones((TILE, TILE), bool))
    new_row = jnp.arange(n_new, dtype=jnp.int32)[:, None]
    new_tab = bias(tok - new_row, (tok >= new_row) & (new_row < dec_seq))
    far = jnp.full((SUBLANES, TILE), w_buf, jnp.int32)
    far_row = bias(far, jnp.ones((SUBLANES, TILE), bool))
    win0_tab = bias(jnp.full((TILE, TILE), w_buf, jnp.int32), row > tok)
    same = (col[:, None] // (GQA * dec_seq) == col[None, :] // (GQA * dec_seq)) & \
           (col[:, None] % dec_seq == col[None, :] % dec_seq)
    return ctab, last_tab, new_tab, far_row, win0_tab, same.astype(F32)


def _layer_weights(ffn_pre, ffn_post, ffn_w_in, ffn_w_out):
    f = ffn_w_out.shape[0]
    return (ffn_pre[None, :], ffn_post[None, :], ffn_w_in[:, :f].astype(BF16), ffn_w_in[:, f:].astype(BF16),
            ffn_w_out.astype(BF16))


def kernel(x_prompt, x_sample, cache_cmp_kv, cache_slc_kv, cache_win_kv, state_ssm, state_conv, page_table,
           rel_table, ffn1_pre, ffn1_post, ffn1_w_in, ffn1_w_out, mix_pre, mix_post, w_in, w_out, att_out_norm,
           cmp_pe, w_cmp, conv_w, conv_b, dt_bias, a_log, d_skip, ssd_norm,
           ffn2_pre, ffn2_post, ffn2_w_in, ffn2_w_out):
    depth = ffn1_pre.shape[0]
    bp, seq, d = x_prompt.shape
    bs, dec_seq, _ = x_sample.shape
    n_pool = cache_cmp_kv.shape[1]
    n_pages = page_table.shape[1]
    past_len = n_pages * PAGE_SIZE
    w_buf = cache_win_kv.shape[2]
    kv_shape = (N_KV_HEADS, 2, HEAD_DIM)
    assert GQA * dec_seq * N_KV_HEADS == TILE and w_buf == WINDOW and seq % TILE == 0

    yp = x_prompt.reshape(bp * seq, d)
    ys = x_sample.reshape(bs * dec_seq, d)
    outs = [[] for _ in range(10)]
    for l in range(depth):
        ffn1 = _layer_weights(ffn1_pre[l], ffn1_post[l], ffn1_w_in[l], ffn1_w_out[l])
        ffn2 = _layer_weights(ffn2_pre[l], ffn2_post[l], ffn2_w_in[l], ffn2_w_out[l])
        w_proj, w_proj_t = _pack_inproj(w_in[l])
        pe, bd = _pack_compress(cmp_pe[l], w_cmp[l])
        wo_att = w_out[l][:ATT_WIDTH].astype(BF16)
        wo_ssd = w_out[l][ATT_WIDTH:].astype(BF16)
        ssd_consts = (conv_w[l], conv_b[l][None, :], dt_bias[l][None, :], dt_bias[l][:, None], a_log[l][None, :],
                      a_log[l][:, None], jnp.repeat(d_skip[l], SSD_HEAD_DIM)[None, :], ssd_norm[l][None, :])
        mix_pre_l, mix_post_l, att_norm_l = mix_pre[l][None, :], mix_post[l][None, :], att_out_norm[l][None, :]

        yp = _ffn(yp, *ffn1)
        (q, ckv, skv, wkv, skv_b, wkv_b, gates, z, xbc, dt, dt_t, ckv_t, skv_t, wkv_t) = _inproj(
            yp, mix_pre_l, w_proj, w_proj_t, bp)
        kvc = _compress(ckv.reshape(bp, seq, KV_WIDTH), pe, bd, TILE)
        tiles, cmp_tab = _prompt_bias_tables(rel_table)
        o_att = _nsa_prompt(q.reshape(bp, seq, ATT_WIDTH), gates.reshape(bp, seq, N_KV_HEADS * TILE), kvc,
                            skv_b.reshape(bp, seq, KV_WIDTH), wkv_b.reshape(bp, seq, KV_WIDTH), tiles, cmp_tab)
        cl = min(CHUNK, seq)
        o_ssd, h_new, conv_new = _ssd(
            z.reshape(bp, seq, SSD_INNER), xbc.reshape(bp, seq, CONV_DIM), dt.reshape(bp, seq, TILE),
            dt_t, jnp.zeros((bp, SUBLANES, CONV_DIM), F32),
            jnp.zeros((bp, SSD_HEADS, SSD_HEAD_DIM, SSD_STATE), F32), ssd_consts, cl)
        yp = _mixout(yp, o_att.reshape(bp * seq, ATT_WIDTH), o_ssd.reshape(bp * seq, SSD_INNER), att_norm_l,
                     mix_post_l, wo_att, wo_ssd)
        def token_major(a_t):
            return a_t.reshape((bp,) + kv_shape + (a_t.shape[-1],)).transpose(0, 4, 1, 2, 3)

        outs[0].append(token_major(ckv_t))
        outs[1].append(token_major(skv_t))
        outs[2].append(token_major(wkv_t[:, :, -min(WINDOW, seq):]))
        outs[3].append(h_new)
        outs[4].append(conv_new)

        ys = _ffn(ys, *ffn1)
        q, ckv, skv, wkv, skv_b, wkv_b, gates, z, xbc, dt, dt_t, _, _, _ = _inproj(ys, mix_pre_l, w_proj, w_proj_t, 1)
        native = (0, 2, 3, 4, 1)
        kc_past = _compress_paged(cache_cmp_kv[l].transpose(native), page_table, pe, bd)
        cmp_new = jnp.pad(ckv.reshape(bs, dec_seq, KV_WIDTH), ((0, 0), (0, BLK - dec_seq), (0, 0)))
        kc_new = _compress(cmp_new.reshape(1, bs * BLK, KV_WIDTH), pe, bd, bs).reshape(bs, 1, KV_WIDTH)
        n_past = kc_past.shape[1]
        n_rows = -(-(n_past + 1) // 16) * 16
        kc = jnp.concatenate([kc_past, kc_new, jnp.zeros((bs, n_rows - n_past - 1, KV_WIDTH), BF16)], axis=1)
        n_new = 16
        ctab, last_tab, new_tab, far_row, win0_tab, gsum = _sample_tables(rel_table, dec_seq, past_len, n_rows,
                                                                          n_new, w_buf)
        q5 = q.reshape(bs, dec_seq, N_KV_HEADS, GQA, HEAD_DIM).transpose(0, 2, 4, 3, 1)
        q5 = q5.reshape(bs, N_KV_HEADS, HEAD_DIM, GQA * dec_seq)
        eye = jnp.eye(N_KV_HEADS, dtype=F32)[None, :, None, :, None]
        qbd = jnp.pad(q5, ((0, 0), (0, 0), (0, HEAD_DIM), (0, 0)))[:, :, :, None, :] * eye
        qbd = qbd.reshape(bs, KV_WIDTH, TILE).astype(BF16)
        g5 = gates.reshape(bs, dec_seq, N_KV_HEADS, TILE)[..., :GQA * N_BRANCH]
        g5 = g5.reshape(bs, dec_seq, N_KV_HEADS, GQA, N_BRANCH).transpose(0, 4, 2, 3, 1).reshape(bs, N_BRANCH, TILE)
        gates_t = jnp.pad(g5, ((0, 0), (0, SUBLANES - N_BRANCH), (0, 0)))

        def pad_new(a):
            return jnp.pad(a.reshape(bs, dec_seq, KV_WIDTH), ((0, 0), (0, n_new - dec_seq), (0, 0)))

        o_t = _nsa_sample(page_table, cache_slc_kv[l].transpose(native), qbd, gates_t, kc, ctab,
                          gsum, pad_new(skv_b), cache_win_kv[l].transpose(native), pad_new(wkv_b),
                          last_tab, new_tab, far_row, win0_tab, dec_seq=dec_seq, past_len=past_len)
        o7 = o_t.reshape(bs, N_KV_HEADS, 2, HEAD_DIM, N_KV_HEADS, GQA, dec_seq)
        o_att = jnp.stack([o7[:, k, 1, :, k] for k in range(N_KV_HEADS)], axis=1).transpose(0, 4, 1, 3, 2)
        o_att = o_att.reshape(bs * dec_seq, ATT_WIDTH)
        conv_prev = jnp.pad(state_conv[l], ((0, 0), (SUBLANES - (CONV_W - 1), 0), (0, 0)))
        o_ssd, h_new, conv_new = _ssd(
            z.reshape(bs, dec_seq, SSD_INNER), xbc.reshape(bs, dec_seq, CONV_DIM), dt.reshape(bs, dec_seq, TILE),
            dt_t.reshape(TILE, bs, dec_seq).transpose(1, 0, 2), conv_prev, state_ssm[l], ssd_consts,
            min(CHUNK, dec_seq))
        ys = _mixout(ys, o_att, o_ssd.reshape(bs * dec_seq, SSD_INNER), att_norm_l, mix_post_l, wo_att, wo_ssd)
        win_all = jnp.concatenate([cache_win_kv[l], wkv.reshape((bs, dec_seq) + kv_shape)], axis=1)
        outs[5].append(ckv.reshape((bs, dec_seq) + kv_shape))
        outs[6].append(skv.reshape((bs, dec_seq) + kv_shape))
        outs[7].append(win_all[:, -min(WINDOW, w_buf + dec_seq):])
        outs[8].append(h_new)
        outs[9].append(conv_new)

        yp = _ffn(yp, *ffn2)
        ys = _ffn(ys, *ffn2)

    return (yp.reshape(bp, seq, d), ys.reshape(bs, dec_seq, d)) + tuple(jnp.stack(o) for o in outs)
```

```python
import functools
import math

import jax
import jax.numpy as jnp
from jax import lax
from jax.experimental import pallas as pl
from jax.experimental.pallas import tpu as pltpu

F32 = jnp.float32
BF16 = jnp.bfloat16

D_MODEL = 1024
N_ATT_HEADS = 16
HEAD_DIM = 64
N_KV_HEADS = 4
GQA = N_ATT_HEADS // N_KV_HEADS
ATT_WIDTH = N_ATT_HEADS * HEAD_DIM
KV_WIDTH = 2 * N_KV_HEADS * HEAD_DIM
KV_GROUP = 2 * HEAD_DIM
BLK = 64
N_SEL = 16
WINDOW = 512
N_BRANCH = 3
N_BUCKETS = 32
MAX_DISTANCE = 128
SSD_HEADS = 16
SSD_HEAD_DIM = 64
SSD_INNER = SSD_HEADS * SSD_HEAD_DIM
SSD_GROUPS = 2
SSD_STATE = 128
CONV_W = 4
CONV_DIM = SSD_INNER + 2 * SSD_GROUPS * SSD_STATE
CHUNK = 128
PAGE_SIZE = 128
EPS = 1e-6

TILE = 128
SUBLANES = 8
MASK_VALUE = -(2.0 ** 100)
M_INIT = -1e30
LOG2E = math.log2(math.e)
VMEM_LIMIT = 56 * 1024 * 1024
PAGES_PER_STEP = 8
COMPRESS_ROWS_PER_DOT = 16
STAGE_PITCH = 72

NT_DIMS = (((1,), (1,)), ((), ()))
TN_DIMS = (((0,), (0,)), ((), ()))


def _params(n_grid_dims):
    return pltpu.CompilerParams(dimension_semantics=("arbitrary",) * n_grid_dims,
                                vmem_limit_bytes=VMEM_LIMIT)


def _rms(x, g):
    return x * lax.rsqrt(jnp.mean(x * x, axis=-1, keepdims=True) + EPS) * g


def _const_spec(shape):
    zeros = (0,) * len(shape)
    return pl.BlockSpec(shape, lambda *_: zeros, pipeline_mode=pl.Buffered(1))


def _ffn_body(x_ref, pre_ref, post_ref, wg_ref, wu_ref, wo_ref, o_ref):
    x = x_ref[...]
    h = _rms(x, pre_ref[...]).astype(BF16)
    gate = jnp.dot(h, wg_ref[...], preferred_element_type=F32)
    up = jnp.dot(h, wu_ref[...], preferred_element_type=F32)
    a = (gate * jax.nn.sigmoid(gate) * up).astype(BF16)
    y = jnp.dot(a, wo_ref[...], preferred_element_type=F32)
    o_ref[...] = x + 0.5 * _rms(y, post_ref[...])


def _ffn(x, pre, post, wg, wu, wo):
    t, d = x.shape
    f = wg.shape[1]
    tm = min(256, t)
    row = pl.BlockSpec((tm, d), lambda i: (i, 0))
    return pl.pallas_call(
        _ffn_body, grid=(t // tm,),
        in_specs=[row, _const_spec((1, d)), _const_spec((1, d)), _const_spec((d, f)), _const_spec((d, f)),
                  _const_spec((f, d))],
        out_specs=row, out_shape=jax.ShapeDtypeStruct((t, d), F32),
        compiler_params=_params(1), name="ffn")(x, pre, post, wg, wu, wo)


_SEG = {}
_off = 0
for _name, _w in (("q", ATT_WIDTH), ("ckv", KV_WIDTH), ("skv", KV_WIDTH), ("wkv", KV_WIDTH),
                  ("gates", N_KV_HEADS * TILE), ("z", SSD_INNER), ("xbc", CONV_DIM), ("dt", TILE)):
    _SEG[_name] = (_off, _off + _w)
    _off += _w
PROJ_WIDTH = _off


def _inproj_body(x_ref, pre_ref, w_ref, wt_ref, q_ref, ckv_ref, skv_ref, wkv_ref, skvb_ref, wkvb_ref,
                 gates_ref, z_ref, xbc_ref, dt_ref, dtt_ref, ckvt_ref, skvt_ref, wkvt_ref):
    h = _rms(x_ref[...], pre_ref[...]).astype(BF16)

    def proj(name):
        lo, hi = _SEG[name]
        return jnp.dot(h, w_ref[:, lo:hi], preferred_element_type=F32)

    q_ref[...] = proj("q")
    ckv_ref[...] = proj("ckv")
    skv = proj("skv")
    skv_ref[...] = skv
    skvb_ref[...] = skv.astype(BF16)
    wkv = proj("wkv")
    wkv_ref[...] = wkv
    wkvb_ref[...] = wkv.astype(BF16)
    gates_ref[...] = proj("gates")
    z_ref[...] = proj("z")
    xbc_ref[...] = proj("xbc")
    dt_ref[...] = proj("dt")
    dtt_ref[...] = lax.dot_general(wt_ref[0:TILE, :], h, NT_DIMS, preferred_element_type=F32)
    for n, ref in enumerate((ckvt_ref, skvt_ref, wkvt_ref)):
        lo = TILE + n * KV_WIDTH
        ref[...] = lax.dot_general(wt_ref[lo:lo + KV_WIDTH, :], h, NT_DIMS, preferred_element_type=F32)


def _inproj(x, pre, w, w_t, n_seq):
    t, d = x.shape
    tm = min(256, t)
    seq = t // n_seq
    per_seq = seq // tm
    assert seq % tm == 0

    def cols(height):
        return pl.BlockSpec((None, height, tm), lambda i: (i // per_seq, 0, i % per_seq))

    def rows(width):
        return pl.BlockSpec((tm, width), lambda i: (i, 0))

    widths = (ATT_WIDTH, KV_WIDTH, KV_WIDTH, KV_WIDTH, KV_WIDTH, KV_WIDTH, N_KV_HEADS * TILE, SSD_INNER,
              CONV_DIM, TILE)
    dtypes = (F32, F32, F32, F32, BF16, BF16, F32, F32, F32, F32)
    out_shape = [jax.ShapeDtypeStruct((t, wd), dt) for wd, dt in zip(widths, dtypes)]
    t_heights = (TILE, KV_WIDTH, KV_WIDTH, KV_WIDTH)
    out_shape += [jax.ShapeDtypeStruct((n_seq, ht, seq), F32) for ht in t_heights]
    out_specs = [rows(wd) for wd in widths] + [cols(ht) for ht in t_heights]
    return pl.pallas_call(
        _inproj_body, grid=(t // tm,),
        in_specs=[rows(d), _const_spec((1, d)), _const_spec((d, PROJ_WIDTH)), _const_spec(w_t.shape)],
        out_specs=out_specs, out_shape=out_shape,
        compiler_params=_params(1), name="inproj")(x, pre, w, w_t)


def _mixout_body(x_ref, oa_ref, os_ref, an_ref, post_ref, wa_ref, ws_ref, o_ref):
    a = _rms(oa_ref[...], an_ref[...]).astype(BF16)
    y = jnp.dot(a, wa_ref[...], preferred_element_type=F32)
    y = y + jnp.dot(os_ref[...].astype(BF16), ws_ref[...], preferred_element_type=F32)
    o_ref[...] = x_ref[...] + _rms(y, post_ref[...])


def _mixout(x, o_att, o_ssd, att_norm, post, wa, ws):
    t, d = x.shape
    tm = min(256, t)
    row = pl.BlockSpec((tm, d), lambda i: (i, 0))
    return pl.pallas_call(
        _mixout_body, grid=(t // tm,),
        in_specs=[row, row, row, _const_spec((1, d)), _const_spec((1, d)), _const_spec((d, d)),
                  _const_spec((d, d))],
        out_specs=row, out_shape=jax.ShapeDtypeStruct((t, d), F32),
        compiler_params=_params(1), name="mixout")(x, o_att, o_ssd, att_norm, post, wa, ws)


def _token_rows(ref, k, lo=0, n=None):
    n = ref.shape[-1] if n is None else n
    return ref[k, :, :, lo:lo + n].reshape(KV_GROUP, n).T


def _compress_rows(x_refs, pe_ref, bd_ref, n_blocks, pitch=BLK):
    acc = None
    for l0 in range(0, BLK, COMPRESS_ROWS_PER_DOT):
        pieces = []
        for l in range(l0, l0 + COMPRESS_ROWS_PER_DOT):
            pe = pe_ref[l:l + 1, :]
            xs = jnp.concatenate([x_ref[pl.ds(l, n_blocks, stride=pitch), :] + pe for x_ref in x_refs], axis=0)
            pieces.append(xs.astype(BF16))
        w = bd_ref[l0:l0 + COMPRESS_ROWS_PER_DOT].reshape(COMPRESS_ROWS_PER_DOT * KV_GROUP, KV_GROUP)
        part = jnp.dot(jnp.concatenate(pieces, axis=1), w, preferred_element_type=F32)
        acc = part if acc is None else acc + part
    return acc


def _store_summaries(acc, o_ref, n_blocks):
    for k in range(N_KV_HEADS):
        o_ref[0:n_blocks, k * KV_GROUP:(k + 1) * KV_GROUP] = acc[k * n_blocks:(k + 1) * n_blocks].astype(BF16)


def _compress_body(*refs, n_blocks, n_out):
    x_refs = refs[:N_KV_HEADS]
    pe_ref, bd_ref, o_ref = refs[N_KV_HEADS:]
    _store_summaries(_compress_rows(x_refs, pe_ref, bd_ref, n_blocks), o_ref, n_blocks)
    if n_out > n_blocks:
        o_ref[n_blocks:n_out, :] = jnp.zeros((n_out - n_blocks, KV_WIDTH), BF16)


def _compress(kv, pe, bd, n_out):
    b, s, _ = kv.shape
    n_blocks = s // BLK

    def head_spec(k):
        return pl.BlockSpec((None, s, KV_GROUP), lambda i: (i, 0, k))

    return pl.pallas_call(
        functools.partial(_compress_body, n_blocks=n_blocks, n_out=n_out), grid=(b,),
        in_specs=[head_spec(k) for k in range(N_KV_HEADS)]
        + [_const_spec((BLK, KV_GROUP)), _const_spec((BLK, KV_GROUP, KV_GROUP))],
        out_specs=pl.BlockSpec((None, n_out, KV_WIDTH), lambda i: (i, 0, 0)),
        out_shape=jax.ShapeDtypeStruct((b, n_out, KV_WIDTH), BF16),
        compiler_params=_params(1), name="compress")(*([kv] * N_KV_HEADS), pe, bd)


def _compress_paged_body(pt_ref, *refs, n_pages):
    page_refs = refs[:PAGES_PER_STEP]
    pe_ref, bd_ref, o_ref, stage_ref = refs[PAGES_PER_STEP:]
    j = pl.program_id(1)
    blocks_per_page = PAGE_SIZE // BLK
    for p in range(PAGES_PER_STEP):
        first_block = (j * PAGES_PER_STEP + p) * blocks_per_page
        for k in range(N_KV_HEADS):
            rows = _token_rows(page_refs[p], k)
            for h in range(blocks_per_page):
                at = pl.multiple_of((first_block + h) * STAGE_PITCH, SUBLANES)
                stage_ref[k, pl.ds(at, BLK), :] = rows[h * BLK:(h + 1) * BLK]

    @pl.when(j == pl.num_programs(1) - 1)
    def _():
        n_blocks = n_pages * blocks_per_page
        x_refs = [stage_ref.at[k] for k in range(N_KV_HEADS)]
        _store_summaries(_compress_rows(x_refs, pe_ref, bd_ref, n_blocks, STAGE_PITCH), o_ref, n_blocks)


def _compress_paged(cache, page_table, pe, bd):
    b, n_pages = page_table.shape
    steps = n_pages // PAGES_PER_STEP
    n_blocks = n_pages * (PAGE_SIZE // BLK)

    def page_spec(p):
        return pl.BlockSpec((None, N_KV_HEADS, 2, HEAD_DIM, PAGE_SIZE),
                            lambda i, j, pt: (pt[i, j * PAGES_PER_STEP + p], 0, 0, 0, 0))

    grid_spec = pltpu.PrefetchScalarGridSpec(
        num_scalar_prefetch=1, grid=(b, steps),
        in_specs=[page_spec(p) for p in range(PAGES_PER_STEP)]
        + [pl.BlockSpec((BLK, KV_GROUP), lambda i, j, pt: (0, 0)),
           pl.BlockSpec((BLK, KV_GROUP, KV_GROUP), lambda i, j, pt: (0, 0, 0))],
        out_specs=pl.BlockSpec((None, n_blocks, KV_WIDTH), lambda i, j, pt: (i, 0, 0)),
        scratch_shapes=[pltpu.VMEM((N_KV_HEADS, n_blocks * STAGE_PITCH, KV_GROUP), F32)])
    return pl.pallas_call(
        functools.partial(_compress_paged_body, n_pages=n_pages), grid_spec=grid_spec,
        out_shape=jax.ShapeDtypeStruct((b, n_blocks, KV_WIDTH), BF16),
        compiler_params=_params(2), name="compress_paged")(page_table, *([cache] * PAGES_PER_STEP), pe, bd)


def _rel_bucket(dist):
    n = jnp.maximum(dist, 0)
    max_exact = N_BUCKETS // 2
    nf = jnp.maximum(n, 1).astype(F32)
    log_b = max_exact + (jnp.log(nf / max_exact) / math.log(MAX_DISTANCE / max_exact)
                         * (N_BUCKETS - max_exact)).astype(jnp.int32)
    return jnp.where(n < max_exact, n, jnp.minimum(log_b, N_BUCKETS - 1))


def _lookup(table, bucket):
    onehot = (bucket[..., None] == jnp.arange(table.shape[0], dtype=jnp.int32)).astype(F32)
    return jnp.einsum('...b,b...->...', onehot, table, precision=lax.Precision.HIGHEST)


def _bias_of(rel_table, dist, visible):
    onehot = (_rel_bucket(dist)[..., None] == jnp.arange(N_BUCKETS, dtype=jnp.int32)).astype(F32)
    bias = jnp.einsum('rcb,bh->hrc', onehot, rel_table.astype(F32) * LOG2E, precision=lax.Precision.HIGHEST)
    return jnp.where(visible[None], bias, MASK_VALUE).reshape(-1, dist.shape[-1])


KIND_ZERO, KIND_DIAG, KIND_PREV, KIND_MASKED, KIND_OLDEST = range(5)


def _prompt_bias_tables(rel_table):
    r = jnp.arange(TILE, dtype=jnp.int32)[:, None]
    c = jnp.arange(TILE, dtype=jnp.int32)[None, :]
    true = jnp.ones((TILE, TILE), bool)
    far = _bias_of(rel_table, jnp.full((TILE, TILE), MAX_DISTANCE, jnp.int32), true)
    tiles = jnp.stack([
        jnp.zeros_like(far),
        _bias_of(rel_table, r - c, r >= c) - far,
        _bias_of(rel_table, TILE + r - c, true) - far,
        jnp.full_like(far, MASK_VALUE),
        jnp.where(jnp.tile(r < c, (N_ATT_HEADS, 1)), 0.0, MASK_VALUE),
    ])
    c_dist = r - (BLK - 1) + BLK * (BLK - 1 - c)
    cmp_tab = _bias_of(rel_table, c_dist, c_dist >= 0)
    return tiles, cmp_tab


def _selection_mask(score, n_rows):
    n_groups = score.shape[0] // SUBLANES
    groups = [score[SUBLANES * j:SUBLANES * (j + 1)] for j in range(n_groups)]
    ranks = [jnp.zeros(groups[0].shape, jnp.int32) for _ in range(n_groups)]
    row8 = lax.broadcasted_iota(jnp.int32, groups[0].shape, 0)
    for n in range(n_rows):
        other = jnp.broadcast_to(score[n:n + 1, :], groups[0].shape)
        for j in range(n_groups):
            lo = SUBLANES * j
            if lo > n:
                inc = jnp.where(other >= groups[j], 1, 0)
            elif lo + SUBLANES - 1 <= n:
                inc = jnp.where(other > groups[j], 1, 0)
            else:
                inc = jnp.where(row8 + lo > n, jnp.where(other >= groups[j], 1, 0),
                                jnp.where(other > groups[j], 1, 0))
            ranks[j] = ranks[j] + inc
    rank = jnp.concatenate(ranks, axis=0)
    return jnp.where(rank < N_SEL, jnp.where(score > -jnp.inf, 0.0, MASK_VALUE), MASK_VALUE)


PAIR = 2 * TILE
Q_TILES = PAIR // TILE
SEL_UNROLL = 4
_SEL_KINDS = {0: ((KIND_DIAG, KIND_MASKED), (KIND_PREV, KIND_DIAG)),
              1: ((KIND_ZERO, KIND_PREV), (KIND_ZERO, KIND_ZERO))}
_WIN_OLDEST = ((KIND_OLDEST, KIND_ZERO), (KIND_MASKED, KIND_OLDEST))


def _nsa_prompt_body(q_ref, gates_ref, kvc_ref, skv_ref, wkv_ref, tb_ref, ctab_ref, o_ref,
                     s_ref, sw_ref, mx_ref, mb_ref, acc_ref, *, n_blocks):
    step = pl.program_id(2)
    head_rows = GQA * TILE
    rows = Q_TILES * head_rows
    lane = lax.broadcasted_iota(jnp.int32, (rows, TILE), 1)
    low = lane < HEAD_DIM
    key_lane = lax.broadcasted_iota(jnp.int32, (PAIR, TILE), 1)
    key_blk = lax.broadcasted_iota(jnp.int32, (PAIR, TILE), 0) // BLK
    n_pairs = skv_ref.shape[0] // PAIR

    qf = q_ref[...]
    halves = [qf[:, c * TILE:(c + 1) * TILE] for c in range(GQA // 2)]
    rolled = [pltpu.roll(h, HEAD_DIM, 1) for h in halves]
    q_all = jnp.concatenate([(halves if g % 2 == 0 else rolled)[g // 2][t * TILE:(t + 1) * TILE]
                             for t in range(Q_TILES) for g in range(GQA)], axis=0)
    lhs_plain = jnp.where(low, q_all, 0.0).astype(BF16)

    def ones_and_values(kv):
        kv_lane = lax.broadcasted_iota(jnp.int32, kv.shape, 1)
        return jnp.where(kv_lane < HEAD_DIM, jnp.ones_like(kv), kv)

    def normalise(acc):
        return acc / jnp.maximum(pltpu.roll(acc, HEAD_DIM, 1), 1e-30)

    def bias_of(kinds_per_tile, valid):
        parts = []
        for left, right in kinds_per_tile:
            left = jnp.where(valid, left, KIND_MASKED)
            right = jnp.where(valid, right, KIND_MASKED)
            parts.append(jnp.concatenate([tb_ref[left], tb_ref[right]], axis=1))
        return jnp.concatenate(parts, axis=0)

    def sel_kinds(j):
        behind = step - j
        return [tuple(jnp.where(behind == 0, _SEL_KINDS[0][t][side],
                                jnp.where(behind == 1, _SEL_KINDS[1][t][side], KIND_ZERO)) for side in range(2))
                for t in range(Q_TILES)]

    def keys_of(kv_ref, j):
        jc = jnp.clip(j, 0, n_pairs - 1)
        return kv_ref[pl.ds(pl.multiple_of(jc * PAIR, PAIR), PAIR), :]

    def score_pair(lhs, kv_ref, j, bias, block_columns):
        rhs = keys_of(kv_ref, j)
        if block_columns:
            onehot = jnp.where(key_lane - HEAD_DIM == (PAIR // BLK) * j + key_blk, 1.0, 0.0).astype(BF16)
            rhs = jnp.where(key_lane < HEAD_DIM, rhs, onehot)
        return lax.dot_general(lhs, rhs, NT_DIMS, preferred_element_type=F32) + bias

    def half_max(sc):
        return jnp.maximum(sc[:, :TILE], sc[:, TILE:])

    def weigh_pair(sc, mb, kv_ref, j):
        pr = jnp.exp2(sc - jnp.concatenate([mb, mb], axis=1))
        return jnp.dot(pr.astype(BF16), ones_and_values(keys_of(kv_ref, j)), preferred_element_type=F32)

    def row_max(mx):
        return jnp.broadcast_to(jnp.max(mx, axis=1, keepdims=True), (rows, TILE))

    n_win = WINDOW // PAIR + 1
    w_pairs = [step - (n_win - 1) + w for w in range(n_win)]
    w_kinds = [_WIN_OLDEST] + [[(KIND_ZERO, KIND_ZERO)] * Q_TILES] * (n_win - 3) + [_SEL_KINDS[1], _SEL_KINDS[0]]
    w_mx = None
    for w, j in enumerate(w_pairs):
        sc = score_pair(lhs_plain, wkv_ref, j, bias_of(w_kinds[w], j >= 0), False)
        sw_ref[w] = sc
        w_mx = half_max(sc) if w_mx is None else jnp.maximum(w_mx, half_max(sc))

    kvc = kvc_ref[...]
    bias = []
    for t in range(Q_TILES):
        i = Q_TILES * step + t
        shift = (2 * i + (TILE - (BLK - 1))) % TILE
        bias += [pltpu.roll(ctab_ref[g * TILE:(g + 1) * TILE, :], shift, 1) for g in range(GQA)]
    s = lax.dot_general(lhs_plain, kvc, NT_DIMS, preferred_element_type=F32)
    s = jnp.where(lane < n_blocks, s + jnp.concatenate(bias, axis=0), MASK_VALUE)
    m = jnp.maximum(jnp.max(s, axis=1, keepdims=True), M_INIT)
    p = jnp.exp2(s - m)
    c_acc = jnp.dot(p.astype(BF16), ones_and_values(kvc), preferred_element_type=F32)
    p = p / jnp.maximum(c_acc, 1e-30)
    o_cmp = normalise(c_acc)

    w_mb = row_max(jnp.maximum(w_mx, M_INIT))
    w_acc = None
    for w, j in enumerate(w_pairs):
        dacc = weigh_pair(sw_ref[w], w_mb, wkv_ref, j)
        w_acc = dacc if w_acc is None else w_acc + dacc
    o_win = normalise(w_acc)

    blk = lax.broadcasted_iota(jnp.int32, (n_blocks, PAIR), 0)
    query = lax.broadcasted_iota(jnp.int32, (n_blocks, PAIR), 1)
    cur = (PAIR // BLK) * step + query // BLK
    imp_t = []
    for t in range(Q_TILES):
        imp = p[t * head_rows:t * head_rows + TILE]
        for g in range(1, GQA):
            imp = imp + p[t * head_rows + g * TILE:t * head_rows + (g + 1) * TILE]
        imp_t.append(imp.T[0:n_blocks])
    forced = (blk == 0) | (blk == cur) | (blk == cur - 1)
    score = jnp.where(forced, jnp.inf, jnp.where(blk <= cur, jnp.concatenate(imp_t, axis=1), -jnp.inf))
    neg_t = _selection_mask(score, n_blocks)
    neg = []
    for t in range(Q_TILES):
        pieces = [jnp.zeros((HEAD_DIM, TILE), F32), neg_t[:, t * TILE:(t + 1) * TILE]]
        if n_blocks < TILE - HEAD_DIM:
            pieces.append(jnp.zeros((TILE - HEAD_DIM - n_blocks, TILE), F32))
        neg += [jnp.concatenate(pieces, axis=0).T] * GQA
    lhs_sel = jnp.where(low, q_all, jnp.concatenate(neg, axis=0)).astype(BF16)

    n_trips = (step + SEL_UNROLL) // SEL_UNROLL
    mx_ref[...] = jnp.full(mx_ref.shape, M_INIT, F32)

    def score_trip(t, carry):
        mx = None
        for u in range(SEL_UNROLL):
            j = t * SEL_UNROLL + u
            sc = score_pair(lhs_sel, skv_ref, j, bias_of(sel_kinds(j), j <= step), True)
            s_ref[j] = sc
            mx = half_max(sc) if mx is None else jnp.maximum(mx, half_max(sc))
        mx_ref[...] = jnp.maximum(mx_ref[...], mx)
        return carry

    lax.fori_loop(0, n_trips, score_trip, 0)
    mb_ref[...] = row_max(mx_ref[...])
    acc_ref[...] = jnp.zeros(acc_ref.shape, F32)

    def weigh_trip(t, carry):
        mb = mb_ref[...]
        acc = None
        for u in range(SEL_UNROLL):
            j = t * SEL_UNROLL + u
            dacc = weigh_pair(s_ref[j], mb, skv_ref, j)
            acc = dacc if acc is None else acc + dacc
        acc_ref[...] += acc
        return carry

    lax.fori_loop(0, n_trips, weigh_trip, 0)
    o_sel = normalise(acc_ref[...])

    gate = jax.nn.sigmoid(gates_ref[...])
    out_low = lax.broadcasted_iota(jnp.int32, (TILE, TILE), 1) < HEAD_DIM
    for t in range(Q_TILES):
        outs = []
        for g in range(GQA):
            def gcol(br):
                c = g * N_BRANCH + br
                return gate[t * TILE:(t + 1) * TILE, c:c + 1]
            head = slice(t * head_rows + g * TILE, t * head_rows + (g + 1) * TILE)
            outs.append(gcol(0) * o_cmp[head] + gcol(1) * o_sel[head] + gcol(2) * o_win[head])
        for c in range(GQA // 2):
            o_ref[t * TILE:(t + 1) * TILE, c * TILE:(c + 1) * TILE] = jnp.where(
                out_low, pltpu.roll(outs[2 * c], HEAD_DIM, 1), outs[2 * c + 1])


def _nsa_prompt(q, gates, kvc, skv_b, wkv_b, tiles, cmp_tab):
    b, s, _ = q.shape
    n_blocks = s // BLK
    assert n_blocks <= TILE - HEAD_DIM and kvc.shape[1] == TILE and s % PAIR == 0 and WINDOW // PAIR >= 2
    gw = GQA * HEAD_DIM
    head_rows = GQA * TILE
    rows = Q_TILES * head_rows
    return pl.pallas_call(
        functools.partial(_nsa_prompt_body, n_blocks=n_blocks), grid=(b, N_KV_HEADS, s // PAIR),
        in_specs=[pl.BlockSpec((None, PAIR, gw), lambda bi, k, i: (bi, i, k)),
                  pl.BlockSpec((None, PAIR, TILE), lambda bi, k, i: (bi, i, k)),
                  pl.BlockSpec((None, TILE, KV_GROUP), lambda bi, k, i: (bi, 0, k)),
                  pl.BlockSpec((None, s, KV_GROUP), lambda bi, k, i: (bi, 0, k)),
                  pl.BlockSpec((None, s, KV_GROUP), lambda bi, k, i: (bi, 0, k)),
                  pl.BlockSpec((len(tiles), head_rows, TILE), lambda bi, k, i: (0, k, 0)),
                  pl.BlockSpec((head_rows, TILE), lambda bi, k, i: (k, 0))],
        out_specs=pl.BlockSpec((None, PAIR, gw), lambda bi, k, i: (bi, i, k)),
        out_shape=jax.ShapeDtypeStruct((b, s, ATT_WIDTH), F32),
        scratch_shapes=[pltpu.VMEM((s // PAIR + SEL_UNROLL - 1, rows, PAIR), F32),
                        pltpu.VMEM((WINDOW // PAIR + 1, rows, PAIR), F32),
                        pltpu.VMEM((rows, TILE), F32), pltpu.VMEM((rows, TILE), F32),
                        pltpu.VMEM((rows, KV_GROUP), F32)],
        compiler_params=_params(3), name="nsa_prompt")(q, gates, kvc, skv_b, wkv_b, tiles, cmp_tab)


def _t_softmax_step(kv, qbd, bias, m_ref, l_ref, acc_ref):
    s = jnp.dot(kv, qbd, preferred_element_type=F32) + bias
    m_prev = m_ref[...]
    m_new = jnp.maximum(m_prev, jnp.max(s, axis=0, keepdims=True))
    alpha = jnp.exp2(m_prev - m_new)
    p = jnp.exp2(s - m_new)
    l_ref[...] = alpha * l_ref[...] + jnp.sum(p, axis=0, keepdims=True)
    acc_ref[...] = alpha * acc_ref[...] + lax.dot_general(kv, p.astype(BF16), TN_DIMS,
                                                          preferred_element_type=F32)
    m_ref[...] = m_new


def _nsa_sample_body(pt_ref, *refs, n_pages, dec_seq, past_len):
    page_refs = refs[:PAGES_PER_STEP]
    (qbd_ref, gates_ref, kc_ref, ctab_ref, gsum_ref, snew_ref, cwin_ref, wnew_ref, last_ref, new_ref, far_ref,
     win0_ref, o_ref, neg_ref, m_ref, l_ref, acc_ref, oc_ref, ow_ref) = refs[PAGES_PER_STEP:]
    j = pl.program_id(1)
    n_past = n_pages * (PAGE_SIZE // BLK)
    n_rows = kc_ref.shape[0]
    sub = lax.broadcasted_iota(jnp.int32, (TILE, TILE), 0)
    qbd = qbd_ref[...]
    far = far_ref[0:1, :]

    def reset():
        m_ref[...] = jnp.full(m_ref.shape, M_INIT, F32)
        l_ref[...] = jnp.zeros(l_ref.shape, F32)
        acc_ref[...] = jnp.zeros(acc_ref.shape, F32)

    def result():
        return acc_ref[...] / jnp.maximum(l_ref[...], 1e-30)

    @pl.when(j == 0)
    def _():
        kc = kc_ref[...]
        s = jnp.dot(kc, qbd, preferred_element_type=F32) + ctab_ref[...]
        m = jnp.maximum(jnp.max(s, axis=0, keepdims=True), M_INIT)
        p = jnp.exp2(s - m)
        p = p / jnp.maximum(jnp.sum(p, axis=0, keepdims=True), 1e-30)
        oc_ref[...] = lax.dot_general(kc, p.astype(BF16), TN_DIMS, preferred_element_type=F32)
        imp = jnp.dot(p, gsum_ref[...], preferred_element_type=F32, precision=lax.Precision.HIGHEST)
        blk = lax.broadcasted_iota(jnp.int32, (n_rows, TILE), 0)
        tok = lax.broadcasted_iota(jnp.int32, (n_rows, TILE), 1) % dec_seq
        cur = (past_len + tok) // BLK
        forced = (blk == 0) | (blk == cur) | (blk == cur - 1)
        score = jnp.where(forced, jnp.inf, jnp.where(blk <= cur, imp, -jnp.inf))
        neg_ref[...] = _selection_mask(score, n_past + 1)

        reset()
        n_win = cwin_ref.shape[-1] // TILE
        for t in range(n_win):
            kv = jnp.concatenate([_token_rows(cwin_ref, k, t * TILE, TILE) for k in range(N_KV_HEADS)],
                                 axis=1).astype(BF16)
            bias = win0_ref[...] if t == 0 else (last_ref[...] if t == n_win - 1 else far)
            _t_softmax_step(kv, qbd, bias, m_ref, l_ref, acc_ref)
        _t_softmax_step(wnew_ref[...], qbd, new_ref[...], m_ref, l_ref, acc_ref)
        ow_ref[...] = result()
        reset()

    blocks_per_step = PAGES_PER_STEP * (PAGE_SIZE // BLK)
    neg_rows = neg_ref[pl.ds(pl.multiple_of(j * blocks_per_step, blocks_per_step), blocks_per_step), :]
    for p in range(PAGES_PER_STEP):
        page = j * PAGES_PER_STEP + p
        kv = jnp.concatenate([_token_rows(page_refs[p], k) for k in range(N_KV_HEADS)], axis=1).astype(BF16)
        first = neg_rows[2 * p:2 * p + 1, :]
        second = neg_rows[2 * p + 1:2 * p + 2, :]
        bias = jnp.where(sub < BLK, first, second) + jnp.where(page == n_pages - 1, last_ref[...], far)
        _t_softmax_step(kv, qbd, bias, m_ref, l_ref, acc_ref)

    @pl.when(j == pl.num_programs(1) - 1)
    def _():
        _t_softmax_step(snew_ref[...], qbd, new_ref[...] + neg_ref[n_past:n_past + 1, :], m_ref, l_ref, acc_ref)
        gate = jax.nn.sigmoid(gates_ref[...])
        o_ref[...] = gate[0:1, :] * oc_ref[...] + gate[1:2, :] * result() + gate[2:3, :] * ow_ref[...]


def _nsa_sample(page_table, cache_slc, qbd, gates_t, kc, ctab, gsum, slc_new, cache_win, win_new, last_tab,
                new_tab, far_row, win0_tab, *, dec_seq, past_len):
    b, n_pages = page_table.shape
    steps = n_pages // PAGES_PER_STEP
    n_rows = kc.shape[1]
    w_buf = cache_win.shape[-1]
    n_new = slc_new.shape[1]

    def page_spec(p):
        return pl.BlockSpec((None, N_KV_HEADS, 2, HEAD_DIM, PAGE_SIZE),
                            lambda i, j, pt: (pt[i, j * PAGES_PER_STEP + p], 0, 0, 0, 0))

    def per_batch(*shape):
        zeros = (0,) * len(shape)
        return pl.BlockSpec((None,) + shape, lambda i, j, pt: (i,) + zeros)

    def const(*shape):
        zeros = (0,) * len(shape)
        return pl.BlockSpec(shape, lambda i, j, pt: zeros)

    grid_spec = pltpu.PrefetchScalarGridSpec(
        num_scalar_prefetch=1, grid=(b, steps),
        in_specs=[page_spec(p) for p in range(PAGES_PER_STEP)]
        + [per_batch(KV_WIDTH, TILE), per_batch(SUBLANES, TILE), per_batch(n_rows, KV_WIDTH), const(n_rows, TILE),
           const(TILE, TILE), per_batch(n_new, KV_WIDTH), per_batch(N_KV_HEADS, 2, HEAD_DIM, w_buf),
           per_batch(n_new, KV_WIDTH), const(TILE, TILE), const(n_new, TILE), const(SUBLANES, TILE),
           const(TILE, TILE)],
        out_specs=per_batch(KV_WIDTH, TILE),
        scratch_shapes=[pltpu.VMEM((n_rows, TILE), F32), pltpu.VMEM((1, TILE), F32), pltpu.VMEM((1, TILE), F32),
                        pltpu.VMEM((KV_WIDTH, TILE), F32), pltpu.VMEM((KV_WIDTH, TILE), F32),
                        pltpu.VMEM((KV_WIDTH, TILE), F32)])
    return pl.pallas_call(
        functools.partial(_nsa_sample_body, n_pages=n_pages, dec_seq=dec_seq, past_len=past_len),
        grid_spec=grid_spec, out_shape=jax.ShapeDtypeStruct((b, KV_WIDTH, TILE), F32),
        compiler_params=_params(2), name="nsa_sample")(
            page_table, *([cache_slc] * PAGES_PER_STEP), qbd, gates_t, kc, ctab, gsum, slc_new, cache_win, win_new,
            last_tab, new_tab, far_row, win0_tab)


def _split3(x):
    hi = x.astype(BF16)
    r = x - hi.astype(F32)
    mid = r.astype(BF16)
    return hi, mid, (r - mid.astype(F32)).astype(BF16)


def _select_sum(x, onehot, x_on_left):
    out = None
    for part in _split3(x):
        term = (jnp.dot(part, onehot, preferred_element_type=F32) if x_on_left
                else jnp.dot(onehot, part, preferred_element_type=F32))
        out = term if out is None else out + term
    return out


def _softplus(x):
    return jnp.maximum(x, 0.0) + jnp.log(1.0 + jnp.exp(-jnp.abs(x)))


def _ssd_body(z_ref, xbc_ref, dt_ref, dtt_ref, convp_ref, h0_ref, cw_ref, cb_ref, dtb_ref, dtbt_ref, al_ref,
              alt_ref, dskip_ref, norm_ref, o_ref, hout_ref, convout_ref, xp_ref, *, cl):
    c = pl.program_id(1)
    pad = SUBLANES

    @pl.when(c == 0)
    def _():
        xp_ref[0:pad, :] = convp_ref[...]
        hout_ref[...] = h0_ref[...]

    xp_ref[pad:pad + cl, :] = xbc_ref[...]
    conv = cb_ref[...]
    for k in range(CONV_W):
        lo = pad - (CONV_W - 1) + k
        conv = conv + cw_ref[k:k + 1, :] * xp_ref[lo:lo + cl, :]
    u = conv * jax.nn.sigmoid(conv)
    tail = xp_ref[pad + cl - (CONV_W - 1):pad + cl, :]
    convout_ref[...] = tail
    xp_ref[pad - (CONV_W - 1):pad, :] = tail

    xs = u[:, :SSD_INNER]
    dt = _softplus(dt_ref[:, 0:SSD_HEADS] + dtb_ref[...])
    dt_t = _softplus(dtt_ref[0:SSD_HEADS, :] + dtbt_ref[...])
    la = dt * -jnp.exp(al_ref[...])
    la_t = dt_t * -jnp.exp(alt_ref[...])
    row = lax.broadcasted_iota(jnp.int32, (cl, cl), 0)
    col = lax.broadcasted_iota(jnp.int32, (cl, cl), 1)
    causal = row >= col
    cs = _select_sum(la, jnp.where(causal, 1.0, 0.0).astype(BF16), False)
    cs_t = _select_sum(la_t, jnp.where(row <= col, 1.0, 0.0).astype(BF16), True)

    def spread(width):
        lane_head = lax.broadcasted_iota(jnp.int32, (SSD_HEADS, SSD_HEADS * width), 1) // width
        return jnp.where(lane_head == lax.broadcasted_iota(jnp.int32, lane_head.shape, 0), 1.0, 0.0).astype(BF16)

    dt_x = _select_sum(dt, spread(SSD_HEAD_DIM), True)
    cs_x = _select_sum(cs, spread(SSD_HEAD_DIM), True)
    cs_wide = _select_sum(cs, spread(TILE), True)
    last_x = cs_x[cl - 1:cl, :]
    xdt = xs * dt_x
    xdt_b = xdt.astype(BF16)
    to_end = (xdt * jnp.exp(last_x - cs_x)).astype(BF16)
    grow = jnp.exp(cs_x)
    skip = dskip_ref[...] * xs

    heads_per_group = SSD_HEADS // SSD_GROUPS
    bms, cms, cbs = [], [], []
    for g in range(SSD_GROUPS):
        b_lo = SSD_INNER + g * SSD_STATE
        c_lo = SSD_INNER + SSD_GROUPS * SSD_STATE + g * SSD_STATE
        bms.append(u[:, b_lo:b_lo + SSD_STATE].astype(BF16))
        cms.append(u[:, c_lo:c_lo + SSD_STATE].astype(BF16))
        cbs.append(lax.dot_general(cms[g], bms[g], NT_DIMS, preferred_element_type=F32))
    first_head = lax.broadcasted_iota(jnp.int32, (cl, TILE), 1) < SSD_HEAD_DIM
    state_row = lax.broadcasted_iota(jnp.int32, (2 * SSD_HEAD_DIM, SSD_STATE), 0)
    ys = []
    for pair in range(SSD_HEADS // 2):
        g = 2 * pair // heads_per_group
        lanes = slice(pair * TILE, (pair + 1) * TILE)
        y_heads, keep = [], []
        for h in (2 * pair, 2 * pair + 1):
            diff = cs_wide[:, h * TILE:h * TILE + cl] - cs_t[h:h + 1, :]
            decay = jnp.where(causal, jnp.exp(jnp.where(causal, diff, 0.0)), 0.0)
            y_heads.append(jnp.dot((cbs[g] * decay).astype(BF16), xdt_b[:, lanes], preferred_element_type=F32))
            keep.append(jnp.exp(cs[cl - 1:cl, h:h + 1]))
        state_in = lax.dot_general(to_end[:, lanes], bms[g], TN_DIMS, preferred_element_type=F32)
        h_prev = jnp.concatenate([hout_ref[2 * pair], hout_ref[2 * pair + 1]], axis=0)
        y_off = lax.dot_general(cms[g], h_prev.astype(BF16), NT_DIMS, preferred_element_type=F32)
        h_new = h_prev * jnp.where(state_row < SSD_HEAD_DIM, keep[0], keep[1]) + state_in
        hout_ref[2 * pair] = h_new[0:SSD_HEAD_DIM]
        hout_ref[2 * pair + 1] = h_new[SSD_HEAD_DIM:]
        ys.append(jnp.where(first_head, y_heads[0], y_heads[1]) + y_off * grow[:, lanes] + skip[:, lanes])
    y = jnp.concatenate(ys, axis=1)
    z = z_ref[...]
    y = y * (z * jax.nn.sigmoid(z))
    gw = SSD_INNER // SSD_GROUPS
    normed = []
    for g in range(SSD_GROUPS):
        yg = y[:, g * gw:(g + 1) * gw]
        normed.append(yg * lax.rsqrt(jnp.mean(yg * yg, axis=-1, keepdims=True) + EPS))
    o_ref[...] = jnp.concatenate(normed, axis=1) * norm_ref[...]


def _ssd(z, xbc, dt, dt_t, conv_prev, h0, consts, cl):
    b, seq, _ = z.shape
    nc = seq // cl
    cw, cbias, dtb, dtb_t, al, al_t, dskip, norm = consts

    def rows(width):
        return pl.BlockSpec((None, cl, width), lambda i, c: (i, c, 0))

    state = pl.BlockSpec((None, SSD_HEADS, SSD_HEAD_DIM, SSD_STATE), lambda i, c: (i, 0, 0, 0))

    def const(x):
        zeros = (0,) * x.ndim
        return pl.BlockSpec(x.shape, lambda i, c: zeros)

    return pl.pallas_call(
        functools.partial(_ssd_body, cl=cl), grid=(b, nc),
        in_specs=[rows(SSD_INNER), rows(CONV_DIM), rows(TILE),
                  pl.BlockSpec((None, TILE, cl), lambda i, c: (i, 0, c)),
                  pl.BlockSpec((None, SUBLANES, CONV_DIM), lambda i, c: (i, 0, 0)), state,
                  const(cw), const(cbias), const(dtb), const(dtb_t), const(al), const(al_t), const(dskip),
                  const(norm)],
        out_specs=[rows(SSD_INNER), state, pl.BlockSpec((None, CONV_W - 1, CONV_DIM), lambda i, c: (i, 0, 0))],
        out_shape=[jax.ShapeDtypeStruct((b, seq, SSD_INNER), F32),
                   jax.ShapeDtypeStruct((b, SSD_HEADS, SSD_HEAD_DIM, SSD_STATE), F32),
                   jax.ShapeDtypeStruct((b, CONV_W - 1, CONV_DIM), F32)],
        scratch_shapes=[pltpu.VMEM((SUBLANES + cl, CONV_DIM), F32)],
        compiler_params=_params(2), name="ssd")(z, xbc, dt, dt_t, conv_prev, h0, cw, cbias, dtb, dtb_t, al, al_t,
                                                 dskip, norm)


def _pack_inproj(w_in):
    splits = (ATT_WIDTH, KV_WIDTH, KV_WIDTH, KV_WIDTH, N_BRANCH * N_ATT_HEADS, SSD_INNER, CONV_DIM, SSD_HEADS)
    offs = [0]
    for s in splits:
        offs.append(offs[-1] + s)
    wq, wc, ws, ww, wg, wz, wx, wd = [w_in[:, offs[i]:offs[i + 1]] for i in range(len(splits))]
    d = w_in.shape[0]
    per_kv = GQA * N_BRANCH
    wg4 = jnp.pad(wg.reshape(d, N_KV_HEADS, per_kv), ((0, 0), (0, 0), (0, TILE - per_kv)))
    wd_pad = jnp.pad(wd, ((0, 0), (0, TILE - SSD_HEADS)))
    w = jnp.concatenate([wq * (HEAD_DIM ** -0.5 * LOG2E), wc, ws, ww, wg4.reshape(d, N_KV_HEADS * TILE), wz, wx,
                         wd_pad], axis=1)
    assert w.shape[1] == PROJ_WIDTH
    w_t = jnp.concatenate([wd_pad, wc, ws, ww], axis=1).T
    return w.astype(BF16), w_t.astype(BF16)


def _pack_compress(cmp_pe, w_cmp):
    zero = jnp.zeros((BLK, HEAD_DIM, HEAD_DIM), w_cmp.dtype)
    bd = jnp.concatenate([jnp.concatenate([w_cmp[:, 0], zero], axis=2),
                          jnp.concatenate([zero, w_cmp[:, 1]], axis=2)], axis=1)
    pe = cmp_pe.reshape(BLK, KV_GROUP)
    return pe.astype(F32), bd.astype(BF16)


def _sample_tables(rel_table, dec_seq, past_len, n_rows, n_new, w_buf):
    col = jnp.arange(TILE, dtype=jnp.int32)
    head = (col // (GQA * dec_seq)) * GQA + (col // dec_seq) % GQA
    tok = (col % dec_seq)[None, :]
    table = rel_table.astype(F32) * LOG2E

    def bias(dist, visible):
        return jnp.where(visible, _lookup(table[:, head][:, None, :], _rel_bucket(dist)), MASK_VALUE)

    row = jnp.arange(TILE, dtype=jnp.int32)[:, None]
    blk = jnp.arange(n_rows, dtype=jnp.int32)[:, None]
    c_dist = past_len + tok - ((blk + 1) * BLK - 1)
    ctab = bias(c_dist, c_dist >= 0)
    last_tab = bias(TILE + tok - row, jnp.ones((TILE, TILE), bool))
    new_row = jnp.arange(n_new, dtype=jnp.int32)[:, None]
    new_tab = bias(tok - new_row, (tok >= new_row) & (new_row < dec_seq))
    far = jnp.full((SUBLANES, TILE), w_buf, jnp.int32)
    far_row = bias(far, jnp.ones((SUBLANES, TILE), bool))
    win0_tab = bias(jnp.full((TILE, TILE), w_buf, jnp.int32), row > tok)
    same = (col[:, None] // (GQA * dec_seq) == col[None, :] // (GQA * dec_seq)) & \
           (col[:, None] % dec_seq == col[None, :] % dec_seq)
    return ctab, last_tab, new_tab, far_row, win0_tab, same.astype(F32)


def _layer_weights(ffn_pre, ffn_post, ffn_w_in, ffn_w_out):
    f = ffn_w_out.shape[0]
    return (ffn_pre[None, :], ffn_post[None, :], ffn_w_in[:, :f].astype(BF16), ffn_w_in[:, f:].astype(BF16),
            ffn_w_out.astype(BF16))


def kernel(x_prompt, x_sample, cache_cmp_kv, cache_slc_kv, cache_win_kv, state_ssm, state_conv, page_table,
           rel_table, ffn1_pre, ffn1_post, ffn1_w_in, ffn1_w_out, mix_pre, mix_post, w_in, w_out, att_out_norm,
           cmp_pe, w_cmp, conv_w, conv_b, dt_bias, a_log, d_skip, ssd_norm,
           ffn2_pre, ffn2_post, ffn2_w_in, ffn2_w_out):
    depth = ffn1_pre.shape[0]
    bp, seq, d = x_prompt.shape
    bs, dec_seq, _ = x_sample.shape
    n_pages = page_table.shape[1]
    past_len = n_pages * PAGE_SIZE
    w_buf = cache_win_kv.shape[2]
    kv_shape = (N_KV_HEADS, 2, HEAD_DIM)
    assert GQA * dec_seq * N_KV_HEADS == TILE and w_buf == WINDOW and seq % TILE == 0

    yp = x_prompt.reshape(bp * seq, d)
    ys = x_sample.reshape(bs * dec_seq, d)
    outs = [[] for _ in range(10)]
    for l in range(depth):
        ffn1 = _layer_weights(ffn1_pre[l], ffn1_post[l], ffn1_w_in[l], ffn1_w_out[l])
        ffn2 = _layer_weights(ffn2_pre[l], ffn2_post[l], ffn2_w_in[l], ffn2_w_out[l])
        w_proj, w_proj_t = _pack_inproj(w_in[l])
        pe, bd = _pack_compress(cmp_pe[l], w_cmp[l])
        wo_att = w_out[l][:ATT_WIDTH].astype(BF16)
        wo_ssd = w_out[l][ATT_WIDTH:].astype(BF16)
        ssd_consts = (conv_w[l], conv_b[l][None, :], dt_bias[l][None, :], dt_bias[l][:, None], a_log[l][None, :],
                      a_log[l][:, None], jnp.repeat(d_skip[l], SSD_HEAD_DIM)[None, :], ssd_norm[l][None, :])
        mix_pre_l, mix_post_l, att_norm_l = mix_pre[l][None, :], mix_post[l][None, :], att_out_norm[l][None, :]

        yp = _ffn(yp, *ffn1)
        (q, ckv, skv, wkv, skv_b, wkv_b, gates, z, xbc, dt, dt_t, ckv_t, skv_t, wkv_t) = _inproj(
            yp, mix_pre_l, w_proj, w_proj_t, bp)
        kvc = _compress(ckv.reshape(bp, seq, KV_WIDTH), pe, bd, TILE)
        tiles, cmp_tab = _prompt_bias_tables(rel_table)
        o_att = _nsa_prompt(q.reshape(bp, seq, ATT_WIDTH), gates.reshape(bp, seq, N_KV_HEADS * TILE), kvc,
                            skv_b.reshape(bp, seq, KV_WIDTH), wkv_b.reshape(bp, seq, KV_WIDTH), tiles, cmp_tab)
        cl = min(CHUNK, seq)
        o_ssd, h_new, conv_new = _ssd(
            z.reshape(bp, seq, SSD_INNER), xbc.reshape(bp, seq, CONV_DIM), dt.reshape(bp, seq, TILE),
            dt_t, jnp.zeros((bp, SUBLANES, CONV_DIM), F32),
            jnp.zeros((bp, SSD_HEADS, SSD_HEAD_DIM, SSD_STATE), F32), ssd_consts, cl)
        yp = _mixout(yp, o_att.reshape(bp * seq, ATT_WIDTH), o_ssd.reshape(bp * seq, SSD_INNER), att_norm_l,
                     mix_post_l, wo_att, wo_ssd)

        def token_major(a_t):
            return a_t.reshape((bp,) + kv_shape + (a_t.shape[-1],)).transpose(0, 4, 1, 2, 3)

        outs[0].append(token_major(ckv_t))
        outs[1].append(token_major(skv_t))
        outs[2].append(token_major(wkv_t[:, :, -min(WINDOW, seq):]))
        outs[3].append(h_new)
        outs[4].append(conv_new)

        ys = _ffn(ys, *ffn1)
        q, ckv, skv, wkv, skv_b, wkv_b, gates, z, xbc, dt, dt_t, _, _, _ = _inproj(ys, mix_pre_l, w_proj, w_proj_t, 1)
        native = (0, 2, 3, 4, 1)
        kc_past = _compress_paged(cache_cmp_kv[l].transpose(native), page_table, pe, bd)
        cmp_new = jnp.pad(ckv.reshape(bs, dec_seq, KV_WIDTH), ((0, 0), (0, BLK - dec_seq), (0, 0)))
        kc_new = _compress(cmp_new.reshape(1, bs * BLK, KV_WIDTH), pe, bd, bs).reshape(bs, 1, KV_WIDTH)
        n_past = kc_past.shape[1]
        n_rows = -(-(n_past + 1) // 16) * 16
        kc = jnp.concatenate([kc_past, kc_new, jnp.zeros((bs, n_rows - n_past - 1, KV_WIDTH), BF16)], axis=1)
        n_new = 16
        ctab, last_tab, new_tab, far_row, win0_tab, gsum = _sample_tables(rel_table, dec_seq, past_len, n_rows,
                                                                          n_new, w_buf)
        q5 = q.reshape(bs, dec_seq, N_KV_HEADS, GQA, HEAD_DIM).transpose(0, 2, 4, 3, 1)
        q5 = q5.reshape(bs, N_KV_HEADS, HEAD_DIM, GQA * dec_seq)
        eye = jnp.eye(N_KV_HEADS, dtype=F32)[None, :, None, :, None]
        qbd = jnp.pad(q5, ((0, 0), (0, 0), (0, HEAD_DIM), (0, 0)))[:, :, :, None, :] * eye
        qbd = qbd.reshape(bs, KV_WIDTH, TILE).astype(BF16)
        g5 = gates.reshape(bs, dec_seq, N_KV_HEADS, TILE)[..., :GQA * N_BRANCH]
        g5 = g5.reshape(bs, dec_seq, N_KV_HEADS, GQA, N_BRANCH).transpose(0, 4, 2, 3, 1).reshape(bs, N_BRANCH, TILE)
        gates_t = jnp.pad(g5, ((0, 0), (0, SUBLANES - N_BRANCH), (0, 0)))

        def pad_new(a):
            return jnp.pad(a.reshape(bs, dec_seq, KV_WIDTH), ((0, 0), (0, n_new - dec_seq), (0, 0)))

        o_t = _nsa_sample(page_table, cache_slc_kv[l].transpose(native), qbd, gates_t, kc, ctab,
                          gsum, pad_new(skv_b), cache_win_kv[l].transpose(native), pad_new(wkv_b),
                          last_tab, new_tab, far_row, win0_tab, dec_seq=dec_seq, past_len=past_len)
        o7 = o_t.reshape(bs, N_KV_HEADS, 2, HEAD_DIM, N_KV_HEADS, GQA, dec_seq)
        o_att = jnp.stack([o7[:, k, 1, :, k] for k in range(N_KV_HEADS)], axis=1).transpose(0, 4, 1, 3, 2)
        o_att = o_att.reshape(bs * dec_seq, ATT_WIDTH)
        conv_prev = jnp.pad(state_conv[l], ((0, 0), (SUBLANES - (CONV_W - 1), 0), (0, 0)))
        o_ssd, h_new, conv_new = _ssd(
            z.reshape(bs, dec_seq, SSD_INNER), xbc.reshape(bs, dec_seq, CONV_DIM), dt.reshape(bs, dec_seq, TILE),
            dt_t.reshape(TILE, bs, dec_seq).transpose(1, 0, 2), conv_prev, state_ssm[l], ssd_consts,
            min(CHUNK, dec_seq))
        ys = _mixout(ys, o_att, o_ssd.reshape(bs * dec_seq, SSD_INNER), att_norm_l, mix_post_l, wo_att, wo_ssd)
        win_all = jnp.concatenate([cache_win_kv[l], wkv.reshape((bs, dec_seq) + kv_shape)], axis=1)
        outs[5].append(ckv.reshape((bs, dec_seq) + kv_shape))
        outs[6].append(skv.reshape((bs, dec_seq) + kv_shape))
        outs[7].append(win_all[:, -min(WINDOW, w_buf + dec_seq):])
        outs[8].append(h_new)
        outs[9].append(conv_new)

        yp = _ffn(yp, *ffn2)
        ys = _ffn(ys, *ffn2)

    return (yp.reshape(bp, seq, d), ys.reshape(bs, dec_seq, d)) + tuple(jnp.stack(o) for o in outs)
```

```python
import functools
import math

import jax
import jax.numpy as jnp
from jax import lax
from jax.experimental import pallas as pl
from jax.experimental.pallas import tpu as pltpu

F32 = jnp.float32
BF16 = jnp.bfloat16

D_MODEL = 1024
N_ATT_HEADS = 16
HEAD_DIM = 64
N_KV_HEADS = 4
GQA = N_ATT_HEADS // N_KV_HEADS
ATT_WIDTH = N_ATT_HEADS * HEAD_DIM
KV_WIDTH = 2 * N_KV_HEADS * HEAD_DIM
KV_GROUP = 2 * HEAD_DIM
BLK = 64
N_SEL = 16
WINDOW = 512
N_BRANCH = 3
N_BUCKETS = 32
MAX_DISTANCE = 128
SSD_HEADS = 16
SSD_HEAD_DIM = 64
SSD_INNER = SSD_HEADS * SSD_HEAD_DIM
SSD_GROUPS = 2
SSD_STATE = 128
CONV_W = 4
CONV_DIM = SSD_INNER + 2 * SSD_GROUPS * SSD_STATE
CHUNK = 128
PAGE_SIZE = 128
EPS = 1e-6

TILE = 128
SUBLANES = 8
MASK_VALUE = -(2.0 ** 100)
M_INIT = -1e30
LOG2E = math.log2(math.e)
VMEM_LIMIT = 56 * 1024 * 1024
PAGES_PER_STEP = 16
COMPRESS_ROWS_PER_DOT = 16
STAGE_PITCH = 72

NT_DIMS = (((1,), (1,)), ((), ()))
TN_DIMS = (((0,), (0,)), ((), ()))


def _params(n_grid_dims):
    return pltpu.CompilerParams(dimension_semantics=("arbitrary",) * n_grid_dims,
                                vmem_limit_bytes=VMEM_LIMIT)


def _rms(x, g):
    return x * lax.rsqrt(jnp.mean(x * x, axis=-1, keepdims=True) + EPS) * g


def _const_spec(shape):
    zeros = (0,) * len(shape)
    return pl.BlockSpec(shape, lambda *_: zeros, pipeline_mode=pl.Buffered(1))


def _ffn_body(x_ref, pre_ref, post_ref, wg_ref, wu_ref, wo_ref, o_ref):
    x = x_ref[...]
    h = _rms(x, pre_ref[...]).astype(BF16)
    gate = jnp.dot(h, wg_ref[...], preferred_element_type=F32)
    up = jnp.dot(h, wu_ref[...], preferred_element_type=F32)
    a = (gate * jax.nn.sigmoid(gate) * up).astype(BF16)
    y = jnp.dot(a, wo_ref[...], preferred_element_type=F32)
    o_ref[...] = x + 0.5 * _rms(y, post_ref[...])


def _ffn(x, pre, post, wg, wu, wo):
    t, d = x.shape
    f = wg.shape[1]
    tm = min(256, t)
    row = pl.BlockSpec((tm, d), lambda i: (i, 0))
    return pl.pallas_call(
        _ffn_body, grid=(t // tm,),
        in_specs=[row, _const_spec((1, d)), _const_spec((1, d)), _const_spec((d, f)), _const_spec((d, f)),
                  _const_spec((f, d))],
        out_specs=row, out_shape=jax.ShapeDtypeStruct((t, d), F32),
        compiler_params=_params(1), name="ffn")(x, pre, post, wg, wu, wo)


_SEG = {}
_off = 0
for _name, _w in (("q", ATT_WIDTH), ("ckv", KV_WIDTH), ("skv", KV_WIDTH), ("wkv", KV_WIDTH),
                  ("gates", N_KV_HEADS * TILE), ("z", SSD_INNER), ("xbc", CONV_DIM), ("dt", TILE)):
    _SEG[_name] = (_off, _off + _w)
    _off += _w
PROJ_WIDTH = _off


def _inproj_body(x_ref, pre_ref, w_ref, wt_ref, q_ref, ckv_ref, skv_ref, wkv_ref, skvb_ref, wkvb_ref,
                 gates_ref, z_ref, xbc_ref, dt_ref, dtt_ref, ckvt_ref, skvt_ref, wkvt_ref):
    h = _rms(x_ref[...], pre_ref[...]).astype(BF16)

    def proj(name):
        lo, hi = _SEG[name]
        return jnp.dot(h, w_ref[:, lo:hi], preferred_element_type=F32)

    q_ref[...] = proj("q")
    ckv_ref[...] = proj("ckv")
    skv = proj("skv")
    skv_ref[...] = skv
    skvb_ref[...] = skv.astype(BF16)
    wkv = proj("wkv")
    wkv_ref[...] = wkv
    wkvb_ref[...] = wkv.astype(BF16)
    gates_ref[...] = proj("gates")
    z_ref[...] = proj("z")
    xbc_ref[...] = proj("xbc")
    dt_ref[...] = proj("dt")
    dtt_ref[...] = lax.dot_general(wt_ref[0:TILE, :], h, NT_DIMS, preferred_element_type=F32)
    for n, ref in enumerate((ckvt_ref, skvt_ref, wkvt_ref)):
        lo = TILE + n * KV_WIDTH
        ref[...] = lax.dot_general(wt_ref[lo:lo + KV_WIDTH, :], h, NT_DIMS, preferred_element_type=F32)


def _inproj(x, pre, w, w_t, n_seq):
    t, d = x.shape
    tm = min(256, t)
    seq = t // n_seq
    per_seq = seq // tm
    assert seq % tm == 0

    def cols(height):
        return pl.BlockSpec((None, height, tm), lambda i: (i // per_seq, 0, i % per_seq))

    def rows(width):
        return pl.BlockSpec((tm, width), lambda i: (i, 0))

    widths = (ATT_WIDTH, KV_WIDTH, KV_WIDTH, KV_WIDTH, KV_WIDTH, KV_WIDTH, N_KV_HEADS * TILE, SSD_INNER,
              CONV_DIM, TILE)
    dtypes = (F32, F32, F32, F32, BF16, BF16, F32, F32, F32, F32)
    out_shape = [jax.ShapeDtypeStruct((t, wd), dt) for wd, dt in zip(widths, dtypes)]
    t_heights = (TILE, KV_WIDTH, KV_WIDTH, KV_WIDTH)
    out_shape += [jax.ShapeDtypeStruct((n_seq, ht, seq), F32) for ht in t_heights]
    out_specs = [rows(wd) for wd in widths] + [cols(ht) for ht in t_heights]
    return pl.pallas_call(
        _inproj_body, grid=(t // tm,),
        in_specs=[rows(d), _const_spec((1, d)), _const_spec((d, PROJ_WIDTH)), _const_spec(w_t.shape)],
        out_specs=out_specs, out_shape=out_shape,
        compiler_params=_params(1), name="inproj")(x, pre, w, w_t)


def _mixout_body(x_ref, oa_ref, os_ref, an_ref, post_ref, wa_ref, ws_ref, o_ref):
    a = _rms(oa_ref[...], an_ref[...]).astype(BF16)
    y = jnp.dot(a, wa_ref[...], preferred_element_type=F32)
    y = y + jnp.dot(os_ref[...].astype(BF16), ws_ref[...], preferred_element_type=F32)
    o_ref[...] = x_ref[...] + _rms(y, post_ref[...])


def _mixout_ffn_body(x_ref, oa_ref, os_ref, an_ref, mpost_ref, wa_ref, ws_ref, pre_ref, post_ref, wg_ref, wu_ref,
                     wo_ref, o_ref):
    _mixout_body(x_ref, oa_ref, os_ref, an_ref, mpost_ref, wa_ref, ws_ref, o_ref)
    _ffn_body(o_ref, pre_ref, post_ref, wg_ref, wu_ref, wo_ref, o_ref)


def _mixout_ffn(x, o_att, o_ssd, att_norm, mix_post, wa, ws, pre, post, wg, wu, wo):
    t, d = x.shape
    f = wg.shape[1]
    tm = min(256, t)
    row = pl.BlockSpec((tm, d), lambda i: (i, 0))
    vec = _const_spec((1, d))
    return pl.pallas_call(
        _mixout_ffn_body, grid=(t // tm,),
        in_specs=[row, row, row, vec, vec, _const_spec((d, d)), _const_spec((d, d)), vec, vec,
                  _const_spec((d, f)), _const_spec((d, f)), _const_spec((f, d))],
        out_specs=row, out_shape=jax.ShapeDtypeStruct((t, d), F32),
        compiler_params=_params(1), name="mixout_ffn")(x, o_att, o_ssd, att_norm, mix_post, wa, ws, pre, post, wg,
                                                       wu, wo)


def _token_rows(ref, k, lo=0, n=None):
    n = ref.shape[-1] if n is None else n
    return ref[k, :, :, lo:lo + n].reshape(KV_GROUP, n).T


def _compress_rows(x_refs, pe_ref, bd_ref, n_blocks, pitch=BLK):
    acc = None
    for l0 in range(0, BLK, COMPRESS_ROWS_PER_DOT):
        pieces = []
        for l in range(l0, l0 + COMPRESS_ROWS_PER_DOT):
            pe = pe_ref[l:l + 1, :]
            xs = jnp.concatenate([x_ref[pl.ds(l, n_blocks, stride=pitch), :] + pe for x_ref in x_refs], axis=0)
            pieces.append(xs.astype(BF16))
        w = bd_ref[l0:l0 + COMPRESS_ROWS_PER_DOT].reshape(COMPRESS_ROWS_PER_DOT * KV_GROUP, KV_GROUP)
        part = jnp.dot(jnp.concatenate(pieces, axis=1), w, preferred_element_type=F32)
        acc = part if acc is None else acc + part
    return acc


def _store_summaries(acc, o_ref, n_blocks):
    for k in range(N_KV_HEADS):
        o_ref[0:n_blocks, k * KV_GROUP:(k + 1) * KV_GROUP] = acc[k * n_blocks:(k + 1) * n_blocks].astype(BF16)


def _compress_body(*refs, n_blocks, n_out):
    x_refs = refs[:N_KV_HEADS]
    pe_ref, bd_ref, o_ref = refs[N_KV_HEADS:]
    _store_summaries(_compress_rows(x_refs, pe_ref, bd_ref, n_blocks), o_ref, n_blocks)
    if n_out > n_blocks:
        o_ref[n_blocks:n_out, :] = jnp.zeros((n_out - n_blocks, KV_WIDTH), BF16)


def _compress(kv, pe, bd, n_out):
    b, s, _ = kv.shape
    n_blocks = s // BLK

    def head_spec(k):
        return pl.BlockSpec((None, s, KV_GROUP), lambda i: (i, 0, k))

    return pl.pallas_call(
        functools.partial(_compress_body, n_blocks=n_blocks, n_out=n_out), grid=(b,),
        in_specs=[head_spec(k) for k in range(N_KV_HEADS)]
        + [_const_spec((BLK, KV_GROUP)), _const_spec((BLK, KV_GROUP, KV_GROUP))],
        out_specs=pl.BlockSpec((None, n_out, KV_WIDTH), lambda i: (i, 0, 0)),
        out_shape=jax.ShapeDtypeStruct((b, n_out, KV_WIDTH), BF16),
        compiler_params=_params(1), name="compress")(*([kv] * N_KV_HEADS), pe, bd)


def _compress_paged_body(pt_ref, *refs, n_pages):
    page_refs = refs[:PAGES_PER_STEP]
    pe_ref, bd_ref, o_ref, stage_ref = refs[PAGES_PER_STEP:]
    j = pl.program_id(1)
    blocks_per_page = PAGE_SIZE // BLK
    for p in range(PAGES_PER_STEP):
        first_block = (j * PAGES_PER_STEP + p) * blocks_per_page
        for k in range(N_KV_HEADS):
            rows = _token_rows(page_refs[p], k)
            for h in range(blocks_per_page):
                at = pl.multiple_of((first_block + h) * STAGE_PITCH, SUBLANES)
                stage_ref[k, pl.ds(at, BLK), :] = rows[h * BLK:(h + 1) * BLK]

    @pl.when(j == pl.num_programs(1) - 1)
    def _():
        n_blocks = n_pages * blocks_per_page
        x_refs = [stage_ref.at[k] for k in range(N_KV_HEADS)]
        _store_summaries(_compress_rows(x_refs, pe_ref, bd_ref, n_blocks, STAGE_PITCH), o_ref, n_blocks)


def _compress_paged(cache, page_table, pe, bd):
    b, n_pages = page_table.shape
    steps = n_pages // PAGES_PER_STEP
    n_blocks = n_pages * (PAGE_SIZE // BLK)

    def page_spec(p):
        return pl.BlockSpec((None, N_KV_HEADS, 2, HEAD_DIM, PAGE_SIZE),
                            lambda i, j, pt: (pt[i, j * PAGES_PER_STEP + p], 0, 0, 0, 0))

    grid_spec = pltpu.PrefetchScalarGridSpec(
        num_scalar_prefetch=1, grid=(b, steps),
        in_specs=[page_spec(p) for p in range(PAGES_PER_STEP)]
        + [pl.BlockSpec((BLK, KV_GROUP), lambda i, j, pt: (0, 0)),
           pl.BlockSpec((BLK, KV_GROUP, KV_GROUP), lambda i, j, pt: (0, 0, 0))],
        out_specs=pl.BlockSpec((None, n_blocks, KV_WIDTH), lambda i, j, pt: (i, 0, 0)),
        scratch_shapes=[pltpu.VMEM((N_KV_HEADS, n_blocks * STAGE_PITCH, KV_GROUP), F32)])
    return pl.pallas_call(
        functools.partial(_compress_paged_body, n_pages=n_pages), grid_spec=grid_spec,
        out_shape=jax.ShapeDtypeStruct((b, n_blocks, KV_WIDTH), BF16),
        compiler_params=_params(2), name="compress_paged")(page_table, *([cache] * PAGES_PER_STEP), pe, bd)


def _rel_bucket(dist):
    n = jnp.maximum(dist, 0)
    max_exact = N_BUCKETS // 2
    nf = jnp.maximum(n, 1).astype(F32)
    log_b = max_exact + (jnp.log(nf / max_exact) / math.log(MAX_DISTANCE / max_exact)
                         * (N_BUCKETS - max_exact)).astype(jnp.int32)
    return jnp.where(n < max_exact, n, jnp.minimum(log_b, N_BUCKETS - 1))


def _bias_of(rel_table, dist, visible):
    onehot = (_rel_bucket(dist)[..., None] == jnp.arange(N_BUCKETS, dtype=jnp.int32)).astype(F32)
    bias = jnp.einsum('rcb,bh->hrc', onehot, rel_table.astype(F32) * LOG2E, precision=lax.Precision.HIGHEST)
    return jnp.where(visible[None], bias, MASK_VALUE).reshape(-1, dist.shape[-1])


KIND_ZERO, KIND_DIAG, KIND_PREV, KIND_MASKED, KIND_OLDEST = range(5)


def _prompt_bias_tables(rel_table):
    r = jnp.arange(TILE, dtype=jnp.int32)[:, None]
    c = jnp.arange(TILE, dtype=jnp.int32)[None, :]
    true = jnp.ones((TILE, TILE), bool)
    far = _bias_of(rel_table, jnp.full((TILE, TILE), MAX_DISTANCE, jnp.int32), true)
    tiles = jnp.stack([
        jnp.zeros_like(far),
        _bias_of(rel_table, r - c, r >= c) - far,
        _bias_of(rel_table, TILE + r - c, true) - far,
        jnp.full_like(far, MASK_VALUE),
        jnp.where(jnp.tile(r < c, (N_ATT_HEADS, 1)), 0.0, MASK_VALUE),
    ])
    c_dist = r - (BLK - 1) + BLK * (BLK - 1 - c)
    cmp_tab = _bias_of(rel_table, c_dist, c_dist >= 0)
    return tiles, cmp_tab


def _selection_mask(score, n_rows, live=None):
    n_groups = score.shape[0] // SUBLANES
    groups = [score[SUBLANES * j:SUBLANES * (j + 1)] for j in range(n_groups)]
    row8 = lax.broadcasted_iota(jnp.int32, groups[0].shape, 0)

    def beaten_by(rows, ranks):
        for n in rows:
            other = jnp.broadcast_to(score[n:n + 1, :], groups[0].shape)
            for j in range(n_groups):
                lo = SUBLANES * j
                if lo > n:
                    inc = jnp.where(other >= groups[j], 1, 0)
                elif lo + SUBLANES - 1 <= n:
                    inc = jnp.where(other > groups[j], 1, 0)
                else:
                    inc = jnp.where(row8 + lo > n, jnp.where(other >= groups[j], 1, 0),
                                    jnp.where(other > groups[j], 1, 0))
                ranks[j] = ranks[j] + inc
        return jnp.concatenate(ranks, axis=0)

    zeros = [jnp.zeros(groups[0].shape, jnp.int32) for _ in range(n_groups)]
    if live is None:
        rank = beaten_by(range(n_rows), zeros)
    else:
        n_live, rank_ref = live
        rank_ref[...] = jnp.zeros(rank_ref.shape, jnp.int32)
        for lo in range(0, n_rows, SUBLANES):
            @pl.when(lo < n_live)
            def _():
                rank_ref[...] += beaten_by(range(lo, min(lo + SUBLANES, n_rows)), list(zeros))
        rank = rank_ref[...]
    return jnp.where(rank < N_SEL, jnp.where(score > -jnp.inf, 0.0, MASK_VALUE), MASK_VALUE)


PAIR = 2 * TILE
Q_TILES = PAIR // TILE
SEL_UNROLL = 4
_SEL_KINDS = {0: ((KIND_DIAG, KIND_MASKED), (KIND_PREV, KIND_DIAG)),
              1: ((KIND_ZERO, KIND_PREV), (KIND_ZERO, KIND_ZERO))}
_WIN_OLDEST = ((KIND_OLDEST, KIND_ZERO), (KIND_MASKED, KIND_OLDEST))


def _nsa_prompt_body(q_ref, gates_ref, kvc_ref, skv_ref, wkv_ref, tb_ref, ctab_ref, o_ref,
                     s_ref, sw_ref, mx_ref, mb_ref, acc_ref, rank_ref, *, n_blocks):
    step = pl.program_id(2)
    head_rows = GQA * TILE
    rows = Q_TILES * head_rows
    lane = lax.broadcasted_iota(jnp.int32, (rows, TILE), 1)
    low = lane < HEAD_DIM
    key_lane = lax.broadcasted_iota(jnp.int32, (PAIR, TILE), 1)
    key_blk = lax.broadcasted_iota(jnp.int32, (PAIR, TILE), 0) // BLK
    n_pairs = skv_ref.shape[0] // PAIR

    qf = q_ref[...]
    halves = [qf[:, c * TILE:(c + 1) * TILE] for c in range(GQA // 2)]
    rolled = [pltpu.roll(h, HEAD_DIM, 1) for h in halves]
    q_all = jnp.concatenate([(halves if g % 2 == 0 else rolled)[g // 2][t * TILE:(t + 1) * TILE]
                             for t in range(Q_TILES) for g in range(GQA)], axis=0)
    lhs_plain = jnp.where(low, q_all, 0.0).astype(BF16)

    def ones_and_values(kv):
        kv_lane = lax.broadcasted_iota(jnp.int32, kv.shape, 1)
        return jnp.where(kv_lane < HEAD_DIM, jnp.ones_like(kv), kv)

    def normalise(acc):
        return acc / jnp.maximum(pltpu.roll(acc, HEAD_DIM, 1), 1e-30)

    def bias_of(kinds_per_tile, valid):
        parts = []
        for left, right in kinds_per_tile:
            left = jnp.where(valid, left, KIND_MASKED)
            right = jnp.where(valid, right, KIND_MASKED)
            parts.append(jnp.concatenate([tb_ref[left], tb_ref[right]], axis=1))
        return jnp.concatenate(parts, axis=0)

    def sel_kinds(j):
        behind = step - j
        return [tuple(jnp.where(behind == 0, _SEL_KINDS[0][t][side],
                                jnp.where(behind == 1, _SEL_KINDS[1][t][side], KIND_ZERO)) for side in range(2))
                for t in range(Q_TILES)]

    def keys_of(kv_ref, j):
        jc = jnp.clip(j, 0, n_pairs - 1)
        return kv_ref[pl.ds(pl.multiple_of(jc * PAIR, PAIR), PAIR), :]

    def score_pair(lhs, kv_ref, j, bias, block_columns):
        rhs = keys_of(kv_ref, j)
        if block_columns:
            onehot = jnp.where(key_lane - HEAD_DIM == (PAIR // BLK) * j + key_blk, 1.0, 0.0).astype(BF16)
            rhs = jnp.where(key_lane < HEAD_DIM, rhs, onehot)
        return lax.dot_general(lhs, rhs, NT_DIMS, preferred_element_type=F32) + bias

    def half_max(sc):
        return jnp.maximum(sc[:, :TILE], sc[:, TILE:])

    def weigh_pair(sc, mb, kv_ref, j):
        pr = jnp.exp2(sc - jnp.concatenate([mb, mb], axis=1))
        return jnp.dot(pr.astype(BF16), ones_and_values(keys_of(kv_ref, j)), preferred_element_type=F32)

    def row_max(mx):
        return jnp.broadcast_to(jnp.max(mx, axis=1, keepdims=True), (rows, TILE))

    n_win = WINDOW // PAIR + 1
    w_pairs = [step - (n_win - 1) + w for w in range(n_win)]
    w_kinds = [_WIN_OLDEST] + [[(KIND_ZERO, KIND_ZERO)] * Q_TILES] * (n_win - 3) + [_SEL_KINDS[1], _SEL_KINDS[0]]
    w_mx = None
    for w, j in enumerate(w_pairs):
        sc = score_pair(lhs_plain, wkv_ref, j, bias_of(w_kinds[w], j >= 0), False)
        sw_ref[w] = sc
        w_mx = half_max(sc) if w_mx is None else jnp.maximum(w_mx, half_max(sc))

    kvc = kvc_ref[...]
    bias = []
    for t in range(Q_TILES):
        i = Q_TILES * step + t
        shift = (2 * i + (TILE - (BLK - 1))) % TILE
        bias += [pltpu.roll(ctab_ref[g * TILE:(g + 1) * TILE, :], shift, 1) for g in range(GQA)]
    s = lax.dot_general(lhs_plain, kvc, NT_DIMS, preferred_element_type=F32)
    s = jnp.where(lane < n_blocks, s + jnp.concatenate(bias, axis=0), MASK_VALUE)
    m = jnp.maximum(jnp.max(s, axis=1, keepdims=True), M_INIT)
    p = jnp.exp2(s - m)
    c_acc = jnp.dot(p.astype(BF16), ones_and_values(kvc), preferred_element_type=F32)
    p = p / jnp.maximum(c_acc, 1e-30)
    o_cmp = normalise(c_acc)

    w_mb = row_max(jnp.maximum(w_mx, M_INIT))
    w_acc = None
    for w, j in enumerate(w_pairs):
        dacc = weigh_pair(sw_ref[w], w_mb, wkv_ref, j)
        w_acc = dacc if w_acc is None else w_acc + dacc
    o_win = normalise(w_acc)

    blk = lax.broadcasted_iota(jnp.int32, (n_blocks, PAIR), 0)
    query = lax.broadcasted_iota(jnp.int32, (n_blocks, PAIR), 1)
    cur = (PAIR // BLK) * step + query // BLK
    imp_t = []
    for t in range(Q_TILES):
        imp = p[t * head_rows:t * head_rows + TILE]
        for g in range(1, GQA):
            imp = imp + p[t * head_rows + g * TILE:t * head_rows + (g + 1) * TILE]
        imp_t.append(imp.T[0:n_blocks])
    forced = (blk == 0) | (blk == cur) | (blk == cur - 1)
    score = jnp.where(forced, jnp.inf, jnp.where(blk <= cur, jnp.concatenate(imp_t, axis=1), -jnp.inf))
    neg_t = _selection_mask(score, n_blocks, ((PAIR // BLK) * (step + 1), rank_ref))
    neg = []
    for t in range(Q_TILES):
        pieces = [jnp.zeros((HEAD_DIM, TILE), F32), neg_t[:, t * TILE:(t + 1) * TILE]]
        if n_blocks < TILE - HEAD_DIM:
            pieces.append(jnp.zeros((TILE - HEAD_DIM - n_blocks, TILE), F32))
        neg += [jnp.concatenate(pieces, axis=0).T] * GQA
    lhs_sel = jnp.where(low, q_all, jnp.concatenate(neg, axis=0)).astype(BF16)

    n_trips = (step + SEL_UNROLL) // SEL_UNROLL
    mx_ref[...] = jnp.full(mx_ref.shape, M_INIT, F32)

    def score_trip(t, carry):
        mx = None
        for u in range(SEL_UNROLL):
            j = t * SEL_UNROLL + u
            sc = score_pair(lhs_sel, skv_ref, j, bias_of(sel_kinds(j), j <= step), True)
            s_ref[j] = sc
            mx = half_max(sc) if mx is None else jnp.maximum(mx, half_max(sc))
        mx_ref[...] = jnp.maximum(mx_ref[...], mx)
        return carry

    lax.fori_loop(0, n_trips, score_trip, 0)
    mb_ref[...] = row_max(mx_ref[...])
    acc_ref[...] = jnp.zeros(acc_ref.shape, F32)

    def weigh_trip(t, carry):
        mb = mb_ref[...]
        acc = None
        for u in range(SEL_UNROLL):
            j = t * SEL_UNROLL + u
            dacc = weigh_pair(s_ref[j], mb, skv_ref, j)
            acc = dacc if acc is None else acc + dacc
        acc_ref[...] += acc
        return carry

    lax.fori_loop(0, n_trips, weigh_trip, 0)
    o_sel = normalise(acc_ref[...])

    gate = jax.nn.sigmoid(gates_ref[...])
    out_low = lax.broadcasted_iota(jnp.int32, (TILE, TILE), 1) < HEAD_DIM
    for t in range(Q_TILES):
        outs = []
        for g in range(GQA):
            def gcol(br):
                c = g * N_BRANCH + br
                return gate[t * TILE:(t + 1) * TILE, c:c + 1]
            head = slice(t * head_rows + g * TILE, t * head_rows + (g + 1) * TILE)
            outs.append(gcol(0) * o_cmp[head] + gcol(1) * o_sel[head] + gcol(2) * o_win[head])
        for c in range(GQA // 2):
            o_ref[t * TILE:(t + 1) * TILE, c * TILE:(c + 1) * TILE] = jnp.where(
                out_low, pltpu.roll(outs[2 * c], HEAD_DIM, 1), outs[2 * c + 1])


def _nsa_prompt(q, gates, kvc, skv_b, wkv_b, tiles, cmp_tab):
    b, s, _ = q.shape
    n_blocks = s // BLK
    assert n_blocks <= TILE - HEAD_DIM and kvc.shape[1] == TILE and s % PAIR == 0 and WINDOW // PAIR >= 2
    gw = GQA * HEAD_DIM
    head_rows = GQA * TILE
    rows = Q_TILES * head_rows
    return pl.pallas_call(
        functools.partial(_nsa_prompt_body, n_blocks=n_blocks), grid=(b, N_KV_HEADS, s // PAIR),
        in_specs=[pl.BlockSpec((None, PAIR, gw), lambda bi, k, i: (bi, i, k)),
                  pl.BlockSpec((None, PAIR, TILE), lambda bi, k, i: (bi, i, k)),
                  pl.BlockSpec((None, TILE, KV_GROUP), lambda bi, k, i: (bi, 0, k)),
                  pl.BlockSpec((None, s, KV_GROUP), lambda bi, k, i: (bi, 0, k)),
                  pl.BlockSpec((None, s, KV_GROUP), lambda bi, k, i: (bi, 0, k)),
                  pl.BlockSpec((len(tiles), head_rows, TILE), lambda bi, k, i: (0, k, 0)),
                  pl.BlockSpec((head_rows, TILE), lambda bi, k, i: (k, 0))],
        out_specs=pl.BlockSpec((None, PAIR, gw), lambda bi, k, i: (bi, i, k)),
        out_shape=jax.ShapeDtypeStruct((b, s, ATT_WIDTH), F32),
        scratch_shapes=[pltpu.VMEM((s // PAIR + SEL_UNROLL - 1, rows, PAIR), F32),
                        pltpu.VMEM((WINDOW // PAIR + 1, rows, PAIR), F32),
                        pltpu.VMEM((rows, TILE), F32), pltpu.VMEM((rows, TILE), F32),
                        pltpu.VMEM((rows, KV_GROUP), F32), pltpu.VMEM((n_blocks, PAIR), jnp.int32)],
        compiler_params=_params(3), name="nsa_prompt")(q, gates, kvc, skv_b, wkv_b, tiles, cmp_tab)


def _nsa_sample_body(pt_ref, *refs, n_pages, dec_seq, past_len):
    page_refs = refs[:PAGES_PER_STEP]
    (qk_ref, qc_ref, gates_ref, kc_ref, ctab_ref, gsum_ref, knew_ref, vnew_ref, cwin_ref, kwnew_ref, vwnew_ref,
     last_ref, new_ref, win0_ref, osw_ref, oc_ref, lhs_ref, m_ref, l_ref, acc_ref, ow_ref) = refs[PAGES_PER_STEP:]
    j = pl.program_id(1)
    last_step = pl.num_programs(1) - 1
    blocks_per_page = PAGE_SIZE // BLK
    n_past = n_pages * blocks_per_page
    kd = N_KV_HEADS * HEAD_DIM
    n_lanes = kc_ref.shape[0]
    step_keys = PAGES_PER_STEP * PAGE_SIZE

    def gate_col(br):
        return jax.nn.sigmoid(gates_ref[:, br:br + 1])

    def head_major(ref, c, lo=0, n=None):
        n = ref.shape[-1] if n is None else n
        return ref[:, c, :, lo:lo + n].reshape(kd, n).astype(BF16)

    def lane_tiles(x):
        return [x[:, t * TILE:(t + 1) * TILE] for t in range(x.shape[1] // TILE)]

    def tile_max(tiles):
        m = tiles[0]
        for t in tiles[1:]:
            m = jnp.maximum(m, t)
        return jnp.broadcast_to(jnp.max(m, axis=1, keepdims=True), (TILE, TILE))

    def tile_sum(tiles):
        s = tiles[0]
        for t in tiles[1:]:
            s = s + t
        return s

    def online_update(s, values_t):
        tiles = lane_tiles(s)
        m_prev = m_ref[...]
        m_new = jnp.maximum(m_prev, tile_max(tiles))
        alpha = jnp.exp2(m_prev - m_new)
        p = [jnp.exp2(t - m_new) for t in tiles]
        l_ref[...] = alpha * l_ref[...] + tile_sum(p)
        pv = lax.dot_general(jnp.concatenate(p, axis=1).astype(BF16), values_t, NT_DIMS, preferred_element_type=F32)
        acc_ref[...] = jnp.concatenate([alpha] * (kd // TILE), axis=1) * acc_ref[...] + pv
        m_ref[...] = m_new

    @pl.when(j == 0)
    def _():
        kc = kc_ref[...]
        s = lax.dot_general(qc_ref[...], kc, NT_DIMS, preferred_element_type=F32) + ctab_ref[...]
        m = jnp.maximum(jnp.max(s, axis=1, keepdims=True), M_INIT)
        p = jnp.exp2(s - m)
        p = p / jnp.maximum(jnp.sum(p, axis=1, keepdims=True), 1e-30)
        oc_ref[...] = gate_col(0) * jnp.dot(p.astype(BF16), kc, preferred_element_type=F32)
        imp = _select_sum(p, gsum_ref[...], False)
        imp_t = jnp.concatenate([t.T for t in lane_tiles(imp)], axis=0)
        n_rows = -(-(n_past + 1) // SUBLANES) * SUBLANES
        blk = lax.broadcasted_iota(jnp.int32, (n_rows, TILE), 0)
        tok = lax.broadcasted_iota(jnp.int32, (n_rows, TILE), 1) % dec_seq
        cur = (past_len + tok) // BLK
        forced = (blk == 0) | (blk == cur) | (blk == cur - 1)
        score = jnp.where(forced, jnp.inf, jnp.where(blk <= cur, imp_t[0:n_rows], -jnp.inf))
        neg_t = jnp.concatenate([_selection_mask(score, n_past + 1), jnp.zeros((n_lanes - n_rows, TILE), F32)], axis=0)
        neg = jnp.concatenate([neg_t[t * TILE:(t + 1) * TILE].T for t in range(n_lanes // TILE)], axis=1)
        lhs_ref[:, 0:kd] = qk_ref[...]
        lhs_ref[:, kd:kd + n_lanes] = neg.astype(BF16)

        qk = qk_ref[...]
        n_win = cwin_ref.shape[-1] // TILE
        zero = jnp.zeros((TILE, TILE), F32)
        s_w = jnp.dot(qk, head_major(cwin_ref, 0), preferred_element_type=F32)
        s_w = s_w + jnp.concatenate([win0_ref[...]] + [zero] * (n_win - 2) + [last_ref[...]], axis=1)
        s_n = jnp.dot(qk, kwnew_ref[...], preferred_element_type=F32) + new_ref[...]
        tiles = lane_tiles(s_w) + [s_n]
        m = tile_max(tiles)
        p = [jnp.exp2(t - m) for t in tiles]
        l = jnp.sum(tile_sum(p), axis=1, keepdims=True)
        o_w = lax.dot_general(jnp.concatenate(p[:-1], axis=1).astype(BF16), head_major(cwin_ref, 1), NT_DIMS,
                              preferred_element_type=F32)
        o_w = o_w + lax.dot_general(p[-1].astype(BF16), vwnew_ref[...], NT_DIMS, preferred_element_type=F32)
        ow_ref[...] = gate_col(2) * (o_w / l)

        m_ref[...] = jnp.full(m_ref.shape, M_INIT, F32)
        l_ref[...] = jnp.zeros(l_ref.shape, F32)
        acc_ref[...] = jnp.zeros(acc_ref.shape, F32)

    keys_t = jnp.concatenate([head_major(r, 0) for r in page_refs], axis=1)
    values_t = jnp.concatenate([head_major(r, 1) for r in page_refs], axis=1)
    blk_row = lax.broadcasted_iota(jnp.int32, (n_lanes, step_keys), 0)
    key_blk = j * (PAGES_PER_STEP * blocks_per_page) + lax.broadcasted_iota(jnp.int32, (n_lanes, step_keys), 1) // BLK
    onehot = jnp.where(blk_row == key_blk, 1.0, 0.0).astype(BF16)
    s = jnp.dot(lhs_ref[...], jnp.concatenate([keys_t, onehot], axis=0), preferred_element_type=F32)
    tiles = lane_tiles(s)
    tiles[-1] = tiles[-1] + jnp.where(j == last_step, last_ref[...], 0.0)
    online_update(jnp.concatenate(tiles, axis=1), values_t)

    @pl.when(j == last_step)
    def _():
        new_blk = lax.broadcasted_iota(jnp.int32, (n_lanes, TILE), 0) == n_past
        rhs = jnp.concatenate([knew_ref[...], jnp.where(new_blk, 1.0, 0.0).astype(BF16)], axis=0)
        online_update(jnp.dot(lhs_ref[...], rhs, preferred_element_type=F32) + new_ref[...], vnew_ref[...])
        l = jnp.sum(l_ref[...], axis=1, keepdims=True)
        osw_ref[...] = gate_col(1) * (acc_ref[...] / l) + ow_ref[...]


def _nsa_sample(page_table, cache_slc, qk, qc, gates_c, kc, ctab, gsum, k_new, v_new, cache_win, kw_new, vw_new,
                last_tab, new_tab, win0_tab, *, dec_seq, past_len):
    b, n_pages = page_table.shape
    steps = n_pages // PAGES_PER_STEP
    n_lanes = kc.shape[1]
    w_buf = cache_win.shape[-1]
    kd = N_KV_HEADS * HEAD_DIM

    def page_spec(p):
        return pl.BlockSpec((None, N_KV_HEADS, 2, HEAD_DIM, PAGE_SIZE),
                            lambda i, j, pt: (pt[i, j * PAGES_PER_STEP + p], 0, 0, 0, 0))

    def per_batch(*shape):
        zeros = (0,) * len(shape)
        return pl.BlockSpec((None,) + shape, lambda i, j, pt: (i,) + zeros)

    def const(*shape):
        zeros = (0,) * len(shape)
        return pl.BlockSpec(shape, lambda i, j, pt: zeros)

    grid_spec = pltpu.PrefetchScalarGridSpec(
        num_scalar_prefetch=1, grid=(b, steps),
        in_specs=[page_spec(p) for p in range(PAGES_PER_STEP)]
        + [per_batch(TILE, kd), per_batch(TILE, KV_WIDTH), per_batch(TILE, TILE), per_batch(n_lanes, KV_WIDTH),
           const(TILE, n_lanes), const(TILE, TILE), per_batch(kd, TILE), per_batch(kd, TILE),
           per_batch(N_KV_HEADS, 2, HEAD_DIM, w_buf), per_batch(kd, TILE), per_batch(kd, TILE),
           const(TILE, TILE), const(TILE, TILE), const(TILE, TILE)],
        out_specs=[per_batch(TILE, kd), per_batch(TILE, KV_WIDTH)],
        scratch_shapes=[pltpu.VMEM((TILE, kd + n_lanes), BF16), pltpu.VMEM((TILE, TILE), F32),
                        pltpu.VMEM((TILE, TILE), F32), pltpu.VMEM((TILE, kd), F32), pltpu.VMEM((TILE, kd), F32)])
    return pl.pallas_call(
        functools.partial(_nsa_sample_body, n_pages=n_pages, dec_seq=dec_seq, past_len=past_len),
        grid_spec=grid_spec,
        out_shape=[jax.ShapeDtypeStruct((b, TILE, kd), F32), jax.ShapeDtypeStruct((b, TILE, KV_WIDTH), F32)],
        compiler_params=_params(2), name="nsa_sample")(
            page_table, *([cache_slc] * PAGES_PER_STEP), qk, qc, gates_c, kc, ctab, gsum, k_new, v_new, cache_win,
            kw_new, vw_new, last_tab, new_tab, win0_tab)


def _split3(x):
    hi = x.astype(BF16)
    r = x - hi.astype(F32)
    mid = r.astype(BF16)
    return hi, mid, (r - mid.astype(F32)).astype(BF16)


def _select_sum(x, onehot, x_on_left):
    out = None
    for part in _split3(x):
        term = (jnp.dot(part, onehot, preferred_element_type=F32) if x_on_left
                else jnp.dot(onehot, part, preferred_element_type=F32))
        out = term if out is None else out + term
    return out


def _softplus(x):
    return jnp.maximum(x, 0.0) + jnp.log(1.0 + jnp.exp(-jnp.abs(x)))


def _ssd_body(z_ref, xbc_ref, dt_ref, dtt_ref, convp_ref, h0_ref, cw_ref, cb_ref, dtb_ref, dtbt_ref, al_ref,
              alt_ref, dskip_ref, norm_ref, o_ref, hout_ref, convout_ref, xp_ref, *, cl):
    c = pl.program_id(1)
    pad = SUBLANES

    @pl.when(c == 0)
    def _():
        xp_ref[0:pad, :] = convp_ref[...]
        hout_ref[...] = h0_ref[...]

    xp_ref[pad:pad + cl, :] = xbc_ref[...]
    conv = cb_ref[...]
    for k in range(CONV_W):
        lo = pad - (CONV_W - 1) + k
        conv = conv + cw_ref[k:k + 1, :] * xp_ref[lo:lo + cl, :]
    u = conv * jax.nn.sigmoid(conv)
    tail = xp_ref[pad + cl - (CONV_W - 1):pad + cl, :]
    convout_ref[...] = tail
    xp_ref[pad - (CONV_W - 1):pad, :] = tail

    xs = u[:, :SSD_INNER]
    dt = _softplus(dt_ref[:, 0:SSD_HEADS] + dtb_ref[...])
    dt_t = _softplus(dtt_ref[0:SSD_HEADS, :] + dtbt_ref[...])
    la = dt * -jnp.exp(al_ref[...])
    la_t = dt_t * -jnp.exp(alt_ref[...])
    row = lax.broadcasted_iota(jnp.int32, (cl, cl), 0)
    col = lax.broadcasted_iota(jnp.int32, (cl, cl), 1)
    causal = row >= col
    cs = _select_sum(la, jnp.where(causal, 1.0, 0.0).astype(BF16), False)
    cs_t = _select_sum(la_t, jnp.where(row <= col, 1.0, 0.0).astype(BF16), True)

    def spread(width):
        lane_head = lax.broadcasted_iota(jnp.int32, (SSD_HEADS, SSD_HEADS * width), 1) // width
        return jnp.where(lane_head == lax.broadcasted_iota(jnp.int32, lane_head.shape, 0), 1.0, 0.0).astype(BF16)

    dt_x = _select_sum(dt, spread(SSD_HEAD_DIM), True)
    cs_x = _select_sum(cs, spread(SSD_HEAD_DIM), True)
    cs_wide = _select_sum(cs, spread(TILE), True)
    last_x = cs_x[cl - 1:cl, :]
    xdt = xs * dt_x
    xdt_b = xdt.astype(BF16)
    to_end = (xdt * jnp.exp(last_x - cs_x)).astype(BF16)
    grow = jnp.exp(cs_x)
    skip = dskip_ref[...] * xs

    heads_per_group = SSD_HEADS // SSD_GROUPS
    bms, cms, cbs = [], [], []
    for g in range(SSD_GROUPS):
        b_lo = SSD_INNER + g * SSD_STATE
        c_lo = SSD_INNER + SSD_GROUPS * SSD_STATE + g * SSD_STATE
        bms.append(u[:, b_lo:b_lo + SSD_STATE].astype(BF16))
        cms.append(u[:, c_lo:c_lo + SSD_STATE].astype(BF16))
        cbs.append(lax.dot_general(cms[g], bms[g], NT_DIMS, preferred_element_type=F32))
    first_head = lax.broadcasted_iota(jnp.int32, (cl, TILE), 1) < SSD_HEAD_DIM
    state_row = lax.broadcasted_iota(jnp.int32, (2 * SSD_HEAD_DIM, SSD_STATE), 0)
    ys = []
    for pair in range(SSD_HEADS // 2):
        g = 2 * pair // heads_per_group
        lanes = slice(pair * TILE, (pair + 1) * TILE)
        y_heads, keep = [], []
        for h in (2 * pair, 2 * pair + 1):
            diff = cs_wide[:, h * TILE:h * TILE + cl] - cs_t[h:h + 1, :]
            decay = jnp.where(causal, jnp.exp(jnp.where(causal, diff, 0.0)), 0.0)
            y_heads.append(jnp.dot((cbs[g] * decay).astype(BF16), xdt_b[:, lanes], preferred_element_type=F32))
            keep.append(jnp.exp(cs[cl - 1:cl, h:h + 1]))
        state_in = lax.dot_general(to_end[:, lanes], bms[g], TN_DIMS, preferred_element_type=F32)
        h_prev = jnp.concatenate([hout_ref[2 * pair], hout_ref[2 * pair + 1]], axis=0)
        y_off = lax.dot_general(cms[g], h_prev.astype(BF16), NT_DIMS, preferred_element_type=F32)
        h_new = h_prev * jnp.where(state_row < SSD_HEAD_DIM, keep[0], keep[1]) + state_in
        hout_ref[2 * pair] = h_new[0:SSD_HEAD_DIM]
        hout_ref[2 * pair + 1] = h_new[SSD_HEAD_DIM:]
        ys.append(jnp.where(first_head, y_heads[0], y_heads[1]) + y_off * grow[:, lanes] + skip[:, lanes])
    y = jnp.concatenate(ys, axis=1)
    z = z_ref[...]
    y = y * (z * jax.nn.sigmoid(z))
    gw = SSD_INNER // SSD_GROUPS
    normed = []
    for g in range(SSD_GROUPS):
        yg = y[:, g * gw:(g + 1) * gw]
        normed.append(yg * lax.rsqrt(jnp.mean(yg * yg, axis=-1, keepdims=True) + EPS))
    o_ref[...] = jnp.concatenate(normed, axis=1) * norm_ref[...]


def _ssd(z, xbc, dt, dt_t, conv_prev, h0, consts, cl):
    b, seq, _ = z.shape
    nc = seq // cl
    cw, cbias, dtb, dtb_t, al, al_t, dskip, norm = consts

    def rows(width):
        return pl.BlockSpec((None, cl, width), lambda i, c: (i, c, 0))

    state = pl.BlockSpec((None, SSD_HEADS, SSD_HEAD_DIM, SSD_STATE), lambda i, c: (i, 0, 0, 0))

    def const(x):
        zeros = (0,) * x.ndim
        return pl.BlockSpec(x.shape, lambda i, c: zeros)

    return pl.pallas_call(
        functools.partial(_ssd_body, cl=cl), grid=(b, nc),
        in_specs=[rows(SSD_INNER), rows(CONV_DIM), rows(TILE),
                  pl.BlockSpec((None, TILE, cl), lambda i, c: (i, 0, c)),
                  pl.BlockSpec((None, SUBLANES, CONV_DIM), lambda i, c: (i, 0, 0)), state,
                  const(cw), const(cbias), const(dtb), const(dtb_t), const(al), const(al_t), const(dskip),
                  const(norm)],
        out_specs=[rows(SSD_INNER), state, pl.BlockSpec((None, CONV_W - 1, CONV_DIM), lambda i, c: (i, 0, 0))],
        out_shape=[jax.ShapeDtypeStruct((b, seq, SSD_INNER), F32),
                   jax.ShapeDtypeStruct((b, SSD_HEADS, SSD_HEAD_DIM, SSD_STATE), F32),
                   jax.ShapeDtypeStruct((b, CONV_W - 1, CONV_DIM), F32)],
        scratch_shapes=[pltpu.VMEM((SUBLANES + cl, CONV_DIM), F32)],
        compiler_params=_params(2), name="ssd")(z, xbc, dt, dt_t, conv_prev, h0, cw, cbias, dtb, dtb_t, al, al_t,
                                                 dskip, norm)


def _pack_inproj(w_in):
    splits = (ATT_WIDTH, KV_WIDTH, KV_WIDTH, KV_WIDTH, N_BRANCH * N_ATT_HEADS, SSD_INNER, CONV_DIM, SSD_HEADS)
    offs = [0]
    for s in splits:
        offs.append(offs[-1] + s)
    wq, wc, ws, ww, wg, wz, wx, wd = [w_in[:, offs[i]:offs[i + 1]] for i in range(len(splits))]
    d = w_in.shape[0]
    per_kv = GQA * N_BRANCH
    wg4 = jnp.pad(wg.reshape(d, N_KV_HEADS, per_kv), ((0, 0), (0, 0), (0, TILE - per_kv)))
    wd_pad = jnp.pad(wd, ((0, 0), (0, TILE - SSD_HEADS)))
    w = jnp.concatenate([wq * (HEAD_DIM ** -0.5 * LOG2E), wc, ws, ww, wg4.reshape(d, N_KV_HEADS * TILE), wz, wx,
                         wd_pad], axis=1)
    assert w.shape[1] == PROJ_WIDTH
    w_t = jnp.concatenate([wd_pad, wc, ws, ww], axis=1).T
    return w.astype(BF16), w_t.astype(BF16)


def _pack_compress(cmp_pe, w_cmp):
    zero = jnp.zeros((BLK, HEAD_DIM, HEAD_DIM), w_cmp.dtype)
    bd = jnp.concatenate([jnp.concatenate([w_cmp[:, 0], zero], axis=2),
                          jnp.concatenate([zero, w_cmp[:, 1]], axis=2)], axis=1)
    pe = cmp_pe.reshape(BLK, KV_GROUP)
    return pe.astype(F32), bd.astype(BF16)


def _sample_tables(rel_table, dec_seq, past_len, n_lanes):
    col = jnp.arange(TILE, dtype=jnp.int32)
    head = (col // (GQA * dec_seq)) * GQA + (col // dec_seq) % GQA
    tok = (col % dec_seq)[:, None]
    per_col = (rel_table.astype(F32) * LOG2E)[:, head]

    def bias(dist):
        onehot = (_rel_bucket(dist)[..., None] == jnp.arange(N_BUCKETS, dtype=jnp.int32)).astype(F32)
        return jnp.einsum('ckb,bc->ck', onehot, per_col, precision=lax.Precision.HIGHEST)

    lane = jnp.arange(TILE, dtype=jnp.int32)[None, :]
    far = bias(jnp.full((TILE, 1), MAX_DISTANCE, jnp.int32))
    blk = jnp.arange(n_lanes, dtype=jnp.int32)[None, :]
    c_dist = past_len + tok - ((blk + 1) * BLK - 1)
    ctab = jnp.where(c_dist >= 0, bias(c_dist), MASK_VALUE)
    last_tab = bias(TILE + tok - lane) - far
    new_tab = jnp.where((lane <= tok) & (lane < dec_seq), bias(tok - lane) - far, MASK_VALUE)
    win0_tab = jnp.where(lane > tok, 0.0, MASK_VALUE)
    same = (col[:, None] // (GQA * dec_seq) == col[None, :] // (GQA * dec_seq)) & \
           (col[:, None] % dec_seq == col[None, :] % dec_seq)
    return ctab, last_tab, new_tab, win0_tab, same.astype(BF16)


def _layer_weights(ffn_pre, ffn_post, ffn_w_in, ffn_w_out):
    f = ffn_w_out.shape[0]
    return (ffn_pre[None, :], ffn_post[None, :], ffn_w_in[:, :f].astype(BF16), ffn_w_in[:, f:].astype(BF16),
            ffn_w_out.astype(BF16))


def kernel(x_prompt, x_sample, cache_cmp_kv, cache_slc_kv, cache_win_kv, state_ssm, state_conv, page_table,
           rel_table, ffn1_pre, ffn1_post, ffn1_w_in, ffn1_w_out, mix_pre, mix_post, w_in, w_out, att_out_norm,
           cmp_pe, w_cmp, conv_w, conv_b, dt_bias, a_log, d_skip, ssd_norm,
           ffn2_pre, ffn2_post, ffn2_w_in, ffn2_w_out):
    depth = ffn1_pre.shape[0]
    bp, seq, d = x_prompt.shape
    bs, dec_seq, _ = x_sample.shape
    n_pages = page_table.shape[1]
    past_len = n_pages * PAGE_SIZE
    w_buf = cache_win_kv.shape[2]
    kv_shape = (N_KV_HEADS, 2, HEAD_DIM)
    assert GQA * dec_seq * N_KV_HEADS == TILE and w_buf == WINDOW and seq % TILE == 0

    yp = x_prompt.reshape(bp * seq, d)
    ys = x_sample.reshape(bs * dec_seq, d)
    outs = [[] for _ in range(10)]
    for l in range(depth):
        ffn1 = _layer_weights(ffn1_pre[l], ffn1_post[l], ffn1_w_in[l], ffn1_w_out[l])
        ffn2 = _layer_weights(ffn2_pre[l], ffn2_post[l], ffn2_w_in[l], ffn2_w_out[l])
        w_proj, w_proj_t = _pack_inproj(w_in[l])
        pe, bd = _pack_compress(cmp_pe[l], w_cmp[l])
        wo_att = w_out[l][:ATT_WIDTH].astype(BF16)
        wo_ssd = w_out[l][ATT_WIDTH:].astype(BF16)
        ssd_consts = (conv_w[l], conv_b[l][None, :], dt_bias[l][None, :], dt_bias[l][:, None], a_log[l][None, :],
                      a_log[l][:, None], jnp.repeat(d_skip[l], SSD_HEAD_DIM)[None, :], ssd_norm[l][None, :])
        mix_pre_l, mix_post_l, att_norm_l = mix_pre[l][None, :], mix_post[l][None, :], att_out_norm[l][None, :]

        yp = _ffn(yp, *ffn1)
        (q, ckv, skv, wkv, skv_b, wkv_b, gates, z, xbc, dt, dt_t, ckv_t, skv_t, wkv_t) = _inproj(
            yp, mix_pre_l, w_proj, w_proj_t, bp)
        kvc = _compress(ckv.reshape(bp, seq, KV_WIDTH), pe, bd, TILE)
        tiles, cmp_tab = _prompt_bias_tables(rel_table)
        o_att = _nsa_prompt(q.reshape(bp, seq, ATT_WIDTH), gates.reshape(bp, seq, N_KV_HEADS * TILE), kvc,
                            skv_b.reshape(bp, seq, KV_WIDTH), wkv_b.reshape(bp, seq, KV_WIDTH), tiles, cmp_tab)
        cl = min(CHUNK, seq)
        o_ssd, h_new, conv_new = _ssd(
            z.reshape(bp, seq, SSD_INNER), xbc.reshape(bp, seq, CONV_DIM), dt.reshape(bp, seq, TILE),
            dt_t, jnp.zeros((bp, SUBLANES, CONV_DIM), F32),
            jnp.zeros((bp, SSD_HEADS, SSD_HEAD_DIM, SSD_STATE), F32), ssd_consts, cl)
        yp = _mixout_ffn(yp, o_att.reshape(bp * seq, ATT_WIDTH), o_ssd.reshape(bp * seq, SSD_INNER), att_norm_l,
                         mix_post_l, wo_att, wo_ssd, *ffn2)

        def token_major(a_t):
            return a_t.reshape((bp,) + kv_shape + (a_t.shape[-1],)).transpose(0, 4, 1, 2, 3)

        outs[0].append(token_major(ckv_t))
        outs[1].append(token_major(skv_t))
        outs[2].append(token_major(wkv_t[:, :, -min(WINDOW, seq):]))
        outs[3].append(h_new)
        outs[4].append(conv_new)

        ys = _ffn(ys, *ffn1)
        q, ckv, skv, wkv, skv_b, wkv_b, gates, z, xbc, dt, dt_t, _, _, _ = _inproj(ys, mix_pre_l, w_proj, w_proj_t, 1)
        native = (0, 2, 3, 4, 1)
        kc_past = _compress_paged(cache_cmp_kv[l].transpose(native), page_table, pe, bd)
        cmp_new = jnp.pad(ckv.reshape(bs, dec_seq, KV_WIDTH), ((0, 0), (0, BLK - dec_seq), (0, 0)))
        kc_new = _compress(cmp_new.reshape(1, bs * BLK, KV_WIDTH), pe, bd, bs).reshape(bs, 1, KV_WIDTH)
        n_past = kc_past.shape[1]
        n_lanes = -(-(n_past + 1) // TILE) * TILE
        kc = jnp.concatenate([kc_past, kc_new, jnp.zeros((bs, n_lanes - n_past - 1, KV_WIDTH), BF16)], axis=1)
        ctab, last_tab, new_tab, win0_tab, gsum = _sample_tables(rel_table, dec_seq, past_len, n_lanes)
        q_rows = q.reshape(bs, dec_seq, N_KV_HEADS, GQA, HEAD_DIM).transpose(0, 2, 3, 1, 4)
        own_head = jnp.eye(N_KV_HEADS, dtype=F32)[None, :, None, None, :, None]
        q_bd = q_rows[:, :, :, :, None, :] * own_head
        qk = q_bd.reshape(bs, TILE, N_KV_HEADS * HEAD_DIM).astype(BF16)
        qc = jnp.pad(q_bd[..., None, :], ((0, 0),) * 5 + ((0, 1), (0, 0))).reshape(bs, TILE, KV_WIDTH).astype(BF16)
        g_rows = gates.reshape(bs, dec_seq, N_KV_HEADS, TILE)[..., :GQA * N_BRANCH]
        g_rows = g_rows.reshape(bs, dec_seq, N_KV_HEADS, GQA, N_BRANCH).transpose(0, 2, 3, 1, 4)
        gates_c = jnp.pad(g_rows.reshape(bs, TILE, N_BRANCH), ((0, 0), (0, 0), (0, TILE - N_BRANCH)))

        def new_t(a):
            a = a.reshape(bs, dec_seq, N_KV_HEADS, 2, HEAD_DIM).transpose(0, 3, 2, 4, 1)
            a = jnp.pad(a.reshape(bs, 2, N_KV_HEADS * HEAD_DIM, dec_seq), ((0, 0),) * 3 + ((0, TILE - dec_seq),))
            return a[:, 0].astype(BF16), a[:, 1].astype(BF16)

        k_new, v_new = new_t(skv)
        kw_new, vw_new = new_t(wkv)
        o_sw, o_c = _nsa_sample(page_table, cache_slc_kv[l].transpose(native), qk, qc, gates_c, kc, ctab, gsum,
                                k_new, v_new, cache_win_kv[l].transpose(native), kw_new, vw_new, last_tab, new_tab,
                                win0_tab, dec_seq=dec_seq, past_len=past_len)
        heads = jnp.arange(N_KV_HEADS)
        o_sw = o_sw.reshape(bs, N_KV_HEADS, GQA, dec_seq, N_KV_HEADS, HEAD_DIM)[:, heads, :, :, heads]
        o_c = o_c.reshape(bs, N_KV_HEADS, GQA, dec_seq, N_KV_HEADS, 2, HEAD_DIM)[:, heads, :, :, heads, 1]
        o_att = (o_sw + o_c).transpose(1, 3, 0, 2, 4).reshape(bs * dec_seq, ATT_WIDTH)
        conv_prev = jnp.pad(state_conv[l], ((0, 0), (SUBLANES - (CONV_W - 1), 0), (0, 0)))
        o_ssd, h_new, conv_new = _ssd(
            z.reshape(bs, dec_seq, SSD_INNER), xbc.reshape(bs, dec_seq, CONV_DIM), dt.reshape(bs, dec_seq, TILE),
            dt_t.reshape(TILE, bs, dec_seq).transpose(1, 0, 2), conv_prev, state_ssm[l], ssd_consts,
            min(CHUNK, dec_seq))
        ys = _mixout_ffn(ys, o_att, o_ssd.reshape(bs * dec_seq, SSD_INNER), att_norm_l, mix_post_l, wo_att,
                         wo_ssd, *ffn2)
        win_all = jnp.concatenate([cache_win_kv[l], wkv.reshape((bs, dec_seq) + kv_shape)], axis=1)
        outs[5].append(ckv.reshape((bs, dec_seq) + kv_shape))
        outs[6].append(skv.reshape((bs, dec_seq) + kv_shape))
        outs[7].append(win_all[:, -min(WINDOW, w_buf + dec_seq):])
        outs[8].append(h_new)
        outs[9].append(conv_new)

    return (yp.reshape(bp, seq, d), ys.reshape(bs, dec_seq, d)) + tuple(jnp.stack(o) for o in outs)
```

```python
import functools
import math

import jax
import jax.numpy as jnp
from jax import lax
from jax.experimental import pallas as pl
from jax.experimental.pallas import tpu as pltpu

F32 = jnp.float32
BF16 = jnp.bfloat16

D_MODEL = 1024
N_ATT_HEADS = 16
HEAD_DIM = 64
N_KV_HEADS = 4
GQA = N_ATT_HEADS // N_KV_HEADS
ATT_WIDTH = N_ATT_HEADS * HEAD_DIM
KV_WIDTH = 2 * N_KV_HEADS * HEAD_DIM
KV_GROUP = 2 * HEAD_DIM
BLK = 64
N_SEL = 16
WINDOW = 512
N_BRANCH = 3
N_BUCKETS = 32
MAX_DISTANCE = 128
SSD_HEADS = 16
SSD_HEAD_DIM = 64
SSD_INNER = SSD_HEADS * SSD_HEAD_DIM
SSD_GROUPS = 2
SSD_STATE = 128
CONV_W = 4
CONV_DIM = SSD_INNER + 2 * SSD_GROUPS * SSD_STATE
CHUNK = 128
PAGE_SIZE = 128
EPS = 1e-6

TILE = 128
SUBLANES = 8
MASK_VALUE = -(2.0 ** 100)
M_INIT = -1e30
LOG2E = math.log2(math.e)
VMEM_LIMIT = 56 * 1024 * 1024
PAGES_PER_STEP = 16
COMPRESS_ROWS_PER_DOT = 16
STAGE_PITCH = 72

NT_DIMS = (((1,), (1,)), ((), ()))
TN_DIMS = (((0,), (0,)), ((), ()))


def _params(n_grid_dims):
    return pltpu.CompilerParams(dimension_semantics=("arbitrary",) * n_grid_dims,
                                vmem_limit_bytes=VMEM_LIMIT)


def _rms(x, g):
    return x * lax.rsqrt(jnp.mean(x * x, axis=-1, keepdims=True) + EPS) * g


def _const_spec(shape):
    zeros = (0,) * len(shape)
    return pl.BlockSpec(shape, lambda *_: zeros, pipeline_mode=pl.Buffered(1))


def _ffn_body(x_ref, pre_ref, post_ref, wg_ref, wu_ref, wo_ref, o_ref):
    x = x_ref[...]
    h = _rms(x, pre_ref[...]).astype(BF16)
    gate = jnp.dot(h, wg_ref[...], preferred_element_type=F32)
    up = jnp.dot(h, wu_ref[...], preferred_element_type=F32)
    a = (gate * jax.nn.sigmoid(gate) * up).astype(BF16)
    y = jnp.dot(a, wo_ref[...], preferred_element_type=F32)
    o_ref[...] = x + 0.5 * _rms(y, post_ref[...])


def _ffn(x, pre, post, wg, wu, wo):
    t, d = x.shape
    f = wg.shape[1]
    tm = min(256, t)
    row = pl.BlockSpec((tm, d), lambda i: (i, 0))
    return pl.pallas_call(
        _ffn_body, grid=(t // tm,),
        in_specs=[row, _const_spec((1, d)), _const_spec((1, d)), _const_spec((d, f)), _const_spec((d, f)),
                  _const_spec((f, d))],
        out_specs=row, out_shape=jax.ShapeDtypeStruct((t, d), F32),
        compiler_params=_params(1), name="ffn")(x, pre, post, wg, wu, wo)


_SEG = {}
_off = 0
for _name, _w in (("q", ATT_WIDTH), ("ckv", KV_WIDTH), ("skv", KV_WIDTH), ("wkv", KV_WIDTH),
                  ("gates", N_KV_HEADS * TILE), ("z", SSD_INNER), ("xbc", CONV_DIM), ("dt", TILE)):
    _SEG[_name] = (_off, _off + _w)
    _off += _w
PROJ_WIDTH = _off


def _inproj_body(x_ref, pre_ref, w_ref, wt_ref, q_ref, ckv_ref, skvb_ref, wkvb_ref, gates_ref, z_ref, xbc_ref,
                 dt_ref, dtt_ref, *kv_refs, feature_major_kv):
    h = _rms(x_ref[...], pre_ref[...]).astype(BF16)

    def proj(name):
        lo, hi = _SEG[name]
        return jnp.dot(h, w_ref[:, lo:hi], preferred_element_type=F32)

    q_ref[...] = proj("q")
    ckv_ref[...] = proj("ckv")
    skv = proj("skv")
    skvb_ref[...] = skv.astype(BF16)
    wkv = proj("wkv")
    wkvb_ref[...] = wkv.astype(BF16)
    gates_ref[...] = proj("gates")
    z_ref[...] = proj("z")
    xbc_ref[...] = proj("xbc")
    dt_ref[...] = proj("dt")
    dtt_ref[...] = lax.dot_general(wt_ref[0:TILE, :], h, NT_DIMS, preferred_element_type=F32)
    if feature_major_kv:
        for n, ref in enumerate(kv_refs):
            lo = TILE + n * KV_WIDTH
            ref[...] = lax.dot_general(wt_ref[lo:lo + KV_WIDTH, :], h, NT_DIMS, preferred_element_type=F32)
    else:
        kv_refs[0][...] = skv
        kv_refs[1][...] = wkv


def _inproj(x, pre, w, w_t, n_seq, feature_major_kv):
    t, d = x.shape
    tm = min(256, t)
    seq = t // n_seq
    per_seq = seq // tm
    assert seq % tm == 0

    def cols(height):
        return pl.BlockSpec((None, height, tm), lambda i: (i // per_seq, 0, i % per_seq))

    def rows(width):
        return pl.BlockSpec((tm, width), lambda i: (i, 0))

    widths = (ATT_WIDTH, KV_WIDTH, KV_WIDTH, KV_WIDTH, N_KV_HEADS * TILE, SSD_INNER, CONV_DIM, TILE)
    dtypes = (F32, F32, BF16, BF16, F32, F32, F32, F32)
    out_shape = [jax.ShapeDtypeStruct((t, wd), dt) for wd, dt in zip(widths, dtypes)]
    out_specs = [rows(wd) for wd in widths]
    out_shape.append(jax.ShapeDtypeStruct((n_seq, TILE, seq), F32))
    out_specs.append(cols(TILE))
    if feature_major_kv:
        out_shape += [jax.ShapeDtypeStruct((n_seq, KV_WIDTH, seq), F32)] * 3
        out_specs += [cols(KV_WIDTH)] * 3
    else:
        out_shape += [jax.ShapeDtypeStruct((t, KV_WIDTH), F32)] * 2
        out_specs += [rows(KV_WIDTH)] * 2
    return pl.pallas_call(
        functools.partial(_inproj_body, feature_major_kv=feature_major_kv), grid=(t // tm,),
        in_specs=[rows(d), _const_spec((1, d)), _const_spec((d, PROJ_WIDTH)), _const_spec(w_t.shape)],
        out_specs=out_specs, out_shape=out_shape,
        compiler_params=_params(1), name="inproj")(x, pre, w, w_t)


def _mixout_body(x_ref, oa_ref, os_ref, an_ref, post_ref, wa_ref, ws_ref, o_ref):
    a = _rms(oa_ref[...], an_ref[...]).astype(BF16)
    y = jnp.dot(a, wa_ref[...], preferred_element_type=F32)
    y = y + jnp.dot(os_ref[...], ws_ref[...], preferred_element_type=F32)
    o_ref[...] = x_ref[...] + _rms(y, post_ref[...])


def _mixout_ffn_body(x_ref, oa_ref, os_ref, an_ref, mpost_ref, wa_ref, ws_ref, pre_ref, post_ref, wg_ref, wu_ref,
                     wo_ref, o_ref):
    _mixout_body(x_ref, oa_ref, os_ref, an_ref, mpost_ref, wa_ref, ws_ref, o_ref)
    _ffn_body(o_ref, pre_ref, post_ref, wg_ref, wu_ref, wo_ref, o_ref)


def _mixout_ffn(x, o_att, o_ssd, att_norm, mix_post, wa, ws, pre, post, wg, wu, wo):
    t, d = x.shape
    f = wg.shape[1]
    tm = min(256, t)
    row = pl.BlockSpec((tm, d), lambda i: (i, 0))
    vec = _const_spec((1, d))
    return pl.pallas_call(
        _mixout_ffn_body, grid=(t // tm,),
        in_specs=[row, row, row, vec, vec, _const_spec((d, d)), _const_spec((d, d)), vec, vec,
                  _const_spec((d, f)), _const_spec((d, f)), _const_spec((f, d))],
        out_specs=row, out_shape=jax.ShapeDtypeStruct((t, d), F32),
        compiler_params=_params(1), name="mixout_ffn")(x, o_att, o_ssd, att_norm, mix_post, wa, ws, pre, post, wg,
                                                       wu, wo)


def _token_rows(ref, k, lo=0, n=None):
    n = ref.shape[-1] if n is None else n
    return ref[k, :, :, lo:lo + n].reshape(KV_GROUP, n).T


def _compress_rows(x_refs, pe_ref, bd_ref, n_blocks, pitch=BLK):
    acc = None
    for l0 in range(0, BLK, COMPRESS_ROWS_PER_DOT):
        pieces = []
        for l in range(l0, l0 + COMPRESS_ROWS_PER_DOT):
            pe = pe_ref[l:l + 1, :]
            xs = jnp.concatenate([x_ref[pl.ds(l, n_blocks, stride=pitch), :] + pe for x_ref in x_refs], axis=0)
            pieces.append(xs.astype(BF16))
        w = bd_ref[l0:l0 + COMPRESS_ROWS_PER_DOT].reshape(COMPRESS_ROWS_PER_DOT * KV_GROUP, KV_GROUP)
        part = jnp.dot(jnp.concatenate(pieces, axis=1), w, preferred_element_type=F32)
        acc = part if acc is None else acc + part
    return acc


def _store_summaries(acc, o_ref, n_blocks):
    for k in range(N_KV_HEADS):
        o_ref[0:n_blocks, k * KV_GROUP:(k + 1) * KV_GROUP] = acc[k * n_blocks:(k + 1) * n_blocks].astype(BF16)


def _compress_body(*refs, n_blocks, n_out):
    x_refs = refs[:N_KV_HEADS]
    pe_ref, bd_ref, o_ref = refs[N_KV_HEADS:]
    _store_summaries(_compress_rows(x_refs, pe_ref, bd_ref, n_blocks), o_ref, n_blocks)
    if n_out > n_blocks:
        o_ref[n_blocks:n_out, :] = jnp.zeros((n_out - n_blocks, KV_WIDTH), BF16)


def _compress(kv, pe, bd, n_out):
    b, s, _ = kv.shape
    n_blocks = s // BLK

    def head_spec(k):
        return pl.BlockSpec((None, s, KV_GROUP), lambda i: (i, 0, k))

    return pl.pallas_call(
        functools.partial(_compress_body, n_blocks=n_blocks, n_out=n_out), grid=(b,),
        in_specs=[head_spec(k) for k in range(N_KV_HEADS)]
        + [_const_spec((BLK, KV_GROUP)), _const_spec((BLK, KV_GROUP, KV_GROUP))],
        out_specs=pl.BlockSpec((None, n_out, KV_WIDTH), lambda i: (i, 0, 0)),
        out_shape=jax.ShapeDtypeStruct((b, n_out, KV_WIDTH), BF16),
        compiler_params=_params(1), name="compress")(*([kv] * N_KV_HEADS), pe, bd)


def _compress_paged_body(pt_ref, *refs, n_pages):
    page_refs = refs[:PAGES_PER_STEP]
    pe_ref, bd_ref, o_ref, stage_ref = refs[PAGES_PER_STEP:]
    j = pl.program_id(1)
    blocks_per_page = PAGE_SIZE // BLK
    for p in range(PAGES_PER_STEP):
        first_block = (j * PAGES_PER_STEP + p) * blocks_per_page
        for k in range(N_KV_HEADS):
            rows = _token_rows(page_refs[p], k)
            for h in range(blocks_per_page):
                at = pl.multiple_of((first_block + h) * STAGE_PITCH, SUBLANES)
                stage_ref[k, pl.ds(at, BLK), :] = rows[h * BLK:(h + 1) * BLK]

    @pl.when(j == pl.num_programs(1) - 1)
    def _():
        n_blocks = n_pages * blocks_per_page
        x_refs = [stage_ref.at[k] for k in range(N_KV_HEADS)]
        _store_summaries(_compress_rows(x_refs, pe_ref, bd_ref, n_blocks, STAGE_PITCH), o_ref, n_blocks)


def _compress_paged(cache, page_table, pe, bd):
    b, n_pages = page_table.shape
    steps = n_pages // PAGES_PER_STEP
    n_blocks = n_pages * (PAGE_SIZE // BLK)

    def page_spec(p):
        return pl.BlockSpec((None, N_KV_HEADS, 2, HEAD_DIM, PAGE_SIZE),
                            lambda i, j, pt: (pt[i, j * PAGES_PER_STEP + p], 0, 0, 0, 0))

    grid_spec = pltpu.PrefetchScalarGridSpec(
        num_scalar_prefetch=1, grid=(b, steps),
        in_specs=[page_spec(p) for p in range(PAGES_PER_STEP)]
        + [pl.BlockSpec((BLK, KV_GROUP), lambda i, j, pt: (0, 0)),
           pl.BlockSpec((BLK, KV_GROUP, KV_GROUP), lambda i, j, pt: (0, 0, 0))],
        out_specs=pl.BlockSpec((None, n_blocks, KV_WIDTH), lambda i, j, pt: (i, 0, 0)),
        scratch_shapes=[pltpu.VMEM((N_KV_HEADS, n_blocks * STAGE_PITCH, KV_GROUP), F32)])
    return pl.pallas_call(
        functools.partial(_compress_paged_body, n_pages=n_pages), grid_spec=grid_spec,
        out_shape=jax.ShapeDtypeStruct((b, n_blocks, KV_WIDTH), BF16),
        compiler_params=_params(2), name="compress_paged")(page_table, *([cache] * PAGES_PER_STEP), pe, bd)


def _rel_bucket(dist):
    n = jnp.maximum(dist, 0)
    max_exact = N_BUCKETS // 2
    nf = jnp.maximum(n, 1).astype(F32)
    log_b = max_exact + (jnp.log(nf / max_exact) / math.log(MAX_DISTANCE / max_exact)
                         * (N_BUCKETS - max_exact)).astype(jnp.int32)
    return jnp.where(n < max_exact, n, jnp.minimum(log_b, N_BUCKETS - 1))


def _bias_of(rel_table, dist, visible):
    onehot = (_rel_bucket(dist)[..., None] == jnp.arange(N_BUCKETS, dtype=jnp.int32)).astype(F32)
    bias = jnp.einsum('rcb,bh->hrc', onehot, rel_table.astype(F32) * LOG2E, precision=lax.Precision.HIGHEST)
    return jnp.where(visible[None], bias, MASK_VALUE).reshape(-1, dist.shape[-1])


KIND_ZERO, KIND_DIAG, KIND_PREV, KIND_MASKED, KIND_OLDEST = range(5)


def _prompt_bias_tables(rel_table):
    r = jnp.arange(TILE, dtype=jnp.int32)[:, None]
    c = jnp.arange(TILE, dtype=jnp.int32)[None, :]
    true = jnp.ones((TILE, TILE), bool)
    far = _bias_of(rel_table, jnp.full((TILE, TILE), MAX_DISTANCE, jnp.int32), true)
    tiles = jnp.stack([
        jnp.zeros_like(far),
        _bias_of(rel_table, r - c, r >= c) - far,
        _bias_of(rel_table, TILE + r - c, true) - far,
        jnp.full_like(far, MASK_VALUE),
        jnp.where(jnp.tile(r < c, (N_ATT_HEADS, 1)), 0.0, MASK_VALUE),
    ])
    c_dist = r - (BLK - 1) + BLK * (BLK - 1 - c)
    cmp_tab = _bias_of(rel_table, c_dist, c_dist >= 0)
    return tiles, cmp_tab


def _selection_mask(score, n_rows, live=None):
    n_groups = score.shape[0] // SUBLANES
    groups = [score[SUBLANES * j:SUBLANES * (j + 1)] for j in range(n_groups)]
    row8 = lax.broadcasted_iota(jnp.int32, groups[0].shape, 0)

    def beaten_by(rows, ranks):
        for n in rows:
            other = jnp.broadcast_to(score[n:n + 1, :], groups[0].shape)
            for j in range(n_groups):
                lo = SUBLANES * j
                if lo > n:
                    inc = jnp.where(other >= groups[j], 1, 0)
                elif lo + SUBLANES - 1 <= n:
                    inc = jnp.where(other > groups[j], 1, 0)
                else:
                    inc = jnp.where(row8 + lo > n, jnp.where(other >= groups[j], 1, 0),
                                    jnp.where(other > groups[j], 1, 0))
                ranks[j] = ranks[j] + inc
        return jnp.concatenate(ranks, axis=0)

    zeros = [jnp.zeros(groups[0].shape, jnp.int32) for _ in range(n_groups)]
    if live is None:
        rank = beaten_by(range(n_rows), zeros)
    else:
        n_live, rank_ref = live
        rank_ref[...] = jnp.zeros(rank_ref.shape, jnp.int32)
        for lo in range(0, n_rows, SUBLANES):
            @pl.when(lo < n_live)
            def _():
                rank_ref[...] += beaten_by(range(lo, min(lo + SUBLANES, n_rows)), list(zeros))
        rank = rank_ref[...]
    return jnp.where(rank < N_SEL, jnp.where(score > -jnp.inf, 0.0, MASK_VALUE), MASK_VALUE)


PAIR = 2 * TILE
Q_TILES = PAIR // TILE
SEL_UNROLL = 4
_SEL_KINDS = {0: ((KIND_DIAG, KIND_MASKED), (KIND_PREV, KIND_DIAG)),
              1: ((KIND_ZERO, KIND_PREV), (KIND_ZERO, KIND_ZERO))}
_WIN_OLDEST = ((KIND_OLDEST, KIND_ZERO), (KIND_MASKED, KIND_OLDEST))


def _nsa_prompt_body(q_ref, gates_ref, kvc_ref, skv_ref, wkv_ref, tb_ref, ctab_ref, o_ref,
                     s_ref, sw_ref, mx_ref, mb_ref, acc_ref, rank_ref, part_ref, *, n_blocks):
    step = pl.program_id(2)
    head_rows = GQA * TILE
    rows = Q_TILES * head_rows
    lane = lax.broadcasted_iota(jnp.int32, (rows, TILE), 1)
    low = lane < HEAD_DIM
    key_lane = lax.broadcasted_iota(jnp.int32, (PAIR, TILE), 1)
    key_blk = lax.broadcasted_iota(jnp.int32, (PAIR, TILE), 0) // BLK
    n_pairs = skv_ref.shape[0] // PAIR

    qf = q_ref[...]
    halves = [qf[:, c * TILE:(c + 1) * TILE] for c in range(GQA // 2)]
    rolled = [pltpu.roll(h, HEAD_DIM, 1) for h in halves]
    q_all = jnp.concatenate([(halves if g % 2 == 0 else rolled)[g // 2][t * TILE:(t + 1) * TILE]
                             for t in range(Q_TILES) for g in range(GQA)], axis=0)
    lhs_plain = jnp.where(low, q_all, 0.0).astype(BF16)

    def ones_and_values(kv):
        kv_lane = lax.broadcasted_iota(jnp.int32, kv.shape, 1)
        return jnp.where(kv_lane < HEAD_DIM, jnp.ones_like(kv), kv)

    def normalise(acc):
        return acc / jnp.maximum(pltpu.roll(acc, HEAD_DIM, 1), 1e-30)

    def bias_of(kinds_per_tile, valid):
        parts = []
        for left, right in kinds_per_tile:
            left = jnp.where(valid, left, KIND_MASKED)
            right = jnp.where(valid, right, KIND_MASKED)
            parts.append(jnp.concatenate([tb_ref[left], tb_ref[right]], axis=1))
        return jnp.concatenate(parts, axis=0)

    def sel_kinds(j):
        behind = step - j
        return [tuple(jnp.where(behind == 0, _SEL_KINDS[0][t][side],
                                jnp.where(behind == 1, _SEL_KINDS[1][t][side], KIND_ZERO)) for side in range(2))
                for t in range(Q_TILES)]

    def keys_of(kv_ref, j):
        jc = jnp.clip(j, 0, n_pairs - 1)
        return kv_ref[pl.ds(pl.multiple_of(jc * PAIR, PAIR), PAIR), :]

    def score_pair(lhs, kv_ref, j, bias, block_columns):
        rhs = keys_of(kv_ref, j)
        if block_columns:
            onehot = jnp.where(key_lane - HEAD_DIM == (PAIR // BLK) * j + key_blk, 1.0, 0.0).astype(BF16)
            rhs = jnp.where(key_lane < HEAD_DIM, rhs, onehot)
        return lax.dot_general(lhs, rhs, NT_DIMS, preferred_element_type=F32) + bias

    def half_max(sc):
        return jnp.maximum(sc[:, :TILE], sc[:, TILE:])

    def weigh_pair(sc, mb, kv_ref, j):
        pr = jnp.exp2(sc - jnp.concatenate([mb, mb], axis=1))
        return jnp.dot(pr.astype(BF16), ones_and_values(keys_of(kv_ref, j)), preferred_element_type=F32)

    def row_max(mx):
        return jnp.broadcast_to(jnp.max(mx, axis=1, keepdims=True), (rows, TILE))

    n_win = WINDOW // PAIR + 1
    w_pairs = [step - (n_win - 1) + w for w in range(n_win)]
    w_kinds = [_WIN_OLDEST] + [[(KIND_ZERO, KIND_ZERO)] * Q_TILES] * (n_win - 3) + [_SEL_KINDS[1], _SEL_KINDS[0]]
    w_mx = None
    for w, j in enumerate(w_pairs):
        sc = score_pair(lhs_plain, wkv_ref, j, bias_of(w_kinds[w], j >= 0), False)
        sw_ref[w] = sc
        w_mx = half_max(sc) if w_mx is None else jnp.maximum(w_mx, half_max(sc))

    kvc = kvc_ref[...]
    bias = []
    for t in range(Q_TILES):
        i = Q_TILES * step + t
        shift = (2 * i + (TILE - (BLK - 1))) % TILE
        bias += [pltpu.roll(ctab_ref[g * TILE:(g + 1) * TILE, :], shift, 1) for g in range(GQA)]
    s = lax.dot_general(lhs_plain, kvc, NT_DIMS, preferred_element_type=F32)
    s = jnp.where(lane < n_blocks, s + jnp.concatenate(bias, axis=0), MASK_VALUE)
    m = jnp.maximum(jnp.max(s, axis=1, keepdims=True), M_INIT)
    p = jnp.exp2(s - m)
    c_acc = jnp.dot(p.astype(BF16), ones_and_values(kvc), preferred_element_type=F32)
    p = p / jnp.maximum(c_acc, 1e-30)
    o_cmp = normalise(c_acc)

    w_mb = row_max(jnp.maximum(w_mx, M_INIT))
    w_acc = None
    for w, j in enumerate(w_pairs):
        dacc = weigh_pair(sw_ref[w], w_mb, wkv_ref, j)
        w_acc = dacc if w_acc is None else w_acc + dacc
    o_win = normalise(w_acc)

    gate = jax.nn.sigmoid(gates_ref[...])

    def gate_col(t, g, br):
        c = g * N_BRANCH + br
        return gate[t * TILE:(t + 1) * TILE, c:c + 1]

    heads = [(t, g, slice(t * head_rows + g * TILE, t * head_rows + (g + 1) * TILE))
             for t in range(Q_TILES) for g in range(GQA)]
    for t, g, head in heads:
        part_ref[head, :] = gate_col(t, g, 0) * o_cmp[head] + gate_col(t, g, 2) * o_win[head]

    blk = lax.broadcasted_iota(jnp.int32, (n_blocks, PAIR), 0)
    query = lax.broadcasted_iota(jnp.int32, (n_blocks, PAIR), 1)
    cur = (PAIR // BLK) * step + query // BLK
    imp_t = []
    for t in range(Q_TILES):
        imp = p[t * head_rows:t * head_rows + TILE]
        for g in range(1, GQA):
            imp = imp + p[t * head_rows + g * TILE:t * head_rows + (g + 1) * TILE]
        imp_t.append(imp.T[0:n_blocks])
    forced = (blk == 0) | (blk == cur) | (blk == cur - 1)
    score = jnp.where(forced, jnp.inf, jnp.where(blk <= cur, jnp.concatenate(imp_t, axis=1), -jnp.inf))
    neg_t = _selection_mask(score, n_blocks, ((PAIR // BLK) * (step + 1), rank_ref))
    neg = []
    for t in range(Q_TILES):
        pieces = [jnp.zeros((HEAD_DIM, TILE), F32), neg_t[:, t * TILE:(t + 1) * TILE]]
        if n_blocks < TILE - HEAD_DIM:
            pieces.append(jnp.zeros((TILE - HEAD_DIM - n_blocks, TILE), F32))
        neg += [jnp.concatenate(pieces, axis=0).T] * GQA
    lhs_sel = jnp.where(low, q_all, jnp.concatenate(neg, axis=0)).astype(BF16)

    n_trips = (step + SEL_UNROLL) // SEL_UNROLL
    mx_ref[...] = jnp.full(mx_ref.shape, M_INIT, F32)

    def score_trip(t, carry):
        mx = None
        for u in range(SEL_UNROLL):
            j = t * SEL_UNROLL + u
            sc = score_pair(lhs_sel, skv_ref, j, bias_of(sel_kinds(j), j <= step), True)
            s_ref[j] = sc
            mx = half_max(sc) if mx is None else jnp.maximum(mx, half_max(sc))
        mx_ref[...] = jnp.maximum(mx_ref[...], mx)
        return carry

    lax.fori_loop(0, n_trips, score_trip, 0)
    mb_ref[...] = row_max(mx_ref[...])
    acc_ref[...] = jnp.zeros(acc_ref.shape, F32)

    def weigh_trip(t, carry):
        mb = mb_ref[...]
        acc = None
        for u in range(SEL_UNROLL):
            j = t * SEL_UNROLL + u
            dacc = weigh_pair(s_ref[j], mb, skv_ref, j)
            acc = dacc if acc is None else acc + dacc
        acc_ref[...] += acc
        return carry

    lax.fori_loop(0, n_trips, weigh_trip, 0)
    o_sel = normalise(acc_ref[...])

    out_low = lax.broadcasted_iota(jnp.int32, (TILE, TILE), 1) < HEAD_DIM
    outs = {(t, g): part_ref[head, :] + gate_col(t, g, 1) * o_sel[head] for t, g, head in heads}
    for t in range(Q_TILES):
        for c in range(GQA // 2):
            o_ref[t * TILE:(t + 1) * TILE, c * TILE:(c + 1) * TILE] = jnp.where(
                out_low, pltpu.roll(outs[t, 2 * c], HEAD_DIM, 1), outs[t, 2 * c + 1])


def _nsa_prompt(q, gates, kvc, skv_b, wkv_b, tiles, cmp_tab):
    b, s, _ = q.shape
    n_blocks = s // BLK
    assert n_blocks <= TILE - HEAD_DIM and kvc.shape[1] == TILE and s % PAIR == 0 and WINDOW // PAIR >= 2
    gw = GQA * HEAD_DIM
    head_rows = GQA * TILE
    rows = Q_TILES * head_rows
    return pl.pallas_call(
        functools.partial(_nsa_prompt_body, n_blocks=n_blocks), grid=(b, N_KV_HEADS, s // PAIR),
        in_specs=[pl.BlockSpec((None, PAIR, gw), lambda bi, k, i: (bi, i, k)),
                  pl.BlockSpec((None, PAIR, TILE), lambda bi, k, i: (bi, i, k)),
                  pl.BlockSpec((None, TILE, KV_GROUP), lambda bi, k, i: (bi, 0, k)),
                  pl.BlockSpec((None, s, KV_GROUP), lambda bi, k, i: (bi, 0, k)),
                  pl.BlockSpec((None, s, KV_GROUP), lambda bi, k, i: (bi, 0, k)),
                  pl.BlockSpec((len(tiles), head_rows, TILE), lambda bi, k, i: (0, k, 0)),
                  pl.BlockSpec((head_rows, TILE), lambda bi, k, i: (k, 0))],
        out_specs=pl.BlockSpec((None, PAIR, gw), lambda bi, k, i: (bi, i, k)),
        out_shape=jax.ShapeDtypeStruct((b, s, ATT_WIDTH), F32),
        scratch_shapes=[pltpu.VMEM((s // PAIR + SEL_UNROLL - 1, rows, PAIR), F32),
                        pltpu.VMEM((WINDOW // PAIR + 1, rows, PAIR), F32),
                        pltpu.VMEM((rows, TILE), F32), pltpu.VMEM((rows, TILE), F32),
                        pltpu.VMEM((rows, KV_GROUP), F32), pltpu.VMEM((n_blocks, PAIR), jnp.int32),
                        pltpu.VMEM((rows, TILE), F32)],
        compiler_params=_params(3), name="nsa_prompt")(q, gates, kvc, skv_b, wkv_b, tiles, cmp_tab)


def _nsa_sample_body(pt_ref, *refs, n_pages, dec_seq, past_len):
    page_refs = refs[:PAGES_PER_STEP]
    (qk_ref, gates_ref, kc_ref, ctab_ref, gsum_ref, knew_ref, vnew_ref, cwin_ref, kwnew_ref, vwnew_ref,
     last_ref, new_ref, win0_ref, o_ref, lhs_ref, m_ref, l_ref, acc_ref, ow_ref, oc_ref) = refs[PAGES_PER_STEP:]
    j = pl.program_id(1)
    last_step = pl.num_programs(1) - 1
    blocks_per_page = PAGE_SIZE // BLK
    n_past = n_pages * blocks_per_page
    kd = N_KV_HEADS * HEAD_DIM
    n_lanes = kc_ref.shape[0]
    step_keys = PAGES_PER_STEP * PAGE_SIZE

    def gate_col(br):
        return jax.nn.sigmoid(gates_ref[:, br:br + 1])

    def head_major(ref, c, lo=0, n=None):
        n = ref.shape[-1] if n is None else n
        return ref[:, c, :, lo:lo + n].reshape(kd, n).astype(BF16)

    def lane_tiles(x):
        return [x[:, t * TILE:(t + 1) * TILE] for t in range(x.shape[1] // TILE)]

    def tile_max(tiles):
        m = tiles[0]
        for t in tiles[1:]:
            m = jnp.maximum(m, t)
        return jnp.broadcast_to(jnp.max(m, axis=1, keepdims=True), (TILE, TILE))

    def tile_sum(tiles):
        s = tiles[0]
        for t in tiles[1:]:
            s = s + t
        return s

    def online_update(s, values_t):
        tiles = lane_tiles(s)
        m_prev = m_ref[...]
        m_new = jnp.maximum(m_prev, tile_max(tiles))
        alpha = jnp.exp2(m_prev - m_new)
        p = [jnp.exp2(t - m_new) for t in tiles]
        l_ref[...] = alpha * l_ref[...] + tile_sum(p)
        pv = lax.dot_general(jnp.concatenate(p, axis=1).astype(BF16), values_t, NT_DIMS, preferred_element_type=F32)
        acc_ref[...] = jnp.concatenate([alpha] * (kd // TILE), axis=1) * acc_ref[...] + pv
        m_ref[...] = m_new

    @pl.when(j == 0)
    def _():
        kc = kc_ref[...]
        qf = qk_ref[...].astype(F32)
        gap = jnp.zeros((TILE, HEAD_DIM), F32)
        qc = jnp.concatenate([piece for k in range(N_KV_HEADS)
                              for piece in (qf[:, k * HEAD_DIM:(k + 1) * HEAD_DIM], gap)], axis=1).astype(BF16)
        s = lax.dot_general(qc, kc, NT_DIMS, preferred_element_type=F32) + ctab_ref[...]
        m = jnp.maximum(jnp.max(s, axis=1, keepdims=True), M_INIT)
        p = jnp.exp2(s - m)
        p = p / jnp.maximum(jnp.sum(p, axis=1, keepdims=True), 1e-30)
        oc_ref[...] = gate_col(0) * jnp.dot(p.astype(BF16), kc, preferred_element_type=F32)
        imp = _select_sum(p, gsum_ref[...], False)
        imp_t = jnp.concatenate([t.T for t in lane_tiles(imp)], axis=0)
        n_rows = -(-(n_past + 1) // SUBLANES) * SUBLANES
        blk = lax.broadcasted_iota(jnp.int32, (n_rows, TILE), 0)
        tok = lax.broadcasted_iota(jnp.int32, (n_rows, TILE), 1) % dec_seq
        cur = (past_len + tok) // BLK
        forced = (blk == 0) | (blk == cur) | (blk == cur - 1)
        score = jnp.where(forced, jnp.inf, jnp.where(blk <= cur, imp_t[0:n_rows], -jnp.inf))
        neg_t = jnp.concatenate([_selection_mask(score, n_past + 1), jnp.zeros((n_lanes - n_rows, TILE), F32)], axis=0)
        neg = jnp.concatenate([neg_t[t * TILE:(t + 1) * TILE].T for t in range(n_lanes // TILE)], axis=1)
        lhs_ref[:, 0:kd] = qk_ref[...]
        lhs_ref[:, kd:kd + n_lanes] = neg.astype(BF16)

        qk = qk_ref[...]
        n_win = cwin_ref.shape[-1] // TILE
        zero = jnp.zeros((TILE, TILE), F32)
        s_w = jnp.dot(qk, head_major(cwin_ref, 0), preferred_element_type=F32)
        s_w = s_w + jnp.concatenate([win0_ref[...]] + [zero] * (n_win - 2) + [last_ref[...]], axis=1)
        s_n = jnp.dot(qk, kwnew_ref[...], preferred_element_type=F32) + new_ref[...]
        tiles = lane_tiles(s_w) + [s_n]
        m = tile_max(tiles)
        p = [jnp.exp2(t - m) for t in tiles]
        l = jnp.sum(tile_sum(p), axis=1, keepdims=True)
        o_w = lax.dot_general(jnp.concatenate(p[:-1], axis=1).astype(BF16), head_major(cwin_ref, 1), NT_DIMS,
                              preferred_element_type=F32)
        o_w = o_w + lax.dot_general(p[-1].astype(BF16), vwnew_ref[...], NT_DIMS, preferred_element_type=F32)
        ow_ref[...] = gate_col(2) * (o_w / l)

        m_ref[...] = jnp.full(m_ref.shape, M_INIT, F32)
        l_ref[...] = jnp.zeros(l_ref.shape, F32)
        acc_ref[...] = jnp.zeros(acc_ref.shape, F32)

    keys_t = jnp.concatenate([head_major(r, 0) for r in page_refs], axis=1)
    values_t = jnp.concatenate([head_major(r, 1) for r in page_refs], axis=1)
    blk_row = lax.broadcasted_iota(jnp.int32, (n_lanes, step_keys), 0)
    key_blk = j * (PAGES_PER_STEP * blocks_per_page) + lax.broadcasted_iota(jnp.int32, (n_lanes, step_keys), 1) // BLK
    onehot = jnp.where(blk_row == key_blk, 1.0, 0.0).astype(BF16)
    s = jnp.dot(lhs_ref[...], jnp.concatenate([keys_t, onehot], axis=0), preferred_element_type=F32)
    tiles = lane_tiles(s)
    tiles[-1] = tiles[-1] + jnp.where(j == last_step, last_ref[...], 0.0)
    online_update(jnp.concatenate(tiles, axis=1), values_t)

    @pl.when(j == last_step)
    def _():
        new_blk = lax.broadcasted_iota(jnp.int32, (n_lanes, TILE), 0) == n_past
        rhs = jnp.concatenate([knew_ref[...], jnp.where(new_blk, 1.0, 0.0).astype(BF16)], axis=0)
        online_update(jnp.dot(lhs_ref[...], rhs, preferred_element_type=F32) + new_ref[...], vnew_ref[...])
        l = jnp.sum(l_ref[...], axis=1, keepdims=True)
        o_sw = gate_col(1) * (acc_ref[...] / l) + ow_ref[...]
        o_c = oc_ref[...]
        rows_per_head = TILE // N_KV_HEADS
        for k in range(N_KV_HEADS):
            rows = slice(k * rows_per_head, (k + 1) * rows_per_head)
            v_lo = k * KV_GROUP + HEAD_DIM
            o_ref[rows, :] = o_sw[rows, k * HEAD_DIM:(k + 1) * HEAD_DIM] + o_c[rows, v_lo:v_lo + HEAD_DIM]


def _nsa_sample(page_table, cache_slc, qk, gates_c, kc, ctab, gsum, k_new, v_new, cache_win, kw_new, vw_new,
                last_tab, new_tab, win0_tab, *, dec_seq, past_len):
    b, n_pages = page_table.shape
    steps = n_pages // PAGES_PER_STEP
    n_lanes = kc.shape[1]
    w_buf = cache_win.shape[-1]
    kd = N_KV_HEADS * HEAD_DIM

    def page_spec(p):
        return pl.BlockSpec((None, N_KV_HEADS, 2, HEAD_DIM, PAGE_SIZE),
                            lambda i, j, pt: (pt[i, j * PAGES_PER_STEP + p], 0, 0, 0, 0))

    def per_batch(*shape):
        zeros = (0,) * len(shape)
        return pl.BlockSpec((None,) + shape, lambda i, j, pt: (i,) + zeros)

    def const(*shape):
        zeros = (0,) * len(shape)
        return pl.BlockSpec(shape, lambda i, j, pt: zeros)

    grid_spec = pltpu.PrefetchScalarGridSpec(
        num_scalar_prefetch=1, grid=(b, steps),
        in_specs=[page_spec(p) for p in range(PAGES_PER_STEP)]
        + [per_batch(TILE, kd), per_batch(TILE, TILE), per_batch(n_lanes, KV_WIDTH),
           const(TILE, n_lanes), const(TILE, TILE), per_batch(kd, TILE), per_batch(kd, TILE),
           per_batch(N_KV_HEADS, 2, HEAD_DIM, w_buf), per_batch(kd, TILE), per_batch(kd, TILE),
           const(TILE, TILE), const(TILE, TILE), const(TILE, TILE)],
        out_specs=per_batch(TILE, HEAD_DIM),
        scratch_shapes=[pltpu.VMEM((TILE, kd + n_lanes), BF16), pltpu.VMEM((TILE, TILE), F32),
                        pltpu.VMEM((TILE, TILE), F32), pltpu.VMEM((TILE, kd), F32), pltpu.VMEM((TILE, kd), F32),
                        pltpu.VMEM((TILE, KV_WIDTH), F32)])
    return pl.pallas_call(
        functools.partial(_nsa_sample_body, n_pages=n_pages, dec_seq=dec_seq, past_len=past_len),
        grid_spec=grid_spec, out_shape=jax.ShapeDtypeStruct((b, TILE, HEAD_DIM), F32),
        compiler_params=_params(2), name="nsa_sample")(
            page_table, *([cache_slc] * PAGES_PER_STEP), qk, gates_c, kc, ctab, gsum, k_new, v_new, cache_win,
            kw_new, vw_new, last_tab, new_tab, win0_tab)


def _split3(x):
    hi = x.astype(BF16)
    r = x - hi.astype(F32)
    mid = r.astype(BF16)
    return hi, mid, (r - mid.astype(F32)).astype(BF16)


def _select_sum(x, onehot, x_on_left):
    out = None
    for part in _split3(x):
        term = (jnp.dot(part, onehot, preferred_element_type=F32) if x_on_left
                else jnp.dot(onehot, part, preferred_element_type=F32))
        out = term if out is None else out + term
    return out


def _softplus(x):
    return jnp.maximum(x, 0.0) + jnp.log(1.0 + jnp.exp(-jnp.abs(x)))


def _ssd_body(z_ref, xbc_ref, dt_ref, dtt_ref, convp_ref, h0_ref, cw_ref, cb_ref, dtb_ref, dtbt_ref, al_ref,
              alt_ref, dskip_ref, norm_ref, o_ref, hout_ref, convout_ref, xp_ref, *, cl):
    c = pl.program_id(1)
    pad = SUBLANES

    @pl.when(c == 0)
    def _():
        xp_ref[0:pad, :] = convp_ref[...]
        hout_ref[...] = h0_ref[...]

    xp_ref[pad:pad + cl, :] = xbc_ref[...]
    conv = cb_ref[...]
    for k in range(CONV_W):
        lo = pad - (CONV_W - 1) + k
        conv = conv + cw_ref[k:k + 1, :] * xp_ref[lo:lo + cl, :]
    u = conv * jax.nn.sigmoid(conv)
    tail = xp_ref[pad + cl - (CONV_W - 1):pad + cl, :]
    convout_ref[...] = tail
    xp_ref[pad - (CONV_W - 1):pad, :] = tail

    xs = u[:, :SSD_INNER]
    dt = _softplus(dt_ref[:, 0:SSD_HEADS] + dtb_ref[...])
    dt_t = _softplus(dtt_ref[0:SSD_HEADS, :] + dtbt_ref[...])
    la = dt * -jnp.exp(al_ref[...])
    la_t = dt_t * -jnp.exp(alt_ref[...])
    row = lax.broadcasted_iota(jnp.int32, (cl, cl), 0)
    col = lax.broadcasted_iota(jnp.int32, (cl, cl), 1)
    causal = row >= col
    cs = _select_sum(la, jnp.where(causal, 1.0, 0.0).astype(BF16), False)
    cs_t = _select_sum(la_t, jnp.where(row <= col, 1.0, 0.0).astype(BF16), True)

    def spread(width):
        lane_head = lax.broadcasted_iota(jnp.int32, (SSD_HEADS, SSD_HEADS * width), 1) // width
        return jnp.where(lane_head == lax.broadcasted_iota(jnp.int32, lane_head.shape, 0), 1.0, 0.0).astype(BF16)

    dt_x = _select_sum(dt, spread(SSD_HEAD_DIM), True)
    cs_x = _select_sum(cs, spread(SSD_HEAD_DIM), True)
    cs_wide = _select_sum(cs, spread(TILE), True)
    last_x = cs_x[cl - 1:cl, :]
    xdt = xs * dt_x
    xdt_b = xdt.astype(BF16)
    to_end = (xdt * jnp.exp(last_x - cs_x)).astype(BF16)
    grow = jnp.exp(cs_x)
    skip = dskip_ref[...] * xs

    heads_per_group = SSD_HEADS // SSD_GROUPS
    bms, cms, cbs = [], [], []
    for g in range(SSD_GROUPS):
        b_lo = SSD_INNER + g * SSD_STATE
        c_lo = SSD_INNER + SSD_GROUPS * SSD_STATE + g * SSD_STATE
        bms.append(u[:, b_lo:b_lo + SSD_STATE].astype(BF16))
        cms.append(u[:, c_lo:c_lo + SSD_STATE].astype(BF16))
        cbs.append(lax.dot_general(cms[g], bms[g], NT_DIMS, preferred_element_type=F32))
    first_head = lax.broadcasted_iota(jnp.int32, (cl, TILE), 1) < SSD_HEAD_DIM
    state_row = lax.broadcasted_iota(jnp.int32, (2 * SSD_HEAD_DIM, SSD_STATE), 0)
    ys = []
    for pair in range(SSD_HEADS // 2):
        g = 2 * pair // heads_per_group
        lanes = slice(pair * TILE, (pair + 1) * TILE)
        y_heads, keep = [], []
        for h in (2 * pair, 2 * pair + 1):
            diff = cs_wide[:, h * TILE:h * TILE + cl] - cs_t[h:h + 1, :]
            decay = jnp.where(causal, jnp.exp(jnp.where(causal, diff, 0.0)), 0.0)
            y_heads.append(jnp.dot((cbs[g] * decay).astype(BF16), xdt_b[:, lanes], preferred_element_type=F32))
            keep.append(jnp.exp(cs[cl - 1:cl, h:h + 1]))
        state_in = lax.dot_general(to_end[:, lanes], bms[g], TN_DIMS, preferred_element_type=F32)
        h_prev = jnp.concatenate([hout_ref[2 * pair], hout_ref[2 * pair + 1]], axis=0)
        y_off = lax.dot_general(cms[g], h_prev.astype(BF16), NT_DIMS, preferred_element_type=F32)
        h_new = h_prev * jnp.where(state_row < SSD_HEAD_DIM, keep[0], keep[1]) + state_in
        hout_ref[2 * pair] = h_new[0:SSD_HEAD_DIM]
        hout_ref[2 * pair + 1] = h_new[SSD_HEAD_DIM:]
        ys.append(jnp.where(first_head, y_heads[0], y_heads[1]) + y_off * grow[:, lanes] + skip[:, lanes])
    y = jnp.concatenate(ys, axis=1)
    z = z_ref[...]
    y = y * (z * jax.nn.sigmoid(z))
    gw = SSD_INNER // SSD_GROUPS
    normed = []
    for g in range(SSD_GROUPS):
        yg = y[:, g * gw:(g + 1) * gw]
        normed.append(yg * lax.rsqrt(jnp.mean(yg * yg, axis=-1, keepdims=True) + EPS))
    o_ref[...] = (jnp.concatenate(normed, axis=1) * norm_ref[...]).astype(o_ref.dtype)


def _ssd(z, xbc, dt, dt_t, conv_prev, h0, consts, cl):
    b, seq, _ = z.shape
    nc = seq // cl
    cw, cbias, dtb, dtb_t, al, al_t, dskip, norm = consts

    def rows(width):
        return pl.BlockSpec((None, cl, width), lambda i, c: (i, c, 0))

    state = pl.BlockSpec((None, SSD_HEADS, SSD_HEAD_DIM, SSD_STATE), lambda i, c: (i, 0, 0, 0))

    def const(x):
        zeros = (0,) * x.ndim
        return pl.BlockSpec(x.shape, lambda i, c: zeros)

    return pl.pallas_call(
        functools.partial(_ssd_body, cl=cl), grid=(b, nc),
        in_specs=[rows(SSD_INNER), rows(CONV_DIM), rows(TILE),
                  pl.BlockSpec((None, TILE, cl), lambda i, c: (i, 0, c)),
                  pl.BlockSpec((None, SUBLANES, CONV_DIM), lambda i, c: (i, 0, 0)), state,
                  const(cw), const(cbias), const(dtb), const(dtb_t), const(al), const(al_t), const(dskip),
                  const(norm)],
        out_specs=[rows(SSD_INNER), state, pl.BlockSpec((None, CONV_W - 1, CONV_DIM), lambda i, c: (i, 0, 0))],
        out_shape=[jax.ShapeDtypeStruct((b, seq, SSD_INNER), BF16),
                   jax.ShapeDtypeStruct((b, SSD_HEADS, SSD_HEAD_DIM, SSD_STATE), F32),
                   jax.ShapeDtypeStruct((b, CONV_W - 1, CONV_DIM), F32)],
        scratch_shapes=[pltpu.VMEM((SUBLANES + cl, CONV_DIM), F32)],
        compiler_params=_params(2), name="ssd")(z, xbc, dt, dt_t, conv_prev, h0, cw, cbias, dtb, dtb_t, al, al_t,
                                                 dskip, norm)


def _pack_inproj(w_in):
    splits = (ATT_WIDTH, KV_WIDTH, KV_WIDTH, KV_WIDTH, N_BRANCH * N_ATT_HEADS, SSD_INNER, CONV_DIM, SSD_HEADS)
    offs = [0]
    for s in splits:
        offs.append(offs[-1] + s)
    wq, wc, ws, ww, wg, wz, wx, wd = [w_in[:, offs[i]:offs[i + 1]] for i in range(len(splits))]
    d = w_in.shape[0]
    per_kv = GQA * N_BRANCH
    wg4 = jnp.pad(wg.reshape(d, N_KV_HEADS, per_kv), ((0, 0), (0, 0), (0, TILE - per_kv)))
    wd_pad = jnp.pad(wd, ((0, 0), (0, TILE - SSD_HEADS)))
    w = jnp.concatenate([wq * (HEAD_DIM ** -0.5 * LOG2E), wc, ws, ww, wg4.reshape(d, N_KV_HEADS * TILE), wz, wx,
                         wd_pad], axis=1)
    assert w.shape[1] == PROJ_WIDTH
    w_t = jnp.concatenate([wd_pad, wc, ws, ww], axis=1).T
    return w.astype(BF16), w_t.astype(BF16)


def _pack_compress(cmp_pe, w_cmp):
    zero = jnp.zeros((BLK, HEAD_DIM, HEAD_DIM), w_cmp.dtype)
    bd = jnp.concatenate([jnp.concatenate([w_cmp[:, 0], zero], axis=2),
                          jnp.concatenate([zero, w_cmp[:, 1]], axis=2)], axis=1)
    pe = cmp_pe.reshape(BLK, KV_GROUP)
    return pe.astype(F32), bd.astype(BF16)


def _sample_tables(rel_table, dec_seq, past_len, n_lanes):
    col = jnp.arange(TILE, dtype=jnp.int32)
    head = (col // (GQA * dec_seq)) * GQA + (col // dec_seq) % GQA
    tok = (col % dec_seq)[:, None]
    per_col = (rel_table.astype(F32) * LOG2E)[:, head]

    def bias(dist):
        onehot = (_rel_bucket(dist)[..., None] == jnp.arange(N_BUCKETS, dtype=jnp.int32)).astype(F32)
        return jnp.einsum('ckb,bc->ck', onehot, per_col, precision=lax.Precision.HIGHEST)

    lane = jnp.arange(TILE, dtype=jnp.int32)[None, :]
    far = bias(jnp.full((TILE, 1), MAX_DISTANCE, jnp.int32))
    blk = jnp.arange(n_lanes, dtype=jnp.int32)[None, :]
    c_dist = past_len + tok - ((blk + 1) * BLK - 1)
    ctab = jnp.where(c_dist >= 0, bias(c_dist), MASK_VALUE)
    last_tab = bias(TILE + tok - lane) - far
    new_tab = jnp.where((lane <= tok) & (lane < dec_seq), bias(tok - lane) - far, MASK_VALUE)
    win0_tab = jnp.where(lane > tok, 0.0, MASK_VALUE)
    same = (col[:, None] // (GQA * dec_seq) == col[None, :] // (GQA * dec_seq)) & \
           (col[:, None] % dec_seq == col[None, :] % dec_seq)
    return ctab, last_tab, new_tab, win0_tab, same.astype(BF16)


def _layer_weights(ffn_pre, ffn_post, ffn_w_in, ffn_w_out):
    f = ffn_w_out.shape[0]
    return (ffn_pre[None, :], ffn_post[None, :], ffn_w_in[:, :f].astype(BF16), ffn_w_in[:, f:].astype(BF16),
            ffn_w_out.astype(BF16))


def kernel(x_prompt, x_sample, cache_cmp_kv, cache_slc_kv, cache_win_kv, state_ssm, state_conv, page_table,
           rel_table, ffn1_pre, ffn1_post, ffn1_w_in, ffn1_w_out, mix_pre, mix_post, w_in, w_out, att_out_norm,
           cmp_pe, w_cmp, conv_w, conv_b, dt_bias, a_log, d_skip, ssd_norm,
           ffn2_pre, ffn2_post, ffn2_w_in, ffn2_w_out):
    depth = ffn1_pre.shape[0]
    bp, seq, d = x_prompt.shape
    bs, dec_seq, _ = x_sample.shape
    n_pages = page_table.shape[1]
    past_len = n_pages * PAGE_SIZE
    w_buf = cache_win_kv.shape[2]
    kv_shape = (N_KV_HEADS, 2, HEAD_DIM)
    assert GQA * dec_seq * N_KV_HEADS == TILE and w_buf == WINDOW and seq % TILE == 0

    yp = x_prompt.reshape(bp * seq, d)
    ys = x_sample.reshape(bs * dec_seq, d)
    outs = [[] for _ in range(10)]
    for l in range(depth):
        ffn1 = _layer_weights(ffn1_pre[l], ffn1_post[l], ffn1_w_in[l], ffn1_w_out[l])
        ffn2 = _layer_weights(ffn2_pre[l], ffn2_post[l], ffn2_w_in[l], ffn2_w_out[l])
        w_proj, w_proj_t = _pack_inproj(w_in[l])
        pe, bd = _pack_compress(cmp_pe[l], w_cmp[l])
        wo_att = w_out[l][:ATT_WIDTH].astype(BF16)
        wo_ssd = w_out[l][ATT_WIDTH:].astype(BF16)
        ssd_consts = (conv_w[l], conv_b[l][None, :], dt_bias[l][None, :], dt_bias[l][:, None], a_log[l][None, :],
                      a_log[l][:, None], jnp.repeat(d_skip[l], SSD_HEAD_DIM)[None, :], ssd_norm[l][None, :])
        mix_pre_l, mix_post_l, att_norm_l = mix_pre[l][None, :], mix_post[l][None, :], att_out_norm[l][None, :]

        yp = _ffn(yp, *ffn1)
        q, ckv, skv_b, wkv_b, gates, z, xbc, dt, dt_t, ckv_t, skv_t, wkv_t = _inproj(
            yp, mix_pre_l, w_proj, w_proj_t, bp, True)
        kvc = _compress(ckv.reshape(bp, seq, KV_WIDTH), pe, bd, TILE)
        tiles, cmp_tab = _prompt_bias_tables(rel_table)
        o_att = _nsa_prompt(q.reshape(bp, seq, ATT_WIDTH), gates.reshape(bp, seq, N_KV_HEADS * TILE), kvc,
                            skv_b.reshape(bp, seq, KV_WIDTH), wkv_b.reshape(bp, seq, KV_WIDTH), tiles, cmp_tab)
        cl = min(CHUNK, seq)
        o_ssd, h_new, conv_new = _ssd(
            z.reshape(bp, seq, SSD_INNER), xbc.reshape(bp, seq, CONV_DIM), dt.reshape(bp, seq, TILE),
            dt_t, jnp.zeros((bp, SUBLANES, CONV_DIM), F32),
            jnp.zeros((bp, SSD_HEADS, SSD_HEAD_DIM, SSD_STATE), F32), ssd_consts, cl)
        yp = _mixout_ffn(yp, o_att.reshape(bp * seq, ATT_WIDTH), o_ssd.reshape(bp * seq, SSD_INNER), att_norm_l,
                         mix_post_l, wo_att, wo_ssd, *ffn2)

        def token_major(a_t):
            return a_t.reshape((bp,) + kv_shape + (a_t.shape[-1],)).transpose(0, 4, 1, 2, 3)

        outs[0].append(token_major(ckv_t))
        outs[1].append(token_major(skv_t))
        outs[2].append(token_major(wkv_t[:, :, -min(WINDOW, seq):]))
        outs[3].append(h_new)
        outs[4].append(conv_new)

        ys = _ffn(ys, *ffn1)
        q, ckv, _, _, gates, z, xbc, dt, dt_t, skv, wkv = _inproj(ys, mix_pre_l, w_proj, w_proj_t, 1, False)
        native = (0, 2, 3, 4, 1)
        kc_past = _compress_paged(cache_cmp_kv[l].transpose(native), page_table, pe, bd)
        cmp_new = jnp.pad(ckv.reshape(bs, dec_seq, KV_WIDTH), ((0, 0), (0, BLK - dec_seq), (0, 0)))
        kc_new = _compress(cmp_new.reshape(1, bs * BLK, KV_WIDTH), pe, bd, bs).reshape(bs, 1, KV_WIDTH)
        n_past = kc_past.shape[1]
        n_lanes = -(-(n_past + 1) // TILE) * TILE
        kc = jnp.concatenate([kc_past, kc_new, jnp.zeros((bs, n_lanes - n_past - 1, KV_WIDTH), BF16)], axis=1)
        ctab, last_tab, new_tab, win0_tab, gsum = _sample_tables(rel_table, dec_seq, past_len, n_lanes)
        q_rows = q.reshape(bs, dec_seq, N_KV_HEADS, GQA, HEAD_DIM).transpose(0, 2, 3, 1, 4)
        own_head = jnp.eye(N_KV_HEADS, dtype=F32)[None, :, None, None, :, None]
        q_bd = q_rows[:, :, :, :, None, :] * own_head
        qk = q_bd.reshape(bs, TILE, N_KV_HEADS * HEAD_DIM).astype(BF16)
        g_rows = gates.reshape(bs, dec_seq, N_KV_HEADS, TILE)[..., :GQA * N_BRANCH]
        g_rows = g_rows.reshape(bs, dec_seq, N_KV_HEADS, GQA, N_BRANCH).transpose(0, 2, 3, 1, 4)
        gates_c = jnp.pad(g_rows.reshape(bs, TILE, N_BRANCH), ((0, 0), (0, 0), (0, TILE - N_BRANCH)))

        def new_t(a):
            a = a.reshape(bs, dec_seq, N_KV_HEADS, 2, HEAD_DIM).transpose(0, 3, 2, 4, 1)
            a = jnp.pad(a.reshape(bs, 2, N_KV_HEADS * HEAD_DIM, dec_seq), ((0, 0),) * 3 + ((0, TILE - dec_seq),))
            return a[:, 0].astype(BF16), a[:, 1].astype(BF16)

        k_new, v_new = new_t(skv)
        kw_new, vw_new = new_t(wkv)
        o_rows = _nsa_sample(page_table, cache_slc_kv[l].transpose(native), qk, gates_c, kc, ctab, gsum,
                             k_new, v_new, cache_win_kv[l].transpose(native), kw_new, vw_new, last_tab, new_tab,
                             win0_tab, dec_seq=dec_seq, past_len=past_len)
        o_att = o_rows.reshape(bs, N_KV_HEADS, GQA, dec_seq, HEAD_DIM).transpose(0, 3, 1, 2, 4)
        o_att = o_att.reshape(bs * dec_seq, ATT_WIDTH)
        conv_prev = jnp.pad(state_conv[l], ((0, 0), (SUBLANES - (CONV_W - 1), 0), (0, 0)))
        o_ssd, h_new, conv_new = _ssd(
            z.reshape(bs, dec_seq, SSD_INNER), xbc.reshape(bs, dec_seq, CONV_DIM), dt.reshape(bs, dec_seq, TILE),
            dt_t.reshape(TILE, bs, dec_seq).transpose(1, 0, 2), conv_prev, state_ssm[l], ssd_consts,
            min(CHUNK, dec_seq))
        ys = _mixout_ffn(ys, o_att, o_ssd.reshape(bs * dec_seq, SSD_INNER), att_norm_l, mix_post_l, wo_att,
                         wo_ssd, *ffn2)
        win_all = jnp.concatenate([cache_win_kv[l], wkv.reshape((bs, dec_seq) + kv_shape)], axis=1)
        outs[5].append(ckv.reshape((bs, dec_seq) + kv_shape))
        outs[6].append(skv.reshape((bs, dec_seq) + kv_shape))
        outs[7].append(win_all[:, -min(WINDOW, w_buf + dec_seq):])
        outs[8].append(h_new)
        outs[9].append(conv_new)

    return (yp.reshape(bp, seq, d), ys.reshape(bs, dec_seq, d)) + tuple(jnp.stack(o) for o in outs)
```

```python
import functools
import math

import jax
import jax.numpy as jnp
from jax import lax
from jax.experimental import pallas as pl
from jax.experimental.pallas import tpu as pltpu

F32 = jnp.float32
BF16 = jnp.bfloat16

D_MODEL = 1024
N_ATT_HEADS = 16
HEAD_DIM = 64
N_KV_HEADS = 4
GQA = N_ATT_HEADS // N_KV_HEADS
ATT_WIDTH = N_ATT_HEADS * HEAD_DIM
KV_WIDTH = 2 * N_KV_HEADS * HEAD_DIM
KV_GROUP = 2 * HEAD_DIM
BLK = 64
N_SEL = 16
WINDOW = 512
N_BRANCH = 3
N_BUCKETS = 32
MAX_DISTANCE = 128
SSD_HEADS = 16
SSD_HEAD_DIM = 64
SSD_INNER = SSD_HEADS * SSD_HEAD_DIM
SSD_GROUPS = 2
SSD_STATE = 128
CONV_W = 4
CONV_DIM = SSD_INNER + 2 * SSD_GROUPS * SSD_STATE
CHUNK = 128
PAGE_SIZE = 128
EPS = 1e-6

TILE = 128
SUBLANES = 8
MASK_VALUE = -(2.0 ** 100)
M_INIT = -1e30
LOG2E = math.log2(math.e)
VMEM_LIMIT = 56 * 1024 * 1024
PAGES_PER_STEP = 16
COMPRESS_ROWS_PER_DOT = 16
STAGE_PITCH = 72

NT_DIMS = (((1,), (1,)), ((), ()))
TN_DIMS = (((0,), (0,)), ((), ()))


def _params(n_grid_dims):
    return pltpu.CompilerParams(dimension_semantics=("arbitrary",) * n_grid_dims,
                                vmem_limit_bytes=VMEM_LIMIT)


def _rms(x, g):
    return x * lax.rsqrt(jnp.mean(x * x, axis=-1, keepdims=True) + EPS) * g


def _const_spec(shape):
    zeros = (0,) * len(shape)
    return pl.BlockSpec(shape, lambda *_: zeros, pipeline_mode=pl.Buffered(1))


def _ffn_body(x_ref, pre_ref, post_ref, wg_ref, wu_ref, wo_ref, o_ref):
    x = x_ref[...]
    h = _rms(x, pre_ref[...]).astype(BF16)
    gate = jnp.dot(h, wg_ref[...], preferred_element_type=F32)
    up = jnp.dot(h, wu_ref[...], preferred_element_type=F32)
    a = (gate * jax.nn.sigmoid(gate) * up).astype(BF16)
    y = jnp.dot(a, wo_ref[...], preferred_element_type=F32)
    o_ref[...] = x + 0.5 * _rms(y, post_ref[...])


def _ffn(x, pre, post, wg, wu, wo):
    t, d = x.shape
    f = wg.shape[1]
    tm = min(512, t)
    row = pl.BlockSpec((tm, d), lambda i: (i, 0))
    return pl.pallas_call(
        _ffn_body, grid=(t // tm,),
        in_specs=[row, _const_spec((1, d)), _const_spec((1, d)), _const_spec((d, f)), _const_spec((d, f)),
                  _const_spec((f, d))],
        out_specs=row, out_shape=jax.ShapeDtypeStruct((t, d), F32),
        compiler_params=_params(1), name="ffn")(x, pre, post, wg, wu, wo)


_SEG = {}
_off = 0
for _name, _w in (("q", ATT_WIDTH), ("ckv", KV_WIDTH), ("skv", KV_WIDTH), ("wkv", KV_WIDTH),
                  ("gates", N_KV_HEADS * TILE), ("z", SSD_INNER), ("xbc", CONV_DIM), ("dt", TILE)):
    _SEG[_name] = (_off, _off + _w)
    _off += _w
PROJ_WIDTH = _off


def _inproj_body(x_ref, pre_ref, w_ref, q_ref, ckv_ref, skvb_ref, wkvb_ref, gates_ref, z_ref, xbc_ref,
                 dt_ref, dtt_ref, *kv_refs, feature_major_kv):
    h = _rms(x_ref[...], pre_ref[...]).astype(BF16)

    def proj(name):
        lo, hi = _SEG[name]
        return jnp.dot(h, w_ref[:, lo:hi], preferred_element_type=F32)

    q_ref[...] = proj("q")
    ckv = proj("ckv")
    ckv_ref[...] = ckv
    skv = proj("skv")
    skvb_ref[...] = skv.astype(BF16)
    wkv = proj("wkv")
    wkvb_ref[...] = wkv.astype(BF16)
    gates_ref[...] = proj("gates")
    z_ref[...] = proj("z")
    xbc_ref[...] = proj("xbc")
    dt = proj("dt")
    dt_ref[...] = dt
    dtt_ref[...] = dt.T
    if feature_major_kv:
        for ref, rows in zip(kv_refs, (ckv, skv, wkv)):
            ref[...] = rows.T
    else:
        kv_refs[0][...] = skv
        kv_refs[1][...] = wkv


def _inproj(x, pre, w, n_seq, feature_major_kv):
    t, d = x.shape
    tm = min(256, t)
    seq = t // n_seq
    per_seq = seq // tm
    assert seq % tm == 0

    def cols(height):
        return pl.BlockSpec((None, height, tm), lambda i: (i // per_seq, 0, i % per_seq))

    def rows(width):
        return pl.BlockSpec((tm, width), lambda i: (i, 0))

    widths = (ATT_WIDTH, KV_WIDTH, KV_WIDTH, KV_WIDTH, N_KV_HEADS * TILE, SSD_INNER, CONV_DIM, TILE)
    dtypes = (F32, F32, BF16, BF16, F32, F32, F32, F32)
    out_shape = [jax.ShapeDtypeStruct((t, wd), dt) for wd, dt in zip(widths, dtypes)]
    out_specs = [rows(wd) for wd in widths]
    out_shape.append(jax.ShapeDtypeStruct((n_seq, TILE, seq), F32))
    out_specs.append(cols(TILE))
    if feature_major_kv:
        out_shape += [jax.ShapeDtypeStruct((n_seq, KV_WIDTH, seq), F32)] * 3
        out_specs += [cols(KV_WIDTH)] * 3
    else:
        out_shape += [jax.ShapeDtypeStruct((t, KV_WIDTH), F32)] * 2
        out_specs += [rows(KV_WIDTH)] * 2
    return pl.pallas_call(
        functools.partial(_inproj_body, feature_major_kv=feature_major_kv), grid=(t // tm,),
        in_specs=[rows(d), _const_spec((1, d)), _const_spec((d, PROJ_WIDTH))],
        out_specs=out_specs, out_shape=out_shape,
        compiler_params=_params(1), name="inproj")(x, pre, w)


def _mixout_body(x_ref, oa_ref, os_ref, an_ref, post_ref, wa_ref, ws_ref, o_ref):
    a = _rms(oa_ref[...], an_ref[...]).astype(BF16)
    y = jnp.dot(a, wa_ref[...], preferred_element_type=F32)
    y = y + jnp.dot(os_ref[...], ws_ref[...], preferred_element_type=F32)
    o_ref[...] = x_ref[...] + _rms(y, post_ref[...])


def _mixout_ffn_body(x_ref, oa_ref, os_ref, an_ref, mpost_ref, wa_ref, ws_ref, pre_ref, post_ref, wg_ref, wu_ref,
                     wo_ref, o_ref):
    _mixout_body(x_ref, oa_ref, os_ref, an_ref, mpost_ref, wa_ref, ws_ref, o_ref)
    _ffn_body(o_ref, pre_ref, post_ref, wg_ref, wu_ref, wo_ref, o_ref)


def _mixout_ffn(x, o_att, o_ssd, att_norm, mix_post, wa, ws, pre, post, wg, wu, wo):
    t, d = x.shape
    f = wg.shape[1]
    tm = min(256, t)
    row = pl.BlockSpec((tm, d), lambda i: (i, 0))
    vec = _const_spec((1, d))
    return pl.pallas_call(
        _mixout_ffn_body, grid=(t // tm,),
        in_specs=[row, row, row, vec, vec, _const_spec((d, d)), _const_spec((d, d)), vec, vec,
                  _const_spec((d, f)), _const_spec((d, f)), _const_spec((f, d))],
        out_specs=row, out_shape=jax.ShapeDtypeStruct((t, d), F32),
        compiler_params=_params(1), name="mixout_ffn")(x, o_att, o_ssd, att_norm, mix_post, wa, ws, pre, post, wg,
                                                       wu, wo)


def _token_rows(ref, k, lo=0, n=None):
    n = ref.shape[-1] if n is None else n
    return ref[k, :, :, lo:lo + n].reshape(KV_GROUP, n).T


def _compress_rows(x_refs, pe_ref, bd_ref, n_blocks, pitch=BLK):
    acc = None
    for l0 in range(0, BLK, COMPRESS_ROWS_PER_DOT):
        pieces = []
        for l in range(l0, l0 + COMPRESS_ROWS_PER_DOT):
            pe = pe_ref[l:l + 1, :]
            xs = jnp.concatenate([x_ref[pl.ds(l, n_blocks, stride=pitch), :] + pe for x_ref in x_refs], axis=0)
            pieces.append(xs.astype(BF16))
        w = bd_ref[l0:l0 + COMPRESS_ROWS_PER_DOT].reshape(COMPRESS_ROWS_PER_DOT * KV_GROUP, KV_GROUP)
        part = jnp.dot(jnp.concatenate(pieces, axis=1), w, preferred_element_type=F32)
        acc = part if acc is None else acc + part
    return acc


def _store_summaries(acc, o_ref, n_blocks):
    for k in range(N_KV_HEADS):
        o_ref[0:n_blocks, k * KV_GROUP:(k + 1) * KV_GROUP] = acc[k * n_blocks:(k + 1) * n_blocks].astype(BF16)


def _compress_body(*refs, n_blocks, n_out):
    x_refs = refs[:N_KV_HEADS]
    pe_ref, bd_ref, o_ref = refs[N_KV_HEADS:]
    _store_summaries(_compress_rows(x_refs, pe_ref, bd_ref, n_blocks), o_ref, n_blocks)
    if n_out > n_blocks:
        o_ref[n_blocks:n_out, :] = jnp.zeros((n_out - n_blocks, KV_WIDTH), BF16)


def _compress(kv, pe, bd, n_out):
    b, s, _ = kv.shape
    n_blocks = s // BLK

    def head_spec(k):
        return pl.BlockSpec((None, s, KV_GROUP), lambda i: (i, 0, k))

    return pl.pallas_call(
        functools.partial(_compress_body, n_blocks=n_blocks, n_out=n_out), grid=(b,),
        in_specs=[head_spec(k) for k in range(N_KV_HEADS)]
        + [_const_spec((BLK, KV_GROUP)), _const_spec((BLK, KV_GROUP, KV_GROUP))],
        out_specs=pl.BlockSpec((None, n_out, KV_WIDTH), lambda i: (i, 0, 0)),
        out_shape=jax.ShapeDtypeStruct((b, n_out, KV_WIDTH), BF16),
        compiler_params=_params(1), name="compress")(*([kv] * N_KV_HEADS), pe, bd)


def _compress_paged_body(pt_ref, *refs, n_pages):
    page_refs = refs[:PAGES_PER_STEP]
    pe_ref, bd_ref, o_ref, stage_ref = refs[PAGES_PER_STEP:]
    j = pl.program_id(1)
    blocks_per_page = PAGE_SIZE // BLK
    for p in range(PAGES_PER_STEP):
        first_block = (j * PAGES_PER_STEP + p) * blocks_per_page
        for k in range(N_KV_HEADS):
            rows = _token_rows(page_refs[p], k)
            for h in range(blocks_per_page):
                at = pl.multiple_of((first_block + h) * STAGE_PITCH, SUBLANES)
                stage_ref[k, pl.ds(at, BLK), :] = rows[h * BLK:(h + 1) * BLK]

    @pl.when(j == pl.num_programs(1) - 1)
    def _():
        n_blocks = n_pages * blocks_per_page
        x_refs = [stage_ref.at[k] for k in range(N_KV_HEADS)]
        _store_summaries(_compress_rows(x_refs, pe_ref, bd_ref, n_blocks, STAGE_PITCH), o_ref, n_blocks)


def _compress_paged(cache, page_table, pe, bd):
    b, n_pages = page_table.shape
    steps = n_pages // PAGES_PER_STEP
    n_blocks = n_pages * (PAGE_SIZE // BLK)

    def page_spec(p):
        return pl.BlockSpec((None, N_KV_HEADS, 2, HEAD_DIM, PAGE_SIZE),
                            lambda i, j, pt: (pt[i, j * PAGES_PER_STEP + p], 0, 0, 0, 0))

    grid_spec = pltpu.PrefetchScalarGridSpec(
        num_scalar_prefetch=1, grid=(b, steps),
        in_specs=[page_spec(p) for p in range(PAGES_PER_STEP)]
        + [pl.BlockSpec((BLK, KV_GROUP), lambda i, j, pt: (0, 0)),
           pl.BlockSpec((BLK, KV_GROUP, KV_GROUP), lambda i, j, pt: (0, 0, 0))],
        out_specs=pl.BlockSpec((None, n_blocks, KV_WIDTH), lambda i, j, pt: (i, 0, 0)),
        scratch_shapes=[pltpu.VMEM((N_KV_HEADS, n_blocks * STAGE_PITCH, KV_GROUP), F32)])
    return pl.pallas_call(
        functools.partial(_compress_paged_body, n_pages=n_pages), grid_spec=grid_spec,
        out_shape=jax.ShapeDtypeStruct((b, n_blocks, KV_WIDTH), BF16),
        compiler_params=_params(2), name="compress_paged")(page_table, *([cache] * PAGES_PER_STEP), pe, bd)


def _rel_bucket(dist):
    n = jnp.maximum(dist, 0)
    max_exact = N_BUCKETS // 2
    nf = jnp.maximum(n, 1).astype(F32)
    log_b = max_exact + (jnp.log(nf / max_exact) / math.log(MAX_DISTANCE / max_exact)
                         * (N_BUCKETS - max_exact)).astype(jnp.int32)
    return jnp.where(n < max_exact, n, jnp.minimum(log_b, N_BUCKETS - 1))


def _bias_of(rel_table, dist, visible):
    onehot = (_rel_bucket(dist)[..., None] == jnp.arange(N_BUCKETS, dtype=jnp.int32)).astype(F32)
    bias = jnp.einsum('rcb,bh->hrc', onehot, rel_table.astype(F32) * LOG2E, precision=lax.Precision.HIGHEST)
    return jnp.where(visible[None], bias, MASK_VALUE).reshape(-1, dist.shape[-1])


KIND_ZERO, KIND_DIAG, KIND_PREV, KIND_MASKED, KIND_OLDEST = range(5)


def _prompt_bias_tables(rel_table):
    r = jnp.arange(TILE, dtype=jnp.int32)[:, None]
    c = jnp.arange(TILE, dtype=jnp.int32)[None, :]
    true = jnp.ones((TILE, TILE), bool)
    far = _bias_of(rel_table, jnp.full((TILE, TILE), MAX_DISTANCE, jnp.int32), true)
    tiles = jnp.stack([
        jnp.zeros_like(far),
        _bias_of(rel_table, r - c, r >= c) - far,
        _bias_of(rel_table, TILE + r - c, true) - far,
        jnp.full_like(far, MASK_VALUE),
        jnp.where(jnp.tile(r < c, (N_ATT_HEADS, 1)), 0.0, MASK_VALUE),
    ])
    c_dist = r - (BLK - 1) + BLK * (BLK - 1 - c)
    cmp_tab = _bias_of(rel_table, c_dist, c_dist >= 0)
    return tiles, cmp_tab


def _selection_mask(score, n_rows, live=None):
    n_groups = score.shape[0] // SUBLANES
    groups = [score[SUBLANES * j:SUBLANES * (j + 1)] for j in range(n_groups)]
    row8 = lax.broadcasted_iota(jnp.int32, groups[0].shape, 0)

    def beaten_by(rows, ranks):
        for n in rows:
            other = jnp.broadcast_to(score[n:n + 1, :], groups[0].shape)
            for j in range(n_groups):
                lo = SUBLANES * j
                if lo > n:
                    inc = jnp.where(other >= groups[j], 1, 0)
                elif lo + SUBLANES - 1 <= n:
                    inc = jnp.where(other > groups[j], 1, 0)
                else:
                    inc = jnp.where(row8 + lo > n, jnp.where(other >= groups[j], 1, 0),
                                    jnp.where(other > groups[j], 1, 0))
                ranks[j] = ranks[j] + inc
        return jnp.concatenate(ranks, axis=0)

    zeros = [jnp.zeros(groups[0].shape, jnp.int32) for _ in range(n_groups)]
    if live is None:
        rank = beaten_by(range(n_rows), zeros)
    else:
        n_live, rank_ref = live
        rank_ref[...] = jnp.zeros(rank_ref.shape, jnp.int32)
        for lo in range(0, n_rows, SUBLANES):
            @pl.when(lo < n_live)
            def _():
                rank_ref[...] += beaten_by(range(lo, min(lo + SUBLANES, n_rows)), list(zeros))
        rank = rank_ref[...]
    return jnp.where(rank < N_SEL, jnp.where(score > -jnp.inf, 0.0, MASK_VALUE), MASK_VALUE)


PAIR = 2 * TILE
Q_TILES = PAIR // TILE
SEL_UNROLL = 4
_SEL_KINDS = {0: ((KIND_DIAG, KIND_MASKED), (KIND_PREV, KIND_DIAG)),
              1: ((KIND_ZERO, KIND_PREV), (KIND_ZERO, KIND_ZERO))}
_WIN_OLDEST = ((KIND_OLDEST, KIND_ZERO), (KIND_MASKED, KIND_OLDEST))


def _nsa_prompt_body(q_ref, gates_ref, kvc_ref, skv_ref, wkv_ref, tb_ref, ctab_ref, o_ref,
                     s_ref, sw_ref, mx_ref, mb_ref, acc_ref, rank_ref, part_ref, *, n_blocks):
    step = pl.program_id(2)
    head_rows = GQA * TILE
    rows = Q_TILES * head_rows
    lane = lax.broadcasted_iota(jnp.int32, (rows, TILE), 1)
    low = lane < HEAD_DIM
    key_lane = lax.broadcasted_iota(jnp.int32, (PAIR, TILE), 1)
    key_blk = lax.broadcasted_iota(jnp.int32, (PAIR, TILE), 0) // BLK
    n_pairs = skv_ref.shape[0] // PAIR

    qf = q_ref[...]
    halves = [qf[:, c * TILE:(c + 1) * TILE] for c in range(GQA // 2)]
    rolled = [pltpu.roll(h, HEAD_DIM, 1) for h in halves]
    q_all = jnp.concatenate([(halves if g % 2 == 0 else rolled)[g // 2][t * TILE:(t + 1) * TILE]
                             for t in range(Q_TILES) for g in range(GQA)], axis=0)
    lhs_plain = jnp.where(low, q_all, 0.0).astype(BF16)

    def ones_and_values(kv):
        kv_lane = lax.broadcasted_iota(jnp.int32, kv.shape, 1)
        return jnp.where(kv_lane < HEAD_DIM, jnp.ones_like(kv), kv)

    def normalise(acc):
        return acc / jnp.maximum(pltpu.roll(acc, HEAD_DIM, 1), 1e-30)

    def bias_of(kinds_per_tile, valid):
        parts = []
        for left, right in kinds_per_tile:
            left = jnp.where(valid, left, KIND_MASKED)
            right = jnp.where(valid, right, KIND_MASKED)
            parts.append(jnp.concatenate([tb_ref[left], tb_ref[right]], axis=1))
        return jnp.concatenate(parts, axis=0)

    def sel_kinds(j):
        behind = step - j
        return [tuple(jnp.where(behind == 0, _SEL_KINDS[0][t][side],
                                jnp.where(behind == 1, _SEL_KINDS[1][t][side], KIND_ZERO)) for side in range(2))
                for t in range(Q_TILES)]

    def keys_of(kv_ref, j):
        jc = jnp.clip(j, 0, n_pairs - 1)
        return kv_ref[pl.ds(pl.multiple_of(jc * PAIR, PAIR), PAIR), :]

    def score_pair(lhs, kv_ref, j, bias, block_columns):
        rhs = keys_of(kv_ref, j)
        if block_columns:
            onehot = jnp.where(key_lane - HEAD_DIM == (PAIR // BLK) * j + key_blk, 1.0, 0.0).astype(BF16)
            rhs = jnp.where(key_lane < HEAD_DIM, rhs, onehot)
        return lax.dot_general(lhs, rhs, NT_DIMS, preferred_element_type=F32) + bias

    def half_max(sc):
        return jnp.maximum(sc[:, :TILE], sc[:, TILE:])

    def weigh_pair(sc, mb, kv_ref, j):
        pr = jnp.exp2(sc - jnp.concatenate([mb, mb], axis=1))
        return jnp.dot(pr.astype(BF16), ones_and_values(keys_of(kv_ref, j)), preferred_element_type=F32)

    def row_max(mx):
        return jnp.broadcast_to(jnp.max(mx, axis=1, keepdims=True), (rows, TILE))

    n_win = WINDOW // PAIR + 1
    w_pairs = [step - (n_win - 1) + w for w in range(n_win)]
    w_kinds = [_WIN_OLDEST] + [[(KIND_ZERO, KIND_ZERO)] * Q_TILES] * (n_win - 3) + [_SEL_KINDS[1], _SEL_KINDS[0]]
    w_mx = None
    for w, j in enumerate(w_pairs):
        sc = score_pair(lhs_plain, wkv_ref, j, bias_of(w_kinds[w], j >= 0), False)
        sw_ref[w] = sc
        w_mx = half_max(sc) if w_mx is None else jnp.maximum(w_mx, half_max(sc))

    kvc = kvc_ref[...]
    bias = []
    for t in range(Q_TILES):
        i = Q_TILES * step + t
        shift = (2 * i + (TILE - (BLK - 1))) % TILE
        bias += [pltpu.roll(ctab_ref[g * TILE:(g + 1) * TILE, :], shift, 1) for g in range(GQA)]
    s = lax.dot_general(lhs_plain, kvc, NT_DIMS, preferred_element_type=F32)
    s = jnp.where(lane < n_blocks, s + jnp.concatenate(bias, axis=0), MASK_VALUE)
    m = jnp.maximum(jnp.max(s, axis=1, keepdims=True), M_INIT)
    p = jnp.exp2(s - m)
    c_acc = jnp.dot(p.astype(BF16), ones_and_values(kvc), preferred_element_type=F32)
    p = p / jnp.maximum(c_acc, 1e-30)
    o_cmp = normalise(c_acc)

    w_mb = row_max(jnp.maximum(w_mx, M_INIT))
    w_acc = None
    for w, j in enumerate(w_pairs):
        dacc = weigh_pair(sw_ref[w], w_mb, wkv_ref, j)
        w_acc = dacc if w_acc is None else w_acc + dacc
    o_win = normalise(w_acc)

    gate = jax.nn.sigmoid(gates_ref[...])

    def gate_col(t, g, br):
        c = g * N_BRANCH + br
        return gate[t * TILE:(t + 1) * TILE, c:c + 1]

    heads = [(t, g, slice(t * head_rows + g * TILE, t * head_rows + (g + 1) * TILE))
             for t in range(Q_TILES) for g in range(GQA)]
    for t, g, head in heads:
        part_ref[head, :] = gate_col(t, g, 0) * o_cmp[head] + gate_col(t, g, 2) * o_win[head]

    blk = lax.broadcasted_iota(jnp.int32, (n_blocks, PAIR), 0)
    query = lax.broadcasted_iota(jnp.int32, (n_blocks, PAIR), 1)
    cur = (PAIR // BLK) * step + query // BLK
    imp_t = []
    for t in range(Q_TILES):
        imp = p[t * head_rows:t * head_rows + TILE]
        for g in range(1, GQA):
            imp = imp + p[t * head_rows + g * TILE:t * head_rows + (g + 1) * TILE]
        imp_t.append(imp.T[0:n_blocks])
    forced = (blk == 0) | (blk == cur) | (blk == cur - 1)
    score = jnp.where(forced, jnp.inf, jnp.where(blk <= cur, jnp.concatenate(imp_t, axis=1), -jnp.inf))
    neg_t = _selection_mask(score, n_blocks, ((PAIR // BLK) * (step + 1), rank_ref))
    neg = []
    for t in range(Q_TILES):
        pieces = [jnp.zeros((HEAD_DIM, TILE), F32), neg_t[:, t * TILE:(t + 1) * TILE]]
        if n_blocks < TILE - HEAD_DIM:
            pieces.append(jnp.zeros((TILE - HEAD_DIM - n_blocks, TILE), F32))
        neg += [jnp.concatenate(pieces, axis=0).T] * GQA
    lhs_sel = jnp.where(low, q_all, jnp.concatenate(neg, axis=0)).astype(BF16)

    n_trips = (step + SEL_UNROLL) // SEL_UNROLL
    mx_ref[...] = jnp.full(mx_ref.shape, M_INIT, F32)

    def score_trip(t, carry):
        mx = None
        for u in range(SEL_UNROLL):
            j = t * SEL_UNROLL + u
            sc = score_pair(lhs_sel, skv_ref, j, bias_of(sel_kinds(j), j <= step), True)
            s_ref[j] = sc
            mx = half_max(sc) if mx is None else jnp.maximum(mx, half_max(sc))
        mx_ref[...] = jnp.maximum(mx_ref[...], mx)
        return carry

    lax.fori_loop(0, n_trips, score_trip, 0)
    mb_ref[...] = row_max(mx_ref[...])
    acc_ref[...] = jnp.zeros(acc_ref.shape, F32)

    def weigh_trip(t, carry):
        mb = mb_ref[...]
        acc = None
        for u in range(SEL_UNROLL):
            j = t * SEL_UNROLL + u
            dacc = weigh_pair(s_ref[j], mb, skv_ref, j)
            acc = dacc if acc is None else acc + dacc
        acc_ref[...] += acc
        return carry

    lax.fori_loop(0, n_trips, weigh_trip, 0)
    o_sel = normalise(acc_ref[...])

    out_low = lax.broadcasted_iota(jnp.int32, (TILE, TILE), 1) < HEAD_DIM
    outs = {(t, g): part_ref[head, :] + gate_col(t, g, 1) * o_sel[head] for t, g, head in heads}
    for t in range(Q_TILES):
        for c in range(GQA // 2):
            o_ref[t * TILE:(t + 1) * TILE, c * TILE:(c + 1) * TILE] = jnp.where(
                out_low, pltpu.roll(outs[t, 2 * c], HEAD_DIM, 1), outs[t, 2 * c + 1])


def _nsa_prompt(q, gates, kvc, skv_b, wkv_b, tiles, cmp_tab):
    b, s, _ = q.shape
    n_blocks = s // BLK
    assert n_blocks <= TILE - HEAD_DIM and kvc.shape[1] == TILE and s % PAIR == 0 and WINDOW // PAIR >= 2
    gw = GQA * HEAD_DIM
    head_rows = GQA * TILE
    rows = Q_TILES * head_rows
    return pl.pallas_call(
        functools.partial(_nsa_prompt_body, n_blocks=n_blocks), grid=(b, N_KV_HEADS, s // PAIR),
        in_specs=[pl.BlockSpec((None, PAIR, gw), lambda bi, k, i: (bi, i, k)),
                  pl.BlockSpec((None, PAIR, TILE), lambda bi, k, i: (bi, i, k)),
                  pl.BlockSpec((None, TILE, KV_GROUP), lambda bi, k, i: (bi, 0, k)),
                  pl.BlockSpec((None, s, KV_GROUP), lambda bi, k, i: (bi, 0, k)),
                  pl.BlockSpec((None, s, KV_GROUP), lambda bi, k, i: (bi, 0, k)),
                  pl.BlockSpec((len(tiles), head_rows, TILE), lambda bi, k, i: (0, k, 0)),
                  pl.BlockSpec((head_rows, TILE), lambda bi, k, i: (k, 0))],
        out_specs=pl.BlockSpec((None, PAIR, gw), lambda bi, k, i: (bi, i, k)),
        out_shape=jax.ShapeDtypeStruct((b, s, ATT_WIDTH), F32),
        scratch_shapes=[pltpu.VMEM((s // PAIR + SEL_UNROLL - 1, rows, PAIR), F32),
                        pltpu.VMEM((WINDOW // PAIR + 1, rows, PAIR), F32),
                        pltpu.VMEM((rows, TILE), F32), pltpu.VMEM((rows, TILE), F32),
                        pltpu.VMEM((rows, KV_GROUP), F32), pltpu.VMEM((n_blocks, PAIR), jnp.int32),
                        pltpu.VMEM((rows, TILE), F32)],
        compiler_params=_params(3), name="nsa_prompt")(q, gates, kvc, skv_b, wkv_b, tiles, cmp_tab)


def _nsa_sample_body(pt_ref, *refs, n_pages, dec_seq, past_len):
    page_refs = refs[:PAGES_PER_STEP]
    (qk_ref, gates_ref, kc_ref, ctab_ref, gsum_ref, knew_ref, vnew_ref, cwin_ref, kwnew_ref, vwnew_ref,
     last_ref, new_ref, win0_ref, o_ref, lhs_ref, m_ref, l_ref, acc_ref, ow_ref, oc_ref) = refs[PAGES_PER_STEP:]
    j = pl.program_id(1)
    last_step = pl.num_programs(1) - 1
    blocks_per_page = PAGE_SIZE // BLK
    n_past = n_pages * blocks_per_page
    kd = N_KV_HEADS * HEAD_DIM
    n_lanes = kc_ref.shape[0]
    step_keys = PAGES_PER_STEP * PAGE_SIZE

    def gate_col(br):
        return jax.nn.sigmoid(gates_ref[:, br:br + 1])

    def head_major(ref, c, lo=0, n=None):
        n = ref.shape[-1] if n is None else n
        return ref[:, c, :, lo:lo + n].reshape(kd, n).astype(BF16)

    def lane_tiles(x):
        return [x[:, t * TILE:(t + 1) * TILE] for t in range(x.shape[1] // TILE)]

    def tile_max(tiles):
        m = tiles[0]
        for t in tiles[1:]:
            m = jnp.maximum(m, t)
        return jnp.broadcast_to(jnp.max(m, axis=1, keepdims=True), (TILE, TILE))

    def tile_sum(tiles):
        s = tiles[0]
        for t in tiles[1:]:
            s = s + t
        return s

    def online_update(s, values_t):
        tiles = lane_tiles(s)
        m_prev = m_ref[...]
        m_new = jnp.maximum(m_prev, tile_max(tiles))
        alpha = jnp.exp2(m_prev - m_new)
        p = [jnp.exp2(t - m_new) for t in tiles]
        l_ref[...] = alpha * l_ref[...] + tile_sum(p)
        pv = lax.dot_general(jnp.concatenate(p, axis=1).astype(BF16), values_t, NT_DIMS, preferred_element_type=F32)
        acc_ref[...] = jnp.concatenate([alpha] * (kd // TILE), axis=1) * acc_ref[...] + pv
        m_ref[...] = m_new

    @pl.when(j == 0)
    def _():
        kc = kc_ref[...]
        qf = qk_ref[...].astype(F32)
        gap = jnp.zeros((TILE, HEAD_DIM), F32)
        qc = jnp.concatenate([piece for k in range(N_KV_HEADS)
                              for piece in (qf[:, k * HEAD_DIM:(k + 1) * HEAD_DIM], gap)], axis=1).astype(BF16)
        s = lax.dot_general(qc, kc, NT_DIMS, preferred_element_type=F32) + ctab_ref[...]
        m = jnp.maximum(jnp.max(s, axis=1, keepdims=True), M_INIT)
        p = jnp.exp2(s - m)
        p = p / jnp.maximum(jnp.sum(p, axis=1, keepdims=True), 1e-30)
        oc_ref[...] = gate_col(0) * jnp.dot(p.astype(BF16), kc, preferred_element_type=F32)
        imp = _select_sum(p, gsum_ref[...], False)
        imp_t = jnp.concatenate([t.T for t in lane_tiles(imp)], axis=0)
        n_rows = -(-(n_past + 1) // SUBLANES) * SUBLANES
        blk = lax.broadcasted_iota(jnp.int32, (n_rows, TILE), 0)
        tok = lax.broadcasted_iota(jnp.int32, (n_rows, TILE), 1) % dec_seq
        cur = (past_len + tok) // BLK
        forced = (blk == 0) | (blk == cur) | (blk == cur - 1)
        score = jnp.where(forced, jnp.inf, jnp.where(blk <= cur, imp_t[0:n_rows], -jnp.inf))
        neg_t = jnp.concatenate([_selection_mask(score, n_past + 1), jnp.zeros((n_lanes - n_rows, TILE), F32)], axis=0)
        neg = jnp.concatenate([neg_t[t * TILE:(t + 1) * TILE].T for t in range(n_lanes // TILE)], axis=1)
        lhs_ref[:, 0:kd] = qk_ref[...]
        lhs_ref[:, kd:kd + n_lanes] = neg.astype(BF16)

        qk = qk_ref[...]
        n_win = cwin_ref.shape[-1] // TILE
        zero = jnp.zeros((TILE, TILE), F32)
        s_w = jnp.dot(qk, head_major(cwin_ref, 0), preferred_element_type=F32)
        s_w = s_w + jnp.concatenate([win0_ref[...]] + [zero] * (n_win - 2) + [last_ref[...]], axis=1)
        s_n = jnp.dot(qk, kwnew_ref[...], preferred_element_type=F32) + new_ref[...]
        tiles = lane_tiles(s_w) + [s_n]
        m = tile_max(tiles)
        p = [jnp.exp2(t - m) for t in tiles]
        l = jnp.sum(tile_sum(p), axis=1, keepdims=True)
        o_w = lax.dot_general(jnp.concatenate(p[:-1], axis=1).astype(BF16), head_major(cwin_ref, 1), NT_DIMS,
                              preferred_element_type=F32)
        o_w = o_w + lax.dot_general(p[-1].astype(BF16), vwnew_ref[...], NT_DIMS, preferred_element_type=F32)
        ow_ref[...] = gate_col(2) * (o_w / l)

        m_ref[...] = jnp.full(m_ref.shape, M_INIT, F32)
        l_ref[...] = jnp.zeros(l_ref.shape, F32)
        acc_ref[...] = jnp.zeros(acc_ref.shape, F32)

    keys_t = jnp.concatenate([head_major(r, 0) for r in page_refs], axis=1)
    values_t = jnp.concatenate([head_major(r, 1) for r in page_refs], axis=1)
    blk_row = lax.broadcasted_iota(jnp.int32, (n_lanes, step_keys), 0)
    key_blk = j * (PAGES_PER_STEP * blocks_per_page) + lax.broadcasted_iota(jnp.int32, (n_lanes, step_keys), 1) // BLK
    onehot = jnp.where(blk_row == key_blk, 1.0, 0.0).astype(BF16)
    s = jnp.dot(lhs_ref[...], jnp.concatenate([keys_t, onehot], axis=0), preferred_element_type=F32)
    tiles = lane_tiles(s)
    tiles[-1] = tiles[-1] + jnp.where(j == last_step, last_ref[...], 0.0)
    online_update(jnp.concatenate(tiles, axis=1), values_t)

    @pl.when(j == last_step)
    def _():
        new_blk = lax.broadcasted_iota(jnp.int32, (n_lanes, TILE), 0) == n_past
        rhs = jnp.concatenate([knew_ref[...], jnp.where(new_blk, 1.0, 0.0).astype(BF16)], axis=0)
        online_update(jnp.dot(lhs_ref[...], rhs, preferred_element_type=F32) + new_ref[...], vnew_ref[...])
        l = jnp.sum(l_ref[...], axis=1, keepdims=True)
        o_sw = gate_col(1) * (acc_ref[...] / l) + ow_ref[...]
        o_c = oc_ref[...]
        rows_per_head = TILE // N_KV_HEADS
        for k in range(N_KV_HEADS):
            rows = slice(k * rows_per_head, (k + 1) * rows_per_head)
            v_lo = k * KV_GROUP + HEAD_DIM
            o_ref[rows, :] = o_sw[rows, k * HEAD_DIM:(k + 1) * HEAD_DIM] + o_c[rows, v_lo:v_lo + HEAD_DIM]


def _nsa_sample(page_table, cache_slc, qk, gates_c, kc, ctab, gsum, k_new, v_new, cache_win, kw_new, vw_new,
                last_tab, new_tab, win0_tab, *, dec_seq, past_len):
    b, n_pages = page_table.shape
    steps = n_pages // PAGES_PER_STEP
    n_lanes = kc.shape[1]
    w_buf = cache_win.shape[-1]
    kd = N_KV_HEADS * HEAD_DIM

    def page_spec(p):
        return pl.BlockSpec((None, N_KV_HEADS, 2, HEAD_DIM, PAGE_SIZE),
                            lambda i, j, pt: (pt[i, j * PAGES_PER_STEP + p], 0, 0, 0, 0))

    def per_batch(*shape):
        zeros = (0,) * len(shape)
        return pl.BlockSpec((None,) + shape, lambda i, j, pt: (i,) + zeros)

    def const(*shape):
        zeros = (0,) * len(shape)
        return pl.BlockSpec(shape, lambda i, j, pt: zeros)

    grid_spec = pltpu.PrefetchScalarGridSpec(
        num_scalar_prefetch=1, grid=(b, steps),
        in_specs=[page_spec(p) for p in range(PAGES_PER_STEP)]
        + [per_batch(TILE, kd), per_batch(TILE, TILE), per_batch(n_lanes, KV_WIDTH),
           const(TILE, n_lanes), const(TILE, TILE), per_batch(kd, TILE), per_batch(kd, TILE),
           per_batch(N_KV_HEADS, 2, HEAD_DIM, w_buf), per_batch(kd, TILE), per_batch(kd, TILE),
           const(TILE, TILE), const(TILE, TILE), const(TILE, TILE)],
        out_specs=per_batch(TILE, HEAD_DIM),
        scratch_shapes=[pltpu.VMEM((TILE, kd + n_lanes), BF16), pltpu.VMEM((TILE, TILE), F32),
                        pltpu.VMEM((TILE, TILE), F32), pltpu.VMEM((TILE, kd), F32), pltpu.VMEM((TILE, kd), F32),
                        pltpu.VMEM((TILE, KV_WIDTH), F32)])
    return pl.pallas_call(
        functools.partial(_nsa_sample_body, n_pages=n_pages, dec_seq=dec_seq, past_len=past_len),
        grid_spec=grid_spec, out_shape=jax.ShapeDtypeStruct((b, TILE, HEAD_DIM), F32),
        compiler_params=_params(2), name="nsa_sample")(
            page_table, *([cache_slc] * PAGES_PER_STEP), qk, gates_c, kc, ctab, gsum, k_new, v_new, cache_win,
            kw_new, vw_new, last_tab, new_tab, win0_tab)


def _split3(x):
    hi = x.astype(BF16)
    r = x - hi.astype(F32)
    mid = r.astype(BF16)
    return hi, mid, (r - mid.astype(F32)).astype(BF16)


def _select_sum(x, onehot, x_on_left):
    out = None
    for part in _split3(x):
        term = (jnp.dot(part, onehot, preferred_element_type=F32) if x_on_left
                else jnp.dot(onehot, part, preferred_element_type=F32))
        out = term if out is None else out + term
    return out


def _softplus(x):
    return jnp.maximum(x, 0.0) + jnp.log(1.0 + jnp.exp(-jnp.abs(x)))


def _ssd_body(z_ref, xbc_ref, dt_ref, dtt_ref, convp_ref, h0_ref, cw_ref, cb_ref, dtb_ref, dtbt_ref, al_ref,
              alt_ref, dskip_ref, norm_ref, o_ref, hout_ref, convout_ref, xp_ref, *, cl):
    c = pl.program_id(1)
    pad = SUBLANES

    @pl.when(c == 0)
    def _():
        xp_ref[0:pad, :] = convp_ref[...]
        hout_ref[...] = h0_ref[...]

    xp_ref[pad:pad + cl, :] = xbc_ref[...]
    conv = cb_ref[...]
    for k in range(CONV_W):
        lo = pad - (CONV_W - 1) + k
        conv = conv + cw_ref[k:k + 1, :] * xp_ref[lo:lo + cl, :]
    u = conv * jax.nn.sigmoid(conv)
    tail = xp_ref[pad + cl - (CONV_W - 1):pad + cl, :]
    convout_ref[...] = tail
    xp_ref[pad - (CONV_W - 1):pad, :] = tail

    xs = u[:, :SSD_INNER]
    dt = _softplus(dt_ref[:, 0:SSD_HEADS] + dtb_ref[...])
    dt_t = _softplus(dtt_ref[0:SSD_HEADS, :] + dtbt_ref[...])
    la = dt * -jnp.exp(al_ref[...])
    la_t = dt_t * -jnp.exp(alt_ref[...])
    row = lax.broadcasted_iota(jnp.int32, (cl, cl), 0)
    col = lax.broadcasted_iota(jnp.int32, (cl, cl), 1)
    causal = row >= col
    cs = _select_sum(la, jnp.where(causal, 1.0, 0.0).astype(BF16), False)
    cs_t = _select_sum(la_t, jnp.where(row <= col, 1.0, 0.0).astype(BF16), True)

    def spread(width):
        lane_head = lax.broadcasted_iota(jnp.int32, (SSD_HEADS, SSD_HEADS * width), 1) // width
        return jnp.where(lane_head == lax.broadcasted_iota(jnp.int32, lane_head.shape, 0), 1.0, 0.0).astype(BF16)

    dt_x = _select_sum(dt, spread(SSD_HEAD_DIM), True)
    cs_x = _select_sum(cs, spread(SSD_HEAD_DIM), True)
    cs_wide = _select_sum(cs, spread(TILE), True)
    last_x = cs_x[cl - 1:cl, :]
    xdt = xs * dt_x
    xdt_b = xdt.astype(BF16)
    to_end = (xdt * jnp.exp(last_x - cs_x)).astype(BF16)
    grow = jnp.exp(cs_x)
    skip = dskip_ref[...] * xs

    heads_per_group = SSD_HEADS // SSD_GROUPS
    bms, cms, cbs = [], [], []
    for g in range(SSD_GROUPS):
        b_lo = SSD_INNER + g * SSD_STATE
        c_lo = SSD_INNER + SSD_GROUPS * SSD_STATE + g * SSD_STATE
        bms.append(u[:, b_lo:b_lo + SSD_STATE].astype(BF16))
        cms.append(u[:, c_lo:c_lo + SSD_STATE].astype(BF16))
        cbs.append(lax.dot_general(cms[g], bms[g], NT_DIMS, preferred_element_type=F32))
    first_head = lax.broadcasted_iota(jnp.int32, (cl, TILE), 1) < SSD_HEAD_DIM
    state_row = lax.broadcasted_iota(jnp.int32, (2 * SSD_HEAD_DIM, SSD_STATE), 0)
    ys = []
    for pair in range(SSD_HEADS // 2):
        g = 2 * pair // heads_per_group
        lanes = slice(pair * TILE, (pair + 1) * TILE)
        y_heads, keep = [], []
        for h in (2 * pair, 2 * pair + 1):
            diff = cs_wide[:, h * TILE:h * TILE + cl] - cs_t[h:h + 1, :]
            decay = jnp.where(causal, jnp.exp(jnp.where(causal, diff, 0.0)), 0.0)
            y_heads.append(jnp.dot((cbs[g] * decay).astype(BF16), xdt_b[:, lanes], preferred_element_type=F32))
            keep.append(jnp.exp(cs[cl - 1:cl, h:h + 1]))
        state_in = lax.dot_general(to_end[:, lanes], bms[g], TN_DIMS, preferred_element_type=F32)
        h_prev = jnp.concatenate([hout_ref[2 * pair], hout_ref[2 * pair + 1]], axis=0)
        y_off = lax.dot_general(cms[g], h_prev.astype(BF16), NT_DIMS, preferred_element_type=F32)
        h_new = h_prev * jnp.where(state_row < SSD_HEAD_DIM, keep[0], keep[1]) + state_in
        hout_ref[2 * pair] = h_new[0:SSD_HEAD_DIM]
        hout_ref[2 * pair + 1] = h_new[SSD_HEAD_DIM:]
        ys.append(jnp.where(first_head, y_heads[0], y_heads[1]) + y_off * grow[:, lanes] + skip[:, lanes])
    y = jnp.concatenate(ys, axis=1)
    z = z_ref[...]
    y = y * (z * jax.nn.sigmoid(z))
    gw = SSD_INNER // SSD_GROUPS
    normed = []
    for g in range(SSD_GROUPS):
        yg = y[:, g * gw:(g + 1) * gw]
        normed.append(yg * lax.rsqrt(jnp.mean(yg * yg, axis=-1, keepdims=True) + EPS))
    o_ref[...] = (jnp.concatenate(normed, axis=1) * norm_ref[...]).astype(o_ref.dtype)


def _ssd(z, xbc, dt, dt_t, conv_prev, h0, consts, cl):
    b, seq, _ = z.shape
    nc = seq // cl
    cw, cbias, dtb, dtb_t, al, al_t, dskip, norm = consts

    def rows(width):
        return pl.BlockSpec((None, cl, width), lambda i, c: (i, c, 0))

    state = pl.BlockSpec((None, SSD_HEADS, SSD_HEAD_DIM, SSD_STATE), lambda i, c: (i, 0, 0, 0))

    def const(x):
        zeros = (0,) * x.ndim
        return pl.BlockSpec(x.shape, lambda i, c: zeros)

    return pl.pallas_call(
        functools.partial(_ssd_body, cl=cl), grid=(b, nc),
        in_specs=[rows(SSD_INNER), rows(CONV_DIM), rows(TILE),
                  pl.BlockSpec((None, TILE, cl), lambda i, c: (i, 0, c)),
                  pl.BlockSpec((None, SUBLANES, CONV_DIM), lambda i, c: (i, 0, 0)), state,
                  const(cw), const(cbias), const(dtb), const(dtb_t), const(al), const(al_t), const(dskip),
                  const(norm)],
        out_specs=[rows(SSD_INNER), state, pl.BlockSpec((None, CONV_W - 1, CONV_DIM), lambda i, c: (i, 0, 0))],
        out_shape=[jax.ShapeDtypeStruct((b, seq, SSD_INNER), BF16),
                   jax.ShapeDtypeStruct((b, SSD_HEADS, SSD_HEAD_DIM, SSD_STATE), F32),
                   jax.ShapeDtypeStruct((b, CONV_W - 1, CONV_DIM), F32)],
        scratch_shapes=[pltpu.VMEM((SUBLANES + cl, CONV_DIM), F32)],
        compiler_params=_params(2), name="ssd")(z, xbc, dt, dt_t, conv_prev, h0, cw, cbias, dtb, dtb_t, al, al_t,
                                                 dskip, norm)


def _pack_inproj(w_in):
    splits = (ATT_WIDTH, KV_WIDTH, KV_WIDTH, KV_WIDTH, N_BRANCH * N_ATT_HEADS, SSD_INNER, CONV_DIM, SSD_HEADS)
    offs = [0]
    for s in splits:
        offs.append(offs[-1] + s)
    wq, wc, ws, ww, wg, wz, wx, wd = [w_in[:, offs[i]:offs[i + 1]] for i in range(len(splits))]
    d = w_in.shape[0]
    per_kv = GQA * N_BRANCH
    wg4 = jnp.pad(wg.reshape(d, N_KV_HEADS, per_kv), ((0, 0), (0, 0), (0, TILE - per_kv)))
    wd_pad = jnp.pad(wd, ((0, 0), (0, TILE - SSD_HEADS)))
    w = jnp.concatenate([wq * (HEAD_DIM ** -0.5 * LOG2E), wc, ws, ww, wg4.reshape(d, N_KV_HEADS * TILE), wz, wx,
                         wd_pad], axis=1)
    assert w.shape[1] == PROJ_WIDTH
    return w.astype(BF16)


def _pack_compress(cmp_pe, w_cmp):
    zero = jnp.zeros((BLK, HEAD_DIM, HEAD_DIM), w_cmp.dtype)
    bd = jnp.concatenate([jnp.concatenate([w_cmp[:, 0], zero], axis=2),
                          jnp.concatenate([zero, w_cmp[:, 1]], axis=2)], axis=1)
    pe = cmp_pe.reshape(BLK, KV_GROUP)
    return pe.astype(F32), bd.astype(BF16)


def _sample_tables(rel_table, dec_seq, past_len, n_lanes):
    col = jnp.arange(TILE, dtype=jnp.int32)
    head = (col // (GQA * dec_seq)) * GQA + (col // dec_seq) % GQA
    tok = (col % dec_seq)[:, None]
    per_col = (rel_table.astype(F32) * LOG2E)[:, head]

    def bias(dist):
        onehot = (_rel_bucket(dist)[..., None] == jnp.arange(N_BUCKETS, dtype=jnp.int32)).astype(F32)
        return jnp.einsum('ckb,bc->ck', onehot, per_col, precision=lax.Precision.HIGHEST)

    lane = jnp.arange(TILE, dtype=jnp.int32)[None, :]
    far = bias(jnp.full((TILE, 1), MAX_DISTANCE, jnp.int32))
    blk = jnp.arange(n_lanes, dtype=jnp.int32)[None, :]
    c_dist = past_len + tok - ((blk + 1) * BLK - 1)
    ctab = jnp.where(c_dist >= 0, bias(c_dist), MASK_VALUE)
    last_tab = bias(TILE + tok - lane) - far
    new_tab = jnp.where((lane <= tok) & (lane < dec_seq), bias(tok - lane) - far, MASK_VALUE)
    win0_tab = jnp.where(lane > tok, 0.0, MASK_VALUE)
    same = (col[:, None] // (GQA * dec_seq) == col[None, :] // (GQA * dec_seq)) & \
           (col[:, None] % dec_seq == col[None, :] % dec_seq)
    return ctab, last_tab, new_tab, win0_tab, same.astype(BF16)


def _layer_weights(ffn_pre, ffn_post, ffn_w_in, ffn_w_out):
    f = ffn_w_out.shape[0]
    return (ffn_pre[None, :], ffn_post[None, :], ffn_w_in[:, :f].astype(BF16), ffn_w_in[:, f:].astype(BF16),
            ffn_w_out.astype(BF16))


def kernel(x_prompt, x_sample, cache_cmp_kv, cache_slc_kv, cache_win_kv, state_ssm, state_conv, page_table,
           rel_table, ffn1_pre, ffn1_post, ffn1_w_in, ffn1_w_out, mix_pre, mix_post, w_in, w_out, att_out_norm,
           cmp_pe, w_cmp, conv_w, conv_b, dt_bias, a_log, d_skip, ssd_norm,
           ffn2_pre, ffn2_post, ffn2_w_in, ffn2_w_out):
    depth = ffn1_pre.shape[0]
    bp, seq, d = x_prompt.shape
    bs, dec_seq, _ = x_sample.shape
    n_pages = page_table.shape[1]
    past_len = n_pages * PAGE_SIZE
    w_buf = cache_win_kv.shape[2]
    kv_shape = (N_KV_HEADS, 2, HEAD_DIM)
    assert GQA * dec_seq * N_KV_HEADS == TILE and w_buf == WINDOW and seq % TILE == 0

    yp = x_prompt.reshape(bp * seq, d)
    ys = x_sample.reshape(bs * dec_seq, d)
    outs = [[] for _ in range(10)]
    for l in range(depth):
        ffn1 = _layer_weights(ffn1_pre[l], ffn1_post[l], ffn1_w_in[l], ffn1_w_out[l])
        ffn2 = _layer_weights(ffn2_pre[l], ffn2_post[l], ffn2_w_in[l], ffn2_w_out[l])
        w_proj = _pack_inproj(w_in[l])
        pe, bd = _pack_compress(cmp_pe[l], w_cmp[l])
        wo_att = w_out[l][:ATT_WIDTH].astype(BF16)
        wo_ssd = w_out[l][ATT_WIDTH:].astype(BF16)
        ssd_consts = (conv_w[l], conv_b[l][None, :], dt_bias[l][None, :], dt_bias[l][:, None], a_log[l][None, :],
                      a_log[l][:, None], jnp.repeat(d_skip[l], SSD_HEAD_DIM)[None, :], ssd_norm[l][None, :])
        mix_pre_l, mix_post_l, att_norm_l = mix_pre[l][None, :], mix_post[l][None, :], att_out_norm[l][None, :]

        yp = _ffn(yp, *ffn1)
        q, ckv, skv_b, wkv_b, gates, z, xbc, dt, dt_t, ckv_t, skv_t, wkv_t = _inproj(
            yp, mix_pre_l, w_proj, bp, True)
        kvc = _compress(ckv.reshape(bp, seq, KV_WIDTH), pe, bd, TILE)
        tiles, cmp_tab = _prompt_bias_tables(rel_table)
        o_att = _nsa_prompt(q.reshape(bp, seq, ATT_WIDTH), gates.reshape(bp, seq, N_KV_HEADS * TILE), kvc,
                            skv_b.reshape(bp, seq, KV_WIDTH), wkv_b.reshape(bp, seq, KV_WIDTH), tiles, cmp_tab)
        cl = min(CHUNK, seq)
        o_ssd, h_new, conv_new = _ssd(
            z.reshape(bp, seq, SSD_INNER), xbc.reshape(bp, seq, CONV_DIM), dt.reshape(bp, seq, TILE),
            dt_t, jnp.zeros((bp, SUBLANES, CONV_DIM), F32),
            jnp.zeros((bp, SSD_HEADS, SSD_HEAD_DIM, SSD_STATE), F32), ssd_consts, cl)
        yp = _mixout_ffn(yp, o_att.reshape(bp * seq, ATT_WIDTH), o_ssd.reshape(bp * seq, SSD_INNER), att_norm_l,
                         mix_post_l, wo_att, wo_ssd, *ffn2)

        def token_major(a_t):
            return a_t.reshape((bp,) + kv_shape + (a_t.shape[-1],)).transpose(0, 4, 1, 2, 3)

        outs[0].append(token_major(ckv_t))
        outs[1].append(token_major(skv_t))
        outs[2].append(token_major(wkv_t[:, :, -min(WINDOW, seq):]))
        outs[3].append(h_new)
        outs[4].append(conv_new)

        ys = _ffn(ys, *ffn1)
        q, ckv, _, _, gates, z, xbc, dt, dt_t, skv, wkv = _inproj(ys, mix_pre_l, w_proj, 1, False)
        native = (0, 2, 3, 4, 1)
        kc_past = _compress_paged(cache_cmp_kv[l].transpose(native), page_table, pe, bd)
        cmp_new = jnp.pad(ckv.reshape(bs, dec_seq, KV_WIDTH), ((0, 0), (0, BLK - dec_seq), (0, 0)))
        kc_new = _compress(cmp_new.reshape(1, bs * BLK, KV_WIDTH), pe, bd, bs).reshape(bs, 1, KV_WIDTH)
        n_past = kc_past.shape[1]
        n_lanes = -(-(n_past + 1) // TILE) * TILE
        kc = jnp.concatenate([kc_past, kc_new, jnp.zeros((bs, n_lanes - n_past - 1, KV_WIDTH), BF16)], axis=1)
        ctab, last_tab, new_tab, win0_tab, gsum = _sample_tables(rel_table, dec_seq, past_len, n_lanes)
        q_rows = q.reshape(bs, dec_seq, N_KV_HEADS, GQA, HEAD_DIM).transpose(0, 2, 3, 1, 4)
        own_head = jnp.eye(N_KV_HEADS, dtype=F32)[None, :, None, None, :, None]
        q_bd = q_rows[:, :, :, :, None, :] * own_head
        qk = q_bd.reshape(bs, TILE, N_KV_HEADS * HEAD_DIM).astype(BF16)
        g_rows = gates.reshape(bs, dec_seq, N_KV_HEADS, TILE)[..., :GQA * N_BRANCH]
        g_rows = g_rows.reshape(bs, dec_seq, N_KV_HEADS, GQA, N_BRANCH).transpose(0, 2, 3, 1, 4)
        gates_c = jnp.pad(g_rows.reshape(bs, TILE, N_BRANCH), ((0, 0), (0, 0), (0, TILE - N_BRANCH)))

        def new_t(a):
            a = a.reshape(bs, dec_seq, N_KV_HEADS, 2, HEAD_DIM).transpose(0, 3, 2, 4, 1)
            a = jnp.pad(a.reshape(bs, 2, N_KV_HEADS * HEAD_DIM, dec_seq), ((0, 0),) * 3 + ((0, TILE - dec_seq),))
            return a[:, 0].astype(BF16), a[:, 1].astype(BF16)

        k_new, v_new = new_t(skv)
        kw_new, vw_new = new_t(wkv)
        o_rows = _nsa_sample(page_table, cache_slc_kv[l].transpose(native), qk, gates_c, kc, ctab, gsum,
                             k_new, v_new, cache_win_kv[l].transpose(native), kw_new, vw_new, last_tab, new_tab,
                             win0_tab, dec_seq=dec_seq, past_len=past_len)
        o_att = o_rows.reshape(bs, N_KV_HEADS, GQA, dec_seq, HEAD_DIM).transpose(0, 3, 1, 2, 4)
        o_att = o_att.reshape(bs * dec_seq, ATT_WIDTH)
        conv_prev = jnp.pad(state_conv[l], ((0, 0), (SUBLANES - (CONV_W - 1), 0), (0, 0)))
        o_ssd, h_new, conv_new = _ssd(
            z.reshape(bs, dec_seq, SSD_INNER), xbc.reshape(bs, dec_seq, CONV_DIM), dt.reshape(bs, dec_seq, TILE),
            dt_t.reshape(TILE, bs, dec_seq).transpose(1, 0, 2), conv_prev, state_ssm[l], ssd_consts,
            min(CHUNK, dec_seq))
        ys = _mixout_ffn(ys, o_att, o_ssd.reshape(bs * dec_seq, SSD_INNER), att_norm_l, mix_post_l, wo_att,
                         wo_ssd, *ffn2)
        win_all = jnp.concatenate([cache_win_kv[l], wkv.reshape((bs, dec_seq) + kv_shape)], axis=1)
        outs[5].append(ckv.reshape((bs, dec_seq) + kv_shape))
        outs[6].append(skv.reshape((bs, dec_seq) + kv_shape))
        outs[7].append(win_all[:, -min(WINDOW, w_buf + dec_seq):])
        outs[8].append(h_new)
        outs[9].append(conv_new)

    return (yp.reshape(bp, seq, d), ys.reshape(bs, dec_seq, d)) + tuple(jnp.stack(o) for o in outs)
```

```python
import functools
import math

import jax
import jax.numpy as jnp
from jax import lax
from jax.experimental import pallas as pl
from jax.experimental.pallas import tpu as pltpu

F32 = jnp.float32
BF16 = jnp.bfloat16

D_MODEL = 1024
N_ATT_HEADS = 16
HEAD_DIM = 64
N_KV_HEADS = 4
GQA = N_ATT_HEADS // N_KV_HEADS
ATT_WIDTH = N_ATT_HEADS * HEAD_DIM
KV_WIDTH = 2 * N_KV_HEADS * HEAD_DIM
KV_GROUP = 2 * HEAD_DIM
BLK = 64
N_SEL = 16
WINDOW = 512
N_BRANCH = 3
N_BUCKETS = 32
MAX_DISTANCE = 128
SSD_HEADS = 16
SSD_HEAD_DIM = 64
SSD_INNER = SSD_HEADS * SSD_HEAD_DIM
SSD_GROUPS = 2
SSD_STATE = 128
CONV_W = 4
CONV_DIM = SSD_INNER + 2 * SSD_GROUPS * SSD_STATE
CHUNK = 128
PAGE_SIZE = 128
EPS = 1e-6

TILE = 128
SUBLANES = 8
MASK_VALUE = -(2.0 ** 100)
M_INIT = -1e30
LOG2E = math.log2(math.e)
VMEM_LIMIT = 56 * 1024 * 1024
PAGES_PER_STEP = 16
COMPRESS_ROWS_PER_DOT = 16
STAGE_PITCH = 72

NT_DIMS = (((1,), (1,)), ((), ()))
TN_DIMS = (((0,), (0,)), ((), ()))


def _params(n_grid_dims):
    return pltpu.CompilerParams(dimension_semantics=("arbitrary",) * n_grid_dims,
                                vmem_limit_bytes=VMEM_LIMIT)


def _rms(x, g):
    return x * lax.rsqrt(jnp.mean(x * x, axis=-1, keepdims=True) + EPS) * g


def _const_spec(shape):
    zeros = (0,) * len(shape)
    return pl.BlockSpec(shape, lambda *_: zeros, pipeline_mode=pl.Buffered(1))


def _ffn_body(x_ref, pre_ref, post_ref, wg_ref, wu_ref, wo_ref, o_ref):
    x = x_ref[...]
    h = _rms(x, pre_ref[...]).astype(BF16)
    gate = jnp.dot(h, wg_ref[...], preferred_element_type=F32)
    up = jnp.dot(h, wu_ref[...], preferred_element_type=F32)
    a = (gate * jax.nn.sigmoid(gate) * up).astype(BF16)
    y = jnp.dot(a, wo_ref[...], preferred_element_type=F32)
    o_ref[...] = x + 0.5 * _rms(y, post_ref[...])


def _ffn(x, pre, post, wg, wu, wo):
    t, d = x.shape
    f = wg.shape[1]
    tm = min(512, t)
    row = pl.BlockSpec((tm, d), lambda i: (i, 0))
    return pl.pallas_call(
        _ffn_body, grid=(t // tm,),
        in_specs=[row, _const_spec((1, d)), _const_spec((1, d)), _const_spec((d, f)), _const_spec((d, f)),
                  _const_spec((f, d))],
        out_specs=row, out_shape=jax.ShapeDtypeStruct((t, d), F32),
        compiler_params=_params(1), name="ffn")(x, pre, post, wg, wu, wo)


_SEG = {}
_off = 0
for _name, _w in (("q", ATT_WIDTH), ("ckv", KV_WIDTH), ("skv", KV_WIDTH), ("wkv", KV_WIDTH),
                  ("gates", N_KV_HEADS * TILE), ("z", SSD_INNER), ("xbc", CONV_DIM), ("dt", TILE)):
    _SEG[_name] = (_off, _off + _w)
    _off += _w
PROJ_WIDTH = _off


def _inproj_body(x_ref, pre_ref, w_ref, q_ref, ckv_ref, skvb_ref, wkvb_ref, gates_ref, z_ref, xbc_ref,
                 dt_ref, dtt_ref, *kv_refs, feature_major_kv):
    h = _rms(x_ref[...], pre_ref[...]).astype(BF16)

    def proj(name):
        lo, hi = _SEG[name]
        return jnp.dot(h, w_ref[:, lo:hi], preferred_element_type=F32)

    q_ref[...] = proj("q")
    ckv = proj("ckv")
    ckv_ref[...] = ckv
    skv = proj("skv")
    skvb_ref[...] = skv.astype(BF16)
    wkv = proj("wkv")
    wkvb_ref[...] = wkv.astype(BF16)
    gates_ref[...] = proj("gates")
    z_ref[...] = proj("z")
    xbc_ref[...] = proj("xbc")
    dt = proj("dt")
    dt_ref[...] = dt
    dtt_ref[...] = dt.T
    if feature_major_kv:
        for ref, rows in zip(kv_refs, (ckv, skv, wkv)):
            ref[...] = rows.T
    else:
        kv_refs[0][...] = skv
        kv_refs[1][...] = wkv


def _inproj(x, pre, w, n_seq, feature_major_kv):
    t, d = x.shape
    tm = min(256, t)
    seq = t // n_seq
    per_seq = seq // tm
    assert seq % tm == 0

    def cols(height):
        return pl.BlockSpec((None, height, tm), lambda i: (i // per_seq, 0, i % per_seq))

    def rows(width):
        return pl.BlockSpec((tm, width), lambda i: (i, 0))

    widths = (ATT_WIDTH, KV_WIDTH, KV_WIDTH, KV_WIDTH, N_KV_HEADS * TILE, SSD_INNER, CONV_DIM, TILE)
    dtypes = (F32, F32, BF16, BF16, F32, F32, F32, F32)
    out_shape = [jax.ShapeDtypeStruct((t, wd), dt) for wd, dt in zip(widths, dtypes)]
    out_specs = [rows(wd) for wd in widths]
    out_shape.append(jax.ShapeDtypeStruct((n_seq, TILE, seq), F32))
    out_specs.append(cols(TILE))
    if feature_major_kv:
        out_shape += [jax.ShapeDtypeStruct((n_seq, KV_WIDTH, seq), F32)] * 3
        out_specs += [cols(KV_WIDTH)] * 3
    else:
        out_shape += [jax.ShapeDtypeStruct((t, KV_WIDTH), F32)] * 2
        out_specs += [rows(KV_WIDTH)] * 2
    return pl.pallas_call(
        functools.partial(_inproj_body, feature_major_kv=feature_major_kv), grid=(t // tm,),
        in_specs=[rows(d), _const_spec((1, d)), _const_spec((d, PROJ_WIDTH))],
        out_specs=out_specs, out_shape=out_shape,
        compiler_params=_params(1), name="inproj")(x, pre, w)


def _mixout_body(x_ref, oa_ref, os_ref, an_ref, post_ref, wa_ref, ws_ref, o_ref):
    a = _rms(oa_ref[...], an_ref[...]).astype(BF16)
    y = jnp.dot(a, wa_ref[...], preferred_element_type=F32)
    y = y + jnp.dot(os_ref[...], ws_ref[...], preferred_element_type=F32)
    o_ref[...] = x_ref[...] + _rms(y, post_ref[...])


def _mixout_ffn_body(x_ref, oa_ref, os_ref, an_ref, mpost_ref, wa_ref, ws_ref, pre_ref, post_ref, wg_ref, wu_ref,
                     wo_ref, o_ref):
    _mixout_body(x_ref, oa_ref, os_ref, an_ref, mpost_ref, wa_ref, ws_ref, o_ref)
    _ffn_body(o_ref, pre_ref, post_ref, wg_ref, wu_ref, wo_ref, o_ref)


def _mixout_ffn(x, o_att, o_ssd, att_norm, mix_post, wa, ws, pre, post, wg, wu, wo):
    t, d = x.shape
    f = wg.shape[1]
    tm = min(512, t)
    row = pl.BlockSpec((tm, d), lambda i: (i, 0))
    vec = _const_spec((1, d))
    return pl.pallas_call(
        _mixout_ffn_body, grid=(t // tm,),
        in_specs=[row, row, row, vec, vec, _const_spec((d, d)), _const_spec((d, d)), vec, vec,
                  _const_spec((d, f)), _const_spec((d, f)), _const_spec((f, d))],
        out_specs=row, out_shape=jax.ShapeDtypeStruct((t, d), F32),
        compiler_params=_params(1), name="mixout_ffn")(x, o_att, o_ssd, att_norm, mix_post, wa, ws, pre, post, wg,
                                                       wu, wo)


def _compress_rows(x_refs, pe_ref, bd_ref, n_blocks):
    acc = None
    for l0 in range(0, BLK, COMPRESS_ROWS_PER_DOT):
        pieces = []
        for l in range(l0, l0 + COMPRESS_ROWS_PER_DOT):
            pe = pe_ref[l:l + 1, :]
            xs = jnp.concatenate([x_ref[pl.ds(l, n_blocks, stride=BLK), :] + pe for x_ref in x_refs], axis=0)
            pieces.append(xs.astype(BF16))
        w = bd_ref[l0:l0 + COMPRESS_ROWS_PER_DOT].reshape(COMPRESS_ROWS_PER_DOT * KV_GROUP, KV_GROUP)
        part = jnp.dot(jnp.concatenate(pieces, axis=1), w, preferred_element_type=F32)
        acc = part if acc is None else acc + part
    return acc


def _store_summaries(acc, o_ref, n_blocks):
    for k in range(N_KV_HEADS):
        o_ref[0:n_blocks, k * KV_GROUP:(k + 1) * KV_GROUP] = acc[k * n_blocks:(k + 1) * n_blocks].astype(BF16)


def _compress_body(*refs, n_blocks, n_out):
    x_refs = refs[:N_KV_HEADS]
    pe_ref, bd_ref, o_ref = refs[N_KV_HEADS:]
    _store_summaries(_compress_rows(x_refs, pe_ref, bd_ref, n_blocks), o_ref, n_blocks)
    if n_out > n_blocks:
        o_ref[n_blocks:n_out, :] = jnp.zeros((n_out - n_blocks, KV_WIDTH), BF16)


def _compress(kv, pe, bd, n_out):
    b, s, _ = kv.shape
    n_blocks = s // BLK

    def head_spec(k):
        return pl.BlockSpec((None, s, KV_GROUP), lambda i: (i, 0, k))

    return pl.pallas_call(
        functools.partial(_compress_body, n_blocks=n_blocks, n_out=n_out), grid=(b,),
        in_specs=[head_spec(k) for k in range(N_KV_HEADS)]
        + [_const_spec((BLK, KV_GROUP)), _const_spec((BLK, KV_GROUP, KV_GROUP))],
        out_specs=pl.BlockSpec((None, n_out, KV_WIDTH), lambda i: (i, 0, 0)),
        out_shape=jax.ShapeDtypeStruct((b, n_out, KV_WIDTH), BF16),
        compiler_params=_params(1), name="compress")(*([kv] * N_KV_HEADS), pe, bd)


def _compress_paged_body(pt_ref, *refs, n_pages):
    page_refs = refs[:PAGES_PER_STEP]
    wd_ref, pe_ref, o_ref, stage_ref = refs[PAGES_PER_STEP:]
    j = pl.program_id(1)
    tiles = n_pages * N_KV_HEADS
    for p in range(PAGES_PER_STEP):
        page = j * PAGES_PER_STEP + p
        for k in range(N_KV_HEADS):
            for c in range(2):
                at = pl.multiple_of((c * tiles + page * N_KV_HEADS + k) * STAGE_PITCH, SUBLANES)
                stage_ref[pl.ds(at, HEAD_DIM), :] = page_refs[p][k, c]

    @pl.when(j == pl.num_programs(1) - 1)
    def _():
        for c in range(2):
            acc = None
            for d0 in range(0, HEAD_DIM, COMPRESS_ROWS_PER_DOT):
                pieces = [stage_ref[pl.ds(c * tiles * STAGE_PITCH + d, tiles, stride=STAGE_PITCH), :].astype(BF16)
                          for d in range(d0, d0 + COMPRESS_ROWS_PER_DOT)]
                w = wd_ref[c, d0:d0 + COMPRESS_ROWS_PER_DOT].reshape(COMPRESS_ROWS_PER_DOT * PAGE_SIZE, KV_GROUP)
                part = jnp.dot(jnp.concatenate(pieces, axis=1), w, preferred_element_type=F32)
                acc = part if acc is None else acc + part
            o_ref[c] = (acc + pe_ref[c:c + 1, :]).astype(BF16)


def _compress_paged(cache, page_table, cmp_pe, w_cmp):
    b, n_pages = page_table.shape
    steps = n_pages // PAGES_PER_STEP
    blocks_per_page = PAGE_SIZE // BLK
    assert blocks_per_page * HEAD_DIM == KV_GROUP
    tiles = n_pages * N_KV_HEADS
    w_d = w_cmp.transpose(1, 2, 0, 3)
    eye = jnp.eye(blocks_per_page, dtype=w_cmp.dtype)
    w_d = (eye[None, None, :, None, :, None] * w_d[:, :, None, :, None, :]).reshape(2, HEAD_DIM, PAGE_SIZE, KV_GROUP)
    pe_term = jnp.einsum('lcd,lcde->ce', cmp_pe, w_cmp, precision=lax.Precision.HIGHEST)
    pe_term = jnp.tile(pe_term.astype(F32), (1, blocks_per_page))

    def page_spec(p):
        return pl.BlockSpec((None, N_KV_HEADS, 2, HEAD_DIM, PAGE_SIZE),
                            lambda i, j, pt: (pt[i, j * PAGES_PER_STEP + p], 0, 0, 0, 0))

    grid_spec = pltpu.PrefetchScalarGridSpec(
        num_scalar_prefetch=1, grid=(b, steps),
        in_specs=[page_spec(p) for p in range(PAGES_PER_STEP)]
        + [pl.BlockSpec((2, HEAD_DIM, PAGE_SIZE, KV_GROUP), lambda i, j, pt: (0, 0, 0, 0)),
           pl.BlockSpec((2, KV_GROUP), lambda i, j, pt: (0, 0))],
        out_specs=pl.BlockSpec((None, 2, tiles, KV_GROUP), lambda i, j, pt: (i, 0, 0, 0)),
        scratch_shapes=[pltpu.VMEM((2 * tiles * STAGE_PITCH, PAGE_SIZE), F32)])
    out = pl.pallas_call(
        functools.partial(_compress_paged_body, n_pages=n_pages), grid_spec=grid_spec,
        out_shape=jax.ShapeDtypeStruct((b, 2, tiles, KV_GROUP), BF16),
        compiler_params=_params(2), name="compress_paged")(page_table, *([cache] * PAGES_PER_STEP),
                                                           w_d.astype(BF16), pe_term)
    out = out.reshape(b, 2, n_pages, N_KV_HEADS, blocks_per_page, HEAD_DIM).transpose(0, 2, 4, 3, 1, 5)
    return out.reshape(b, n_pages * blocks_per_page, KV_WIDTH)


def _rel_bucket(dist):
    n = jnp.maximum(dist, 0)
    max_exact = N_BUCKETS // 2
    nf = jnp.maximum(n, 1).astype(F32)
    log_b = max_exact + (jnp.log(nf / max_exact) / math.log(MAX_DISTANCE / max_exact)
                         * (N_BUCKETS - max_exact)).astype(jnp.int32)
    return jnp.where(n < max_exact, n, jnp.minimum(log_b, N_BUCKETS - 1))


def _bias_of(rel_table, dist, visible):
    onehot = (_rel_bucket(dist)[..., None] == jnp.arange(N_BUCKETS, dtype=jnp.int32)).astype(F32)
    bias = jnp.einsum('rcb,bh->hrc', onehot, rel_table.astype(F32) * LOG2E, precision=lax.Precision.HIGHEST)
    return jnp.where(visible[None], bias, MASK_VALUE).reshape(-1, dist.shape[-1])


KIND_ZERO, KIND_DIAG, KIND_PREV, KIND_MASKED, KIND_OLDEST = range(5)


def _prompt_bias_tables(rel_table):
    r = jnp.arange(TILE, dtype=jnp.int32)[:, None]
    c = jnp.arange(TILE, dtype=jnp.int32)[None, :]
    true = jnp.ones((TILE, TILE), bool)
    far = _bias_of(rel_table, jnp.full((TILE, TILE), MAX_DISTANCE, jnp.int32), true)
    tiles = jnp.stack([
        jnp.zeros_like(far),
        _bias_of(rel_table, r - c, r >= c) - far,
        _bias_of(rel_table, TILE + r - c, true) - far,
        jnp.full_like(far, MASK_VALUE),
        jnp.where(jnp.tile(r < c, (N_ATT_HEADS, 1)), 0.0, MASK_VALUE),
    ])
    c_dist = r - (BLK - 1) + BLK * (BLK - 1 - c)
    cmp_tab = _bias_of(rel_table, c_dist, c_dist >= 0)
    return tiles, cmp_tab


def _selection_mask(score, n_rows):
    n_groups = score.shape[0] // SUBLANES
    groups = [score[SUBLANES * j:SUBLANES * (j + 1)] for j in range(n_groups)]
    ranks = [jnp.zeros(groups[0].shape, jnp.int32) for _ in range(n_groups)]
    row8 = lax.broadcasted_iota(jnp.int32, groups[0].shape, 0)
    for n in range(n_rows):
        other = jnp.broadcast_to(score[n:n + 1, :], groups[0].shape)
        for j in range(n_groups):
            lo = SUBLANES * j
            if lo > n:
                inc = jnp.where(other >= groups[j], 1, 0)
            elif lo + SUBLANES - 1 <= n:
                inc = jnp.where(other > groups[j], 1, 0)
            else:
                inc = jnp.where(row8 + lo > n, jnp.where(other >= groups[j], 1, 0),
                                jnp.where(other > groups[j], 1, 0))
            ranks[j] = ranks[j] + inc
    rank = jnp.concatenate(ranks, axis=0)
    return jnp.where(rank < N_SEL, jnp.where(score > -jnp.inf, 0.0, MASK_VALUE), MASK_VALUE)


PAIR = 2 * TILE
Q_TILES = PAIR // TILE
SEL_UNROLL = 4
_SEL_KINDS = {0: ((KIND_DIAG, KIND_MASKED), (KIND_PREV, KIND_DIAG)),
              1: ((KIND_ZERO, KIND_PREV), (KIND_ZERO, KIND_ZERO))}
_WIN_OLDEST = ((KIND_OLDEST, KIND_ZERO), (KIND_MASKED, KIND_OLDEST))


def _nsa_prompt_body(q_ref, gates_ref, kvc_ref, skv_ref, wkv_ref, tb_ref, ctab_ref, o_ref,
                     s_ref, sw_ref, mx_ref, mb_ref, acc_ref, part_ref, *, n_blocks):
    step = pl.program_id(2)
    head_rows = GQA * TILE
    rows = Q_TILES * head_rows
    lane = lax.broadcasted_iota(jnp.int32, (rows, TILE), 1)
    low = lane < HEAD_DIM
    key_lane = lax.broadcasted_iota(jnp.int32, (PAIR, TILE), 1)
    key_blk = lax.broadcasted_iota(jnp.int32, (PAIR, TILE), 0) // BLK
    n_pairs = skv_ref.shape[0] // PAIR

    qf = q_ref[...]
    halves = [qf[:, c * TILE:(c + 1) * TILE] for c in range(GQA // 2)]
    rolled = [pltpu.roll(h, HEAD_DIM, 1) for h in halves]
    q_all = jnp.concatenate([(halves if g % 2 == 0 else rolled)[g // 2][t * TILE:(t + 1) * TILE]
                             for t in range(Q_TILES) for g in range(GQA)], axis=0)
    lhs_plain = jnp.where(low, q_all, 0.0).astype(BF16)

    def ones_and_values(kv):
        kv_lane = lax.broadcasted_iota(jnp.int32, kv.shape, 1)
        return jnp.where(kv_lane < HEAD_DIM, jnp.ones_like(kv), kv)

    def normalise(acc):
        return acc / jnp.maximum(pltpu.roll(acc, HEAD_DIM, 1), 1e-30)

    def bias_of(kinds_per_tile, valid):
        parts = []
        for left, right in kinds_per_tile:
            left = jnp.where(valid, left, KIND_MASKED)
            right = jnp.where(valid, right, KIND_MASKED)
            parts.append(jnp.concatenate([tb_ref[left], tb_ref[right]], axis=1))
        return jnp.concatenate(parts, axis=0)

    def sel_kinds(j):
        behind = step - j
        return [tuple(jnp.where(behind == 0, _SEL_KINDS[0][t][side],
                                jnp.where(behind == 1, _SEL_KINDS[1][t][side], KIND_ZERO)) for side in range(2))
                for t in range(Q_TILES)]

    def keys_of(kv_ref, j):
        jc = jnp.clip(j, 0, n_pairs - 1)
        return kv_ref[pl.ds(pl.multiple_of(jc * PAIR, PAIR), PAIR), :]

    def score_pair(lhs, kv_ref, j, bias, block_columns):
        rhs = keys_of(kv_ref, j)
        if block_columns:
            onehot = jnp.where(key_lane - HEAD_DIM == (PAIR // BLK) * j + key_blk, 1.0, 0.0).astype(BF16)
            rhs = jnp.where(key_lane < HEAD_DIM, rhs, onehot)
        return lax.dot_general(lhs, rhs, NT_DIMS, preferred_element_type=F32) + bias

    def half_max(sc):
        return jnp.maximum(sc[:, :TILE], sc[:, TILE:])

    def weigh_pair(sc, mb, kv_ref, j):
        pr = jnp.exp2(sc - jnp.concatenate([mb, mb], axis=1))
        return jnp.dot(pr.astype(BF16), ones_and_values(keys_of(kv_ref, j)), preferred_element_type=F32)

    def row_max(mx):
        return jnp.broadcast_to(jnp.max(mx, axis=1, keepdims=True), (rows, TILE))

    n_win = WINDOW // PAIR + 1
    w_pairs = [step - (n_win - 1) + w for w in range(n_win)]
    w_kinds = [_WIN_OLDEST] + [[(KIND_ZERO, KIND_ZERO)] * Q_TILES] * (n_win - 3) + [_SEL_KINDS[1], _SEL_KINDS[0]]
    w_mx = None
    for w, j in enumerate(w_pairs):
        sc = score_pair(lhs_plain, wkv_ref, j, bias_of(w_kinds[w], j >= 0), False)
        sw_ref[w] = sc
        w_mx = half_max(sc) if w_mx is None else jnp.maximum(w_mx, half_max(sc))

    kvc = kvc_ref[...]
    bias = []
    for t in range(Q_TILES):
        i = Q_TILES * step + t
        shift = (2 * i + (TILE - (BLK - 1))) % TILE
        bias += [pltpu.roll(ctab_ref[g * TILE:(g + 1) * TILE, :], shift, 1) for g in range(GQA)]
    s = lax.dot_general(lhs_plain, kvc, NT_DIMS, preferred_element_type=F32)
    s = jnp.where(lane < n_blocks, s + jnp.concatenate(bias, axis=0), MASK_VALUE)
    m = jnp.maximum(jnp.max(s, axis=1, keepdims=True), M_INIT)
    p = jnp.exp2(s - m)
    c_acc = jnp.dot(p.astype(BF16), ones_and_values(kvc), preferred_element_type=F32)
    p = p / jnp.maximum(c_acc, 1e-30)
    o_cmp = normalise(c_acc)

    w_mb = row_max(jnp.maximum(w_mx, M_INIT))
    w_acc = None
    for w, j in enumerate(w_pairs):
        dacc = weigh_pair(sw_ref[w], w_mb, wkv_ref, j)
        w_acc = dacc if w_acc is None else w_acc + dacc
    o_win = normalise(w_acc)

    gate = jax.nn.sigmoid(gates_ref[...])

    def gate_col(t, g, br):
        c = g * N_BRANCH + br
        return gate[t * TILE:(t + 1) * TILE, c:c + 1]

    heads = [(t, g, slice(t * head_rows + g * TILE, t * head_rows + (g + 1) * TILE))
             for t in range(Q_TILES) for g in range(GQA)]
    for t, g, head in heads:
        part_ref[head, :] = gate_col(t, g, 0) * o_cmp[head] + gate_col(t, g, 2) * o_win[head]

    blk = lax.broadcasted_iota(jnp.int32, (n_blocks, PAIR), 0)
    query = lax.broadcasted_iota(jnp.int32, (n_blocks, PAIR), 1)
    cur = (PAIR // BLK) * step + query // BLK
    imp_t = []
    for t in range(Q_TILES):
        imp = p[t * head_rows:t * head_rows + TILE]
        for g in range(1, GQA):
            imp = imp + p[t * head_rows + g * TILE:t * head_rows + (g + 1) * TILE]
        imp_t.append(imp.T[0:n_blocks])
    forced = (blk == 0) | (blk == cur) | (blk == cur - 1)
    score = jnp.where(forced, jnp.inf, jnp.where(blk <= cur, jnp.concatenate(imp_t, axis=1), -jnp.inf))
    neg_t = _selection_mask(score, n_blocks)
    neg = []
    for t in range(Q_TILES):
        pieces = [jnp.zeros((HEAD_DIM, TILE), F32), neg_t[:, t * TILE:(t + 1) * TILE]]
        if n_blocks < TILE - HEAD_DIM:
            pieces.append(jnp.zeros((TILE - HEAD_DIM - n_blocks, TILE), F32))
        neg += [jnp.concatenate(pieces, axis=0).T] * GQA
    lhs_sel = jnp.where(low, q_all, jnp.concatenate(neg, axis=0)).astype(BF16)

    n_trips = (step + SEL_UNROLL) // SEL_UNROLL
    mx_ref[...] = jnp.full(mx_ref.shape, M_INIT, F32)

    def score_trip(t, carry):
        mx = None
        for u in range(SEL_UNROLL):
            j = t * SEL_UNROLL + u
            sc = score_pair(lhs_sel, skv_ref, j, bias_of(sel_kinds(j), j <= step), True)
            s_ref[j] = sc
            mx = half_max(sc) if mx is None else jnp.maximum(mx, half_max(sc))
        mx_ref[...] = jnp.maximum(mx_ref[...], mx)
        return carry

    lax.fori_loop(0, n_trips, score_trip, 0)
    mb_ref[...] = row_max(mx_ref[...])
    acc_ref[...] = jnp.zeros(acc_ref.shape, F32)

    def weigh_trip(t, carry):
        mb = mb_ref[...]
        acc = None
        for u in range(SEL_UNROLL):
            j = t * SEL_UNROLL + u
            dacc = weigh_pair(s_ref[j], mb, skv_ref, j)
            acc = dacc if acc is None else acc + dacc
        acc_ref[...] += acc
        return carry

    lax.fori_loop(0, n_trips, weigh_trip, 0)
    o_sel = normalise(acc_ref[...])

    out_low = lax.broadcasted_iota(jnp.int32, (TILE, TILE), 1) < HEAD_DIM
    outs = {(t, g): part_ref[head, :] + gate_col(t, g, 1) * o_sel[head] for t, g, head in heads}
    for t in range(Q_TILES):
        for c in range(GQA // 2):
            o_ref[t * TILE:(t + 1) * TILE, c * TILE:(c + 1) * TILE] = jnp.where(
                out_low, pltpu.roll(outs[t, 2 * c], HEAD_DIM, 1), outs[t, 2 * c + 1])


def _nsa_prompt(q, gates, kvc, skv_b, wkv_b, tiles, cmp_tab):
    b, s, _ = q.shape
    n_blocks = s // BLK
    assert n_blocks <= TILE - HEAD_DIM and kvc.shape[1] == TILE and s % PAIR == 0 and WINDOW // PAIR >= 2
    gw = GQA * HEAD_DIM
    head_rows = GQA * TILE
    rows = Q_TILES * head_rows
    return pl.pallas_call(
        functools.partial(_nsa_prompt_body, n_blocks=n_blocks), grid=(b, N_KV_HEADS, s // PAIR),
        in_specs=[pl.BlockSpec((None, PAIR, gw), lambda bi, k, i: (bi, i, k)),
                  pl.BlockSpec((None, PAIR, TILE), lambda bi, k, i: (bi, i, k)),
                  pl.BlockSpec((None, TILE, KV_GROUP), lambda bi, k, i: (bi, 0, k)),
                  pl.BlockSpec((None, s, KV_GROUP), lambda bi, k, i: (bi, 0, k)),
                  pl.BlockSpec((None, s, KV_GROUP), lambda bi, k, i: (bi, 0, k)),
                  pl.BlockSpec((len(tiles), head_rows, TILE), lambda bi, k, i: (0, k, 0)),
                  pl.BlockSpec((head_rows, TILE), lambda bi, k, i: (k, 0))],
        out_specs=pl.BlockSpec((None, PAIR, gw), lambda bi, k, i: (bi, i, k)),
        out_shape=jax.ShapeDtypeStruct((b, s, ATT_WIDTH), F32),
        scratch_shapes=[pltpu.VMEM((s // PAIR + SEL_UNROLL - 1, rows, PAIR), F32),
                        pltpu.VMEM((WINDOW // PAIR + 1, rows, PAIR), F32),
                        pltpu.VMEM((rows, TILE), F32), pltpu.VMEM((rows, TILE), F32),
                        pltpu.VMEM((rows, KV_GROUP), F32), pltpu.VMEM((rows, TILE), F32)],
        compiler_params=_params(3), name="nsa_prompt")(q, gates, kvc, skv_b, wkv_b, tiles, cmp_tab)


def _nsa_sample_body(pt_ref, *refs, n_pages, dec_seq, past_len):
    page_refs = refs[:PAGES_PER_STEP]
    (qk_ref, gates_ref, kc_ref, ctab_ref, gsum_ref, knew_ref, vnew_ref, cwin_ref, kwnew_ref, vwnew_ref,
     last_ref, new_ref, win0_ref, o_ref, lhs_ref, m_ref, l_ref, acc_ref, ow_ref, oc_ref) = refs[PAGES_PER_STEP:]
    j = pl.program_id(1)
    last_step = pl.num_programs(1) - 1
    blocks_per_page = PAGE_SIZE // BLK
    n_past = n_pages * blocks_per_page
    kd = N_KV_HEADS * HEAD_DIM
    n_lanes = kc_ref.shape[0]
    step_keys = PAGES_PER_STEP * PAGE_SIZE

    def gate_col(br):
        return jax.nn.sigmoid(gates_ref[:, br:br + 1])

    def head_major(ref, c, lo=0, n=None):
        n = ref.shape[-1] if n is None else n
        return ref[:, c, :, lo:lo + n].reshape(kd, n).astype(BF16)

    def lane_tiles(x):
        return [x[:, t * TILE:(t + 1) * TILE] for t in range(x.shape[1] // TILE)]

    def tile_max(tiles):
        m = tiles[0]
        for t in tiles[1:]:
            m = jnp.maximum(m, t)
        return jnp.broadcast_to(jnp.max(m, axis=1, keepdims=True), (TILE, TILE))

    def tile_sum(tiles):
        s = tiles[0]
        for t in tiles[1:]:
            s = s + t
        return s

    def online_update(s, values_t):
        tiles = lane_tiles(s)
        m_prev = m_ref[...]
        m_new = jnp.maximum(m_prev, tile_max(tiles))
        alpha = jnp.exp2(m_prev - m_new)
        p = [jnp.exp2(t - m_new) for t in tiles]
        l_ref[...] = alpha * l_ref[...] + tile_sum(p)
        pv = lax.dot_general(jnp.concatenate(p, axis=1).astype(BF16), values_t, NT_DIMS, preferred_element_type=F32)
        acc_ref[...] = jnp.concatenate([alpha] * (kd // TILE), axis=1) * acc_ref[...] + pv
        m_ref[...] = m_new

    @pl.when(j == 0)
    def _():
        kc = kc_ref[...]
        qf = qk_ref[...].astype(F32)
        gap = jnp.zeros((TILE, HEAD_DIM), F32)
        qc = jnp.concatenate([piece for k in range(N_KV_HEADS)
                              for piece in (qf[:, k * HEAD_DIM:(k + 1) * HEAD_DIM], gap)], axis=1).astype(BF16)
        s = lax.dot_general(qc, kc, NT_DIMS, preferred_element_type=F32) + ctab_ref[...]
        m = jnp.maximum(jnp.max(s, axis=1, keepdims=True), M_INIT)
        p = jnp.exp2(s - m)
        p = p / jnp.maximum(jnp.sum(p, axis=1, keepdims=True), 1e-30)
        oc_ref[...] = gate_col(0) * jnp.dot(p.astype(BF16), kc, preferred_element_type=F32)
        imp = _select_sum(p, gsum_ref[...], False)
        imp_t = jnp.concatenate([t.T for t in lane_tiles(imp)], axis=0)
        n_rows = -(-(n_past + 1) // SUBLANES) * SUBLANES
        blk = lax.broadcasted_iota(jnp.int32, (n_rows, TILE), 0)
        tok = lax.broadcasted_iota(jnp.int32, (n_rows, TILE), 1) % dec_seq
        cur = (past_len + tok) // BLK
        forced = (blk == 0) | (blk == cur) | (blk == cur - 1)
        score = jnp.where(forced, jnp.inf, jnp.where(blk <= cur, imp_t[0:n_rows], -jnp.inf))
        neg_t = jnp.concatenate([_selection_mask(score, n_past + 1), jnp.zeros((n_lanes - n_rows, TILE), F32)], axis=0)
        neg = jnp.concatenate([neg_t[t * TILE:(t + 1) * TILE].T for t in range(n_lanes // TILE)], axis=1)
        lhs_ref[:, 0:kd] = qk_ref[...]
        lhs_ref[:, kd:kd + n_lanes] = neg.astype(BF16)

        qk = qk_ref[...]
        n_win = cwin_ref.shape[-1] // TILE
        zero = jnp.zeros((TILE, TILE), F32)
        s_w = jnp.dot(qk, head_major(cwin_ref, 0), preferred_element_type=F32)
        s_w = s_w + jnp.concatenate([win0_ref[...]] + [zero] * (n_win - 2) + [last_ref[...]], axis=1)
        s_n = jnp.dot(qk, kwnew_ref[...], preferred_element_type=F32) + new_ref[...]
        tiles = lane_tiles(s_w) + [s_n]
        m = tile_max(tiles)
        p = [jnp.exp2(t - m) for t in tiles]
        l = jnp.sum(tile_sum(p), axis=1, keepdims=True)
        o_w = lax.dot_general(jnp.concatenate(p[:-1], axis=1).astype(BF16), head_major(cwin_ref, 1), NT_DIMS,
                              preferred_element_type=F32)
        o_w = o_w + lax.dot_general(p[-1].astype(BF16), vwnew_ref[...], NT_DIMS, preferred_element_type=F32)
        ow_ref[...] = gate_col(2) * (o_w / l)

        m_ref[...] = jnp.full(m_ref.shape, M_INIT, F32)
        l_ref[...] = jnp.zeros(l_ref.shape, F32)
        acc_ref[...] = jnp.zeros(acc_ref.shape, F32)

    keys_t = jnp.concatenate([head_major(r, 0) for r in page_refs], axis=1)
    values_t = jnp.concatenate([head_major(r, 1) for r in page_refs], axis=1)
    blk_row = lax.broadcasted_iota(jnp.int32, (n_lanes, step_keys), 0)
    key_blk = j * (PAGES_PER_STEP * blocks_per_page) + lax.broadcasted_iota(jnp.int32, (n_lanes, step_keys), 1) // BLK
    onehot = jnp.where(blk_row == key_blk, 1.0, 0.0).astype(BF16)
    s = jnp.dot(lhs_ref[...], jnp.concatenate([keys_t, onehot], axis=0), preferred_element_type=F32)
    tiles = lane_tiles(s)
    tiles[-1] = tiles[-1] + jnp.where(j == last_step, last_ref[...], 0.0)
    online_update(jnp.concatenate(tiles, axis=1), values_t)

    @pl.when(j == last_step)
    def _():
        new_blk = lax.broadcasted_iota(jnp.int32, (n_lanes, TILE), 0) == n_past
        rhs = jnp.concatenate([knew_ref[...], jnp.where(new_blk, 1.0, 0.0).astype(BF16)], axis=0)
        online_update(jnp.dot(lhs_ref[...], rhs, preferred_element_type=F32) + new_ref[...], vnew_ref[...])
        l = jnp.sum(l_ref[...], axis=1, keepdims=True)
        o_sw = gate_col(1) * (acc_ref[...] / l) + ow_ref[...]
        o_c = oc_ref[...]
        rows_per_head = TILE // N_KV_HEADS
        for k in range(N_KV_HEADS):
            rows = slice(k * rows_per_head, (k + 1) * rows_per_head)
            v_lo = k * KV_GROUP + HEAD_DIM
            o_ref[rows, :] = o_sw[rows, k * HEAD_DIM:(k + 1) * HEAD_DIM] + o_c[rows, v_lo:v_lo + HEAD_DIM]


def _nsa_sample(page_table, cache_slc, qk, gates_c, kc, ctab, gsum, k_new, v_new, cache_win, kw_new, vw_new,
                last_tab, new_tab, win0_tab, *, dec_seq, past_len):
    b, n_pages = page_table.shape
    steps = n_pages // PAGES_PER_STEP
    n_lanes = kc.shape[1]
    w_buf = cache_win.shape[-1]
    kd = N_KV_HEADS * HEAD_DIM

    def page_spec(p):
        return pl.BlockSpec((None, N_KV_HEADS, 2, HEAD_DIM, PAGE_SIZE),
                            lambda i, j, pt: (pt[i, j * PAGES_PER_STEP + p], 0, 0, 0, 0))

    def per_batch(*shape):
        zeros = (0,) * len(shape)
        return pl.BlockSpec((None,) + shape, lambda i, j, pt: (i,) + zeros)

    def const(*shape):
        zeros = (0,) * len(shape)
        return pl.BlockSpec(shape, lambda i, j, pt: zeros)

    grid_spec = pltpu.PrefetchScalarGridSpec(
        num_scalar_prefetch=1, grid=(b, steps),
        in_specs=[page_spec(p) for p in range(PAGES_PER_STEP)]
        + [per_batch(TILE, kd), per_batch(TILE, TILE), per_batch(n_lanes, KV_WIDTH),
           const(TILE, n_lanes), const(TILE, TILE), per_batch(kd, TILE), per_batch(kd, TILE),
           per_batch(N_KV_HEADS, 2, HEAD_DIM, w_buf), per_batch(kd, TILE), per_batch(kd, TILE),
           const(TILE, TILE), const(TILE, TILE), const(TILE, TILE)],
        out_specs=per_batch(TILE, HEAD_DIM),
        scratch_shapes=[pltpu.VMEM((TILE, kd + n_lanes), BF16), pltpu.VMEM((TILE, TILE), F32),
                        pltpu.VMEM((TILE, TILE), F32), pltpu.VMEM((TILE, kd), F32), pltpu.VMEM((TILE, kd), F32),
                        pltpu.VMEM((TILE, KV_WIDTH), F32)])
    return pl.pallas_call(
        functools.partial(_nsa_sample_body, n_pages=n_pages, dec_seq=dec_seq, past_len=past_len),
        grid_spec=grid_spec, out_shape=jax.ShapeDtypeStruct((b, TILE, HEAD_DIM), F32),
        compiler_params=_params(2), name="nsa_sample")(
            page_table, *([cache_slc] * PAGES_PER_STEP), qk, gates_c, kc, ctab, gsum, k_new, v_new, cache_win,
            kw_new, vw_new, last_tab, new_tab, win0_tab)


def _split3(x):
    hi = x.astype(BF16)
    r = x - hi.astype(F32)
    mid = r.astype(BF16)
    return hi, mid, (r - mid.astype(F32)).astype(BF16)


def _select_sum(x, onehot, x_on_left):
    out = None
    for part in _split3(x):
        term = (jnp.dot(part, onehot, preferred_element_type=F32) if x_on_left
                else jnp.dot(onehot, part, preferred_element_type=F32))
        out = term if out is None else out + term
    return out


def _softplus(x):
    return jnp.maximum(x, 0.0) + jnp.log(1.0 + jnp.exp(-jnp.abs(x)))


def _ssd_body(z_ref, xbc_ref, dt_ref, dtt_ref, convp_ref, h0_ref, cw_ref, cb_ref, dtb_ref, dtbt_ref, al_ref,
              alt_ref, dskip_ref, norm_ref, o_ref, hout_ref, convout_ref, xp_ref, *, cl):
    c = pl.program_id(1)
    pad = SUBLANES

    @pl.when(c == 0)
    def _():
        xp_ref[0:pad, :] = convp_ref[...]
        hout_ref[...] = h0_ref[...]

    xp_ref[pad:pad + cl, :] = xbc_ref[...]
    conv = cb_ref[...]
    for k in range(CONV_W):
        lo = pad - (CONV_W - 1) + k
        conv = conv + cw_ref[k:k + 1, :] * xp_ref[lo:lo + cl, :]
    u = conv * jax.nn.sigmoid(conv)
    tail = xp_ref[pad + cl - (CONV_W - 1):pad + cl, :]
    convout_ref[...] = tail
    xp_ref[pad - (CONV_W - 1):pad, :] = tail

    xs = u[:, :SSD_INNER]
    dt = _softplus(dt_ref[:, 0:SSD_HEADS] + dtb_ref[...])
    dt_t = _softplus(dtt_ref[0:SSD_HEADS, :] + dtbt_ref[...])
    la = dt * -jnp.exp(al_ref[...])
    la_t = dt_t * -jnp.exp(alt_ref[...])
    row = lax.broadcasted_iota(jnp.int32, (cl, cl), 0)
    col = lax.broadcasted_iota(jnp.int32, (cl, cl), 1)
    causal = row >= col
    cs = _select_sum(la, jnp.where(causal, 1.0, 0.0).astype(BF16), False)
    cs_t = _select_sum(la_t, jnp.where(row <= col, 1.0, 0.0).astype(BF16), True)

    def spread(width):
        lane_head = lax.broadcasted_iota(jnp.int32, (SSD_HEADS, SSD_HEADS * width), 1) // width
        return jnp.where(lane_head == lax.broadcasted_iota(jnp.int32, lane_head.shape, 0), 1.0, 0.0).astype(BF16)

    dt_x = _select_sum(dt, spread(SSD_HEAD_DIM), True)
    cs_x = _select_sum(cs, spread(SSD_HEAD_DIM), True)
    cs_wide = _select_sum(cs, spread(TILE), True)
    last_x = cs_x[cl - 1:cl, :]
    xdt = xs * dt_x
    xdt_b = xdt.astype(BF16)
    to_end = (xdt * jnp.exp(last_x - cs_x)).astype(BF16)
    grow = jnp.exp(cs_x)
    skip = dskip_ref[...] * xs

    heads_per_group = SSD_HEADS // SSD_GROUPS
    bms, cms, cbs = [], [], []
    for g in range(SSD_GROUPS):
        b_lo = SSD_INNER + g * SSD_STATE
        c_lo = SSD_INNER + SSD_GROUPS * SSD_STATE + g * SSD_STATE
        bms.append(u[:, b_lo:b_lo + SSD_STATE].astype(BF16))
        cms.append(u[:, c_lo:c_lo + SSD_STATE].astype(BF16))
        cbs.append(lax.dot_general(cms[g], bms[g], NT_DIMS, preferred_element_type=F32))
    first_head = lax.broadcasted_iota(jnp.int32, (cl, TILE), 1) < SSD_HEAD_DIM
    state_row = lax.broadcasted_iota(jnp.int32, (2 * SSD_HEAD_DIM, SSD_STATE), 0)
    ys = []
    for pair in range(SSD_HEADS // 2):
        g = 2 * pair // heads_per_group
        lanes = slice(pair * TILE, (pair + 1) * TILE)
        y_heads, keep = [], []
        for h in (2 * pair, 2 * pair + 1):
            diff = cs_wide[:, h * TILE:h * TILE + cl] - cs_t[h:h + 1, :]
            decay = jnp.where(causal, jnp.exp(jnp.where(causal, diff, 0.0)), 0.0)
            y_heads.append(jnp.dot((cbs[g] * decay).astype(BF16), xdt_b[:, lanes], preferred_element_type=F32))
            keep.append(jnp.exp(cs[cl - 1:cl, h:h + 1]))
        state_in = lax.dot_general(to_end[:, lanes], bms[g], TN_DIMS, preferred_element_type=F32)
        h_prev = jnp.concatenate([hout_ref[2 * pair], hout_ref[2 * pair + 1]], axis=0)
        y_off = lax.dot_general(cms[g], h_prev.astype(BF16), NT_DIMS, preferred_element_type=F32)
        h_new = h_prev * jnp.where(state_row < SSD_HEAD_DIM, keep[0], keep[1]) + state_in
        hout_ref[2 * pair] = h_new[0:SSD_HEAD_DIM]
        hout_ref[2 * pair + 1] = h_new[SSD_HEAD_DIM:]
        ys.append(jnp.where(first_head, y_heads[0], y_heads[1]) + y_off * grow[:, lanes] + skip[:, lanes])
    y = jnp.concatenate(ys, axis=1)
    z = z_ref[...]
    y = y * (z * jax.nn.sigmoid(z))
    gw = SSD_INNER // SSD_GROUPS
    normed = []
    for g in range(SSD_GROUPS):
        yg = y[:, g * gw:(g + 1) * gw]
        normed.append(yg * lax.rsqrt(jnp.mean(yg * yg, axis=-1, keepdims=True) + EPS))
    o_ref[...] = (jnp.concatenate(normed, axis=1) * norm_ref[...]).astype(o_ref.dtype)


def _ssd(z, xbc, dt, dt_t, conv_prev, h0, consts, cl):
    b, seq, _ = z.shape
    nc = seq // cl
    cw, cbias, dtb, dtb_t, al, al_t, dskip, norm = consts

    def rows(width):
        return pl.BlockSpec((None, cl, width), lambda i, c: (i, c, 0))

    state = pl.BlockSpec((None, SSD_HEADS, SSD_HEAD_DIM, SSD_STATE), lambda i, c: (i, 0, 0, 0))

    def const(x):
        zeros = (0,) * x.ndim
        return pl.BlockSpec(x.shape, lambda i, c: zeros)

    return pl.pallas_call(
        functools.partial(_ssd_body, cl=cl), grid=(b, nc),
        in_specs=[rows(SSD_INNER), rows(CONV_DIM), rows(TILE),
                  pl.BlockSpec((None, TILE, cl), lambda i, c: (i, 0, c)),
                  pl.BlockSpec((None, SUBLANES, CONV_DIM), lambda i, c: (i, 0, 0)), state,
                  const(cw), const(cbias), const(dtb), const(dtb_t), const(al), const(al_t), const(dskip),
                  const(norm)],
        out_specs=[rows(SSD_INNER), state, pl.BlockSpec((None, CONV_W - 1, CONV_DIM), lambda i, c: (i, 0, 0))],
        out_shape=[jax.ShapeDtypeStruct((b, seq, SSD_INNER), BF16),
                   jax.ShapeDtypeStruct((b, SSD_HEADS, SSD_HEAD_DIM, SSD_STATE), F32),
                   jax.ShapeDtypeStruct((b, CONV_W - 1, CONV_DIM), F32)],
        scratch_shapes=[pltpu.VMEM((SUBLANES + cl, CONV_DIM), F32)],
        compiler_params=_params(2), name="ssd")(z, xbc, dt, dt_t, conv_prev, h0, cw, cbias, dtb, dtb_t, al, al_t,
                                                 dskip, norm)


def _pack_inproj(w_in):
    splits = (ATT_WIDTH, KV_WIDTH, KV_WIDTH, KV_WIDTH, N_BRANCH * N_ATT_HEADS, SSD_INNER, CONV_DIM, SSD_HEADS)
    offs = [0]
    for s in splits:
        offs.append(offs[-1] + s)
    wq, wc, ws, ww, wg, wz, wx, wd = [w_in[:, offs[i]:offs[i + 1]] for i in range(len(splits))]
    d = w_in.shape[0]
    per_kv = GQA * N_BRANCH
    wg4 = jnp.pad(wg.reshape(d, N_KV_HEADS, per_kv), ((0, 0), (0, 0), (0, TILE - per_kv)))
    wd_pad = jnp.pad(wd, ((0, 0), (0, TILE - SSD_HEADS)))
    w = jnp.concatenate([wq * (HEAD_DIM ** -0.5 * LOG2E), wc, ws, ww, wg4.reshape(d, N_KV_HEADS * TILE), wz, wx,
                         wd_pad], axis=1)
    assert w.shape[1] == PROJ_WIDTH
    return w.astype(BF16)


def _pack_compress(cmp_pe, w_cmp):
    zero = jnp.zeros((BLK, HEAD_DIM, HEAD_DIM), w_cmp.dtype)
    bd = jnp.concatenate([jnp.concatenate([w_cmp[:, 0], zero], axis=2),
                          jnp.concatenate([zero, w_cmp[:, 1]], axis=2)], axis=1)
    pe = cmp_pe.reshape(BLK, KV_GROUP)
    return pe.astype(F32), bd.astype(BF16)


def _sample_tables(rel_table, dec_seq, past_len, n_lanes):
    col = jnp.arange(TILE, dtype=jnp.int32)
    head = (col // (GQA * dec_seq)) * GQA + (col // dec_seq) % GQA
    tok = (col % dec_seq)[:, None]
    per_col = (rel_table.astype(F32) * LOG2E)[:, head]

    def bias(dist):
        onehot = (_rel_bucket(dist)[..., None] == jnp.arange(N_BUCKETS, dtype=jnp.int32)).astype(F32)
        return jnp.einsum('ckb,bc->ck', onehot, per_col, precision=lax.Precision.HIGHEST)

    lane = jnp.arange(TILE, dtype=jnp.int32)[None, :]
    far = bias(jnp.full((TILE, 1), MAX_DISTANCE, jnp.int32))
    blk = jnp.arange(n_lanes, dtype=jnp.int32)[None, :]
    c_dist = past_len + tok - ((blk + 1) * BLK - 1)
    ctab = jnp.where(c_dist >= 0, bias(c_dist), MASK_VALUE)
    last_tab = bias(TILE + tok - lane) - far
    new_tab = jnp.where((lane <= tok) & (lane < dec_seq), bias(tok - lane) - far, MASK_VALUE)
    win0_tab = jnp.where(lane > tok, 0.0, MASK_VALUE)
    same = (col[:, None] // (GQA * dec_seq) == col[None, :] // (GQA * dec_seq)) & \
           (col[:, None] % dec_seq == col[None, :] % dec_seq)
    return ctab, last_tab, new_tab, win0_tab, same.astype(BF16)


def _layer_weights(ffn_pre, ffn_post, ffn_w_in, ffn_w_out):
    f = ffn_w_out.shape[0]
    return (ffn_pre[None, :], ffn_post[None, :], ffn_w_in[:, :f].astype(BF16), ffn_w_in[:, f:].astype(BF16),
            ffn_w_out.astype(BF16))


def kernel(x_prompt, x_sample, cache_cmp_kv, cache_slc_kv, cache_win_kv, state_ssm, state_conv, page_table,
           rel_table, ffn1_pre, ffn1_post, ffn1_w_in, ffn1_w_out, mix_pre, mix_post, w_in, w_out, att_out_norm,
           cmp_pe, w_cmp, conv_w, conv_b, dt_bias, a_log, d_skip, ssd_norm,
           ffn2_pre, ffn2_post, ffn2_w_in, ffn2_w_out):
    depth = ffn1_pre.shape[0]
    bp, seq, d = x_prompt.shape
    bs, dec_seq, _ = x_sample.shape
    n_pages = page_table.shape[1]
    past_len = n_pages * PAGE_SIZE
    w_buf = cache_win_kv.shape[2]
    kv_shape = (N_KV_HEADS, 2, HEAD_DIM)
    assert GQA * dec_seq * N_KV_HEADS == TILE and w_buf == WINDOW and seq % TILE == 0

    yp = x_prompt.reshape(bp * seq, d)
    ys = x_sample.reshape(bs * dec_seq, d)
    outs = [[] for _ in range(10)]
    for l in range(depth):
        ffn1 = _layer_weights(ffn1_pre[l], ffn1_post[l], ffn1_w_in[l], ffn1_w_out[l])
        ffn2 = _layer_weights(ffn2_pre[l], ffn2_post[l], ffn2_w_in[l], ffn2_w_out[l])
        w_proj = _pack_inproj(w_in[l])
        pe, bd = _pack_compress(cmp_pe[l], w_cmp[l])
        wo_att = w_out[l][:ATT_WIDTH].astype(BF16)
        wo_ssd = w_out[l][ATT_WIDTH:].astype(BF16)
        ssd_consts = (conv_w[l], conv_b[l][None, :], dt_bias[l][None, :], dt_bias[l][:, None], a_log[l][None, :],
                      a_log[l][:, None], jnp.repeat(d_skip[l], SSD_HEAD_DIM)[None, :], ssd_norm[l][None, :])
        mix_pre_l, mix_post_l, att_norm_l = mix_pre[l][None, :], mix_post[l][None, :], att_out_norm[l][None, :]

        yp = _ffn(yp, *ffn1)
        q, ckv, skv_b, wkv_b, gates, z, xbc, dt, dt_t, ckv_t, skv_t, wkv_t = _inproj(
            yp, mix_pre_l, w_proj, bp, True)
        kvc = _compress(ckv.reshape(bp, seq, KV_WIDTH), pe, bd, TILE)
        tiles, cmp_tab = _prompt_bias_tables(rel_table)
        o_att = _nsa_prompt(q.reshape(bp, seq, ATT_WIDTH), gates.reshape(bp, seq, N_KV_HEADS * TILE), kvc,
                            skv_b.reshape(bp, seq, KV_WIDTH), wkv_b.reshape(bp, seq, KV_WIDTH), tiles, cmp_tab)
        cl = min(CHUNK, seq)
        o_ssd, h_new, conv_new = _ssd(
            z.reshape(bp, seq, SSD_INNER), xbc.reshape(bp, seq, CONV_DIM), dt.reshape(bp, seq, TILE),
            dt_t, jnp.zeros((bp, SUBLANES, CONV_DIM), F32),
            jnp.zeros((bp, SSD_HEADS, SSD_HEAD_DIM, SSD_STATE), F32), ssd_consts, cl)
        yp = _mixout_ffn(yp, o_att.reshape(bp * seq, ATT_WIDTH), o_ssd.reshape(bp * seq, SSD_INNER), att_norm_l,
                         mix_post_l, wo_att, wo_ssd, *ffn2)

        def token_major(a_t):
            return a_t.reshape((bp,) + kv_shape + (a_t.shape[-1],)).transpose(0, 4, 1, 2, 3)

        outs[0].append(token_major(ckv_t))
        outs[1].append(token_major(skv_t))
        outs[2].append(token_major(wkv_t[:, :, -min(WINDOW, seq):]))
        outs[3].append(h_new)
        outs[4].append(conv_new)

        ys = _ffn(ys, *ffn1)
        q, ckv, _, _, gates, z, xbc, dt, dt_t, skv, wkv = _inproj(ys, mix_pre_l, w_proj, 1, False)
        native = (0, 2, 3, 4, 1)
        kc_past = _compress_paged(cache_cmp_kv[l].transpose(native), page_table, cmp_pe[l], w_cmp[l])
        cmp_new = jnp.pad(ckv.reshape(bs, dec_seq, KV_WIDTH), ((0, 0), (0, BLK - dec_seq), (0, 0)))
        kc_new = _compress(cmp_new.reshape(1, bs * BLK, KV_WIDTH), pe, bd, bs).reshape(bs, 1, KV_WIDTH)
        n_past = kc_past.shape[1]
        n_lanes = -(-(n_past + 1) // TILE) * TILE
        kc = jnp.concatenate([kc_past, kc_new, jnp.zeros((bs, n_lanes - n_past - 1, KV_WIDTH), BF16)], axis=1)
        ctab, last_tab, new_tab, win0_tab, gsum = _sample_tables(rel_table, dec_seq, past_len, n_lanes)
        q_rows = q.reshape(bs, dec_seq, N_KV_HEADS, GQA, HEAD_DIM).transpose(0, 2, 3, 1, 4)
        own_head = jnp.eye(N_KV_HEADS, dtype=F32)[None, :, None, None, :, None]
        q_bd = q_rows[:, :, :, :, None, :] * own_head
        qk = q_bd.reshape(bs, TILE, N_KV_HEADS * HEAD_DIM).astype(BF16)
        g_rows = gates.reshape(bs, dec_seq, N_KV_HEADS, TILE)[..., :GQA * N_BRANCH]
        g_rows = g_rows.reshape(bs, dec_seq, N_KV_HEADS, GQA, N_BRANCH).transpose(0, 2, 3, 1, 4)
        gates_c = jnp.pad(g_rows.reshape(bs, TILE, N_BRANCH), ((0, 0), (0, 0), (0, TILE - N_BRANCH)))

        def new_t(a):
            a = a.reshape(bs, dec_seq, N_KV_HEADS, 2, HEAD_DIM).transpose(0, 3, 2, 4, 1)
            a = jnp.pad(a.reshape(bs, 2, N_KV_HEADS * HEAD_DIM, dec_seq), ((0, 0),) * 3 + ((0, TILE - dec_seq),))
            return a[:, 0].astype(BF16), a[:, 1].astype(BF16)

        k_new, v_new = new_t(skv)
        kw_new, vw_new = new_t(wkv)
        o_rows = _nsa_sample(page_table, cache_slc_kv[l].transpose(native), qk, gates_c, kc, ctab, gsum,
                             k_new, v_new, cache_win_kv[l].transpose(native), kw_new, vw_new, last_tab, new_tab,
                             win0_tab, dec_seq=dec_seq, past_len=past_len)
        o_att = o_rows.reshape(bs, N_KV_HEADS, GQA, dec_seq, HEAD_DIM).transpose(0, 3, 1, 2, 4)
        o_att = o_att.reshape(bs * dec_seq, ATT_WIDTH)
        conv_prev = jnp.pad(state_conv[l], ((0, 0), (SUBLANES - (CONV_W - 1), 0), (0, 0)))
        o_ssd, h_new, conv_new = _ssd(
            z.reshape(bs, dec_seq, SSD_INNER), xbc.reshape(bs, dec_seq, CONV_DIM), dt.reshape(bs, dec_seq, TILE),
            dt_t.reshape(TILE, bs, dec_seq).transpose(1, 0, 2), conv_prev, state_ssm[l], ssd_consts,
            min(CHUNK, dec_seq))
        ys = _mixout_ffn(ys, o_att, o_ssd.reshape(bs * dec_seq, SSD_INNER), att_norm_l, mix_post_l, wo_att,
                         wo_ssd, *ffn2)
        win_all = jnp.concatenate([cache_win_kv[l], wkv.reshape((bs, dec_seq) + kv_shape)], axis=1)
        outs[5].append(ckv.reshape((bs, dec_seq) + kv_shape))
        outs[6].append(skv.reshape((bs, dec_seq) + kv_shape))
        outs[7].append(win_all[:, -min(WINDOW, w_buf + dec_seq):])
        outs[8].append(h_new)
        outs[9].append(conv_new)

    return (yp.reshape(bp, seq, d), ys.reshape(bs, dec_seq, d)) + tuple(jnp.stack(o) for o in outs)
```

```python
import functools
import math

import jax
import jax.numpy as jnp
from jax import lax
from jax.experimental import pallas as pl
from jax.experimental.pallas import tpu as pltpu

F32 = jnp.float32
BF16 = jnp.bfloat16

D_MODEL = 1024
N_ATT_HEADS = 16
HEAD_DIM = 64
N_KV_HEADS = 4
GQA = N_ATT_HEADS // N_KV_HEADS
ATT_WIDTH = N_ATT_HEADS * HEAD_DIM
KV_WIDTH = 2 * N_KV_HEADS * HEAD_DIM
KV_GROUP = 2 * HEAD_DIM
BLK = 64
N_SEL = 16
WINDOW = 512
N_BRANCH = 3
N_BUCKETS = 32
MAX_DISTANCE = 128
SSD_HEADS = 16
SSD_HEAD_DIM = 64
SSD_INNER = SSD_HEADS * SSD_HEAD_DIM
SSD_GROUPS = 2
SSD_STATE = 128
CONV_W = 4
CONV_DIM = SSD_INNER + 2 * SSD_GROUPS * SSD_STATE
CHUNK = 128
PAGE_SIZE = 128
EPS = 1e-6

TILE = 128
SUBLANES = 8
MASK_VALUE = -(2.0 ** 100)
M_INIT = -1e30
LOG2E = math.log2(math.e)
VMEM_LIMIT = 56 * 1024 * 1024
PAGES_PER_STEP = 16
COMPRESS_ROWS_PER_DOT = 16
STAGE_PITCH = 72

NT_DIMS = (((1,), (1,)), ((), ()))
TN_DIMS = (((0,), (0,)), ((), ()))


def _params(n_grid_dims):
    return pltpu.CompilerParams(dimension_semantics=("arbitrary",) * n_grid_dims,
                                vmem_limit_bytes=VMEM_LIMIT)


def _rms(x, g):
    return x * lax.rsqrt(jnp.mean(x * x, axis=-1, keepdims=True) + EPS) * g


def _const_spec(shape):
    zeros = (0,) * len(shape)
    return pl.BlockSpec(shape, lambda *_: zeros, pipeline_mode=pl.Buffered(1))


def _ffn_body(x_ref, pre_ref, post_ref, wg_ref, wu_ref, wo_ref, o_ref):
    x = x_ref[...]
    h = _rms(x, pre_ref[...]).astype(BF16)
    gate = jnp.dot(h, wg_ref[...], preferred_element_type=F32)
    up = jnp.dot(h, wu_ref[...], preferred_element_type=F32)
    a = (gate * jax.nn.sigmoid(gate) * up).astype(BF16)
    y = jnp.dot(a, wo_ref[...], preferred_element_type=F32)
    o_ref[...] = x + 0.5 * _rms(y, post_ref[...])


def _ffn(x, pre, post, wg, wu, wo):
    t, d = x.shape
    f = wg.shape[1]
    tm = min(512, t)
    row = pl.BlockSpec((tm, d), lambda i: (i, 0))
    return pl.pallas_call(
        _ffn_body, grid=(t // tm,),
        in_specs=[row, _const_spec((1, d)), _const_spec((1, d)), _const_spec((d, f)), _const_spec((d, f)),
                  _const_spec((f, d))],
        out_specs=row, out_shape=jax.ShapeDtypeStruct((t, d), F32),
        compiler_params=_params(1), name="ffn")(x, pre, post, wg, wu, wo)


_SEG = {}
_off = 0
for _name, _w in (("q", ATT_WIDTH), ("ckv", KV_WIDTH), ("skv", KV_WIDTH), ("wkv", KV_WIDTH),
                  ("gates", N_KV_HEADS * TILE), ("z", SSD_INNER), ("xbc", CONV_DIM), ("dt", TILE)):
    _SEG[_name] = (_off, _off + _w)
    _off += _w
PROJ_WIDTH = _off


def _inproj_body(x_ref, pre_ref, w_ref, q_ref, ckv_ref, skvb_ref, wkvb_ref, gates_ref, z_ref, xbc_ref,
                 dt_ref, dtt_ref, *kv_refs, feature_major_kv):
    h = _rms(x_ref[...], pre_ref[...]).astype(BF16)

    def proj(name):
        lo, hi = _SEG[name]
        return jnp.dot(h, w_ref[:, lo:hi], preferred_element_type=F32)

    q_ref[...] = proj("q")
    ckv = proj("ckv")
    ckv_ref[...] = ckv
    skv = proj("skv")
    skvb_ref[...] = skv.astype(BF16)
    wkv = proj("wkv")
    wkvb_ref[...] = wkv.astype(BF16)
    gates_ref[...] = proj("gates")
    z_ref[...] = proj("z")
    xbc_ref[...] = proj("xbc")
    dt = proj("dt")
    dt_ref[...] = dt
    dtt_ref[...] = dt.T
    if feature_major_kv:
        for ref, rows in zip(kv_refs, (ckv, skv, wkv)):
            ref[...] = rows.T
    else:
        kv_refs[0][...] = skv
        kv_refs[1][...] = wkv


def _inproj(x, pre, w, n_seq, feature_major_kv):
    t, d = x.shape
    tm = min(256, t)
    seq = t // n_seq
    per_seq = seq // tm
    assert seq % tm == 0

    def cols(height):
        return pl.BlockSpec((None, height, tm), lambda i: (i // per_seq, 0, i % per_seq))

    def rows(width):
        return pl.BlockSpec((tm, width), lambda i: (i, 0))

    widths = (ATT_WIDTH, KV_WIDTH, KV_WIDTH, KV_WIDTH, N_KV_HEADS * TILE, SSD_INNER, CONV_DIM, TILE)
    dtypes = (F32, F32, BF16, BF16, F32, F32, F32, F32)
    out_shape = [jax.ShapeDtypeStruct((t, wd), dt) for wd, dt in zip(widths, dtypes)]
    out_specs = [rows(wd) for wd in widths]
    out_shape.append(jax.ShapeDtypeStruct((n_seq, TILE, seq), F32))
    out_specs.append(cols(TILE))
    if feature_major_kv:
        out_shape += [jax.ShapeDtypeStruct((n_seq, KV_WIDTH, seq), F32)] * 3
        out_specs += [cols(KV_WIDTH)] * 3
    else:
        out_shape += [jax.ShapeDtypeStruct((t, KV_WIDTH), F32)] * 2
        out_specs += [rows(KV_WIDTH)] * 2
    return pl.pallas_call(
        functools.partial(_inproj_body, feature_major_kv=feature_major_kv), grid=(t // tm,),
        in_specs=[rows(d), _const_spec((1, d)), _const_spec((d, PROJ_WIDTH))],
        out_specs=out_specs, out_shape=out_shape,
        compiler_params=_params(1), name="inproj")(x, pre, w)


def _mixout_body(x_ref, oa_ref, os_ref, an_ref, post_ref, wa_ref, ws_ref, o_ref):
    a = _rms(oa_ref[...], an_ref[...]).astype(BF16)
    y = jnp.dot(a, wa_ref[...], preferred_element_type=F32)
    y = y + jnp.dot(os_ref[...], ws_ref[...], preferred_element_type=F32)
    o_ref[...] = x_ref[...] + _rms(y, post_ref[...])


def _mixout_ffn_body(x_ref, oa_ref, os_ref, an_ref, mpost_ref, wa_ref, ws_ref, pre_ref, post_ref, wg_ref, wu_ref,
                     wo_ref, o_ref):
    _mixout_body(x_ref, oa_ref, os_ref, an_ref, mpost_ref, wa_ref, ws_ref, o_ref)
    _ffn_body(o_ref, pre_ref, post_ref, wg_ref, wu_ref, wo_ref, o_ref)


def _mixout_ffn(x, o_att, o_ssd, att_norm, mix_post, wa, ws, pre, post, wg, wu, wo):
    t, d = x.shape
    f = wg.shape[1]
    tm = min(512, t)
    row = pl.BlockSpec((tm, d), lambda i: (i, 0))
    vec = _const_spec((1, d))
    return pl.pallas_call(
        _mixout_ffn_body, grid=(t // tm,),
        in_specs=[row, row, row, vec, vec, _const_spec((d, d)), _const_spec((d, d)), vec, vec,
                  _const_spec((d, f)), _const_spec((d, f)), _const_spec((f, d))],
        out_specs=row, out_shape=jax.ShapeDtypeStruct((t, d), F32),
        compiler_params=_params(1), name="mixout_ffn")(x, o_att, o_ssd, att_norm, mix_post, wa, ws, pre, post, wg,
                                                       wu, wo)


def _token_rows(ref, k, lo=0, n=None):
    n = ref.shape[-1] if n is None else n
    return ref[k, :, :, lo:lo + n].reshape(KV_GROUP, n).T


def _compress_rows(x_refs, pe_ref, bd_ref, n_blocks, pitch=BLK):
    acc = None
    for l0 in range(0, BLK, COMPRESS_ROWS_PER_DOT):
        pieces = []
        for l in range(l0, l0 + COMPRESS_ROWS_PER_DOT):
            pe = pe_ref[l:l + 1, :]
            xs = jnp.concatenate([x_ref[pl.ds(l, n_blocks, stride=pitch), :] + pe for x_ref in x_refs], axis=0)
            pieces.append(xs.astype(BF16))
        w = bd_ref[l0:l0 + COMPRESS_ROWS_PER_DOT].reshape(COMPRESS_ROWS_PER_DOT * KV_GROUP, KV_GROUP)
        part = jnp.dot(jnp.concatenate(pieces, axis=1), w, preferred_element_type=F32)
        acc = part if acc is None else acc + part
    return acc


def _store_summaries(acc, o_ref, n_blocks):
    for k in range(N_KV_HEADS):
        o_ref[0:n_blocks, k * KV_GROUP:(k + 1) * KV_GROUP] = acc[k * n_blocks:(k + 1) * n_blocks].astype(BF16)


def _compress_body(*refs, n_blocks, n_out):
    x_refs = refs[:N_KV_HEADS]
    pe_ref, bd_ref, o_ref = refs[N_KV_HEADS:]
    _store_summaries(_compress_rows(x_refs, pe_ref, bd_ref, n_blocks), o_ref, n_blocks)
    if n_out > n_blocks:
        o_ref[n_blocks:n_out, :] = jnp.zeros((n_out - n_blocks, KV_WIDTH), BF16)


def _compress(kv, pe, bd, n_out):
    b, s, _ = kv.shape
    n_blocks = s // BLK

    def head_spec(k):
        return pl.BlockSpec((None, s, KV_GROUP), lambda i: (i, 0, k))

    return pl.pallas_call(
        functools.partial(_compress_body, n_blocks=n_blocks, n_out=n_out), grid=(b,),
        in_specs=[head_spec(k) for k in range(N_KV_HEADS)]
        + [_const_spec((BLK, KV_GROUP)), _const_spec((BLK, KV_GROUP, KV_GROUP))],
        out_specs=pl.BlockSpec((None, n_out, KV_WIDTH), lambda i: (i, 0, 0)),
        out_shape=jax.ShapeDtypeStruct((b, n_out, KV_WIDTH), BF16),
        compiler_params=_params(1), name="compress")(*([kv] * N_KV_HEADS), pe, bd)


def _compress_paged_body(pt_ref, *refs, n_pages):
    page_refs = refs[:PAGES_PER_STEP]
    pe_ref, bd_ref, o_ref, stage_ref = refs[PAGES_PER_STEP:]
    j = pl.program_id(1)
    blocks_per_page = PAGE_SIZE // BLK
    for p in range(PAGES_PER_STEP):
        first_block = (j * PAGES_PER_STEP + p) * blocks_per_page
        for k in range(N_KV_HEADS):
            rows = _token_rows(page_refs[p], k)
            for h in range(blocks_per_page):
                at = pl.multiple_of((first_block + h) * STAGE_PITCH, SUBLANES)
                stage_ref[k, pl.ds(at, BLK), :] = rows[h * BLK:(h + 1) * BLK]

    @pl.when(j == pl.num_programs(1) - 1)
    def _():
        n_blocks = n_pages * blocks_per_page
        x_refs = [stage_ref.at[k] for k in range(N_KV_HEADS)]
        _store_summaries(_compress_rows(x_refs, pe_ref, bd_ref, n_blocks, STAGE_PITCH), o_ref, n_blocks)


def _compress_paged(cache, page_table, pe, bd):
    b, n_pages = page_table.shape
    steps = n_pages // PAGES_PER_STEP
    n_blocks = n_pages * (PAGE_SIZE // BLK)

    def page_spec(p):
        return pl.BlockSpec((None, N_KV_HEADS, 2, HEAD_DIM, PAGE_SIZE),
                            lambda i, j, pt: (pt[i, j * PAGES_PER_STEP + p], 0, 0, 0, 0))

    grid_spec = pltpu.PrefetchScalarGridSpec(
        num_scalar_prefetch=1, grid=(b, steps),
        in_specs=[page_spec(p) for p in range(PAGES_PER_STEP)]
        + [pl.BlockSpec((BLK, KV_GROUP), lambda i, j, pt: (0, 0)),
           pl.BlockSpec((BLK, KV_GROUP, KV_GROUP), lambda i, j, pt: (0, 0, 0))],
        out_specs=pl.BlockSpec((None, n_blocks, KV_WIDTH), lambda i, j, pt: (i, 0, 0)),
        scratch_shapes=[pltpu.VMEM((N_KV_HEADS, n_blocks * STAGE_PITCH, KV_GROUP), F32)])
    return pl.pallas_call(
        functools.partial(_compress_paged_body, n_pages=n_pages), grid_spec=grid_spec,
        out_shape=jax.ShapeDtypeStruct((b, n_blocks, KV_WIDTH), BF16),
        compiler_params=_params(2), name="compress_paged")(page_table, *([cache] * PAGES_PER_STEP), pe, bd)


def _rel_bucket(dist):
    n = jnp.maximum(dist, 0)
    max_exact = N_BUCKETS // 2
    nf = jnp.maximum(n, 1).astype(F32)
    log_b = max_exact + (jnp.log(nf / max_exact) / math.log(MAX_DISTANCE / max_exact)
                         * (N_BUCKETS - max_exact)).astype(jnp.int32)
    return jnp.where(n < max_exact, n, jnp.minimum(log_b, N_BUCKETS - 1))


def _bias_of(rel_table, dist, visible):
    onehot = (_rel_bucket(dist)[..., None] == jnp.arange(N_BUCKETS, dtype=jnp.int32)).astype(F32)
    bias = jnp.einsum('rcb,bh->hrc', onehot, rel_table.astype(F32) * LOG2E, precision=lax.Precision.HIGHEST)
    return jnp.where(visible[None], bias, MASK_VALUE).reshape(-1, dist.shape[-1])


KIND_ZERO, KIND_DIAG, KIND_PREV, KIND_MASKED, KIND_OLDEST = range(5)


def _prompt_bias_tables(rel_table):
    r = jnp.arange(TILE, dtype=jnp.int32)[:, None]
    c = jnp.arange(TILE, dtype=jnp.int32)[None, :]
    true = jnp.ones((TILE, TILE), bool)
    far = _bias_of(rel_table, jnp.full((TILE, TILE), MAX_DISTANCE, jnp.int32), true)
    tiles = jnp.stack([
        jnp.zeros_like(far),
        _bias_of(rel_table, r - c, r >= c) - far,
        _bias_of(rel_table, TILE + r - c, true) - far,
        jnp.full_like(far, MASK_VALUE),
        jnp.where(jnp.tile(r < c, (N_ATT_HEADS, 1)), 0.0, MASK_VALUE),
    ])
    c_dist = r - (BLK - 1) + BLK * (BLK - 1 - c)
    cmp_tab = _bias_of(rel_table, c_dist, c_dist >= 0)
    return tiles, cmp_tab


def _selection_mask(score, n_rows):
    n_groups = score.shape[0] // SUBLANES
    groups = [score[SUBLANES * j:SUBLANES * (j + 1)] for j in range(n_groups)]
    ranks = [jnp.zeros(groups[0].shape, jnp.int32) for _ in range(n_groups)]
    row8 = lax.broadcasted_iota(jnp.int32, groups[0].shape, 0)
    for n in range(n_rows):
        other = jnp.broadcast_to(score[n:n + 1, :], groups[0].shape)
        for j in range(n_groups):
            lo = SUBLANES * j
            if lo > n:
                inc = jnp.where(other >= groups[j], 1, 0)
            elif lo + SUBLANES - 1 <= n:
                inc = jnp.where(other > groups[j], 1, 0)
            else:
                inc = jnp.where(row8 + lo > n, jnp.where(other >= groups[j], 1, 0),
                                jnp.where(other > groups[j], 1, 0))
            ranks[j] = ranks[j] + inc
    rank = jnp.concatenate(ranks, axis=0)
    return jnp.where(rank < N_SEL, jnp.where(score > -jnp.inf, 0.0, MASK_VALUE), MASK_VALUE)


PAIR = 2 * TILE
Q_TILES = PAIR // TILE
SEL_UNROLL = 4
_SEL_KINDS = {0: ((KIND_DIAG, KIND_MASKED), (KIND_PREV, KIND_DIAG)),
              1: ((KIND_ZERO, KIND_PREV), (KIND_ZERO, KIND_ZERO))}
_WIN_OLDEST = ((KIND_OLDEST, KIND_ZERO), (KIND_MASKED, KIND_OLDEST))


def _nsa_prompt_body(q_ref, gates_ref, kvc_ref, skv_ref, wkv_ref, tb_ref, ctab_ref, o_ref,
                     s_ref, sw_ref, mx_ref, mb_ref, acc_ref, part_ref, *, n_blocks):
    step = pl.program_id(2)
    head_rows = GQA * TILE
    rows = Q_TILES * head_rows
    lane = lax.broadcasted_iota(jnp.int32, (rows, TILE), 1)
    low = lane < HEAD_DIM
    key_lane = lax.broadcasted_iota(jnp.int32, (PAIR, TILE), 1)
    key_blk = lax.broadcasted_iota(jnp.int32, (PAIR, TILE), 0) // BLK
    n_pairs = skv_ref.shape[0] // PAIR

    qf = q_ref[...]
    halves = [qf[:, c * TILE:(c + 1) * TILE] for c in range(GQA // 2)]
    rolled = [pltpu.roll(h, HEAD_DIM, 1) for h in halves]
    q_all = jnp.concatenate([(halves if g % 2 == 0 else rolled)[g // 2][t * TILE:(t + 1) * TILE]
                             for t in range(Q_TILES) for g in range(GQA)], axis=0)
    lhs_plain = jnp.where(low, q_all, 0.0).astype(BF16)

    def ones_and_values(kv):
        kv_lane = lax.broadcasted_iota(jnp.int32, kv.shape, 1)
        return jnp.where(kv_lane < HEAD_DIM, jnp.ones_like(kv), kv)

    def normalise(acc):
        return acc / jnp.maximum(pltpu.roll(acc, HEAD_DIM, 1), 1e-30)

    def bias_of(kinds_per_tile, valid):
        parts = []
        for left, right in kinds_per_tile:
            left = jnp.where(valid, left, KIND_MASKED)
            right = jnp.where(valid, right, KIND_MASKED)
            parts.append(jnp.concatenate([tb_ref[left], tb_ref[right]], axis=1))
        return jnp.concatenate(parts, axis=0)

    def sel_kinds(j):
        behind = step - j
        return [tuple(jnp.where(behind == 0, _SEL_KINDS[0][t][side],
                                jnp.where(behind == 1, _SEL_KINDS[1][t][side], KIND_ZERO)) for side in range(2))
                for t in range(Q_TILES)]

    def keys_of(kv_ref, j):
        jc = jnp.clip(j, 0, n_pairs - 1)
        return kv_ref[pl.ds(pl.multiple_of(jc * PAIR, PAIR), PAIR), :]

    def score_pair(lhs, kv_ref, j, bias, block_columns):
        rhs = keys_of(kv_ref, j)
        if block_columns:
            onehot = jnp.where(key_lane - HEAD_DIM == (PAIR // BLK) * j + key_blk, 1.0, 0.0).astype(BF16)
            rhs = jnp.where(key_lane < HEAD_DIM, rhs, onehot)
        return lax.dot_general(lhs, rhs, NT_DIMS, preferred_element_type=F32) + bias

    def half_max(sc):
        return jnp.maximum(sc[:, :TILE], sc[:, TILE:])

    def weigh_pair(sc, mb, kv_ref, j):
        pr = jnp.exp2(sc - jnp.concatenate([mb, mb], axis=1))
        return jnp.dot(pr.astype(BF16), ones_and_values(keys_of(kv_ref, j)), preferred_element_type=F32)

    def row_max(mx):
        return jnp.broadcast_to(jnp.max(mx, axis=1, keepdims=True), (rows, TILE))

    n_win = WINDOW // PAIR + 1
    w_pairs = [step - (n_win - 1) + w for w in range(n_win)]
    w_kinds = [_WIN_OLDEST] + [[(KIND_ZERO, KIND_ZERO)] * Q_TILES] * (n_win - 3) + [_SEL_KINDS[1], _SEL_KINDS[0]]
    w_mx = None
    for w, j in enumerate(w_pairs):
        sc = score_pair(lhs_plain, wkv_ref, j, bias_of(w_kinds[w], j >= 0), False)
        sw_ref[w] = sc
        w_mx = half_max(sc) if w_mx is None else jnp.maximum(w_mx, half_max(sc))

    kvc = kvc_ref[...]
    bias = []
    for t in range(Q_TILES):
        i = Q_TILES * step + t
        shift = (2 * i + (TILE - (BLK - 1))) % TILE
        bias += [pltpu.roll(ctab_ref[g * TILE:(g + 1) * TILE, :], shift, 1) for g in range(GQA)]
    s = lax.dot_general(lhs_plain, kvc, NT_DIMS, preferred_element_type=F32)
    s = jnp.where(lane < n_blocks, s + jnp.concatenate(bias, axis=0), MASK_VALUE)
    m = jnp.maximum(jnp.max(s, axis=1, keepdims=True), M_INIT)
    p = jnp.exp2(s - m)
    c_acc = jnp.dot(p.astype(BF16), ones_and_values(kvc), preferred_element_type=F32)
    p = p / jnp.maximum(c_acc, 1e-30)
    o_cmp = normalise(c_acc)

    w_mb = row_max(jnp.maximum(w_mx, M_INIT))
    w_acc = None
    for w, j in enumerate(w_pairs):
        dacc = weigh_pair(sw_ref[w], w_mb, wkv_ref, j)
        w_acc = dacc if w_acc is None else w_acc + dacc
    o_win = normalise(w_acc)

    gate = jax.nn.sigmoid(gates_ref[...])

    def gate_col(t, g, br):
        c = g * N_BRANCH + br
        return gate[t * TILE:(t + 1) * TILE, c:c + 1]

    heads = [(t, g, slice(t * head_rows + g * TILE, t * head_rows + (g + 1) * TILE))
             for t in range(Q_TILES) for g in range(GQA)]
    for t, g, head in heads:
        part_ref[head, :] = gate_col(t, g, 0) * o_cmp[head] + gate_col(t, g, 2) * o_win[head]

    blk = lax.broadcasted_iota(jnp.int32, (n_blocks, PAIR), 0)
    query = lax.broadcasted_iota(jnp.int32, (n_blocks, PAIR), 1)
    cur = (PAIR // BLK) * step + query // BLK
    imp_t = []
    for t in range(Q_TILES):
        imp = p[t * head_rows:t * head_rows + TILE]
        for g in range(1, GQA):
            imp = imp + p[t * head_rows + g * TILE:t * head_rows + (g + 1) * TILE]
        imp_t.append(imp.T[0:n_blocks])
    forced = (blk == 0) | (blk == cur) | (blk == cur - 1)
    score = jnp.where(forced, jnp.inf, jnp.where(blk <= cur, jnp.concatenate(imp_t, axis=1), -jnp.inf))
    neg_t = _selection_mask(score, n_blocks)
    neg = []
    for t in range(Q_TILES):
        pieces = [jnp.zeros((HEAD_DIM, TILE), F32), neg_t[:, t * TILE:(t + 1) * TILE]]
        if n_blocks < TILE - HEAD_DIM:
            pieces.append(jnp.zeros((TILE - HEAD_DIM - n_blocks, TILE), F32))
        neg += [jnp.concatenate(pieces, axis=0).T] * GQA
    lhs_sel = jnp.where(low, q_all, jnp.concatenate(neg, axis=0)).astype(BF16)

    full_trips = (step + 2) // SEL_UNROLL
    done = full_trips * SEL_UNROLL
    rest = jnp.maximum(step + 1 - done, 0)
    mx_ref[...] = jnp.full(mx_ref.shape, M_INIT, F32)

    def score_trips(unroll, first, n_trips):
        def trip(t, carry):
            mx = None
            for u in range(unroll):
                j = first + t * unroll + u
                sc = score_pair(lhs_sel, skv_ref, j, bias_of(sel_kinds(j), j <= step), True)
                s_ref[j] = sc
                mx = half_max(sc) if mx is None else jnp.maximum(mx, half_max(sc))
            mx_ref[...] = jnp.maximum(mx_ref[...], mx)
            return carry
        lax.fori_loop(0, n_trips, trip, 0)

    score_trips(SEL_UNROLL, 0, full_trips)
    score_trips(1, done, rest)
    mb_ref[...] = row_max(mx_ref[...])
    acc_ref[...] = jnp.zeros(acc_ref.shape, F32)

    def weigh_trips(unroll, first, n_trips):
        def trip(t, carry):
            mb = mb_ref[...]
            acc = None
            for u in range(unroll):
                j = first + t * unroll + u
                dacc = weigh_pair(s_ref[j], mb, skv_ref, j)
                acc = dacc if acc is None else acc + dacc
            acc_ref[...] += acc
            return carry
        lax.fori_loop(0, n_trips, trip, 0)

    weigh_trips(SEL_UNROLL, 0, full_trips)
    weigh_trips(1, done, rest)
    o_sel = normalise(acc_ref[...])

    out_low = lax.broadcasted_iota(jnp.int32, (TILE, TILE), 1) < HEAD_DIM
    outs = {(t, g): part_ref[head, :] + gate_col(t, g, 1) * o_sel[head] for t, g, head in heads}
    for t in range(Q_TILES):
        for c in range(GQA // 2):
            o_ref[t * TILE:(t + 1) * TILE, c * TILE:(c + 1) * TILE] = jnp.where(
                out_low, pltpu.roll(outs[t, 2 * c], HEAD_DIM, 1), outs[t, 2 * c + 1])


def _nsa_prompt(q, gates, kvc, skv_b, wkv_b, tiles, cmp_tab):
    b, s, _ = q.shape
    n_blocks = s // BLK
    assert n_blocks <= TILE - HEAD_DIM and kvc.shape[1] == TILE and s % PAIR == 0 and WINDOW // PAIR >= 2
    gw = GQA * HEAD_DIM
    head_rows = GQA * TILE
    rows = Q_TILES * head_rows
    return pl.pallas_call(
        functools.partial(_nsa_prompt_body, n_blocks=n_blocks), grid=(b, N_KV_HEADS, s // PAIR),
        in_specs=[pl.BlockSpec((None, PAIR, gw), lambda bi, k, i: (bi, i, k)),
                  pl.BlockSpec((None, PAIR, TILE), lambda bi, k, i: (bi, i, k)),
                  pl.BlockSpec((None, TILE, KV_GROUP), lambda bi, k, i: (bi, 0, k)),
                  pl.BlockSpec((None, s, KV_GROUP), lambda bi, k, i: (bi, 0, k)),
                  pl.BlockSpec((None, s, KV_GROUP), lambda bi, k, i: (bi, 0, k)),
                  pl.BlockSpec((len(tiles), head_rows, TILE), lambda bi, k, i: (0, k, 0)),
                  pl.BlockSpec((head_rows, TILE), lambda bi, k, i: (k, 0))],
        out_specs=pl.BlockSpec((None, PAIR, gw), lambda bi, k, i: (bi, i, k)),
        out_shape=jax.ShapeDtypeStruct((b, s, ATT_WIDTH), F32),
        scratch_shapes=[pltpu.VMEM((s // PAIR, rows, PAIR), F32),
                        pltpu.VMEM((WINDOW // PAIR + 1, rows, PAIR), F32),
                        pltpu.VMEM((rows, TILE), F32), pltpu.VMEM((rows, TILE), F32),
                        pltpu.VMEM((rows, KV_GROUP), F32), pltpu.VMEM((rows, TILE), F32)],
        compiler_params=_params(3), name="nsa_prompt")(q, gates, kvc, skv_b, wkv_b, tiles, cmp_tab)


def _nsa_sample_body(pt_ref, *refs, n_pages, dec_seq, past_len):
    page_refs = refs[:PAGES_PER_STEP]
    (qk_ref, gates_ref, kc_ref, ctab_ref, gsum_ref, knew_ref, vnew_ref, cwin_ref, kwnew_ref, vwnew_ref,
     last_ref, new_ref, win0_ref, o_ref, lhs_ref, m_ref, l_ref, acc_ref, ow_ref, oc_ref) = refs[PAGES_PER_STEP:]
    j = pl.program_id(1)
    last_step = pl.num_programs(1) - 1
    blocks_per_page = PAGE_SIZE // BLK
    n_past = n_pages * blocks_per_page
    kd = N_KV_HEADS * HEAD_DIM
    n_lanes = kc_ref.shape[0]
    step_keys = PAGES_PER_STEP * PAGE_SIZE

    def gate_col(br):
        return jax.nn.sigmoid(gates_ref[:, br:br + 1])

    def head_major(ref, c, lo=0, n=None):
        n = ref.shape[-1] if n is None else n
        return ref[:, c, :, lo:lo + n].reshape(kd, n).astype(BF16)

    def lane_tiles(x):
        return [x[:, t * TILE:(t + 1) * TILE] for t in range(x.shape[1] // TILE)]

    def tile_max(tiles):
        m = tiles[0]
        for t in tiles[1:]:
            m = jnp.maximum(m, t)
        return jnp.broadcast_to(jnp.max(m, axis=1, keepdims=True), (TILE, TILE))

    def tile_sum(tiles):
        s = tiles[0]
        for t in tiles[1:]:
            s = s + t
        return s

    def online_update(s, values_t):
        tiles = lane_tiles(s)
        m_prev = m_ref[...]
        m_new = jnp.maximum(m_prev, tile_max(tiles))
        alpha = jnp.exp2(m_prev - m_new)
        p = [jnp.exp2(t - m_new) for t in tiles]
        l_ref[...] = alpha * l_ref[...] + tile_sum(p)
        pv = lax.dot_general(jnp.concatenate(p, axis=1).astype(BF16), values_t, NT_DIMS, preferred_element_type=F32)
        acc_ref[...] = jnp.concatenate([alpha] * (kd // TILE), axis=1) * acc_ref[...] + pv
        m_ref[...] = m_new

    @pl.when(j == 0)
    def _():
        kc = kc_ref[...]
        qf = qk_ref[...].astype(F32)
        gap = jnp.zeros((TILE, HEAD_DIM), F32)
        qc = jnp.concatenate([piece for k in range(N_KV_HEADS)
                              for piece in (qf[:, k * HEAD_DIM:(k + 1) * HEAD_DIM], gap)], axis=1).astype(BF16)
        s = lax.dot_general(qc, kc, NT_DIMS, preferred_element_type=F32) + ctab_ref[...]
        m = jnp.maximum(jnp.max(s, axis=1, keepdims=True), M_INIT)
        p = jnp.exp2(s - m)
        p = p / jnp.maximum(jnp.sum(p, axis=1, keepdims=True), 1e-30)
        oc_ref[...] = gate_col(0) * jnp.dot(p.astype(BF16), kc, preferred_element_type=F32)
        imp = _select_sum(p, gsum_ref[...], False)
        imp_t = jnp.concatenate([t.T for t in lane_tiles(imp)], axis=0)
        n_rows = -(-(n_past + 1) // SUBLANES) * SUBLANES
        blk = lax.broadcasted_iota(jnp.int32, (n_rows, TILE), 0)
        tok = lax.broadcasted_iota(jnp.int32, (n_rows, TILE), 1) % dec_seq
        cur = (past_len + tok) // BLK
        forced = (blk == 0) | (blk == cur) | (blk == cur - 1)
        score = jnp.where(forced, jnp.inf, jnp.where(blk <= cur, imp_t[0:n_rows], -jnp.inf))
        neg_t = jnp.concatenate([_selection_mask(score, n_past + 1), jnp.zeros((n_lanes - n_rows, TILE), F32)], axis=0)
        neg = jnp.concatenate([neg_t[t * TILE:(t + 1) * TILE].T for t in range(n_lanes // TILE)], axis=1)
        lhs_ref[:, 0:kd] = qk_ref[...]
        lhs_ref[:, kd:kd + n_lanes] = neg.astype(BF16)

        qk = qk_ref[...]
        n_win = cwin_ref.shape[-1] // TILE
        zero = jnp.zeros((TILE, TILE), F32)
        s_w = jnp.dot(qk, head_major(cwin_ref, 0), preferred_element_type=F32)
        s_w = s_w + jnp.concatenate([win0_ref[...]] + [zero] * (n_win - 2) + [last_ref[...]], axis=1)
        s_n = jnp.dot(qk, kwnew_ref[...], preferred_element_type=F32) + new_ref[...]
        tiles = lane_tiles(s_w) + [s_n]
        m = tile_max(tiles)
        p = [jnp.exp2(t - m) for t in tiles]
        l = jnp.sum(tile_sum(p), axis=1, keepdims=True)
        o_w = lax.dot_general(jnp.concatenate(p[:-1], axis=1).astype(BF16), head_major(cwin_ref, 1), NT_DIMS,
                              preferred_element_type=F32)
        o_w = o_w + lax.dot_general(p[-1].astype(BF16), vwnew_ref[...], NT_DIMS, preferred_element_type=F32)
        ow_ref[...] = gate_col(2) * (o_w / l)

        m_ref[...] = jnp.full(m_ref.shape, M_INIT, F32)
        l_ref[...] = jnp.zeros(l_ref.shape, F32)
        acc_ref[...] = jnp.zeros(acc_ref.shape, F32)

    keys_t = jnp.concatenate([head_major(r, 0) for r in page_refs], axis=1)
    values_t = jnp.concatenate([head_major(r, 1) for r in page_refs], axis=1)
    blk_row = lax.broadcasted_iota(jnp.int32, (n_lanes, step_keys), 0)
    key_blk = j * (PAGES_PER_STEP * blocks_per_page) + lax.broadcasted_iota(jnp.int32, (n_lanes, step_keys), 1) // BLK
    onehot = jnp.where(blk_row == key_blk, 1.0, 0.0).astype(BF16)
    s = jnp.dot(lhs_ref[...], jnp.concatenate([keys_t, onehot], axis=0), preferred_element_type=F32)
    tiles = lane_tiles(s)
    tiles[-1] = tiles[-1] + jnp.where(j == last_step, last_ref[...], 0.0)
    online_update(jnp.concatenate(tiles, axis=1), values_t)

    @pl.when(j == last_step)
    def _():
        new_blk = lax.broadcasted_iota(jnp.int32, (n_lanes, TILE), 0) == n_past
        rhs = jnp.concatenate([knew_ref[...], jnp.where(new_blk, 1.0, 0.0).astype(BF16)], axis=0)
        online_update(jnp.dot(lhs_ref[...], rhs, preferred_element_type=F32) + new_ref[...], vnew_ref[...])
        l = jnp.sum(l_ref[...], axis=1, keepdims=True)
        o_sw = gate_col(1) * (acc_ref[...] / l) + ow_ref[...]
        o_c = oc_ref[...]
        rows_per_head = TILE // N_KV_HEADS
        for k in range(N_KV_HEADS):
            rows = slice(k * rows_per_head, (k + 1) * rows_per_head)
            v_lo = k * KV_GROUP + HEAD_DIM
            o_ref[rows, :] = o_sw[rows, k * HEAD_DIM:(k + 1) * HEAD_DIM] + o_c[rows, v_lo:v_lo + HEAD_DIM]


def _nsa_sample(page_table, cache_slc, qk, gates_c, kc, ctab, gsum, k_new, v_new, cache_win, kw_new, vw_new,
                last_tab, new_tab, win0_tab, *, dec_seq, past_len):
    b, n_pages = page_table.shape
    steps = n_pages // PAGES_PER_STEP
    n_lanes = kc.shape[1]
    w_buf = cache_win.shape[-1]
    kd = N_KV_HEADS * HEAD_DIM

    def page_spec(p):
        return pl.BlockSpec((None, N_KV_HEADS, 2, HEAD_DIM, PAGE_SIZE),
                            lambda i, j, pt: (pt[i, j * PAGES_PER_STEP + p], 0, 0, 0, 0))

    def per_batch(*shape):
        zeros = (0,) * len(shape)
        return pl.BlockSpec((None,) + shape, lambda i, j, pt: (i,) + zeros)

    def const(*shape):
        zeros = (0,) * len(shape)
        return pl.BlockSpec(shape, lambda i, j, pt: zeros)

    grid_spec = pltpu.PrefetchScalarGridSpec(
        num_scalar_prefetch=1, grid=(b, steps),
        in_specs=[page_spec(p) for p in range(PAGES_PER_STEP)]
        + [per_batch(TILE, kd), per_batch(TILE, TILE), per_batch(n_lanes, KV_WIDTH),
           const(TILE, n_lanes), const(TILE, TILE), per_batch(kd, TILE), per_batch(kd, TILE),
           per_batch(N_KV_HEADS, 2, HEAD_DIM, w_buf), per_batch(kd, TILE), per_batch(kd, TILE),
           const(TILE, TILE), const(TILE, TILE), const(TILE, TILE)],
        out_specs=per_batch(TILE, HEAD_DIM),
        scratch_shapes=[pltpu.VMEM((TILE, kd + n_lanes), BF16), pltpu.VMEM((TILE, TILE), F32),
                        pltpu.VMEM((TILE, TILE), F32), pltpu.VMEM((TILE, kd), F32), pltpu.VMEM((TILE, kd), F32),
                        pltpu.VMEM((TILE, KV_WIDTH), F32)])
    return pl.pallas_call(
        functools.partial(_nsa_sample_body, n_pages=n_pages, dec_seq=dec_seq, past_len=past_len),
        grid_spec=grid_spec, out_shape=jax.ShapeDtypeStruct((b, TILE, HEAD_DIM), F32),
        compiler_params=_params(2), name="nsa_sample")(
            page_table, *([cache_slc] * PAGES_PER_STEP), qk, gates_c, kc, ctab, gsum, k_new, v_new, cache_win,
            kw_new, vw_new, last_tab, new_tab, win0_tab)


def _split3(x):
    hi = x.astype(BF16)
    r = x - hi.astype(F32)
    mid = r.astype(BF16)
    return hi, mid, (r - mid.astype(F32)).astype(BF16)


def _select_sum(x, onehot, x_on_left):
    out = None
    for part in _split3(x):
        term = (jnp.dot(part, onehot, preferred_element_type=F32) if x_on_left
                else jnp.dot(onehot, part, preferred_element_type=F32))
        out = term if out is None else out + term
    return out


def _softplus(x):
    return jnp.maximum(x, 0.0) + jnp.log(1.0 + jnp.exp(-jnp.abs(x)))


def _ssd_body(z_ref, xbc_ref, dt_ref, dtt_ref, convp_ref, h0_ref, cw_ref, cb_ref, dtb_ref, dtbt_ref, al_ref,
              alt_ref, dskip_ref, norm_ref, o_ref, hout_ref, convout_ref, xp_ref, *, cl):
    c = pl.program_id(1)
    pad = SUBLANES

    @pl.when(c == 0)
    def _():
        xp_ref[0:pad, :] = convp_ref[...]
        hout_ref[...] = h0_ref[...]

    xp_ref[pad:pad + cl, :] = xbc_ref[...]
    conv = cb_ref[...]
    for k in range(CONV_W):
        lo = pad - (CONV_W - 1) + k
        conv = conv + cw_ref[k:k + 1, :] * xp_ref[lo:lo + cl, :]
    u = conv * jax.nn.sigmoid(conv)
    tail = xp_ref[pad + cl - (CONV_W - 1):pad + cl, :]
    convout_ref[...] = tail
    xp_ref[pad - (CONV_W - 1):pad, :] = tail

    xs = u[:, :SSD_INNER]
    dt = _softplus(dt_ref[:, 0:SSD_HEADS] + dtb_ref[...])
    dt_t = _softplus(dtt_ref[0:SSD_HEADS, :] + dtbt_ref[...])
    la = dt * -jnp.exp(al_ref[...])
    la_t = dt_t * -jnp.exp(alt_ref[...])
    row = lax.broadcasted_iota(jnp.int32, (cl, cl), 0)
    col = lax.broadcasted_iota(jnp.int32, (cl, cl), 1)
    causal = row >= col
    cs = _select_sum(la, jnp.where(causal, 1.0, 0.0).astype(BF16), False)
    cs_t = _select_sum(la_t, jnp.where(row <= col, 1.0, 0.0).astype(BF16), True)

    def spread(width):
        lane_head = lax.broadcasted_iota(jnp.int32, (SSD_HEADS, SSD_HEADS * width), 1) // width
        return jnp.where(lane_head == lax.broadcasted_iota(jnp.int32, lane_head.shape, 0), 1.0, 0.0).astype(BF16)

    dt_x = _select_sum(dt, spread(SSD_HEAD_DIM), True)
    cs_x = _select_sum(cs, spread(SSD_HEAD_DIM), True)
    cs_wide = _select_sum(cs, spread(TILE), True)
    last_x = cs_x[cl - 1:cl, :]
    xdt = xs * dt_x
    xdt_b = xdt.astype(BF16)
    to_end = (xdt * jnp.exp(last_x - cs_x)).astype(BF16)
    grow = jnp.exp(cs_x)
    skip = dskip_ref[...] * xs

    heads_per_group = SSD_HEADS // SSD_GROUPS
    bms, cms, cbs = [], [], []
    for g in range(SSD_GROUPS):
        b_lo = SSD_INNER + g * SSD_STATE
        c_lo = SSD_INNER + SSD_GROUPS * SSD_STATE + g * SSD_STATE
        bms.append(u[:, b_lo:b_lo + SSD_STATE].astype(BF16))
        cms.append(u[:, c_lo:c_lo + SSD_STATE].astype(BF16))
        cbs.append(lax.dot_general(cms[g], bms[g], NT_DIMS, preferred_element_type=F32))
    first_head = lax.broadcasted_iota(jnp.int32, (cl, TILE), 1) < SSD_HEAD_DIM
    state_row = lax.broadcasted_iota(jnp.int32, (2 * SSD_HEAD_DIM, SSD_STATE), 0)
    ys = []
    for pair in range(SSD_HEADS // 2):
        g = 2 * pair // heads_per_group
        lanes = slice(pair * TILE, (pair + 1) * TILE)
        y_heads, keep = [], []
        for h in (2 * pair, 2 * pair + 1):
            diff = cs_wide[:, h * TILE:h * TILE + cl] - cs_t[h:h + 1, :]
            decay = jnp.where(causal, jnp.exp(jnp.where(causal, diff, 0.0)), 0.0)
            y_heads.append(jnp.dot((cbs[g] * decay).astype(BF16), xdt_b[:, lanes], preferred_element_type=F32))
            keep.append(jnp.exp(cs[cl - 1:cl, h:h + 1]))
        state_in = lax.dot_general(to_end[:, lanes], bms[g], TN_DIMS, preferred_element_type=F32)
        h_prev = jnp.concatenate([hout_ref[2 * pair], hout_ref[2 * pair + 1]], axis=0)
        y_off = lax.dot_general(cms[g], h_prev.astype(BF16), NT_DIMS, preferred_element_type=F32)
        h_new = h_prev * jnp.where(state_row < SSD_HEAD_DIM, keep[0], keep[1]) + state_in
        hout_ref[2 * pair] = h_new[0:SSD_HEAD_DIM]
        hout_ref[2 * pair + 1] = h_new[SSD_HEAD_DIM:]
        ys.append(jnp.where(first_head, y_heads[0], y_heads[1]) + y_off * grow[:, lanes] + skip[:, lanes])
    y = jnp.concatenate(ys, axis=1)
    z = z_ref[...]
    y = y * (z * jax.nn.sigmoid(z))
    gw = SSD_INNER // SSD_GROUPS
    normed = []
    for g in range(SSD_GROUPS):
        yg = y[:, g * gw:(g + 1) * gw]
        normed.append(yg * lax.rsqrt(jnp.mean(yg * yg, axis=-1, keepdims=True) + EPS))
    o_ref[...] = (jnp.concatenate(normed, axis=1) * norm_ref[...]).astype(o_ref.dtype)


def _ssd(z, xbc, dt, dt_t, conv_prev, h0, consts, cl):
    b, seq, _ = z.shape
    nc = seq // cl
    cw, cbias, dtb, dtb_t, al, al_t, dskip, norm = consts

    def rows(width):
        return pl.BlockSpec((None, cl, width), lambda i, c: (i, c, 0))

    state = pl.BlockSpec((None, SSD_HEADS, SSD_HEAD_DIM, SSD_STATE), lambda i, c: (i, 0, 0, 0))

    def const(x):
        zeros = (0,) * x.ndim
        return pl.BlockSpec(x.shape, lambda i, c: zeros)

    return pl.pallas_call(
        functools.partial(_ssd_body, cl=cl), grid=(b, nc),
        in_specs=[rows(SSD_INNER), rows(CONV_DIM), rows(TILE),
                  pl.BlockSpec((None, TILE, cl), lambda i, c: (i, 0, c)),
                  pl.BlockSpec((None, SUBLANES, CONV_DIM), lambda i, c: (i, 0, 0)), state,
                  const(cw), const(cbias), const(dtb), const(dtb_t), const(al), const(al_t), const(dskip),
                  const(norm)],
        out_specs=[rows(SSD_INNER), state, pl.BlockSpec((None, CONV_W - 1, CONV_DIM), lambda i, c: (i, 0, 0))],
        out_shape=[jax.ShapeDtypeStruct((b, seq, SSD_INNER), BF16),
                   jax.ShapeDtypeStruct((b, SSD_HEADS, SSD_HEAD_DIM, SSD_STATE), F32),
                   jax.ShapeDtypeStruct((b, CONV_W - 1, CONV_DIM), F32)],
        scratch_shapes=[pltpu.VMEM((SUBLANES + cl, CONV_DIM), F32)],
        compiler_params=_params(2), name="ssd")(z, xbc, dt, dt_t, conv_prev, h0, cw, cbias, dtb, dtb_t, al, al_t,
                                                 dskip, norm)


def _pack_inproj(w_in):
    splits = (ATT_WIDTH, KV_WIDTH, KV_WIDTH, KV_WIDTH, N_BRANCH * N_ATT_HEADS, SSD_INNER, CONV_DIM, SSD_HEADS)
    offs = [0]
    for s in splits:
        offs.append(offs[-1] + s)
    wq, wc, ws, ww, wg, wz, wx, wd = [w_in[:, offs[i]:offs[i + 1]] for i in range(len(splits))]
    d = w_in.shape[0]
    per_kv = GQA * N_BRANCH
    wg4 = jnp.pad(wg.reshape(d, N_KV_HEADS, per_kv), ((0, 0), (0, 0), (0, TILE - per_kv)))
    wd_pad = jnp.pad(wd, ((0, 0), (0, TILE - SSD_HEADS)))
    w = jnp.concatenate([wq * (HEAD_DIM ** -0.5 * LOG2E), wc, ws, ww, wg4.reshape(d, N_KV_HEADS * TILE), wz, wx,
                         wd_pad], axis=1)
    assert w.shape[1] == PROJ_WIDTH
    return w.astype(BF16)


def _pack_compress(cmp_pe, w_cmp):
    zero = jnp.zeros((BLK, HEAD_DIM, HEAD_DIM), w_cmp.dtype)
    bd = jnp.concatenate([jnp.concatenate([w_cmp[:, 0], zero], axis=2),
                          jnp.concatenate([zero, w_cmp[:, 1]], axis=2)], axis=1)
    pe = cmp_pe.reshape(BLK, KV_GROUP)
    return pe.astype(F32), bd.astype(BF16)


def _sample_tables(rel_table, dec_seq, past_len, n_lanes):
    col = jnp.arange(TILE, dtype=jnp.int32)
    head = (col // (GQA * dec_seq)) * GQA + (col // dec_seq) % GQA
    tok = (col % dec_seq)[:, None]
    per_col = (rel_table.astype(F32) * LOG2E)[:, head]

    def bias(dist):
        onehot = (_rel_bucket(dist)[..., None] == jnp.arange(N_BUCKETS, dtype=jnp.int32)).astype(F32)
        return jnp.einsum('ckb,bc->ck', onehot, per_col, precision=lax.Precision.HIGHEST)

    lane = jnp.arange(TILE, dtype=jnp.int32)[None, :]
    far = bias(jnp.full((TILE, 1), MAX_DISTANCE, jnp.int32))
    blk = jnp.arange(n_lanes, dtype=jnp.int32)[None, :]
    c_dist = past_len + tok - ((blk + 1) * BLK - 1)
    ctab = jnp.where(c_dist >= 0, bias(c_dist), MASK_VALUE)
    last_tab = bias(TILE + tok - lane) - far
    new_tab = jnp.where((lane <= tok) & (lane < dec_seq), bias(tok - lane) - far, MASK_VALUE)
    win0_tab = jnp.where(lane > tok, 0.0, MASK_VALUE)
    same = (col[:, None] // (GQA * dec_seq) == col[None, :] // (GQA * dec_seq)) & \
           (col[:, None] % dec_seq == col[None, :] % dec_seq)
    return ctab, last_tab, new_tab, win0_tab, same.astype(BF16)


def _layer_weights(ffn_pre, ffn_post, ffn_w_in, ffn_w_out):
    f = ffn_w_out.shape[0]
    return (ffn_pre[None, :], ffn_post[None, :], ffn_w_in[:, :f].astype(BF16), ffn_w_in[:, f:].astype(BF16),
            ffn_w_out.astype(BF16))


def kernel(x_prompt, x_sample, cache_cmp_kv, cache_slc_kv, cache_win_kv, state_ssm, state_conv, page_table,
           rel_table, ffn1_pre, ffn1_post, ffn1_w_in, ffn1_w_out, mix_pre, mix_post, w_in, w_out, att_out_norm,
           cmp_pe, w_cmp, conv_w, conv_b, dt_bias, a_log, d_skip, ssd_norm,
           ffn2_pre, ffn2_post, ffn2_w_in, ffn2_w_out):
    depth = ffn1_pre.shape[0]
    bp, seq, d = x_prompt.shape
    bs, dec_seq, _ = x_sample.shape
    n_pages = page_table.shape[1]
    past_len = n_pages * PAGE_SIZE
    w_buf = cache_win_kv.shape[2]
    kv_shape = (N_KV_HEADS, 2, HEAD_DIM)
    assert GQA * dec_seq * N_KV_HEADS == TILE and w_buf == WINDOW and seq % TILE == 0

    yp = x_prompt.reshape(bp * seq, d)
    ys = x_sample.reshape(bs * dec_seq, d)
    outs = [[] for _ in range(10)]
    for l in range(depth):
        ffn1 = _layer_weights(ffn1_pre[l], ffn1_post[l], ffn1_w_in[l], ffn1_w_out[l])
        ffn2 = _layer_weights(ffn2_pre[l], ffn2_post[l], ffn2_w_in[l], ffn2_w_out[l])
        w_proj = _pack_inproj(w_in[l])
        pe, bd = _pack_compress(cmp_pe[l], w_cmp[l])
        wo_att = w_out[l][:ATT_WIDTH].astype(BF16)
        wo_ssd = w_out[l][ATT_WIDTH:].astype(BF16)
        ssd_consts = (conv_w[l], conv_b[l][None, :], dt_bias[l][None, :], dt_bias[l][:, None], a_log[l][None, :],
                      a_log[l][:, None], jnp.repeat(d_skip[l], SSD_HEAD_DIM)[None, :], ssd_norm[l][None, :])
        mix_pre_l, mix_post_l, att_norm_l = mix_pre[l][None, :], mix_post[l][None, :], att_out_norm[l][None, :]

        yp = _ffn(yp, *ffn1)
        q, ckv, skv_b, wkv_b, gates, z, xbc, dt, dt_t, ckv_t, skv_t, wkv_t = _inproj(
            yp, mix_pre_l, w_proj, bp, True)
        kvc = _compress(ckv.reshape(bp, seq, KV_WIDTH), pe, bd, TILE)
        tiles, cmp_tab = _prompt_bias_tables(rel_table)
        o_att = _nsa_prompt(q.reshape(bp, seq, ATT_WIDTH), gates.reshape(bp, seq, N_KV_HEADS * TILE), kvc,
                            skv_b.reshape(bp, seq, KV_WIDTH), wkv_b.reshape(bp, seq, KV_WIDTH), tiles, cmp_tab)
        cl = min(CHUNK, seq)
        o_ssd, h_new, conv_new = _ssd(
            z.reshape(bp, seq, SSD_INNER), xbc.reshape(bp, seq, CONV_DIM), dt.reshape(bp, seq, TILE),
            dt_t, jnp.zeros((bp, SUBLANES, CONV_DIM), F32),
            jnp.zeros((bp, SSD_HEADS, SSD_HEAD_DIM, SSD_STATE), F32), ssd_consts, cl)
        yp = _mixout_ffn(yp, o_att.reshape(bp * seq, ATT_WIDTH), o_ssd.reshape(bp * seq, SSD_INNER), att_norm_l,
                         mix_post_l, wo_att, wo_ssd, *ffn2)

        def token_major(a_t):
            return a_t.reshape((bp,) + kv_shape + (a_t.shape[-1],)).transpose(0, 4, 1, 2, 3)

        outs[0].append(token_major(ckv_t))
        outs[1].append(token_major(skv_t))
        outs[2].append(token_major(wkv_t[:, :, -min(WINDOW, seq):]))
        outs[3].append(h_new)
        outs[4].append(conv_new)

        ys = _ffn(ys, *ffn1)
        q, ckv, _, _, gates, z, xbc, dt, dt_t, skv, wkv = _inproj(ys, mix_pre_l, w_proj, 1, False)
        native = (0, 2, 3, 4, 1)
        kc_past = _compress_paged(cache_cmp_kv[l].transpose(native), page_table, pe, bd)
        cmp_new = jnp.pad(ckv.reshape(bs, dec_seq, KV_WIDTH), ((0, 0), (0, BLK - dec_seq), (0, 0)))
        kc_new = _compress(cmp_new.reshape(1, bs * BLK, KV_WIDTH), pe, bd, bs).reshape(bs, 1, KV_WIDTH)
        n_past = kc_past.shape[1]
        n_lanes = -(-(n_past + 1) // TILE) * TILE
        kc = jnp.concatenate([kc_past, kc_new, jnp.zeros((bs, n_lanes - n_past - 1, KV_WIDTH), BF16)], axis=1)
        ctab, last_tab, new_tab, win0_tab, gsum = _sample_tables(rel_table, dec_seq, past_len, n_lanes)
        q_rows = q.reshape(bs, dec_seq, N_KV_HEADS, GQA, HEAD_DIM).transpose(0, 2, 3, 1, 4)
        own_head = jnp.eye(N_KV_HEADS, dtype=F32)[None, :, None, None, :, None]
        q_bd = q_rows[:, :, :, :, None, :] * own_head
        qk = q_bd.reshape(bs, TILE, N_KV_HEADS * HEAD_DIM).astype(BF16)
        g_rows = gates.reshape(bs, dec_seq, N_KV_HEADS, TILE)[..., :GQA * N_BRANCH]
        g_rows = g_rows.reshape(bs, dec_seq, N_KV_HEADS, GQA, N_BRANCH).transpose(0, 2, 3, 1, 4)
        gates_c = jnp.pad(g_rows.reshape(bs, TILE, N_BRANCH), ((0, 0), (0, 0), (0, TILE - N_BRANCH)))

        def new_t(a):
            a = a.reshape(bs, dec_seq, N_KV_HEADS, 2, HEAD_DIM).transpose(0, 3, 2, 4, 1)
            a = jnp.pad(a.reshape(bs, 2, N_KV_HEADS * HEAD_DIM, dec_seq), ((0, 0),) * 3 + ((0, TILE - dec_seq),))
            return a[:, 0].astype(BF16), a[:, 1].astype(BF16)

        k_new, v_new = new_t(skv)
        kw_new, vw_new = new_t(wkv)
        o_rows = _nsa_sample(page_table, cache_slc_kv[l].transpose(native), qk, gates_c, kc, ctab, gsum,
                             k_new, v_new, cache_win_kv[l].transpose(native), kw_new, vw_new, last_tab, new_tab,
                             win0_tab, dec_seq=dec_seq, past_len=past_len)
        o_att = o_rows.reshape(bs, N_KV_HEADS, GQA, dec_seq, HEAD_DIM).transpose(0, 3, 1, 2, 4)
        o_att = o_att.reshape(bs * dec_seq, ATT_WIDTH)
        conv_prev = jnp.pad(state_conv[l], ((0, 0), (SUBLANES - (CONV_W - 1), 0), (0, 0)))
        o_ssd, h_new, conv_new = _ssd(
            z.reshape(bs, dec_seq, SSD_INNER), xbc.reshape(bs, dec_seq, CONV_DIM), dt.reshape(bs, dec_seq, TILE),
            dt_t.reshape(TILE, bs, dec_seq).transpose(1, 0, 2), conv_prev, state_ssm[l], ssd_consts,
            min(CHUNK, dec_seq))
        ys = _mixout_ffn(ys, o_att, o_ssd.reshape(bs * dec_seq, SSD_INNER), att_norm_l, mix_post_l, wo_att,
                         wo_ssd, *ffn2)
        win_all = jnp.concatenate([cache_win_kv[l], wkv.reshape((bs, dec_seq) + kv_shape)], axis=1)
        outs[5].append(ckv.reshape((bs, dec_seq) + kv_shape))
        outs[6].append(skv.reshape((bs, dec_seq) + kv_shape))
        outs[7].append(win_all[:, -min(WINDOW, w_buf + dec_seq):])
        outs[8].append(h_new)
        outs[9].append(conv_new)

    return (yp.reshape(bp, seq, d), ys.reshape(bs, dec_seq, d)) + tuple(jnp.stack(o) for o in outs)
```

```python
import functools
import math

import jax
import jax.numpy as jnp
from jax import lax
from jax.experimental import pallas as pl
from jax.experimental.pallas import tpu as pltpu

F32 = jnp.float32
BF16 = jnp.bfloat16

D_MODEL = 1024
N_ATT_HEADS = 16
HEAD_DIM = 64
N_KV_HEADS = 4
GQA = N_ATT_HEADS // N_KV_HEADS
ATT_WIDTH = N_ATT_HEADS * HEAD_DIM
KV_WIDTH = 2 * N_KV_HEADS * HEAD_DIM
KV_GROUP = 2 * HEAD_DIM
BLK = 64
N_SEL = 16
WINDOW = 512
N_BRANCH = 3
N_BUCKETS = 32
MAX_DISTANCE = 128
SSD_HEADS = 16
SSD_HEAD_DIM = 64
SSD_INNER = SSD_HEADS * SSD_HEAD_DIM
SSD_GROUPS = 2
SSD_STATE = 128
CONV_W = 4
CONV_DIM = SSD_INNER + 2 * SSD_GROUPS * SSD_STATE
CHUNK = 128
PAGE_SIZE = 128
EPS = 1e-6

TILE = 128
SUBLANES = 8
MASK_VALUE = -(2.0 ** 100)
M_INIT = -1e30
LOG2E = math.log2(math.e)
VMEM_LIMIT = 56 * 1024 * 1024
PAGES_PER_STEP = 32
ROW_TILE = 256
FFN_ROW_TILE = 512
COMPRESS_ROWS_PER_DOT = 16
STAGE_PITCH = 72

NT_DIMS = (((1,), (1,)), ((), ()))
TN_DIMS = (((0,), (0,)), ((), ()))


def _params(n_grid_dims):
    return pltpu.CompilerParams(dimension_semantics=("arbitrary",) * n_grid_dims,
                                vmem_limit_bytes=VMEM_LIMIT)


def _rms(x, g):
    return x * lax.rsqrt(jnp.mean(x * x, axis=-1, keepdims=True) + EPS) * g


def _const_spec(shape):
    zeros = (0,) * len(shape)
    return pl.BlockSpec(shape, lambda *_: zeros, pipeline_mode=pl.Buffered(1))


def _ffn_body(x_ref, pre_ref, post_ref, wg_ref, wu_ref, wo_ref, o_ref):
    x = x_ref[...]
    h = _rms(x, pre_ref[...]).astype(BF16)
    gate = jnp.dot(h, wg_ref[...], preferred_element_type=F32)
    up = jnp.dot(h, wu_ref[...], preferred_element_type=F32)
    a = (gate * jax.nn.sigmoid(gate) * up).astype(BF16)
    y = jnp.dot(a, wo_ref[...], preferred_element_type=F32)
    o_ref[...] = x + 0.5 * _rms(y, post_ref[...])


def _ffn(x, pre, post, wg, wu, wo):
    t, d = x.shape
    f = wg.shape[1]
    tm = min(FFN_ROW_TILE, t)
    row = pl.BlockSpec((tm, d), lambda i: (i, 0))
    return pl.pallas_call(
        _ffn_body, grid=(t // tm,),
        in_specs=[row, _const_spec((1, d)), _const_spec((1, d)), _const_spec((d, f)), _const_spec((d, f)),
                  _const_spec((f, d))],
        out_specs=row, out_shape=jax.ShapeDtypeStruct((t, d), F32),
        compiler_params=_params(1), name="ffn")(x, pre, post, wg, wu, wo)


_SEG = {}
_off = 0
for _name, _w in (("q", ATT_WIDTH), ("ckv", KV_WIDTH), ("skv", KV_WIDTH), ("wkv", KV_WIDTH),
                  ("gates", N_KV_HEADS * TILE), ("z", SSD_INNER), ("xbc", CONV_DIM), ("dt", TILE)):
    _SEG[_name] = (_off, _off + _w)
    _off += _w
PROJ_WIDTH = _off


def _inproj_body(x_ref, pre_ref, w_ref, q_ref, ckv_ref, skvb_ref, wkvb_ref, gates_ref, z_ref, xbc_ref,
                 dt_ref, dtt_ref, *kv_refs, feature_major_kv):
    h = _rms(x_ref[...], pre_ref[...]).astype(BF16)

    def proj(name):
        lo, hi = _SEG[name]
        return jnp.dot(h, w_ref[:, lo:hi], preferred_element_type=F32)

    q_ref[...] = proj("q")
    ckv = proj("ckv")
    ckv_ref[...] = ckv
    skv = proj("skv")
    skvb_ref[...] = skv.astype(BF16)
    wkv = proj("wkv")
    wkvb_ref[...] = wkv.astype(BF16)
    gates_ref[...] = proj("gates")
    z_ref[...] = proj("z")
    xbc_ref[...] = proj("xbc")
    dt = proj("dt")
    dt_ref[...] = dt
    dtt_ref[...] = dt.T
    if feature_major_kv:
        for ref, rows in zip(kv_refs, (ckv, skv, wkv)):
            ref[...] = rows.T
    else:
        kv_refs[0][...] = skv
        kv_refs[1][...] = wkv


def _inproj(x, pre, w, n_seq, feature_major_kv):
    t, d = x.shape
    tm = min(ROW_TILE, t)
    seq = t // n_seq
    per_seq = seq // tm
    assert seq % tm == 0

    def cols(height):
        return pl.BlockSpec((None, height, tm), lambda i: (i // per_seq, 0, i % per_seq))

    def rows(width):
        return pl.BlockSpec((tm, width), lambda i: (i, 0))

    widths = (ATT_WIDTH, KV_WIDTH, KV_WIDTH, KV_WIDTH, N_KV_HEADS * TILE, SSD_INNER, CONV_DIM, TILE)
    dtypes = (F32, F32, BF16, BF16, F32, F32, F32, F32)
    out_shape = [jax.ShapeDtypeStruct((t, wd), dt) for wd, dt in zip(widths, dtypes)]
    out_specs = [rows(wd) for wd in widths]
    out_shape.append(jax.ShapeDtypeStruct((n_seq, TILE, seq), F32))
    out_specs.append(cols(TILE))
    if feature_major_kv:
        out_shape += [jax.ShapeDtypeStruct((n_seq, KV_WIDTH, seq), F32)] * 3
        out_specs += [cols(KV_WIDTH)] * 3
    else:
        out_shape += [jax.ShapeDtypeStruct((t, KV_WIDTH), F32)] * 2
        out_specs += [rows(KV_WIDTH)] * 2
    return pl.pallas_call(
        functools.partial(_inproj_body, feature_major_kv=feature_major_kv), grid=(t // tm,),
        in_specs=[rows(d), _const_spec((1, d)), _const_spec((d, PROJ_WIDTH))],
        out_specs=out_specs, out_shape=out_shape,
        compiler_params=_params(1), name="inproj")(x, pre, w)


def _mixout_body(x_ref, oa_ref, os_ref, an_ref, post_ref, wa_ref, ws_ref, o_ref):
    a = _rms(oa_ref[...], an_ref[...]).astype(BF16)
    y = jnp.dot(a, wa_ref[...], preferred_element_type=F32)
    y = y + jnp.dot(os_ref[...], ws_ref[...], preferred_element_type=F32)
    o_ref[...] = x_ref[...] + _rms(y, post_ref[...])


def _mixout_ffn_body(x_ref, oa_ref, os_ref, an_ref, mpost_ref, wa_ref, ws_ref, pre_ref, post_ref, wg_ref, wu_ref,
                     wo_ref, o_ref):
    _mixout_body(x_ref, oa_ref, os_ref, an_ref, mpost_ref, wa_ref, ws_ref, o_ref)
    _ffn_body(o_ref, pre_ref, post_ref, wg_ref, wu_ref, wo_ref, o_ref)


def _mixout_ffn(x, o_att, o_ssd, att_norm, mix_post, wa, ws, pre, post, wg, wu, wo):
    t, d = x.shape
    f = wg.shape[1]
    tm = min(FFN_ROW_TILE, t)
    row = pl.BlockSpec((tm, d), lambda i: (i, 0))
    vec = _const_spec((1, d))
    return pl.pallas_call(
        _mixout_ffn_body, grid=(t // tm,),
        in_specs=[row, row, row, vec, vec, _const_spec((d, d)), _const_spec((d, d)), vec, vec,
                  _const_spec((d, f)), _const_spec((d, f)), _const_spec((f, d))],
        out_specs=row, out_shape=jax.ShapeDtypeStruct((t, d), F32),
        compiler_params=_params(1), name="mixout_ffn")(x, o_att, o_ssd, att_norm, mix_post, wa, ws, pre, post, wg,
                                                       wu, wo)


def _token_rows(ref, k, lo=0, n=None):
    n = ref.shape[-1] if n is None else n
    return ref[k, :, :, lo:lo + n].reshape(KV_GROUP, n).T


def _compress_rows(x_refs, pe_ref, bd_ref, n_blocks, pitch=BLK):
    acc = None
    for l0 in range(0, BLK, COMPRESS_ROWS_PER_DOT):
        pieces = []
        for l in range(l0, l0 + COMPRESS_ROWS_PER_DOT):
            pe = pe_ref[l:l + 1, :]
            xs = jnp.concatenate([x_ref[pl.ds(l, n_blocks, stride=pitch), :] + pe for x_ref in x_refs], axis=0)
            pieces.append(xs.astype(BF16))
        w = bd_ref[l0:l0 + COMPRESS_ROWS_PER_DOT].reshape(COMPRESS_ROWS_PER_DOT * KV_GROUP, KV_GROUP)
        part = jnp.dot(jnp.concatenate(pieces, axis=1), w, preferred_element_type=F32)
        acc = part if acc is None else acc + part
    return acc


def _store_summaries(acc, o_ref, n_blocks):
    for k in range(N_KV_HEADS):
        o_ref[0:n_blocks, k * KV_GROUP:(k + 1) * KV_GROUP] = acc[k * n_blocks:(k + 1) * n_blocks].astype(BF16)


def _compress_body(*refs, n_blocks, n_out):
    x_refs = refs[:N_KV_HEADS]
    pe_ref, bd_ref, o_ref = refs[N_KV_HEADS:]
    _store_summaries(_compress_rows(x_refs, pe_ref, bd_ref, n_blocks), o_ref, n_blocks)
    if n_out > n_blocks:
        o_ref[n_blocks:n_out, :] = jnp.zeros((n_out - n_blocks, KV_WIDTH), BF16)


def _compress(kv, pe, bd, n_out):
    b, s, _ = kv.shape
    n_blocks = s // BLK

    def head_spec(k):
        return pl.BlockSpec((None, s, KV_GROUP), lambda i: (i, 0, k))

    return pl.pallas_call(
        functools.partial(_compress_body, n_blocks=n_blocks, n_out=n_out), grid=(b,),
        in_specs=[head_spec(k) for k in range(N_KV_HEADS)]
        + [_const_spec((BLK, KV_GROUP)), _const_spec((BLK, KV_GROUP, KV_GROUP))],
        out_specs=pl.BlockSpec((None, n_out, KV_WIDTH), lambda i: (i, 0, 0)),
        out_shape=jax.ShapeDtypeStruct((b, n_out, KV_WIDTH), BF16),
        compiler_params=_params(1), name="compress")(*([kv] * N_KV_HEADS), pe, bd)


def _compress_paged_body(pt_ref, *refs, n_pages):
    page_refs = refs[:PAGES_PER_STEP]
    pe_ref, bd_ref, o_ref, stage_ref = refs[PAGES_PER_STEP:]
    j = pl.program_id(1)
    blocks_per_page = PAGE_SIZE // BLK
    for p in range(PAGES_PER_STEP):
        first_block = (j * PAGES_PER_STEP + p) * blocks_per_page
        for k in range(N_KV_HEADS):
            rows = _token_rows(page_refs[p], k)
            for h in range(blocks_per_page):
                at = pl.multiple_of((first_block + h) * STAGE_PITCH, SUBLANES)
                stage_ref[k, pl.ds(at, BLK), :] = rows[h * BLK:(h + 1) * BLK]

    @pl.when(j == pl.num_programs(1) - 1)
    def _():
        n_blocks = n_pages * blocks_per_page
        x_refs = [stage_ref.at[k] for k in range(N_KV_HEADS)]
        _store_summaries(_compress_rows(x_refs, pe_ref, bd_ref, n_blocks, STAGE_PITCH), o_ref, n_blocks)


def _compress_paged(cache, page_table, pe, bd):
    b, n_pages = page_table.shape
    steps = n_pages // PAGES_PER_STEP
    n_blocks = n_pages * (PAGE_SIZE // BLK)

    def page_spec(p):
        return pl.BlockSpec((None, N_KV_HEADS, 2, HEAD_DIM, PAGE_SIZE),
                            lambda i, j, pt: (pt[i, j * PAGES_PER_STEP + p], 0, 0, 0, 0))

    grid_spec = pltpu.PrefetchScalarGridSpec(
        num_scalar_prefetch=1, grid=(b, steps),
        in_specs=[page_spec(p) for p in range(PAGES_PER_STEP)]
        + [pl.BlockSpec((BLK, KV_GROUP), lambda i, j, pt: (0, 0)),
           pl.BlockSpec((BLK, KV_GROUP, KV_GROUP), lambda i, j, pt: (0, 0, 0))],
        out_specs=pl.BlockSpec((None, n_blocks, KV_WIDTH), lambda i, j, pt: (i, 0, 0)),
        scratch_shapes=[pltpu.VMEM((N_KV_HEADS, n_blocks * STAGE_PITCH, KV_GROUP), F32)])
    return pl.pallas_call(
        functools.partial(_compress_paged_body, n_pages=n_pages), grid_spec=grid_spec,
        out_shape=jax.ShapeDtypeStruct((b, n_blocks, KV_WIDTH), BF16),
        compiler_params=_params(2), name="compress_paged")(page_table, *([cache] * PAGES_PER_STEP), pe, bd)


def _rel_bucket(dist):
    n = jnp.maximum(dist, 0)
    max_exact = N_BUCKETS // 2
    nf = jnp.maximum(n, 1).astype(F32)
    log_b = max_exact + (jnp.log(nf / max_exact) / math.log(MAX_DISTANCE / max_exact)
                         * (N_BUCKETS - max_exact)).astype(jnp.int32)
    return jnp.where(n < max_exact, n, jnp.minimum(log_b, N_BUCKETS - 1))


def _bias_of(rel_table, dist, visible):
    onehot = (_rel_bucket(dist)[..., None] == jnp.arange(N_BUCKETS, dtype=jnp.int32)).astype(F32)
    bias = jnp.einsum('rcb,bh->hrc', onehot, rel_table.astype(F32) * LOG2E, precision=lax.Precision.HIGHEST)
    return jnp.where(visible[None], bias, MASK_VALUE).reshape(-1, dist.shape[-1])


KIND_ZERO, KIND_DIAG, KIND_PREV, KIND_MASKED, KIND_OLDEST = range(5)


def _prompt_bias_tables(rel_table):
    r = jnp.arange(TILE, dtype=jnp.int32)[:, None]
    c = jnp.arange(TILE, dtype=jnp.int32)[None, :]
    true = jnp.ones((TILE, TILE), bool)
    far = _bias_of(rel_table, jnp.full((TILE, TILE), MAX_DISTANCE, jnp.int32), true)
    tiles = jnp.stack([
        jnp.zeros_like(far),
        _bias_of(rel_table, r - c, r >= c) - far,
        _bias_of(rel_table, TILE + r - c, true) - far,
        jnp.full_like(far, MASK_VALUE),
        jnp.where(jnp.tile(r < c, (N_ATT_HEADS, 1)), 0.0, MASK_VALUE),
    ])
    c_dist = r - (BLK - 1) + BLK * (BLK - 1 - c)
    cmp_tab = _bias_of(rel_table, c_dist, c_dist >= 0)
    return tiles, cmp_tab


def _selection_mask(score, n_rows):
    n_groups = score.shape[0] // SUBLANES
    groups = [score[SUBLANES * j:SUBLANES * (j + 1)] for j in range(n_groups)]
    ranks = [jnp.zeros(groups[0].shape, jnp.int32) for _ in range(n_groups)]
    row8 = lax.broadcasted_iota(jnp.int32, groups[0].shape, 0)
    for n in range(n_rows):
        other = jnp.broadcast_to(score[n:n + 1, :], groups[0].shape)
        for j in range(n_groups):
            lo = SUBLANES * j
            if lo > n:
                inc = jnp.where(other >= groups[j], 1, 0)
            elif lo + SUBLANES - 1 <= n:
                inc = jnp.where(other > groups[j], 1, 0)
            else:
                inc = jnp.where(row8 + lo > n, jnp.where(other >= groups[j], 1, 0),
                                jnp.where(other > groups[j], 1, 0))
            ranks[j] = ranks[j] + inc
    rank = jnp.concatenate(ranks, axis=0)
    return jnp.where(rank < N_SEL, jnp.where(score > -jnp.inf, 0.0, MASK_VALUE), MASK_VALUE)


PAIR = 2 * TILE
Q_TILES = PAIR // TILE
SEL_UNROLL = 4
_SEL_KINDS = {0: ((KIND_DIAG, KIND_MASKED), (KIND_PREV, KIND_DIAG)),
              1: ((KIND_ZERO, KIND_PREV), (KIND_ZERO, KIND_ZERO))}
_WIN_OLDEST = ((KIND_OLDEST, KIND_ZERO), (KIND_MASKED, KIND_OLDEST))


def _nsa_prompt_body(q_ref, gates_ref, kvc_ref, skv_ref, wkv_ref, tb_ref, ctab_ref, o_ref,
                     s_ref, sw_ref, mx_ref, mb_ref, acc_ref, part_ref, *, n_blocks):
    step = pl.program_id(2)
    head_rows = GQA * TILE
    rows = Q_TILES * head_rows
    lane = lax.broadcasted_iota(jnp.int32, (rows, TILE), 1)
    low = lane < HEAD_DIM
    key_lane = lax.broadcasted_iota(jnp.int32, (PAIR, TILE), 1)
    key_blk = lax.broadcasted_iota(jnp.int32, (PAIR, TILE), 0) // BLK
    n_pairs = skv_ref.shape[0] // PAIR

    qf = q_ref[...]
    halves = [qf[:, c * TILE:(c + 1) * TILE] for c in range(GQA // 2)]
    rolled = [pltpu.roll(h, HEAD_DIM, 1) for h in halves]
    q_all = jnp.concatenate([(halves if g % 2 == 0 else rolled)[g // 2][t * TILE:(t + 1) * TILE]
                             for t in range(Q_TILES) for g in range(GQA)], axis=0)
    lhs_plain = jnp.where(low, q_all, 0.0).astype(BF16)

    def ones_and_values(kv):
        kv_lane = lax.broadcasted_iota(jnp.int32, kv.shape, 1)
        return jnp.where(kv_lane < HEAD_DIM, jnp.ones_like(kv), kv)

    def normalise(acc):
        return acc / jnp.maximum(pltpu.roll(acc, HEAD_DIM, 1), 1e-30)

    def bias_of(kinds_per_tile, valid):
        parts = []
        for left, right in kinds_per_tile:
            left = jnp.where(valid, left, KIND_MASKED)
            right = jnp.where(valid, right, KIND_MASKED)
            parts.append(jnp.concatenate([tb_ref[left], tb_ref[right]], axis=1))
        return jnp.concatenate(parts, axis=0)

    def sel_kinds(j):
        behind = step - j
        return [tuple(jnp.where(behind == 0, _SEL_KINDS[0][t][side],
                                jnp.where(behind == 1, _SEL_KINDS[1][t][side], KIND_ZERO)) for side in range(2))
                for t in range(Q_TILES)]

    def keys_of(kv_ref, j):
        jc = jnp.clip(j, 0, n_pairs - 1)
        return kv_ref[pl.ds(pl.multiple_of(jc * PAIR, PAIR), PAIR), :]

    def score_pair(lhs, kv_ref, j, bias, block_columns):
        rhs = keys_of(kv_ref, j)
        if block_columns:
            onehot = jnp.where(key_lane - HEAD_DIM == (PAIR // BLK) * j + key_blk, 1.0, 0.0).astype(BF16)
            rhs = jnp.where(key_lane < HEAD_DIM, rhs, onehot)
        return lax.dot_general(lhs, rhs, NT_DIMS, preferred_element_type=F32) + bias

    def half_max(sc):
        return jnp.maximum(sc[:, :TILE], sc[:, TILE:])

    def weigh_pair(sc, mb, kv_ref, j):
        pr = jnp.exp2(sc - jnp.concatenate([mb, mb], axis=1))
        return jnp.dot(pr.astype(BF16), ones_and_values(keys_of(kv_ref, j)), preferred_element_type=F32)

    def row_max(mx):
        return jnp.broadcast_to(jnp.max(mx, axis=1, keepdims=True), (rows, TILE))

    n_win = WINDOW // PAIR + 1
    w_pairs = [step - (n_win - 1) + w for w in range(n_win)]
    w_kinds = [_WIN_OLDEST] + [[(KIND_ZERO, KIND_ZERO)] * Q_TILES] * (n_win - 3) + [_SEL_KINDS[1], _SEL_KINDS[0]]
    w_mx = None
    for w, j in enumerate(w_pairs):
        sc = score_pair(lhs_plain, wkv_ref, j, bias_of(w_kinds[w], j >= 0), False)
        sw_ref[w] = sc
        w_mx = half_max(sc) if w_mx is None else jnp.maximum(w_mx, half_max(sc))

    kvc = kvc_ref[...]
    bias = []
    for t in range(Q_TILES):
        i = Q_TILES * step + t
        shift = (2 * i + (TILE - (BLK - 1))) % TILE
        bias += [pltpu.roll(ctab_ref[g * TILE:(g + 1) * TILE, :], shift, 1) for g in range(GQA)]
    s = lax.dot_general(lhs_plain, kvc, NT_DIMS, preferred_element_type=F32)
    s = jnp.where(lane < n_blocks, s + jnp.concatenate(bias, axis=0), MASK_VALUE)
    m = jnp.maximum(jnp.max(s, axis=1, keepdims=True), M_INIT)
    p = jnp.exp2(s - m)
    c_acc = jnp.dot(p.astype(BF16), ones_and_values(kvc), preferred_element_type=F32)
    p = p / jnp.maximum(c_acc, 1e-30)
    o_cmp = normalise(c_acc)

    w_mb = row_max(jnp.maximum(w_mx, M_INIT))
    w_acc = None
    for w, j in enumerate(w_pairs):
        dacc = weigh_pair(sw_ref[w], w_mb, wkv_ref, j)
        w_acc = dacc if w_acc is None else w_acc + dacc
    o_win = normalise(w_acc)

    gate = jax.nn.sigmoid(gates_ref[...])

    def gate_col(t, g, br):
        c = g * N_BRANCH + br
        return gate[t * TILE:(t + 1) * TILE, c:c + 1]

    heads = [(t, g, slice(t * head_rows + g * TILE, t * head_rows + (g + 1) * TILE))
             for t in range(Q_TILES) for g in range(GQA)]
    for t, g, head in heads:
        part_ref[head, :] = gate_col(t, g, 0) * o_cmp[head] + gate_col(t, g, 2) * o_win[head]

    blk = lax.broadcasted_iota(jnp.int32, (n_blocks, PAIR), 0)
    query = lax.broadcasted_iota(jnp.int32, (n_blocks, PAIR), 1)
    cur = (PAIR // BLK) * step + query // BLK
    imp_t = []
    for t in range(Q_TILES):
        imp = p[t * head_rows:t * head_rows + TILE]
        for g in range(1, GQA):
            imp = imp + p[t * head_rows + g * TILE:t * head_rows + (g + 1) * TILE]
        imp_t.append(imp.T[0:n_blocks])
    forced = (blk == 0) | (blk == cur) | (blk == cur - 1)
    score = jnp.where(forced, jnp.inf, jnp.where(blk <= cur, jnp.concatenate(imp_t, axis=1), -jnp.inf))
    neg_t = _selection_mask(score, n_blocks)
    neg = []
    for t in range(Q_TILES):
        pieces = [jnp.zeros((HEAD_DIM, TILE), F32), neg_t[:, t * TILE:(t + 1) * TILE]]
        if n_blocks < TILE - HEAD_DIM:
            pieces.append(jnp.zeros((TILE - HEAD_DIM - n_blocks, TILE), F32))
        neg += [jnp.concatenate(pieces, axis=0).T] * GQA
    lhs_sel = jnp.where(low, q_all, jnp.concatenate(neg, axis=0)).astype(BF16)

    full_trips = (step + 2) // SEL_UNROLL
    done = full_trips * SEL_UNROLL
    rest = jnp.maximum(step + 1 - done, 0)
    mx_ref[...] = jnp.full(mx_ref.shape, M_INIT, F32)

    def score_trips(unroll, first, n_trips):
        def trip(t, carry):
            mx = None
            for u in range(unroll):
                j = first + t * unroll + u
                sc = score_pair(lhs_sel, skv_ref, j, bias_of(sel_kinds(j), j <= step), True)
                s_ref[j] = sc
                mx = half_max(sc) if mx is None else jnp.maximum(mx, half_max(sc))
            mx_ref[...] = jnp.maximum(mx_ref[...], mx)
            return carry
        lax.fori_loop(0, n_trips, trip, 0)

    score_trips(SEL_UNROLL, 0, full_trips)
    score_trips(1, done, rest)
    mb_ref[...] = row_max(mx_ref[...])
    acc_ref[...] = jnp.zeros(acc_ref.shape, F32)

    def weigh_trips(unroll, first, n_trips):
        def trip(t, carry):
            mb = mb_ref[...]
            acc = None
            for u in range(unroll):
                j = first + t * unroll + u
                dacc = weigh_pair(s_ref[j], mb, skv_ref, j)
                acc = dacc if acc is None else acc + dacc
            acc_ref[...] += acc
            return carry
        lax.fori_loop(0, n_trips, trip, 0)

    weigh_trips(SEL_UNROLL, 0, full_trips)
    weigh_trips(1, done, rest)
    o_sel = normalise(acc_ref[...])

    out_low = lax.broadcasted_iota(jnp.int32, (TILE, TILE), 1) < HEAD_DIM
    outs = {(t, g): part_ref[head, :] + gate_col(t, g, 1) * o_sel[head] for t, g, head in heads}
    for t in range(Q_TILES):
        for c in range(GQA // 2):
            o_ref[t * TILE:(t + 1) * TILE, c * TILE:(c + 1) * TILE] = jnp.where(
                out_low, pltpu.roll(outs[t, 2 * c], HEAD_DIM, 1), outs[t, 2 * c + 1])


def _nsa_prompt(q, gates, kvc, skv_b, wkv_b, tiles, cmp_tab):
    b, s, _ = q.shape
    n_blocks = s // BLK
    assert n_blocks <= TILE - HEAD_DIM and kvc.shape[1] == TILE and s % PAIR == 0 and WINDOW // PAIR >= 2
    gw = GQA * HEAD_DIM
    head_rows = GQA * TILE
    rows = Q_TILES * head_rows
    return pl.pallas_call(
        functools.partial(_nsa_prompt_body, n_blocks=n_blocks), grid=(b, N_KV_HEADS, s // PAIR),
        in_specs=[pl.BlockSpec((None, PAIR, gw), lambda bi, k, i: (bi, i, k)),
                  pl.BlockSpec((None, PAIR, TILE), lambda bi, k, i: (bi, i, k)),
                  pl.BlockSpec((None, TILE, KV_GROUP), lambda bi, k, i: (bi, 0, k)),
                  pl.BlockSpec((None, s, KV_GROUP), lambda bi, k, i: (bi, 0, k)),
                  pl.BlockSpec((None, s, KV_GROUP), lambda bi, k, i: (bi, 0, k)),
                  pl.BlockSpec((len(tiles), head_rows, TILE), lambda bi, k, i: (0, k, 0)),
                  pl.BlockSpec((head_rows, TILE), lambda bi, k, i: (k, 0))],
        out_specs=pl.BlockSpec((None, PAIR, gw), lambda bi, k, i: (bi, i, k)),
        out_shape=jax.ShapeDtypeStruct((b, s, ATT_WIDTH), F32),
        scratch_shapes=[pltpu.VMEM((s // PAIR, rows, PAIR), F32),
                        pltpu.VMEM((WINDOW // PAIR + 1, rows, PAIR), F32),
                        pltpu.VMEM((rows, TILE), F32), pltpu.VMEM((rows, TILE), F32),
                        pltpu.VMEM((rows, KV_GROUP), F32), pltpu.VMEM((rows, TILE), F32)],
        compiler_params=_params(3), name="nsa_prompt")(q, gates, kvc, skv_b, wkv_b, tiles, cmp_tab)


def _nsa_sample_body(pt_ref, *refs, n_pages, dec_seq, past_len):
    page_refs = refs[:PAGES_PER_STEP]
    (qk_ref, gates_ref, kc_ref, ctab_ref, gsum_ref, knew_ref, vnew_ref, cwin_ref, kwnew_ref, vwnew_ref,
     last_ref, new_ref, win0_ref, o_ref, lhs_ref, m_ref, l_ref, acc_ref, ow_ref, oc_ref) = refs[PAGES_PER_STEP:]
    j = pl.program_id(1)
    last_step = pl.num_programs(1) - 1
    blocks_per_page = PAGE_SIZE // BLK
    n_past = n_pages * blocks_per_page
    kd = N_KV_HEADS * HEAD_DIM
    n_lanes = kc_ref.shape[0]
    step_keys = PAGES_PER_STEP * PAGE_SIZE

    def gate_col(br):
        return jax.nn.sigmoid(gates_ref[:, br:br + 1])

    def head_major(ref, c, lo=0, n=None):
        n = ref.shape[-1] if n is None else n
        return ref[:, c, :, lo:lo + n].reshape(kd, n).astype(BF16)

    def lane_tiles(x):
        return [x[:, t * TILE:(t + 1) * TILE] for t in range(x.shape[1] // TILE)]

    def tile_max(tiles):
        m = tiles[0]
        for t in tiles[1:]:
            m = jnp.maximum(m, t)
        return jnp.broadcast_to(jnp.max(m, axis=1, keepdims=True), (TILE, TILE))

    def tile_sum(tiles):
        s = tiles[0]
        for t in tiles[1:]:
            s = s + t
        return s

    def online_update(s, values_t):
        tiles = lane_tiles(s)
        m_prev = m_ref[...]
        m_new = jnp.maximum(m_prev, tile_max(tiles))
        alpha = jnp.exp2(m_prev - m_new)
        p = [jnp.exp2(t - m_new) for t in tiles]
        l_ref[...] = alpha * l_ref[...] + tile_sum(p)
        pv = lax.dot_general(jnp.concatenate(p, axis=1).astype(BF16), values_t, NT_DIMS, preferred_element_type=F32)
        acc_ref[...] = jnp.concatenate([alpha] * (kd // TILE), axis=1) * acc_ref[...] + pv
        m_ref[...] = m_new

    @pl.when(j == 0)
    def _():
        kc = kc_ref[...]
        qf = qk_ref[...].astype(F32)
        gap = jnp.zeros((TILE, HEAD_DIM), F32)
        qc = jnp.concatenate([piece for k in range(N_KV_HEADS)
                              for piece in (qf[:, k * HEAD_DIM:(k + 1) * HEAD_DIM], gap)], axis=1).astype(BF16)
        s = lax.dot_general(qc, kc, NT_DIMS, preferred_element_type=F32) + ctab_ref[...]
        m = jnp.maximum(jnp.max(s, axis=1, keepdims=True), M_INIT)
        p = jnp.exp2(s - m)
        p = p / jnp.maximum(jnp.sum(p, axis=1, keepdims=True), 1e-30)
        oc_ref[...] = gate_col(0) * jnp.dot(p.astype(BF16), kc, preferred_element_type=F32)
        imp = _select_sum(p, gsum_ref[...], False)
        imp_t = jnp.concatenate([t.T for t in lane_tiles(imp)], axis=0)
        n_rows = -(-(n_past + 1) // SUBLANES) * SUBLANES
        blk = lax.broadcasted_iota(jnp.int32, (n_rows, TILE), 0)
        tok = lax.broadcasted_iota(jnp.int32, (n_rows, TILE), 1) % dec_seq
        cur = (past_len + tok) // BLK
        forced = (blk == 0) | (blk == cur) | (blk == cur - 1)
        score = jnp.where(forced, jnp.inf, jnp.where(blk <= cur, imp_t[0:n_rows], -jnp.inf))
        neg_t = jnp.concatenate([_selection_mask(score, n_past + 1), jnp.zeros((n_lanes - n_rows, TILE), F32)], axis=0)
        neg = jnp.concatenate([neg_t[t * TILE:(t + 1) * TILE].T for t in range(n_lanes // TILE)], axis=1)
        lhs_ref[:, 0:kd] = qk_ref[...]
        lhs_ref[:, kd:kd + n_lanes] = neg.astype(BF16)

        qk = qk_ref[...]
        n_win = cwin_ref.shape[-1] // TILE
        zero = jnp.zeros((TILE, TILE), F32)
        s_w = jnp.dot(qk, head_major(cwin_ref, 0), preferred_element_type=F32)
        s_w = s_w + jnp.concatenate([win0_ref[...]] + [zero] * (n_win - 2) + [last_ref[...]], axis=1)
        s_n = jnp.dot(qk, kwnew_ref[...], preferred_element_type=F32) + new_ref[...]
        tiles = lane_tiles(s_w) + [s_n]
        m = tile_max(tiles)
        p = [jnp.exp2(t - m) for t in tiles]
        l = jnp.sum(tile_sum(p), axis=1, keepdims=True)
        o_w = lax.dot_general(jnp.concatenate(p[:-1], axis=1).astype(BF16), head_major(cwin_ref, 1), NT_DIMS,
                              preferred_element_type=F32)
        o_w = o_w + lax.dot_general(p[-1].astype(BF16), vwnew_ref[...], NT_DIMS, preferred_element_type=F32)
        ow_ref[...] = gate_col(2) * (o_w / l)

        m_ref[...] = jnp.full(m_ref.shape, M_INIT, F32)
        l_ref[...] = jnp.zeros(l_ref.shape, F32)
        acc_ref[...] = jnp.zeros(acc_ref.shape, F32)

    keys_t = jnp.concatenate([head_major(r, 0) for r in page_refs], axis=1)
    values_t = jnp.concatenate([head_major(r, 1) for r in page_refs], axis=1)
    blk_row = lax.broadcasted_iota(jnp.int32, (n_lanes, step_keys), 0)
    key_blk = j * (PAGES_PER_STEP * blocks_per_page) + lax.broadcasted_iota(jnp.int32, (n_lanes, step_keys), 1) // BLK
    onehot = jnp.where(blk_row == key_blk, 1.0, 0.0).astype(BF16)
    s = jnp.dot(lhs_ref[...], jnp.concatenate([keys_t, onehot], axis=0), preferred_element_type=F32)
    tiles = lane_tiles(s)
    tiles[-1] = tiles[-1] + jnp.where(j == last_step, last_ref[...], 0.0)
    online_update(jnp.concatenate(tiles, axis=1), values_t)

    @pl.when(j == last_step)
    def _():
        new_blk = lax.broadcasted_iota(jnp.int32, (n_lanes, TILE), 0) == n_past
        rhs = jnp.concatenate([knew_ref[...], jnp.where(new_blk, 1.0, 0.0).astype(BF16)], axis=0)
        online_update(jnp.dot(lhs_ref[...], rhs, preferred_element_type=F32) + new_ref[...], vnew_ref[...])
        l = jnp.sum(l_ref[...], axis=1, keepdims=True)
        o_sw = gate_col(1) * (acc_ref[...] / l) + ow_ref[...]
        o_c = oc_ref[...]
        rows_per_head = TILE // N_KV_HEADS
        for k in range(N_KV_HEADS):
            rows = slice(k * rows_per_head, (k + 1) * rows_per_head)
            v_lo = k * KV_GROUP + HEAD_DIM
            o_ref[rows, :] = o_sw[rows, k * HEAD_DIM:(k + 1) * HEAD_DIM] + o_c[rows, v_lo:v_lo + HEAD_DIM]


def _nsa_sample(page_table, cache_slc, qk, gates_c, kc, ctab, gsum, k_new, v_new, cache_win, kw_new, vw_new,
                last_tab, new_tab, win0_tab, *, dec_seq, past_len):
    b, n_pages = page_table.shape
    steps = n_pages // PAGES_PER_STEP
    n_lanes = kc.shape[1]
    w_buf = cache_win.shape[-1]
    kd = N_KV_HEADS * HEAD_DIM

    def page_spec(p):
        return pl.BlockSpec((None, N_KV_HEADS, 2, HEAD_DIM, PAGE_SIZE),
                            lambda i, j, pt: (pt[i, j * PAGES_PER_STEP + p], 0, 0, 0, 0))

    def per_batch(*shape):
        zeros = (0,) * len(shape)
        return pl.BlockSpec((None,) + shape, lambda i, j, pt: (i,) + zeros)

    def const(*shape):
        zeros = (0,) * len(shape)
        return pl.BlockSpec(shape, lambda i, j, pt: zeros)

    grid_spec = pltpu.PrefetchScalarGridSpec(
        num_scalar_prefetch=1, grid=(b, steps),
        in_specs=[page_spec(p) for p in range(PAGES_PER_STEP)]
        + [per_batch(TILE, kd), per_batch(TILE, TILE), per_batch(n_lanes, KV_WIDTH),
           const(TILE, n_lanes), const(TILE, TILE), per_batch(kd, TILE), per_batch(kd, TILE),
           per_batch(N_KV_HEADS, 2, HEAD_DIM, w_buf), per_batch(kd, TILE), per_batch(kd, TILE),
           const(TILE, TILE), const(TILE, TILE), const(TILE, TILE)],
        out_specs=per_batch(TILE, HEAD_DIM),
        scratch_shapes=[pltpu.VMEM((TILE, kd + n_lanes), BF16), pltpu.VMEM((TILE, TILE), F32),
                        pltpu.VMEM((TILE, TILE), F32), pltpu.VMEM((TILE, kd), F32), pltpu.VMEM((TILE, kd), F32),
                        pltpu.VMEM((TILE, KV_WIDTH), F32)])
    return pl.pallas_call(
        functools.partial(_nsa_sample_body, n_pages=n_pages, dec_seq=dec_seq, past_len=past_len),
        grid_spec=grid_spec, out_shape=jax.ShapeDtypeStruct((b, TILE, HEAD_DIM), F32),
        compiler_params=_params(2), name="nsa_sample")(
            page_table, *([cache_slc] * PAGES_PER_STEP), qk, gates_c, kc, ctab, gsum, k_new, v_new, cache_win,
            kw_new, vw_new, last_tab, new_tab, win0_tab)


def _split3(x):
    hi = x.astype(BF16)
    r = x - hi.astype(F32)
    mid = r.astype(BF16)
    return hi, mid, (r - mid.astype(F32)).astype(BF16)


def _select_sum(x, onehot, x_on_left):
    out = None
    for part in _split3(x):
        term = (jnp.dot(part, onehot, preferred_element_type=F32) if x_on_left
                else jnp.dot(onehot, part, preferred_element_type=F32))
        out = term if out is None else out + term
    return out


def _softplus(x):
    return jnp.maximum(x, 0.0) + jnp.log(1.0 + jnp.exp(-jnp.abs(x)))


def _ssd_body(z_ref, xbc_ref, dt_ref, dtt_ref, convp_ref, h0_ref, cw_ref, cb_ref, dtb_ref, dtbt_ref, al_ref,
              alt_ref, dskip_ref, norm_ref, o_ref, hout_ref, convout_ref, xp_ref, *, cl):
    c = pl.program_id(1)
    pad = SUBLANES

    @pl.when(c == 0)
    def _():
        xp_ref[0:pad, :] = convp_ref[...]
        hout_ref[...] = h0_ref[...]

    xp_ref[pad:pad + cl, :] = xbc_ref[...]
    conv = cb_ref[...]
    for k in range(CONV_W):
        lo = pad - (CONV_W - 1) + k
        conv = conv + cw_ref[k:k + 1, :] * xp_ref[lo:lo + cl, :]
    u = conv * jax.nn.sigmoid(conv)
    tail = xp_ref[pad + cl - (CONV_W - 1):pad + cl, :]
    convout_ref[...] = tail
    xp_ref[pad - (CONV_W - 1):pad, :] = tail

    xs = u[:, :SSD_INNER]
    dt = _softplus(dt_ref[:, 0:SSD_HEADS] + dtb_ref[...])
    dt_t = _softplus(dtt_ref[0:SSD_HEADS, :] + dtbt_ref[...])
    la = dt * -jnp.exp(al_ref[...])
    la_t = dt_t * -jnp.exp(alt_ref[...])
    row = lax.broadcasted_iota(jnp.int32, (cl, cl), 0)
    col = lax.broadcasted_iota(jnp.int32, (cl, cl), 1)
    causal = row >= col
    cs = _select_sum(la, jnp.where(causal, 1.0, 0.0).astype(BF16), False)
    cs_t = _select_sum(la_t, jnp.where(row <= col, 1.0, 0.0).astype(BF16), True)

    def spread(width):
        lane_head = lax.broadcasted_iota(jnp.int32, (SSD_HEADS, SSD_HEADS * width), 1) // width
        return jnp.where(lane_head == lax.broadcasted_iota(jnp.int32, lane_head.shape, 0), 1.0, 0.0).astype(BF16)

    dt_x = _select_sum(dt, spread(SSD_HEAD_DIM), True)
    cs_x = _select_sum(cs, spread(SSD_HEAD_DIM), True)
    cs_wide = _select_sum(cs, spread(TILE), True)
    last_x = cs_x[cl - 1:cl, :]
    xdt = xs * dt_x
    xdt_b = xdt.astype(BF16)
    to_end = (xdt * jnp.exp(last_x - cs_x)).astype(BF16)
    grow = jnp.exp(cs_x)
    skip = dskip_ref[...] * xs

    heads_per_group = SSD_HEADS // SSD_GROUPS
    bms, cms, cbs = [], [], []
    for g in range(SSD_GROUPS):
        b_lo = SSD_INNER + g * SSD_STATE
        c_lo = SSD_INNER + SSD_GROUPS * SSD_STATE + g * SSD_STATE
        bms.append(u[:, b_lo:b_lo + SSD_STATE].astype(BF16))
        cms.append(u[:, c_lo:c_lo + SSD_STATE].astype(BF16))
        cbs.append(lax.dot_general(cms[g], bms[g], NT_DIMS, preferred_element_type=F32))
    first_head = lax.broadcasted_iota(jnp.int32, (cl, TILE), 1) < SSD_HEAD_DIM
    state_row = lax.broadcasted_iota(jnp.int32, (2 * SSD_HEAD_DIM, SSD_STATE), 0)
    ys = []
    for pair in range(SSD_HEADS // 2):
        g = 2 * pair // heads_per_group
        lanes = slice(pair * TILE, (pair + 1) * TILE)
        y_heads, keep = [], []
        for h in (2 * pair, 2 * pair + 1):
            diff = cs_wide[:, h * TILE:h * TILE + cl] - cs_t[h:h + 1, :]
            decay = jnp.where(causal, jnp.exp(jnp.where(causal, diff, 0.0)), 0.0)
            y_heads.append(jnp.dot((cbs[g] * decay).astype(BF16), xdt_b[:, lanes], preferred_element_type=F32))
            keep.append(jnp.exp(cs[cl - 1:cl, h:h + 1]))
        state_in = lax.dot_general(to_end[:, lanes], bms[g], TN_DIMS, preferred_element_type=F32)
        h_prev = jnp.concatenate([hout_ref[2 * pair], hout_ref[2 * pair + 1]], axis=0)
        y_off = lax.dot_general(cms[g], h_prev.astype(BF16), NT_DIMS, preferred_element_type=F32)
        h_new = h_prev * jnp.where(state_row < SSD_HEAD_DIM, keep[0], keep[1]) + state_in
        hout_ref[2 * pair] = h_new[0:SSD_HEAD_DIM]
        hout_ref[2 * pair + 1] = h_new[SSD_HEAD_DIM:]
        ys.append(jnp.where(first_head, y_heads[0], y_heads[1]) + y_off * grow[:, lanes] + skip[:, lanes])
    y = jnp.concatenate(ys, axis=1)
    z = z_ref[...]
    y = y * (z * jax.nn.sigmoid(z))
    gw = SSD_INNER // SSD_GROUPS
    normed = []
    for g in range(SSD_GROUPS):
        yg = y[:, g * gw:(g + 1) * gw]
        normed.append(yg * lax.rsqrt(jnp.mean(yg * yg, axis=-1, keepdims=True) + EPS))
    o_ref[...] = (jnp.concatenate(normed, axis=1) * norm_ref[...]).astype(o_ref.dtype)


def _ssd(z, xbc, dt, dt_t, conv_prev, h0, consts, cl):
    b, seq, _ = z.shape
    nc = seq // cl
    cw, cbias, dtb, dtb_t, al, al_t, dskip, norm = consts

    def rows(width):
        return pl.BlockSpec((None, cl, width), lambda i, c: (i, c, 0))

    state = pl.BlockSpec((None, SSD_HEADS, SSD_HEAD_DIM, SSD_STATE), lambda i, c: (i, 0, 0, 0))

    def const(x):
        zeros = (0,) * x.ndim
        return pl.BlockSpec(x.shape, lambda i, c: zeros)

    return pl.pallas_call(
        functools.partial(_ssd_body, cl=cl), grid=(b, nc),
        in_specs=[rows(SSD_INNER), rows(CONV_DIM), rows(TILE),
                  pl.BlockSpec((None, TILE, cl), lambda i, c: (i, 0, c)),
                  pl.BlockSpec((None, SUBLANES, CONV_DIM), lambda i, c: (i, 0, 0)), state,
                  const(cw), const(cbias), const(dtb), const(dtb_t), const(al), const(al_t), const(dskip),
                  const(norm)],
        out_specs=[rows(SSD_INNER), state, pl.BlockSpec((None, CONV_W - 1, CONV_DIM), lambda i, c: (i, 0, 0))],
        out_shape=[jax.ShapeDtypeStruct((b, seq, SSD_INNER), BF16),
                   jax.ShapeDtypeStruct((b, SSD_HEADS, SSD_HEAD_DIM, SSD_STATE), F32),
                   jax.ShapeDtypeStruct((b, CONV_W - 1, CONV_DIM), F32)],
        scratch_shapes=[pltpu.VMEM((SUBLANES + cl, CONV_DIM), F32)],
        compiler_params=_params(2), name="ssd")(z, xbc, dt, dt_t, conv_prev, h0, cw, cbias, dtb, dtb_t, al, al_t,
                                                 dskip, norm)


def _pack_inproj(w_in):
    splits = (ATT_WIDTH, KV_WIDTH, KV_WIDTH, KV_WIDTH, N_BRANCH * N_ATT_HEADS, SSD_INNER, CONV_DIM, SSD_HEADS)
    offs = [0]
    for s in splits:
        offs.append(offs[-1] + s)
    wq, wc, ws, ww, wg, wz, wx, wd = [w_in[:, offs[i]:offs[i + 1]] for i in range(len(splits))]
    d = w_in.shape[0]
    per_kv = GQA * N_BRANCH
    wg4 = jnp.pad(wg.reshape(d, N_KV_HEADS, per_kv), ((0, 0), (0, 0), (0, TILE - per_kv)))
    wd_pad = jnp.pad(wd, ((0, 0), (0, TILE - SSD_HEADS)))
    w = jnp.concatenate([wq * (HEAD_DIM ** -0.5 * LOG2E), wc, ws, ww, wg4.reshape(d, N_KV_HEADS * TILE), wz, wx,
                         wd_pad], axis=1)
    assert w.shape[1] == PROJ_WIDTH
    return w.astype(BF16)


def _pack_compress(cmp_pe, w_cmp):
    zero = jnp.zeros((BLK, HEAD_DIM, HEAD_DIM), w_cmp.dtype)
    bd = jnp.concatenate([jnp.concatenate([w_cmp[:, 0], zero], axis=2),
                          jnp.concatenate([zero, w_cmp[:, 1]], axis=2)], axis=1)
    pe = cmp_pe.reshape(BLK, KV_GROUP)
    return pe.astype(F32), bd.astype(BF16)


def _sample_tables(rel_table, dec_seq, past_len, n_lanes):
    col = jnp.arange(TILE, dtype=jnp.int32)
    head = (col // (GQA * dec_seq)) * GQA + (col // dec_seq) % GQA
    tok = (col % dec_seq)[:, None]
    per_col = (rel_table.astype(F32) * LOG2E)[:, head]

    def bias(dist):
        onehot = (_rel_bucket(dist)[..., None] == jnp.arange(N_BUCKETS, dtype=jnp.int32)).astype(F32)
        return jnp.einsum('ckb,bc->ck', onehot, per_col, precision=lax.Precision.HIGHEST)

    lane = jnp.arange(TILE, dtype=jnp.int32)[None, :]
    far = bias(jnp.full((TILE, 1), MAX_DISTANCE, jnp.int32))
    blk = jnp.arange(n_lanes, dtype=jnp.int32)[None, :]
    c_dist = past_len + tok - ((blk + 1) * BLK - 1)
    ctab = jnp.where(c_dist >= 0, bias(c_dist), MASK_VALUE)
    last_tab = bias(TILE + tok - lane) - far
    new_tab = jnp.where((lane <= tok) & (lane < dec_seq), bias(tok - lane) - far, MASK_VALUE)
    win0_tab = jnp.where(lane > tok, 0.0, MASK_VALUE)
    same = (col[:, None] // (GQA * dec_seq) == col[None, :] // (GQA * dec_seq)) & \
           (col[:, None] % dec_seq == col[None, :] % dec_seq)
    return ctab, last_tab, new_tab, win0_tab, same.astype(BF16)


def _layer_weights(ffn_pre, ffn_post, ffn_w_in, ffn_w_out):
    f = ffn_w_out.shape[0]
    return (ffn_pre[None, :], ffn_post[None, :], ffn_w_in[:, :f].astype(BF16), ffn_w_in[:, f:].astype(BF16),
            ffn_w_out.astype(BF16))


def kernel(x_prompt, x_sample, cache_cmp_kv, cache_slc_kv, cache_win_kv, state_ssm, state_conv, page_table,
           rel_table, ffn1_pre, ffn1_post, ffn1_w_in, ffn1_w_out, mix_pre, mix_post, w_in, w_out, att_out_norm,
           cmp_pe, w_cmp, conv_w, conv_b, dt_bias, a_log, d_skip, ssd_norm,
           ffn2_pre, ffn2_post, ffn2_w_in, ffn2_w_out):
    depth = ffn1_pre.shape[0]
    bp, seq, d = x_prompt.shape
    bs, dec_seq, _ = x_sample.shape
    n_pages = page_table.shape[1]
    past_len = n_pages * PAGE_SIZE
    w_buf = cache_win_kv.shape[2]
    kv_shape = (N_KV_HEADS, 2, HEAD_DIM)
    assert GQA * dec_seq * N_KV_HEADS == TILE and w_buf == WINDOW and seq % TILE == 0

    yp = x_prompt.reshape(bp * seq, d)
    ys = x_sample.reshape(bs * dec_seq, d)
    outs = [[] for _ in range(10)]
    for l in range(depth):
        ffn1 = _layer_weights(ffn1_pre[l], ffn1_post[l], ffn1_w_in[l], ffn1_w_out[l])
        ffn2 = _layer_weights(ffn2_pre[l], ffn2_post[l], ffn2_w_in[l], ffn2_w_out[l])
        w_proj = _pack_inproj(w_in[l])
        pe, bd = _pack_compress(cmp_pe[l], w_cmp[l])
        wo_att = w_out[l][:ATT_WIDTH].astype(BF16)
        wo_ssd = w_out[l][ATT_WIDTH:].astype(BF16)
        ssd_consts = (conv_w[l], conv_b[l][None, :], dt_bias[l][None, :], dt_bias[l][:, None], a_log[l][None, :],
                      a_log[l][:, None], jnp.repeat(d_skip[l], SSD_HEAD_DIM)[None, :], ssd_norm[l][None, :])
        mix_pre_l, mix_post_l, att_norm_l = mix_pre[l][None, :], mix_post[l][None, :], att_out_norm[l][None, :]

        yp = _ffn(yp, *ffn1)
        q, ckv, skv_b, wkv_b, gates, z, xbc, dt, dt_t, ckv_t, skv_t, wkv_t = _inproj(
            yp, mix_pre_l, w_proj, bp, True)
        kvc = _compress(ckv.reshape(bp, seq, KV_WIDTH), pe, bd, TILE)
        tiles, cmp_tab = _prompt_bias_tables(rel_table)
        o_att = _nsa_prompt(q.reshape(bp, seq, ATT_WIDTH), gates.reshape(bp, seq, N_KV_HEADS * TILE), kvc,
                            skv_b.reshape(bp, seq, KV_WIDTH), wkv_b.reshape(bp, seq, KV_WIDTH), tiles, cmp_tab)
        cl = min(CHUNK, seq)
        o_ssd, h_new, conv_new = _ssd(
            z.reshape(bp, seq, SSD_INNER), xbc.reshape(bp, seq, CONV_DIM), dt.reshape(bp, seq, TILE),
            dt_t, jnp.zeros((bp, SUBLANES, CONV_DIM), F32),
            jnp.zeros((bp, SSD_HEADS, SSD_HEAD_DIM, SSD_STATE), F32), ssd_consts, cl)
        yp = _mixout_ffn(yp, o_att.reshape(bp * seq, ATT_WIDTH), o_ssd.reshape(bp * seq, SSD_INNER), att_norm_l,
                         mix_post_l, wo_att, wo_ssd, *ffn2)

        def token_major(a_t):
            return a_t.reshape((bp,) + kv_shape + (a_t.shape[-1],)).transpose(0, 4, 1, 2, 3)

        outs[0].append(token_major(ckv_t))
        outs[1].append(token_major(skv_t))
        outs[2].append(token_major(wkv_t[:, :, -min(WINDOW, seq):]))
        outs[3].append(h_new)
        outs[4].append(conv_new)

        ys = _ffn(ys, *ffn1)
        q, ckv, _, _, gates, z, xbc, dt, dt_t, skv, wkv = _inproj(ys, mix_pre_l, w_proj, 1, False)
        native = (0, 2, 3, 4, 1)
        kc_past = _compress_paged(cache_cmp_kv[l].transpose(native), page_table, pe, bd)
        cmp_new = jnp.pad(ckv.reshape(bs, dec_seq, KV_WIDTH), ((0, 0), (0, BLK - dec_seq), (0, 0)))
        kc_new = _compress(cmp_new.reshape(1, bs * BLK, KV_WIDTH), pe, bd, bs).reshape(bs, 1, KV_WIDTH)
        n_past = kc_past.shape[1]
        n_lanes = -(-(n_past + 1) // TILE) * TILE
        kc = jnp.concatenate([kc_past, kc_new, jnp.zeros((bs, n_lanes - n_past - 1, KV_WIDTH), BF16)], axis=1)
        ctab, last_tab, new_tab, win0_tab, gsum = _sample_tables(rel_table, dec_seq, past_len, n_lanes)
        q_rows = q.reshape(bs, dec_seq, N_KV_HEADS, GQA, HEAD_DIM).transpose(0, 2, 3, 1, 4)
        own_head = jnp.eye(N_KV_HEADS, dtype=F32)[None, :, None, None, :, None]
        q_bd = q_rows[:, :, :, :, None, :] * own_head
        qk = q_bd.reshape(bs, TILE, N_KV_HEADS * HEAD_DIM).astype(BF16)
        g_rows = gates.reshape(bs, dec_seq, N_KV_HEADS, TILE)[..., :GQA * N_BRANCH]
        g_rows = g_rows.reshape(bs, dec_seq, N_KV_HEADS, GQA, N_BRANCH).transpose(0, 2, 3, 1, 4)
        gates_c = jnp.pad(g_rows.reshape(bs, TILE, N_BRANCH), ((0, 0), (0, 0), (0, TILE - N_BRANCH)))

        def new_t(a):
            a = a.reshape(bs, dec_seq, N_KV_HEADS, 2, HEAD_DIM).transpose(0, 3, 2, 4, 1)
            a = jnp.pad(a.reshape(bs, 2, N_KV_HEADS * HEAD_DIM, dec_seq), ((0, 0),) * 3 + ((0, TILE - dec_seq),))
            return a[:, 0].astype(BF16), a[:, 1].astype(BF16)

        k_new, v_new = new_t(skv)
        kw_new, vw_new = new_t(wkv)
        o_rows = _nsa_sample(page_table, cache_slc_kv[l].transpose(native), qk, gates_c, kc, ctab, gsum,
                             k_new, v_new, cache_win_kv[l].transpose(native), kw_new, vw_new, last_tab, new_tab,
                             win0_tab, dec_seq=dec_seq, past_len=past_len)
        o_att = o_rows.reshape(bs, N_KV_HEADS, GQA, dec_seq, HEAD_DIM).transpose(0, 3, 1, 2, 4)
        o_att = o_att.reshape(bs * dec_seq, ATT_WIDTH)
        conv_prev = jnp.pad(state_conv[l], ((0, 0), (SUBLANES - (CONV_W - 1), 0), (0, 0)))
        o_ssd, h_new, conv_new = _ssd(
            z.reshape(bs, dec_seq, SSD_INNER), xbc.reshape(bs, dec_seq, CONV_DIM), dt.reshape(bs, dec_seq, TILE),
            dt_t.reshape(TILE, bs, dec_seq).transpose(1, 0, 2), conv_prev, state_ssm[l], ssd_consts,
            min(CHUNK, dec_seq))
        ys = _mixout_ffn(ys, o_att, o_ssd.reshape(bs * dec_seq, SSD_INNER), att_norm_l, mix_post_l, wo_att,
                         wo_ssd, *ffn2)
        win_all = jnp.concatenate([cache_win_kv[l], wkv.reshape((bs, dec_seq) + kv_shape)], axis=1)
        outs[5].append(ckv.reshape((bs, dec_seq) + kv_shape))
        outs[6].append(skv.reshape((bs, dec_seq) + kv_shape))
        outs[7].append(win_all[:, -min(WINDOW, w_buf + dec_seq):])
        outs[8].append(h_new)
        outs[9].append(conv_new)

    return (yp.reshape(bp, seq, d), ys.reshape(bs, dec_seq, d)) + tuple(jnp.stack(o) for o in outs)
```

```python
import functools
import math

import jax
import jax.numpy as jnp
from jax import lax
from jax.experimental import pallas as pl
from jax.experimental.pallas import tpu as pltpu

F32 = jnp.float32
BF16 = jnp.bfloat16

D_MODEL = 1024
N_ATT_HEADS = 16
HEAD_DIM = 64
N_KV_HEADS = 4
GQA = N_ATT_HEADS // N_KV_HEADS
ATT_WIDTH = N_ATT_HEADS * HEAD_DIM
KV_WIDTH = 2 * N_KV_HEADS * HEAD_DIM
KV_GROUP = 2 * HEAD_DIM
BLK = 64
N_SEL = 16
WINDOW = 512
N_BRANCH = 3
N_BUCKETS = 32
MAX_DISTANCE = 128
SSD_HEADS = 16
SSD_HEAD_DIM = 64
SSD_INNER = SSD_HEADS * SSD_HEAD_DIM
SSD_GROUPS = 2
SSD_STATE = 128
CONV_W = 4
CONV_DIM = SSD_INNER + 2 * SSD_GROUPS * SSD_STATE
CHUNK = 128
PAGE_SIZE = 128
EPS = 1e-6

TILE = 128
SUBLANES = 8
MASK_VALUE = -(2.0 ** 100)
M_INIT = -1e30
LOG2E = math.log2(math.e)
VMEM_LIMIT = 56 * 1024 * 1024
PAGES_PER_STEP = 32
ROW_TILE = 256
FFN_ROW_TILE = 512
COMPRESS_ROWS_PER_DOT = 16
STAGE_PITCH = 72

NT_DIMS = (((1,), (1,)), ((), ()))
TN_DIMS = (((0,), (0,)), ((), ()))


def _params(n_grid_dims):
    return pltpu.CompilerParams(dimension_semantics=("arbitrary",) * n_grid_dims,
                                vmem_limit_bytes=VMEM_LIMIT)


def _rms(x, g):
    return x * lax.rsqrt(jnp.mean(x * x, axis=-1, keepdims=True) + EPS) * g


def _const_spec(shape):
    zeros = (0,) * len(shape)
    return pl.BlockSpec(shape, lambda *_: zeros, pipeline_mode=pl.Buffered(1))


def _ffn_body(x_ref, pre_ref, post_ref, wg_ref, wu_ref, wo_ref, o_ref):
    x = x_ref[...]
    h = _rms(x, pre_ref[...]).astype(BF16)
    gate = jnp.dot(h, wg_ref[...], preferred_element_type=F32)
    up = jnp.dot(h, wu_ref[...], preferred_element_type=F32)
    a = (gate * jax.nn.sigmoid(gate) * up).astype(BF16)
    y = jnp.dot(a, wo_ref[...], preferred_element_type=F32)
    o_ref[...] = x + 0.5 * _rms(y, post_ref[...])


def _ffn(x, pre, post, wg, wu, wo):
    t, d = x.shape
    f = wg.shape[1]
    tm = min(FFN_ROW_TILE, t)
    row = pl.BlockSpec((tm, d), lambda i: (i, 0))
    return pl.pallas_call(
        _ffn_body, grid=(t // tm,),
        in_specs=[row, _const_spec((1, d)), _const_spec((1, d)), _const_spec((d, f)), _const_spec((d, f)),
                  _const_spec((f, d))],
        out_specs=row, out_shape=jax.ShapeDtypeStruct((t, d), F32),
        compiler_params=_params(1), name="ffn")(x, pre, post, wg, wu, wo)


_SEG = {}
_off = 0
for _name, _w in (("q", ATT_WIDTH), ("ckv", KV_WIDTH), ("skv", KV_WIDTH), ("wkv", KV_WIDTH),
                  ("gates", N_KV_HEADS * TILE), ("z", SSD_INNER), ("xbc", CONV_DIM), ("dt", TILE)):
    _SEG[_name] = (_off, _off + _w)
    _off += _w
PROJ_WIDTH = _off


def _inproj_body(x_ref, pre_ref, w_ref, q_ref, ckv_ref, skvb_ref, wkvb_ref, gates_ref, z_ref, xbc_ref,
                 dt_ref, dtt_ref, *kv_refs, feature_major_kv):
    h = _rms(x_ref[...], pre_ref[...]).astype(BF16)

    def proj(name):
        lo, hi = _SEG[name]
        return jnp.dot(h, w_ref[:, lo:hi], preferred_element_type=F32)

    q_ref[...] = proj("q")
    ckv = proj("ckv")
    ckv_ref[...] = ckv
    skv = proj("skv")
    skvb_ref[...] = skv.astype(BF16)
    wkv = proj("wkv")
    wkvb_ref[...] = wkv.astype(BF16)
    gates_ref[...] = proj("gates")
    z_ref[...] = proj("z")
    xbc_ref[...] = proj("xbc")
    dt = proj("dt")
    dt_ref[...] = dt
    dtt_ref[...] = dt.T
    if feature_major_kv:
        for ref, rows in zip(kv_refs, (ckv, skv, wkv)):
            ref[...] = rows.T
    else:
        kv_refs[0][...] = skv
        kv_refs[1][...] = wkv


def _inproj(x, pre, w, n_seq, feature_major_kv):
    t, d = x.shape
    tm = min(ROW_TILE, t)
    seq = t // n_seq
    per_seq = seq // tm
    assert seq % tm == 0

    def cols(height):
        return pl.BlockSpec((None, height, tm), lambda i: (i // per_seq, 0, i % per_seq))

    def rows(width):
        return pl.BlockSpec((tm, width), lambda i: (i, 0))

    widths = (ATT_WIDTH, KV_WIDTH, KV_WIDTH, KV_WIDTH, N_KV_HEADS * TILE, SSD_INNER, CONV_DIM, TILE)
    dtypes = (F32, F32, BF16, BF16, F32, F32, F32, F32)
    out_shape = [jax.ShapeDtypeStruct((t, wd), dt) for wd, dt in zip(widths, dtypes)]
    out_specs = [rows(wd) for wd in widths]
    out_shape.append(jax.ShapeDtypeStruct((n_seq, TILE, seq), F32))
    out_specs.append(cols(TILE))
    if feature_major_kv:
        out_shape += [jax.ShapeDtypeStruct((n_seq, KV_WIDTH, seq), F32)] * 3
        out_specs += [cols(KV_WIDTH)] * 3
    else:
        out_shape += [jax.ShapeDtypeStruct((t, KV_WIDTH), F32)] * 2
        out_specs += [rows(KV_WIDTH)] * 2
    return pl.pallas_call(
        functools.partial(_inproj_body, feature_major_kv=feature_major_kv), grid=(t // tm,),
        in_specs=[rows(d), _const_spec((1, d)), _const_spec((d, PROJ_WIDTH))],
        out_specs=out_specs, out_shape=out_shape,
        compiler_params=_params(1), name="inproj")(x, pre, w)


def _mixout_body(x_ref, oa_ref, os_ref, an_ref, post_ref, wa_ref, ws_ref, o_ref):
    a = _rms(oa_ref[...], an_ref[...]).astype(BF16)
    y = jnp.dot(a, wa_ref[...], preferred_element_type=F32)
    y = y + jnp.dot(os_ref[...], ws_ref[...], preferred_element_type=F32)
    o_ref[...] = x_ref[...] + _rms(y, post_ref[...])


def _mixout_ffn_body(x_ref, oa_ref, os_ref, an_ref, mpost_ref, wa_ref, ws_ref, pre_ref, post_ref, wg_ref, wu_ref,
                     wo_ref, o_ref):
    _mixout_body(x_ref, oa_ref, os_ref, an_ref, mpost_ref, wa_ref, ws_ref, o_ref)
    _ffn_body(o_ref, pre_ref, post_ref, wg_ref, wu_ref, wo_ref, o_ref)


def _mixout_ffn(x, o_att, o_ssd, att_norm, mix_post, wa, ws, pre, post, wg, wu, wo):
    t, d = x.shape
    f = wg.shape[1]
    tm = min(FFN_ROW_TILE, t)
    row = pl.BlockSpec((tm, d), lambda i: (i, 0))
    vec = _const_spec((1, d))
    return pl.pallas_call(
        _mixout_ffn_body, grid=(t // tm,),
        in_specs=[row, row, row, vec, vec, _const_spec((d, d)), _const_spec((d, d)), vec, vec,
                  _const_spec((d, f)), _const_spec((d, f)), _const_spec((f, d))],
        out_specs=row, out_shape=jax.ShapeDtypeStruct((t, d), F32),
        compiler_params=_params(1), name="mixout_ffn")(x, o_att, o_ssd, att_norm, mix_post, wa, ws, pre, post, wg,
                                                       wu, wo)


def _token_rows(ref, k, lo=0, n=None):
    n = ref.shape[-1] if n is None else n
    return ref[k, :, :, lo:lo + n].reshape(KV_GROUP, n).T


def _compress_rows(x_refs, pe_ref, bd_ref, n_blocks, pitch=BLK):
    acc = None
    for l0 in range(0, BLK, COMPRESS_ROWS_PER_DOT):
        pieces = []
        for l in range(l0, l0 + COMPRESS_ROWS_PER_DOT):
            pe = pe_ref[l:l + 1, :]
            xs = jnp.concatenate([x_ref[pl.ds(l, n_blocks, stride=pitch), :] + pe for x_ref in x_refs], axis=0)
            pieces.append(xs.astype(BF16))
        w = bd_ref[l0:l0 + COMPRESS_ROWS_PER_DOT].reshape(COMPRESS_ROWS_PER_DOT * KV_GROUP, KV_GROUP)
        part = jnp.dot(jnp.concatenate(pieces, axis=1), w, preferred_element_type=F32)
        acc = part if acc is None else acc + part
    return acc


def _store_summaries(acc, o_ref, n_blocks):
    for k in range(N_KV_HEADS):
        o_ref[0:n_blocks, k * KV_GROUP:(k + 1) * KV_GROUP] = acc[k * n_blocks:(k + 1) * n_blocks].astype(BF16)


def _compress_body(*refs, n_blocks, n_out):
    x_refs = refs[:N_KV_HEADS]
    pe_ref, bd_ref, o_ref = refs[N_KV_HEADS:]
    _store_summaries(_compress_rows(x_refs, pe_ref, bd_ref, n_blocks), o_ref, n_blocks)
    if n_out > n_blocks:
        o_ref[n_blocks:n_out, :] = jnp.zeros((n_out - n_blocks, KV_WIDTH), BF16)


def _compress(kv, pe, bd, n_out):
    b, s, _ = kv.shape
    n_blocks = s // BLK

    def head_spec(k):
        return pl.BlockSpec((None, s, KV_GROUP), lambda i: (i, 0, k))

    return pl.pallas_call(
        functools.partial(_compress_body, n_blocks=n_blocks, n_out=n_out), grid=(b,),
        in_specs=[head_spec(k) for k in range(N_KV_HEADS)]
        + [_const_spec((BLK, KV_GROUP)), _const_spec((BLK, KV_GROUP, KV_GROUP))],
        out_specs=pl.BlockSpec((None, n_out, KV_WIDTH), lambda i: (i, 0, 0)),
        out_shape=jax.ShapeDtypeStruct((b, n_out, KV_WIDTH), BF16),
        compiler_params=_params(1), name="compress")(*([kv] * N_KV_HEADS), pe, bd)


def _compress_paged_body(pt_ref, *refs, n_pages):
    page_refs = refs[:PAGES_PER_STEP]
    pe_ref, bd_ref, o_ref, stage_ref = refs[PAGES_PER_STEP:]
    j = pl.program_id(1)
    blocks_per_page = PAGE_SIZE // BLK
    for p in range(PAGES_PER_STEP):
        first_block = (j * PAGES_PER_STEP + p) * blocks_per_page
        for k in range(N_KV_HEADS):
            rows = _token_rows(page_refs[p], k)
            for h in range(blocks_per_page):
                at = pl.multiple_of((first_block + h) * STAGE_PITCH, SUBLANES)
                stage_ref[k, pl.ds(at, BLK), :] = rows[h * BLK:(h + 1) * BLK]

    @pl.when(j == pl.num_programs(1) - 1)
    def _():
        n_blocks = n_pages * blocks_per_page
        x_refs = [stage_ref.at[k] for k in range(N_KV_HEADS)]
        _store_summaries(_compress_rows(x_refs, pe_ref, bd_ref, n_blocks, STAGE_PITCH), o_ref, n_blocks)


def _compress_paged(cache, page_table, pe, bd):
    b, n_pages = page_table.shape
    steps = n_pages // PAGES_PER_STEP
    n_blocks = n_pages * (PAGE_SIZE // BLK)

    def page_spec(p):
        return pl.BlockSpec((None, N_KV_HEADS, 2, HEAD_DIM, PAGE_SIZE),
                            lambda i, j, pt: (pt[i, j * PAGES_PER_STEP + p], 0, 0, 0, 0))

    grid_spec = pltpu.PrefetchScalarGridSpec(
        num_scalar_prefetch=1, grid=(b, steps),
        in_specs=[page_spec(p) for p in range(PAGES_PER_STEP)]
        + [pl.BlockSpec((BLK, KV_GROUP), lambda i, j, pt: (0, 0)),
           pl.BlockSpec((BLK, KV_GROUP, KV_GROUP), lambda i, j, pt: (0, 0, 0))],
        out_specs=pl.BlockSpec((None, n_blocks, KV_WIDTH), lambda i, j, pt: (i, 0, 0)),
        scratch_shapes=[pltpu.VMEM((N_KV_HEADS, n_blocks * STAGE_PITCH, KV_GROUP), F32)])
    return pl.pallas_call(
        functools.partial(_compress_paged_body, n_pages=n_pages), grid_spec=grid_spec,
        out_shape=jax.ShapeDtypeStruct((b, n_blocks, KV_WIDTH), BF16),
        compiler_params=_params(2), name="compress_paged")(page_table, *([cache] * PAGES_PER_STEP), pe, bd)


def _rel_bucket(dist):
    n = jnp.maximum(dist, 0)
    max_exact = N_BUCKETS // 2
    nf = jnp.maximum(n, 1).astype(F32)
    log_b = max_exact + (jnp.log(nf / max_exact) / math.log(MAX_DISTANCE / max_exact)
                         * (N_BUCKETS - max_exact)).astype(jnp.int32)
    return jnp.where(n < max_exact, n, jnp.minimum(log_b, N_BUCKETS - 1))


def _bias_of(rel_table, dist, visible):
    onehot = (_rel_bucket(dist)[..., None] == jnp.arange(N_BUCKETS, dtype=jnp.int32)).astype(F32)
    bias = jnp.einsum('rcb,bh->hrc', onehot, rel_table.astype(F32) * LOG2E, precision=lax.Precision.HIGHEST)
    return jnp.where(visible[None], bias, MASK_VALUE).reshape(-1, dist.shape[-1])


KIND_ZERO, KIND_DIAG, KIND_PREV, KIND_MASKED, KIND_OLDEST = range(5)


def _prompt_bias_tables(rel_table):
    r = jnp.arange(TILE, dtype=jnp.int32)[:, None]
    c = jnp.arange(TILE, dtype=jnp.int32)[None, :]
    true = jnp.ones((TILE, TILE), bool)
    far = _bias_of(rel_table, jnp.full((TILE, TILE), MAX_DISTANCE, jnp.int32), true)
    tiles = jnp.stack([
        jnp.zeros_like(far),
        _bias_of(rel_table, r - c, r >= c) - far,
        _bias_of(rel_table, TILE + r - c, true) - far,
        jnp.full_like(far, MASK_VALUE),
        jnp.where(jnp.tile(r < c, (N_ATT_HEADS, 1)), 0.0, MASK_VALUE),
    ])
    c_dist = r - (BLK - 1) + BLK * (BLK - 1 - c)
    cmp_tab = _bias_of(rel_table, c_dist, c_dist >= 0)
    return tiles, cmp_tab


def _selection_mask(score, n_rows):
    n_groups = score.shape[0] // SUBLANES
    groups = [score[SUBLANES * j:SUBLANES * (j + 1)] for j in range(n_groups)]
    ranks = [jnp.zeros(groups[0].shape, jnp.int32) for _ in range(n_groups)]
    row8 = lax.broadcasted_iota(jnp.int32, groups[0].shape, 0)
    for n in range(n_rows):
        other = jnp.broadcast_to(score[n:n + 1, :], groups[0].shape)
        for j in range(n_groups):
            lo = SUBLANES * j
            if lo > n:
                inc = jnp.where(other >= groups[j], 1, 0)
            elif lo + SUBLANES - 1 <= n:
                inc = jnp.where(other > groups[j], 1, 0)
            else:
                inc = jnp.where(row8 + lo > n, jnp.where(other >= groups[j], 1, 0),
                                jnp.where(other > groups[j], 1, 0))
            ranks[j] = ranks[j] + inc
    rank = jnp.concatenate(ranks, axis=0)
    return jnp.where(rank < N_SEL, jnp.where(score > -jnp.inf, 0.0, MASK_VALUE), MASK_VALUE)


PAIR = 2 * TILE
Q_TILES = PAIR // TILE
SEL_UNROLLS = (8, 4, 1)
SEL_PAD_UNROLL = 4
_SEL_KINDS = {0: ((KIND_DIAG, KIND_MASKED), (KIND_PREV, KIND_DIAG)),
              1: ((KIND_ZERO, KIND_PREV), (KIND_ZERO, KIND_ZERO))}
_WIN_OLDEST = ((KIND_OLDEST, KIND_ZERO), (KIND_MASKED, KIND_OLDEST))


def _nsa_prompt_body(q_ref, gates_ref, kvc_ref, skv_ref, wkv_ref, tb_ref, ctab_ref, o_ref,
                     s_ref, sw_ref, mx_ref, mb_ref, acc_ref, part_ref, *, n_blocks):
    step = pl.program_id(2)
    head_rows = GQA * TILE
    rows = Q_TILES * head_rows
    lane = lax.broadcasted_iota(jnp.int32, (rows, TILE), 1)
    low = lane < HEAD_DIM
    key_lane = lax.broadcasted_iota(jnp.int32, (PAIR, TILE), 1)
    key_blk = lax.broadcasted_iota(jnp.int32, (PAIR, TILE), 0) // BLK
    n_pairs = skv_ref.shape[0] // PAIR

    qf = q_ref[...]
    halves = [qf[:, c * TILE:(c + 1) * TILE] for c in range(GQA // 2)]
    rolled = [pltpu.roll(h, HEAD_DIM, 1) for h in halves]
    q_all = jnp.concatenate([(halves if g % 2 == 0 else rolled)[g // 2][t * TILE:(t + 1) * TILE]
                             for t in range(Q_TILES) for g in range(GQA)], axis=0)
    lhs_plain = jnp.where(low, q_all, 0.0).astype(BF16)

    def ones_and_values(kv):
        kv_lane = lax.broadcasted_iota(jnp.int32, kv.shape, 1)
        return jnp.where(kv_lane < HEAD_DIM, jnp.ones_like(kv), kv)

    def normalise(acc):
        return acc / jnp.maximum(pltpu.roll(acc, HEAD_DIM, 1), 1e-30)

    def bias_of(kinds_per_tile, valid):
        parts = []
        for left, right in kinds_per_tile:
            left = jnp.where(valid, left, KIND_MASKED)
            right = jnp.where(valid, right, KIND_MASKED)
            parts.append(jnp.concatenate([tb_ref[left], tb_ref[right]], axis=1))
        return jnp.concatenate(parts, axis=0)

    def sel_kinds(j):
        behind = step - j
        return [tuple(jnp.where(behind == 0, _SEL_KINDS[0][t][side],
                                jnp.where(behind == 1, _SEL_KINDS[1][t][side], KIND_ZERO)) for side in range(2))
                for t in range(Q_TILES)]

    def keys_of(kv_ref, j):
        jc = jnp.clip(j, 0, n_pairs - 1)
        return kv_ref[pl.ds(pl.multiple_of(jc * PAIR, PAIR), PAIR), :]

    def score_pair(lhs, kv_ref, j, bias, block_columns):
        rhs = keys_of(kv_ref, j)
        if block_columns:
            onehot = jnp.where(key_lane - HEAD_DIM == (PAIR // BLK) * j + key_blk, 1.0, 0.0).astype(BF16)
            rhs = jnp.where(key_lane < HEAD_DIM, rhs, onehot)
        return lax.dot_general(lhs, rhs, NT_DIMS, preferred_element_type=F32) + bias

    def half_max(sc):
        return jnp.maximum(sc[:, :TILE], sc[:, TILE:])

    def weigh_pair(sc, mb, kv_ref, j):
        pr = jnp.exp2(sc - jnp.concatenate([mb, mb], axis=1))
        return jnp.dot(pr.astype(BF16), ones_and_values(keys_of(kv_ref, j)), preferred_element_type=F32)

    def row_max(mx):
        return jnp.broadcast_to(jnp.max(mx, axis=1, keepdims=True), (rows, TILE))

    n_win = WINDOW // PAIR + 1
    w_pairs = [step - (n_win - 1) + w for w in range(n_win)]
    w_kinds = [_WIN_OLDEST] + [[(KIND_ZERO, KIND_ZERO)] * Q_TILES] * (n_win - 3) + [_SEL_KINDS[1], _SEL_KINDS[0]]
    w_mx = None
    for w, j in enumerate(w_pairs):
        sc = score_pair(lhs_plain, wkv_ref, j, bias_of(w_kinds[w], j >= 0), False)
        sw_ref[w] = sc
        w_mx = half_max(sc) if w_mx is None else jnp.maximum(w_mx, half_max(sc))

    kvc = kvc_ref[...]
    bias = []
    for t in range(Q_TILES):
        i = Q_TILES * step + t
        shift = (2 * i + (TILE - (BLK - 1))) % TILE
        bias += [pltpu.roll(ctab_ref[g * TILE:(g + 1) * TILE, :], shift, 1) for g in range(GQA)]
    s = lax.dot_general(lhs_plain, kvc, NT_DIMS, preferred_element_type=F32)
    s = jnp.where(lane < n_blocks, s + jnp.concatenate(bias, axis=0), MASK_VALUE)
    m = jnp.maximum(jnp.max(s, axis=1, keepdims=True), M_INIT)
    p = jnp.exp2(s - m)
    c_acc = jnp.dot(p.astype(BF16), ones_and_values(kvc), preferred_element_type=F32)
    p = p / jnp.maximum(c_acc, 1e-30)
    o_cmp = normalise(c_acc)

    w_mb = row_max(jnp.maximum(w_mx, M_INIT))
    w_acc = None
    for w, j in enumerate(w_pairs):
        dacc = weigh_pair(sw_ref[w], w_mb, wkv_ref, j)
        w_acc = dacc if w_acc is None else w_acc + dacc
    o_win = normalise(w_acc)

    gate = jax.nn.sigmoid(gates_ref[...])

    def gate_col(t, g, br):
        c = g * N_BRANCH + br
        return gate[t * TILE:(t + 1) * TILE, c:c + 1]

    heads = [(t, g, slice(t * head_rows + g * TILE, t * head_rows + (g + 1) * TILE))
             for t in range(Q_TILES) for g in range(GQA)]
    for t, g, head in heads:
        part_ref[head, :] = gate_col(t, g, 0) * o_cmp[head] + gate_col(t, g, 2) * o_win[head]

    blk = lax.broadcasted_iota(jnp.int32, (n_blocks, PAIR), 0)
    query = lax.broadcasted_iota(jnp.int32, (n_blocks, PAIR), 1)
    cur = (PAIR // BLK) * step + query // BLK
    imp_t = []
    for t in range(Q_TILES):
        imp = p[t * head_rows:t * head_rows + TILE]
        for g in range(1, GQA):
            imp = imp + p[t * head_rows + g * TILE:t * head_rows + (g + 1) * TILE]
        imp_t.append(imp.T[0:n_blocks])
    forced = (blk == 0) | (blk == cur) | (blk == cur - 1)
    score = jnp.where(forced, jnp.inf, jnp.where(blk <= cur, jnp.concatenate(imp_t, axis=1), -jnp.inf))
    neg_t = _selection_mask(score, n_blocks)
    neg = []
    for t in range(Q_TILES):
        pieces = [jnp.zeros((HEAD_DIM, TILE), F32), neg_t[:, t * TILE:(t + 1) * TILE]]
        if n_blocks < TILE - HEAD_DIM:
            pieces.append(jnp.zeros((TILE - HEAD_DIM - n_blocks, TILE), F32))
        neg += [jnp.concatenate(pieces, axis=0).T] * GQA
    lhs_sel = jnp.where(low, q_all, jnp.concatenate(neg, axis=0)).astype(BF16)

    n_sel = step + 1
    spans, first = [], 0
    for unroll in SEL_UNROLLS:
        left = n_sel - first
        n_trips = left // unroll if unroll != SEL_PAD_UNROLL else (left + 1) // unroll
        spans.append((unroll, first, jnp.maximum(n_trips, 0)))
        first = first + n_trips * unroll
    mx_ref[...] = jnp.full(mx_ref.shape, M_INIT, F32)

    def score_trips(unroll, first, n_trips):
        def trip(t, carry):
            mx = None
            for u in range(unroll):
                j = first + t * unroll + u
                sc = score_pair(lhs_sel, skv_ref, j, bias_of(sel_kinds(j), j <= step), True)
                s_ref[j] = sc
                mx = half_max(sc) if mx is None else jnp.maximum(mx, half_max(sc))
            mx_ref[...] = jnp.maximum(mx_ref[...], mx)
            return carry
        lax.fori_loop(0, n_trips, trip, 0)

    for span in spans:
        score_trips(*span)
    mb_ref[...] = row_max(mx_ref[...])
    acc_ref[...] = jnp.zeros(acc_ref.shape, F32)

    def weigh_trips(unroll, first, n_trips):
        def trip(t, carry):
            mb = mb_ref[...]
            acc = None
            for u in range(unroll):
                j = first + t * unroll + u
                dacc = weigh_pair(s_ref[j], mb, skv_ref, j)
                acc = dacc if acc is None else acc + dacc
            acc_ref[...] += acc
            return carry
        lax.fori_loop(0, n_trips, trip, 0)

    for span in spans:
        weigh_trips(*span)
    o_sel = normalise(acc_ref[...])

    out_low = lax.broadcasted_iota(jnp.int32, (TILE, TILE), 1) < HEAD_DIM
    outs = {(t, g): part_ref[head, :] + gate_col(t, g, 1) * o_sel[head] for t, g, head in heads}
    for t in range(Q_TILES):
        for c in range(GQA // 2):
            o_ref[t * TILE:(t + 1) * TILE, c * TILE:(c + 1) * TILE] = jnp.where(
                out_low, pltpu.roll(outs[t, 2 * c], HEAD_DIM, 1), outs[t, 2 * c + 1])


def _nsa_prompt(q, gates, kvc, skv_b, wkv_b, tiles, cmp_tab):
    b, s, _ = q.shape
    n_blocks = s // BLK
    assert n_blocks <= TILE - HEAD_DIM and kvc.shape[1] == TILE and s % PAIR == 0 and WINDOW // PAIR >= 2
    gw = GQA * HEAD_DIM
    head_rows = GQA * TILE
    rows = Q_TILES * head_rows
    return pl.pallas_call(
        functools.partial(_nsa_prompt_body, n_blocks=n_blocks), grid=(b, N_KV_HEADS, s // PAIR),
        in_specs=[pl.BlockSpec((None, PAIR, gw), lambda bi, k, i: (bi, i, k)),
                  pl.BlockSpec((None, PAIR, TILE), lambda bi, k, i: (bi, i, k)),
                  pl.BlockSpec((None, TILE, KV_GROUP), lambda bi, k, i: (bi, 0, k)),
                  pl.BlockSpec((None, s, KV_GROUP), lambda bi, k, i: (bi, 0, k)),
                  pl.BlockSpec((None, s, KV_GROUP), lambda bi, k, i: (bi, 0, k)),
                  pl.BlockSpec((len(tiles), head_rows, TILE), lambda bi, k, i: (0, k, 0)),
                  pl.BlockSpec((head_rows, TILE), lambda bi, k, i: (k, 0))],
        out_specs=pl.BlockSpec((None, PAIR, gw), lambda bi, k, i: (bi, i, k)),
        out_shape=jax.ShapeDtypeStruct((b, s, ATT_WIDTH), F32),
        scratch_shapes=[pltpu.VMEM((s // PAIR, rows, PAIR), F32),
                        pltpu.VMEM((WINDOW // PAIR + 1, rows, PAIR), F32),
                        pltpu.VMEM((rows, TILE), F32), pltpu.VMEM((rows, TILE), F32),
                        pltpu.VMEM((rows, KV_GROUP), F32), pltpu.VMEM((rows, TILE), F32)],
        compiler_params=_params(3), name="nsa_prompt")(q, gates, kvc, skv_b, wkv_b, tiles, cmp_tab)


def _nsa_sample_body(pt_ref, *refs, n_pages, dec_seq, past_len):
    page_refs = refs[:PAGES_PER_STEP]
    (qk_ref, gates_ref, kc_ref, ctab_ref, gsum_ref, knew_ref, vnew_ref, cwin_ref, kwnew_ref, vwnew_ref,
     last_ref, new_ref, win0_ref, o_ref, lhs_ref, m_ref, l_ref, acc_ref, ow_ref, oc_ref) = refs[PAGES_PER_STEP:]
    j = pl.program_id(1)
    last_step = pl.num_programs(1) - 1
    blocks_per_page = PAGE_SIZE // BLK
    n_past = n_pages * blocks_per_page
    kd = N_KV_HEADS * HEAD_DIM
    n_lanes = kc_ref.shape[0]
    step_keys = PAGES_PER_STEP * PAGE_SIZE

    def gate_col(br):
        return jax.nn.sigmoid(gates_ref[:, br:br + 1])

    def head_major(ref, c, lo=0, n=None):
        n = ref.shape[-1] if n is None else n
        return ref[:, c, :, lo:lo + n].reshape(kd, n).astype(BF16)

    def lane_tiles(x):
        return [x[:, t * TILE:(t + 1) * TILE] for t in range(x.shape[1] // TILE)]

    def tile_max(tiles):
        m = tiles[0]
        for t in tiles[1:]:
            m = jnp.maximum(m, t)
        return jnp.broadcast_to(jnp.max(m, axis=1, keepdims=True), (TILE, TILE))

    def tile_sum(tiles):
        s = tiles[0]
        for t in tiles[1:]:
            s = s + t
        return s

    def online_update(s, values_t):
        tiles = lane_tiles(s)
        m_prev = m_ref[...]
        m_new = jnp.maximum(m_prev, tile_max(tiles))
        alpha = jnp.exp2(m_prev - m_new)
        p = [jnp.exp2(t - m_new) for t in tiles]
        l_ref[...] = alpha * l_ref[...] + tile_sum(p)
        pv = lax.dot_general(jnp.concatenate(p, axis=1).astype(BF16), values_t, NT_DIMS, preferred_element_type=F32)
        acc_ref[...] = jnp.concatenate([alpha] * (kd // TILE), axis=1) * acc_ref[...] + pv
        m_ref[...] = m_new

    @pl.when(j == 0)
    def _():
        kc = kc_ref[...]
        qf = qk_ref[...].astype(F32)
        gap = jnp.zeros((TILE, HEAD_DIM), F32)
        qc = jnp.concatenate([piece for k in range(N_KV_HEADS)
                              for piece in (qf[:, k * HEAD_DIM:(k + 1) * HEAD_DIM], gap)], axis=1).astype(BF16)
        s = lax.dot_general(qc, kc, NT_DIMS, preferred_element_type=F32) + ctab_ref[...]
        m = jnp.maximum(jnp.max(s, axis=1, keepdims=True), M_INIT)
        p = jnp.exp2(s - m)
        p = p / jnp.maximum(jnp.sum(p, axis=1, keepdims=True), 1e-30)
        oc_ref[...] = gate_col(0) * jnp.dot(p.astype(BF16), kc, preferred_element_type=F32)
        imp = _select_sum(p, gsum_ref[...], False)
        imp_t = jnp.concatenate([t.T for t in lane_tiles(imp)], axis=0)
        n_rows = -(-(n_past + 1) // SUBLANES) * SUBLANES
        blk = lax.broadcasted_iota(jnp.int32, (n_rows, TILE), 0)
        tok = lax.broadcasted_iota(jnp.int32, (n_rows, TILE), 1) % dec_seq
        cur = (past_len + tok) // BLK
        forced = (blk == 0) | (blk == cur) | (blk == cur - 1)
        score = jnp.where(forced, jnp.inf, jnp.where(blk <= cur, imp_t[0:n_rows], -jnp.inf))
        neg_t = jnp.concatenate([_selection_mask(score, n_past + 1), jnp.zeros((n_lanes - n_rows, TILE), F32)], axis=0)
        neg = jnp.concatenate([neg_t[t * TILE:(t + 1) * TILE].T for t in range(n_lanes // TILE)], axis=1)
        lhs_ref[:, 0:kd] = qk_ref[...]
        lhs_ref[:, kd:kd + n_lanes] = neg.astype(BF16)

        qk = qk_ref[...]
        n_win = cwin_ref.shape[-1] // TILE
        zero = jnp.zeros((TILE, TILE), F32)
        s_w = jnp.dot(qk, head_major(cwin_ref, 0), preferred_element_type=F32)
        s_w = s_w + jnp.concatenate([win0_ref[...]] + [zero] * (n_win - 2) + [last_ref[...]], axis=1)
        s_n = jnp.dot(qk, kwnew_ref[...], preferred_element_type=F32) + new_ref[...]
        tiles = lane_tiles(s_w) + [s_n]
        m = tile_max(tiles)
        p = [jnp.exp2(t - m) for t in tiles]
        l = jnp.sum(tile_sum(p), axis=1, keepdims=True)
        o_w = lax.dot_general(jnp.concatenate(p[:-1], axis=1).astype(BF16), head_major(cwin_ref, 1), NT_DIMS,
                              preferred_element_type=F32)
        o_w = o_w + lax.dot_general(p[-1].astype(BF16), vwnew_ref[...], NT_DIMS, preferred_element_type=F32)
        ow_ref[...] = gate_col(2) * (o_w / l)

        m_ref[...] = jnp.full(m_ref.shape, M_INIT, F32)
        l_ref[...] = jnp.zeros(l_ref.shape, F32)
        acc_ref[...] = jnp.zeros(acc_ref.shape, F32)

    keys_t = jnp.concatenate([head_major(r, 0) for r in page_refs], axis=1)
    values_t = jnp.concatenate([head_major(r, 1) for r in page_refs], axis=1)
    blk_row = lax.broadcasted_iota(jnp.int32, (n_lanes, step_keys), 0)
    key_blk = j * (PAGES_PER_STEP * blocks_per_page) + lax.broadcasted_iota(jnp.int32, (n_lanes, step_keys), 1) // BLK
    onehot = jnp.where(blk_row == key_blk, 1.0, 0.0).astype(BF16)
    s = jnp.dot(lhs_ref[...], jnp.concatenate([keys_t, onehot], axis=0), preferred_element_type=F32)
    tiles = lane_tiles(s)
    tiles[-1] = tiles[-1] + jnp.where(j == last_step, last_ref[...], 0.0)
    online_update(jnp.concatenate(tiles, axis=1), values_t)

    @pl.when(j == last_step)
    def _():
        new_blk = lax.broadcasted_iota(jnp.int32, (n_lanes, TILE), 0) == n_past
        rhs = jnp.concatenate([knew_ref[...], jnp.where(new_blk, 1.0, 0.0).astype(BF16)], axis=0)
        online_update(jnp.dot(lhs_ref[...], rhs, preferred_element_type=F32) + new_ref[...], vnew_ref[...])
        l = jnp.sum(l_ref[...], axis=1, keepdims=True)
        o_sw = gate_col(1) * (acc_ref[...] / l) + ow_ref[...]
        o_c = oc_ref[...]
        rows_per_head = TILE // N_KV_HEADS
        for k in range(N_KV_HEADS):
            rows = slice(k * rows_per_head, (k + 1) * rows_per_head)
            v_lo = k * KV_GROUP + HEAD_DIM
            o_ref[rows, :] = o_sw[rows, k * HEAD_DIM:(k + 1) * HEAD_DIM] + o_c[rows, v_lo:v_lo + HEAD_DIM]


def _nsa_sample(page_table, cache_slc, qk, gates_c, kc, ctab, gsum, k_new, v_new, cache_win, kw_new, vw_new,
                last_tab, new_tab, win0_tab, *, dec_seq, past_len):
    b, n_pages = page_table.shape
    steps = n_pages // PAGES_PER_STEP
    n_lanes = kc.shape[1]
    w_buf = cache_win.shape[-1]
    kd = N_KV_HEADS * HEAD_DIM

    def page_spec(p):
        return pl.BlockSpec((None, N_KV_HEADS, 2, HEAD_DIM, PAGE_SIZE),
                            lambda i, j, pt: (pt[i, j * PAGES_PER_STEP + p], 0, 0, 0, 0))

    def per_batch(*shape):
        zeros = (0,) * len(shape)
        return pl.BlockSpec((None,) + shape, lambda i, j, pt: (i,) + zeros)

    def const(*shape):
        zeros = (0,) * len(shape)
        return pl.BlockSpec(shape, lambda i, j, pt: zeros)

    grid_spec = pltpu.PrefetchScalarGridSpec(
        num_scalar_prefetch=1, grid=(b, steps),
        in_specs=[page_spec(p) for p in range(PAGES_PER_STEP)]
        + [per_batch(TILE, kd), per_batch(TILE, TILE), per_batch(n_lanes, KV_WIDTH),
           const(TILE, n_lanes), const(TILE, TILE), per_batch(kd, TILE), per_batch(kd, TILE),
           per_batch(N_KV_HEADS, 2, HEAD_DIM, w_buf), per_batch(kd, TILE), per_batch(kd, TILE),
           const(TILE, TILE), const(TILE, TILE), const(TILE, TILE)],
        out_specs=per_batch(TILE, HEAD_DIM),
        scratch_shapes=[pltpu.VMEM((TILE, kd + n_lanes), BF16), pltpu.VMEM((TILE, TILE), F32),
                        pltpu.VMEM((TILE, TILE), F32), pltpu.VMEM((TILE, kd), F32), pltpu.VMEM((TILE, kd), F32),
                        pltpu.VMEM((TILE, KV_WIDTH), F32)])
    return pl.pallas_call(
        functools.partial(_nsa_sample_body, n_pages=n_pages, dec_seq=dec_seq, past_len=past_len),
        grid_spec=grid_spec, out_shape=jax.ShapeDtypeStruct((b, TILE, HEAD_DIM), F32),
        compiler_params=_params(2), name="nsa_sample")(
            page_table, *([cache_slc] * PAGES_PER_STEP), qk, gates_c, kc, ctab, gsum, k_new, v_new, cache_win,
            kw_new, vw_new, last_tab, new_tab, win0_tab)


def _split3(x):
    hi = x.astype(BF16)
    r = x - hi.astype(F32)
    mid = r.astype(BF16)
    return hi, mid, (r - mid.astype(F32)).astype(BF16)


def _select_sum(x, onehot, x_on_left):
    out = None
    for part in _split3(x):
        term = (jnp.dot(part, onehot, preferred_element_type=F32) if x_on_left
                else jnp.dot(onehot, part, preferred_element_type=F32))
        out = term if out is None else out + term
    return out


def _softplus(x):
    return jnp.maximum(x, 0.0) + jnp.log(1.0 + jnp.exp(-jnp.abs(x)))


def _ssd_body(z_ref, xbc_ref, dt_ref, dtt_ref, convp_ref, h0_ref, cw_ref, cb_ref, dtb_ref, dtbt_ref, al_ref,
              alt_ref, dskip_ref, norm_ref, o_ref, hout_ref, convout_ref, xp_ref, *, cl):
    c = pl.program_id(1)
    pad = SUBLANES

    @pl.when(c == 0)
    def _():
        xp_ref[0:pad, :] = convp_ref[...]
        hout_ref[...] = h0_ref[...]

    xp_ref[pad:pad + cl, :] = xbc_ref[...]
    conv = cb_ref[...]
    for k in range(CONV_W):
        lo = pad - (CONV_W - 1) + k
        conv = conv + cw_ref[k:k + 1, :] * xp_ref[lo:lo + cl, :]
    u = conv * jax.nn.sigmoid(conv)
    tail = xp_ref[pad + cl - (CONV_W - 1):pad + cl, :]
    convout_ref[...] = tail
    xp_ref[pad - (CONV_W - 1):pad, :] = tail

    xs = u[:, :SSD_INNER]
    dt = _softplus(dt_ref[:, 0:SSD_HEADS] + dtb_ref[...])
    dt_t = _softplus(dtt_ref[0:SSD_HEADS, :] + dtbt_ref[...])
    la = dt * -jnp.exp(al_ref[...])
    la_t = dt_t * -jnp.exp(alt_ref[...])
    row = lax.broadcasted_iota(jnp.int32, (cl, cl), 0)
    col = lax.broadcasted_iota(jnp.int32, (cl, cl), 1)
    causal = row >= col
    cs = _select_sum(la, jnp.where(causal, 1.0, 0.0).astype(BF16), False)
    cs_t = _select_sum(la_t, jnp.where(row <= col, 1.0, 0.0).astype(BF16), True)

    def spread(width):
        lane_head = lax.broadcasted_iota(jnp.int32, (SSD_HEADS, SSD_HEADS * width), 1) // width
        return jnp.where(lane_head == lax.broadcasted_iota(jnp.int32, lane_head.shape, 0), 1.0, 0.0).astype(BF16)

    dt_x = _select_sum(dt, spread(SSD_HEAD_DIM), True)
    cs_x = _select_sum(cs, spread(SSD_HEAD_DIM), True)
    cs_wide = _select_sum(cs, spread(TILE), True)
    last_x = cs_x[cl - 1:cl, :]
    xdt = xs * dt_x
    xdt_b = xdt.astype(BF16)
    to_end = (xdt * jnp.exp(last_x - cs_x)).astype(BF16)
    grow = jnp.exp(cs_x)
    skip = dskip_ref[...] * xs

    heads_per_group = SSD_HEADS // SSD_GROUPS
    bms, cms, cbs = [], [], []
    for g in range(SSD_GROUPS):
        b_lo = SSD_INNER + g * SSD_STATE
        c_lo = SSD_INNER + SSD_GROUPS * SSD_STATE + g * SSD_STATE
        bms.append(u[:, b_lo:b_lo + SSD_STATE].astype(BF16))
        cms.append(u[:, c_lo:c_lo + SSD_STATE].astype(BF16))
        cbs.append(lax.dot_general(cms[g], bms[g], NT_DIMS, preferred_element_type=F32))
    first_head = lax.broadcasted_iota(jnp.int32, (cl, TILE), 1) < SSD_HEAD_DIM
    state_row = lax.broadcasted_iota(jnp.int32, (2 * SSD_HEAD_DIM, SSD_STATE), 0)
    ys = []
    for pair in range(SSD_HEADS // 2):
        g = 2 * pair // heads_per_group
        lanes = slice(pair * TILE, (pair + 1) * TILE)
        y_heads, keep = [], []
        for h in (2 * pair, 2 * pair + 1):
            diff = cs_wide[:, h * TILE:h * TILE + cl] - cs_t[h:h + 1, :]
            decay = jnp.where(causal, jnp.exp(jnp.where(causal, diff, 0.0)), 0.0)
            y_heads.append(jnp.dot((cbs[g] * decay).astype(BF16), xdt_b[:, lanes], preferred_element_type=F32))
            keep.append(jnp.exp(cs[cl - 1:cl, h:h + 1]))
        state_in = lax.dot_general(to_end[:, lanes], bms[g], TN_DIMS, preferred_element_type=F32)
        h_prev = jnp.concatenate([hout_ref[2 * pair], hout_ref[2 * pair + 1]], axis=0)
        y_off = lax.dot_general(cms[g], h_prev.astype(BF16), NT_DIMS, preferred_element_type=F32)
        h_new = h_prev * jnp.where(state_row < SSD_HEAD_DIM, keep[0], keep[1]) + state_in
        hout_ref[2 * pair] = h_new[0:SSD_HEAD_DIM]
        hout_ref[2 * pair + 1] = h_new[SSD_HEAD_DIM:]
        ys.append(jnp.where(first_head, y_heads[0], y_heads[1]) + y_off * grow[:, lanes] + skip[:, lanes])
    y = jnp.concatenate(ys, axis=1)
    z = z_ref[...]
    y = y * (z * jax.nn.sigmoid(z))
    gw = SSD_INNER // SSD_GROUPS
    normed = []
    for g in range(SSD_GROUPS):
        yg = y[:, g * gw:(g + 1) * gw]
        normed.append(yg * lax.rsqrt(jnp.mean(yg * yg, axis=-1, keepdims=True) + EPS))
    o_ref[...] = (jnp.concatenate(normed, axis=1) * norm_ref[...]).astype(o_ref.dtype)


def _ssd(z, xbc, dt, dt_t, conv_prev, h0, consts, cl):
    b, seq, _ = z.shape
    nc = seq // cl
    cw, cbias, dtb, dtb_t, al, al_t, dskip, norm = consts

    def rows(width):
        return pl.BlockSpec((None, cl, width), lambda i, c: (i, c, 0))

    state = pl.BlockSpec((None, SSD_HEADS, SSD_HEAD_DIM, SSD_STATE), lambda i, c: (i, 0, 0, 0))

    def const(x):
        zeros = (0,) * x.ndim
        return pl.BlockSpec(x.shape, lambda i, c: zeros)

    return pl.pallas_call(
        functools.partial(_ssd_body, cl=cl), grid=(b, nc),
        in_specs=[rows(SSD_INNER), rows(CONV_DIM), rows(TILE),
                  pl.BlockSpec((None, TILE, cl), lambda i, c: (i, 0, c)),
                  pl.BlockSpec((None, SUBLANES, CONV_DIM), lambda i, c: (i, 0, 0)), state,
                  const(cw), const(cbias), const(dtb), const(dtb_t), const(al), const(al_t), const(dskip),
                  const(norm)],
        out_specs=[rows(SSD_INNER), state, pl.BlockSpec((None, CONV_W - 1, CONV_DIM), lambda i, c: (i, 0, 0))],
        out_shape=[jax.ShapeDtypeStruct((b, seq, SSD_INNER), BF16),
                   jax.ShapeDtypeStruct((b, SSD_HEADS, SSD_HEAD_DIM, SSD_STATE), F32),
                   jax.ShapeDtypeStruct((b, CONV_W - 1, CONV_DIM), F32)],
        scratch_shapes=[pltpu.VMEM((SUBLANES + cl, CONV_DIM), F32)],
        compiler_params=_params(2), name="ssd")(z, xbc, dt, dt_t, conv_prev, h0, cw, cbias, dtb, dtb_t, al, al_t,
                                                 dskip, norm)


def _pack_inproj(w_in):
    splits = (ATT_WIDTH, KV_WIDTH, KV_WIDTH, KV_WIDTH, N_BRANCH * N_ATT_HEADS, SSD_INNER, CONV_DIM, SSD_HEADS)
    offs = [0]
    for s in splits:
        offs.append(offs[-1] + s)
    wq, wc, ws, ww, wg, wz, wx, wd = [w_in[:, offs[i]:offs[i + 1]] for i in range(len(splits))]
    d = w_in.shape[0]
    per_kv = GQA * N_BRANCH
    wg4 = jnp.pad(wg.reshape(d, N_KV_HEADS, per_kv), ((0, 0), (0, 0), (0, TILE - per_kv)))
    wd_pad = jnp.pad(wd, ((0, 0), (0, TILE - SSD_HEADS)))
    w = jnp.concatenate([wq * (HEAD_DIM ** -0.5 * LOG2E), wc, ws, ww, wg4.reshape(d, N_KV_HEADS * TILE), wz, wx,
                         wd_pad], axis=1)
    assert w.shape[1] == PROJ_WIDTH
    return w.astype(BF16)


def _pack_compress(cmp_pe, w_cmp):
    zero = jnp.zeros((BLK, HEAD_DIM, HEAD_DIM), w_cmp.dtype)
    bd = jnp.concatenate([jnp.concatenate([w_cmp[:, 0], zero], axis=2),
                          jnp.concatenate([zero, w_cmp[:, 1]], axis=2)], axis=1)
    pe = cmp_pe.reshape(BLK, KV_GROUP)
    return pe.astype(F32), bd.astype(BF16)


def _sample_tables(rel_table, dec_seq, past_len, n_lanes):
    col = jnp.arange(TILE, dtype=jnp.int32)
    head = (col // (GQA * dec_seq)) * GQA + (col // dec_seq) % GQA
    tok = (col % dec_seq)[:, None]
    per_col = (rel_table.astype(F32) * LOG2E)[:, head]

    def bias(dist):
        onehot = (_rel_bucket(dist)[..., None] == jnp.arange(N_BUCKETS, dtype=jnp.int32)).astype(F32)
        return jnp.einsum('ckb,bc->ck', onehot, per_col, precision=lax.Precision.HIGHEST)

    lane = jnp.arange(TILE, dtype=jnp.int32)[None, :]
    far = bias(jnp.full((TILE, 1), MAX_DISTANCE, jnp.int32))
    blk = jnp.arange(n_lanes, dtype=jnp.int32)[None, :]
    c_dist = past_len + tok - ((blk + 1) * BLK - 1)
    ctab = jnp.where(c_dist >= 0, bias(c_dist), MASK_VALUE)
    last_tab = bias(TILE + tok - lane) - far
    new_tab = jnp.where((lane <= tok) & (lane < dec_seq), bias(tok - lane) - far, MASK_VALUE)
    win0_tab = jnp.where(lane > tok, 0.0, MASK_VALUE)
    same = (col[:, None] // (GQA * dec_seq) == col[None, :] // (GQA * dec_seq)) & \
           (col[:, None] % dec_seq == col[None, :] % dec_seq)
    return ctab, last_tab, new_tab, win0_tab, same.astype(BF16)


def _layer_weights(ffn_pre, ffn_post, ffn_w_in, ffn_w_out):
    f = ffn_w_out.shape[0]
    return (ffn_pre[None, :], ffn_post[None, :], ffn_w_in[:, :f].astype(BF16), ffn_w_in[:, f:].astype(BF16),
            ffn_w_out.astype(BF16))


def kernel(x_prompt, x_sample, cache_cmp_kv, cache_slc_kv, cache_win_kv, state_ssm, state_conv, page_table,
           rel_table, ffn1_pre, ffn1_post, ffn1_w_in, ffn1_w_out, mix_pre, mix_post, w_in, w_out, att_out_norm,
           cmp_pe, w_cmp, conv_w, conv_b, dt_bias, a_log, d_skip, ssd_norm,
           ffn2_pre, ffn2_post, ffn2_w_in, ffn2_w_out):
    depth = ffn1_pre.shape[0]
    bp, seq, d = x_prompt.shape
    bs, dec_seq, _ = x_sample.shape
    n_pages = page_table.shape[1]
    past_len = n_pages * PAGE_SIZE
    w_buf = cache_win_kv.shape[2]
    kv_shape = (N_KV_HEADS, 2, HEAD_DIM)
    assert GQA * dec_seq * N_KV_HEADS == TILE and w_buf == WINDOW and seq % TILE == 0

    yp = x_prompt.reshape(bp * seq, d)
    ys = x_sample.reshape(bs * dec_seq, d)
    outs = [[] for _ in range(10)]
    for l in range(depth):
        ffn1 = _layer_weights(ffn1_pre[l], ffn1_post[l], ffn1_w_in[l], ffn1_w_out[l])
        ffn2 = _layer_weights(ffn2_pre[l], ffn2_post[l], ffn2_w_in[l], ffn2_w_out[l])
        w_proj = _pack_inproj(w_in[l])
        pe, bd = _pack_compress(cmp_pe[l], w_cmp[l])
        wo_att = w_out[l][:ATT_WIDTH].astype(BF16)
        wo_ssd = w_out[l][ATT_WIDTH:].astype(BF16)
        ssd_consts = (conv_w[l], conv_b[l][None, :], dt_bias[l][None, :], dt_bias[l][:, None], a_log[l][None, :],
                      a_log[l][:, None], jnp.repeat(d_skip[l], SSD_HEAD_DIM)[None, :], ssd_norm[l][None, :])
        mix_pre_l, mix_post_l, att_norm_l = mix_pre[l][None, :], mix_post[l][None, :], att_out_norm[l][None, :]

        yp = _ffn(yp, *ffn1)
        q, ckv, skv_b, wkv_b, gates, z, xbc, dt, dt_t, ckv_t, skv_t, wkv_t = _inproj(
            yp, mix_pre_l, w_proj, bp, True)
        kvc = _compress(ckv.reshape(bp, seq, KV_WIDTH), pe, bd, TILE)
        tiles, cmp_tab = _prompt_bias_tables(rel_table)
        o_att = _nsa_prompt(q.reshape(bp, seq, ATT_WIDTH), gates.reshape(bp, seq, N_KV_HEADS * TILE), kvc,
                            skv_b.reshape(bp, seq, KV_WIDTH), wkv_b.reshape(bp, seq, KV_WIDTH), tiles, cmp_tab)
        cl = min(CHUNK, seq)
        o_ssd, h_new, conv_new = _ssd(
            z.reshape(bp, seq, SSD_INNER), xbc.reshape(bp, seq, CONV_DIM), dt.reshape(bp, seq, TILE),
            dt_t, jnp.zeros((bp, SUBLANES, CONV_DIM), F32),
            jnp.zeros((bp, SSD_HEADS, SSD_HEAD_DIM, SSD_STATE), F32), ssd_consts, cl)
        yp = _mixout_ffn(yp, o_att.reshape(bp * seq, ATT_WIDTH), o_ssd.reshape(bp * seq, SSD_INNER), att_norm_l,
                         mix_post_l, wo_att, wo_ssd, *ffn2)

        def token_major(a_t):
            return a_t.reshape((bp,) + kv_shape + (a_t.shape[-1],)).transpose(0, 4, 1, 2, 3)

        outs[0].append(token_major(ckv_t))
        outs[1].append(token_major(skv_t))
        outs[2].append(token_major(wkv_t[:, :, -min(WINDOW, seq):]))
        outs[3].append(h_new)
        outs[4].append(conv_new)

        ys = _ffn(ys, *ffn1)
        q, ckv, _, _, gates, z, xbc, dt, dt_t, skv, wkv = _inproj(ys, mix_pre_l, w_proj, 1, False)
        native = (0, 2, 3, 4, 1)
        kc_past = _compress_paged(cache_cmp_kv[l].transpose(native), page_table, pe, bd)
        cmp_new = jnp.pad(ckv.reshape(bs, dec_seq, KV_WIDTH), ((0, 0), (0, BLK - dec_seq), (0, 0)))
        kc_new = _compress(cmp_new.reshape(1, bs * BLK, KV_WIDTH), pe, bd, bs).reshape(bs, 1, KV_WIDTH)
        n_past = kc_past.shape[1]
        n_lanes = -(-(n_past + 1) // TILE) * TILE
        kc = jnp.concatenate([kc_past, kc_new, jnp.zeros((bs, n_lanes - n_past - 1, KV_WIDTH), BF16)], axis=1)
        ctab, last_tab, new_tab, win0_tab, gsum = _sample_tables(rel_table, dec_seq, past_len, n_lanes)
        q_rows = q.reshape(bs, dec_seq, N_KV_HEADS, GQA, HEAD_DIM).transpose(0, 2, 3, 1, 4)
        own_head = jnp.eye(N_KV_HEADS, dtype=F32)[None, :, None, None, :, None]
        q_bd = q_rows[:, :, :, :, None, :] * own_head
        qk = q_bd.reshape(bs, TILE, N_KV_HEADS * HEAD_DIM).astype(BF16)
        g_rows = gates.reshape(bs, dec_seq, N_KV_HEADS, TILE)[..., :GQA * N_BRANCH]
        g_rows = g_rows.reshape(bs, dec_seq, N_KV_HEADS, GQA, N_BRANCH).transpose(0, 2, 3, 1, 4)
        gates_c = jnp.pad(g_rows.reshape(bs, TILE, N_BRANCH), ((0, 0), (0, 0), (0, TILE - N_BRANCH)))

        def new_t(a):
            a = a.reshape(bs, dec_seq, N_KV_HEADS, 2, HEAD_DIM).transpose(0, 3, 2, 4, 1)
            a = jnp.pad(a.reshape(bs, 2, N_KV_HEADS * HEAD_DIM, dec_seq), ((0, 0),) * 3 + ((0, TILE - dec_seq),))
            return a[:, 0].astype(BF16), a[:, 1].astype(BF16)

        k_new, v_new = new_t(skv)
        kw_new, vw_new = new_t(wkv)
        o_rows = _nsa_sample(page_table, cache_slc_kv[l].transpose(native), qk, gates_c, kc, ctab, gsum,
                             k_new, v_new, cache_win_kv[l].transpose(native), kw_new, vw_new, last_tab, new_tab,
                             win0_tab, dec_seq=dec_seq, past_len=past_len)
        o_att = o_rows.reshape(bs, N_KV_HEADS, GQA, dec_seq, HEAD_DIM).transpose(0, 3, 1, 2, 4)
        o_att = o_att.reshape(bs * dec_seq, ATT_WIDTH)
        conv_prev = jnp.pad(state_conv[l], ((0, 0), (SUBLANES - (CONV_W - 1), 0), (0, 0)))
        o_ssd, h_new, conv_new = _ssd(
            z.reshape(bs, dec_seq, SSD_INNER), xbc.reshape(bs, dec_seq, CONV_DIM), dt.reshape(bs, dec_seq, TILE),
            dt_t.reshape(TILE, bs, dec_seq).transpose(1, 0, 2), conv_prev, state_ssm[l], ssd_consts,
            min(CHUNK, dec_seq))
        ys = _mixout_ffn(ys, o_att, o_ssd.reshape(bs * dec_seq, SSD_INNER), att_norm_l, mix_post_l, wo_att,
                         wo_ssd, *ffn2)
        win_all = jnp.concatenate([cache_win_kv[l], wkv.reshape((bs, dec_seq) + kv_shape)], axis=1)
        outs[5].append(ckv.reshape((bs, dec_seq) + kv_shape))
        outs[6].append(skv.reshape((bs, dec_seq) + kv_shape))
        outs[7].append(win_all[:, -min(WINDOW, w_buf + dec_seq):])
        outs[8].append(h_new)
        outs[9].append(conv_new)

    return (yp.reshape(bp, seq, d), ys.reshape(bs, dec_seq, d)) + tuple(jnp.stack(o) for o in outs)
```

```python
import functools
import math

import jax
import jax.numpy as jnp
from jax import lax
from jax.experimental import pallas as pl
from jax.experimental.pallas import tpu as pltpu

F32 = jnp.float32
BF16 = jnp.bfloat16

D_MODEL = 1024
N_ATT_HEADS = 16
HEAD_DIM = 64
N_KV_HEADS = 4
GQA = N_ATT_HEADS // N_KV_HEADS
ATT_WIDTH = N_ATT_HEADS * HEAD_DIM
KV_WIDTH = 2 * N_KV_HEADS * HEAD_DIM
KV_GROUP = 2 * HEAD_DIM
BLK = 64
N_SEL = 16
WINDOW = 512
N_BRANCH = 3
N_BUCKETS = 32
MAX_DISTANCE = 128
SSD_HEADS = 16
SSD_HEAD_DIM = 64
SSD_INNER = SSD_HEADS * SSD_HEAD_DIM
SSD_GROUPS = 2
SSD_STATE = 128
CONV_W = 4
CONV_DIM = SSD_INNER + 2 * SSD_GROUPS * SSD_STATE
CHUNK = 128
PAGE_SIZE = 128
EPS = 1e-6

TILE = 128
SUBLANES = 8
MASK_VALUE = -(2.0 ** 100)
M_INIT = -1e30
LOG2E = math.log2(math.e)
VMEM_LIMIT = 56 * 1024 * 1024
PAGES_PER_STEP = 32
ROW_TILE = 256
FFN_ROW_TILE = 512
COMPRESS_ROWS_PER_DOT = 16
STAGE_PITCH = 72

NT_DIMS = (((1,), (1,)), ((), ()))
TN_DIMS = (((0,), (0,)), ((), ()))


def _params(n_grid_dims):
    return pltpu.CompilerParams(dimension_semantics=("arbitrary",) * n_grid_dims,
                                vmem_limit_bytes=VMEM_LIMIT)


def _rms(x, g):
    return x * lax.rsqrt(jnp.mean(x * x, axis=-1, keepdims=True) + EPS) * g


def _const_spec(shape):
    zeros = (0,) * len(shape)
    return pl.BlockSpec(shape, lambda *_: zeros, pipeline_mode=pl.Buffered(1))


def _ffn_body(x_ref, pre_ref, post_ref, wg_ref, wu_ref, wo_ref, o_ref):
    x = x_ref[...]
    h = _rms(x, pre_ref[...]).astype(BF16)
    gate = jnp.dot(h, wg_ref[...], preferred_element_type=F32)
    up = jnp.dot(h, wu_ref[...], preferred_element_type=F32)
    a = (gate * jax.nn.sigmoid(gate) * up).astype(BF16)
    y = jnp.dot(a, wo_ref[...], preferred_element_type=F32)
    o_ref[...] = x + 0.5 * _rms(y, post_ref[...])


def _ffn(x, pre, post, wg, wu, wo):
    t, d = x.shape
    f = wg.shape[1]
    tm = min(FFN_ROW_TILE, t)
    row = pl.BlockSpec((tm, d), lambda i: (i, 0))
    return pl.pallas_call(
        _ffn_body, grid=(t // tm,),
        in_specs=[row, _const_spec((1, d)), _const_spec((1, d)), _const_spec((d, f)), _const_spec((d, f)),
                  _const_spec((f, d))],
        out_specs=row, out_shape=jax.ShapeDtypeStruct((t, d), F32),
        compiler_params=_params(1), name="ffn")(x, pre, post, wg, wu, wo)


_SEG = {}
_off = 0
for _name, _w in (("q", ATT_WIDTH), ("ckv", KV_WIDTH), ("skv", KV_WIDTH), ("wkv", KV_WIDTH),
                  ("gates", N_KV_HEADS * TILE), ("z", SSD_INNER), ("xbc", CONV_DIM), ("dt", TILE)):
    _SEG[_name] = (_off, _off + _w)
    _off += _w
PROJ_WIDTH = _off


def _inproj_body(x_ref, pre_ref, w_ref, q_ref, ckv_ref, skvb_ref, wkvb_ref, gates_ref, z_ref, xbc_ref,
                 dt_ref, dtt_ref, *kv_refs, feature_major_kv):
    h = _rms(x_ref[...], pre_ref[...]).astype(BF16)

    def proj(name):
        lo, hi = _SEG[name]
        return jnp.dot(h, w_ref[:, lo:hi], preferred_element_type=F32)

    q_ref[...] = proj("q")
    ckv = proj("ckv")
    ckv_ref[...] = ckv
    skv = proj("skv")
    skvb_ref[...] = skv.astype(BF16)
    wkv = proj("wkv")
    wkvb_ref[...] = wkv.astype(BF16)
    gates_ref[...] = proj("gates")
    z_ref[...] = proj("z")
    xbc_ref[...] = proj("xbc")
    dt = proj("dt")
    dt_ref[...] = dt
    dtt_ref[...] = dt.T
    if feature_major_kv:
        for ref, rows in zip(kv_refs, (ckv, skv, wkv)):
            ref[...] = rows.T
    else:
        kv_refs[0][...] = skv
        kv_refs[1][...] = wkv


def _inproj(x, pre, w, n_seq, feature_major_kv):
    t, d = x.shape
    tm = min(ROW_TILE, t)
    seq = t // n_seq
    per_seq = seq // tm
    assert seq % tm == 0

    def cols(height):
        return pl.BlockSpec((None, height, tm), lambda i: (i // per_seq, 0, i % per_seq))

    def rows(width):
        return pl.BlockSpec((tm, width), lambda i: (i, 0))

    widths = (ATT_WIDTH, KV_WIDTH, KV_WIDTH, KV_WIDTH, N_KV_HEADS * TILE, SSD_INNER, CONV_DIM, TILE)
    dtypes = (F32, F32, BF16, BF16, F32, F32, F32, F32)
    out_shape = [jax.ShapeDtypeStruct((t, wd), dt) for wd, dt in zip(widths, dtypes)]
    out_specs = [rows(wd) for wd in widths]
    out_shape.append(jax.ShapeDtypeStruct((n_seq, TILE, seq), F32))
    out_specs.append(cols(TILE))
    if feature_major_kv:
        out_shape += [jax.ShapeDtypeStruct((n_seq, KV_WIDTH, seq), F32)] * 3
        out_specs += [cols(KV_WIDTH)] * 3
    else:
        out_shape += [jax.ShapeDtypeStruct((t, KV_WIDTH), F32)] * 2
        out_specs += [rows(KV_WIDTH)] * 2
    return pl.pallas_call(
        functools.partial(_inproj_body, feature_major_kv=feature_major_kv), grid=(t // tm,),
        in_specs=[rows(d), _const_spec((1, d)), _const_spec((d, PROJ_WIDTH))],
        out_specs=out_specs, out_shape=out_shape,
        compiler_params=_params(1), name="inproj")(x, pre, w)


def _mixout_body(x_ref, oa_ref, os_ref, an_ref, post_ref, wa_ref, ws_ref, o_ref):
    a = _rms(oa_ref[...], an_ref[...]).astype(BF16)
    y = jnp.dot(a, wa_ref[...], preferred_element_type=F32)
    y = y + jnp.dot(os_ref[...], ws_ref[...], preferred_element_type=F32)
    o_ref[...] = x_ref[...] + _rms(y, post_ref[...])


def _mixout_ffn_body(x_ref, oa_ref, os_ref, an_ref, mpost_ref, wa_ref, ws_ref, pre_ref, post_ref, wg_ref, wu_ref,
                     wo_ref, o_ref):
    _mixout_body(x_ref, oa_ref, os_ref, an_ref, mpost_ref, wa_ref, ws_ref, o_ref)
    _ffn_body(o_ref, pre_ref, post_ref, wg_ref, wu_ref, wo_ref, o_ref)


def _mixout_ffn(x, o_att, o_ssd, att_norm, mix_post, wa, ws, pre, post, wg, wu, wo):
    t, d = x.shape
    f = wg.shape[1]
    tm = min(FFN_ROW_TILE, t)
    row = pl.BlockSpec((tm, d), lambda i: (i, 0))
    vec = _const_spec((1, d))
    return pl.pallas_call(
        _mixout_ffn_body, grid=(t // tm,),
        in_specs=[row, row, row, vec, vec, _const_spec((d, d)), _const_spec((d, d)), vec, vec,
                  _const_spec((d, f)), _const_spec((d, f)), _const_spec((f, d))],
        out_specs=row, out_shape=jax.ShapeDtypeStruct((t, d), F32),
        compiler_params=_params(1), name="mixout_ffn")(x, o_att, o_ssd, att_norm, mix_post, wa, ws, pre, post, wg,
                                                       wu, wo)


def _token_rows(ref, k, lo=0, n=None):
    n = ref.shape[-1] if n is None else n
    return ref[k, :, :, lo:lo + n].reshape(KV_GROUP, n).T


def _compress_rows(x_refs, pe_ref, bd_ref, n_blocks, pitch=BLK):
    acc = None
    for l0 in range(0, BLK, COMPRESS_ROWS_PER_DOT):
        pieces = []
        for l in range(l0, l0 + COMPRESS_ROWS_PER_DOT):
            pe = pe_ref[l:l + 1, :]
            xs = jnp.concatenate([x_ref[pl.ds(l, n_blocks, stride=pitch), :] + pe for x_ref in x_refs], axis=0)
            pieces.append(xs.astype(BF16))
        w = bd_ref[l0:l0 + COMPRESS_ROWS_PER_DOT].reshape(COMPRESS_ROWS_PER_DOT * KV_GROUP, KV_GROUP)
        part = jnp.dot(jnp.concatenate(pieces, axis=1), w, preferred_element_type=F32)
        acc = part if acc is None else acc + part
    return acc


def _store_summaries(acc, o_ref, n_blocks):
    for k in range(N_KV_HEADS):
        o_ref[0:n_blocks, k * KV_GROUP:(k + 1) * KV_GROUP] = acc[k * n_blocks:(k + 1) * n_blocks].astype(BF16)


def _compress_body(*refs, n_blocks, n_out):
    x_refs = refs[:N_KV_HEADS]
    pe_ref, bd_ref, o_ref = refs[N_KV_HEADS:]
    _store_summaries(_compress_rows(x_refs, pe_ref, bd_ref, n_blocks), o_ref, n_blocks)
    if n_out > n_blocks:
        o_ref[n_blocks:n_out, :] = jnp.zeros((n_out - n_blocks, KV_WIDTH), BF16)


def _compress(kv, pe, bd, n_out):
    b, s, _ = kv.shape
    n_blocks = s // BLK

    def head_spec(k):
        return pl.BlockSpec((None, s, KV_GROUP), lambda i: (i, 0, k))

    return pl.pallas_call(
        functools.partial(_compress_body, n_blocks=n_blocks, n_out=n_out), grid=(b,),
        in_specs=[head_spec(k) for k in range(N_KV_HEADS)]
        + [_const_spec((BLK, KV_GROUP)), _const_spec((BLK, KV_GROUP, KV_GROUP))],
        out_specs=pl.BlockSpec((None, n_out, KV_WIDTH), lambda i: (i, 0, 0)),
        out_shape=jax.ShapeDtypeStruct((b, n_out, KV_WIDTH), BF16),
        compiler_params=_params(1), name="compress")(*([kv] * N_KV_HEADS), pe, bd)


def _compress_paged_body(pt_ref, *refs, n_pages):
    page_refs = refs[:PAGES_PER_STEP]
    pe_ref, bd_ref, o_ref, stage_ref = refs[PAGES_PER_STEP:]
    j = pl.program_id(1)
    blocks_per_page = PAGE_SIZE // BLK
    for p in range(PAGES_PER_STEP):
        first_block = (j * PAGES_PER_STEP + p) * blocks_per_page
        for k in range(N_KV_HEADS):
            rows = _token_rows(page_refs[p], k)
            for h in range(blocks_per_page):
                at = pl.multiple_of((first_block + h) * STAGE_PITCH, SUBLANES)
                stage_ref[k, pl.ds(at, BLK), :] = rows[h * BLK:(h + 1) * BLK]

    @pl.when(j == pl.num_programs(1) - 1)
    def _():
        n_blocks = n_pages * blocks_per_page
        x_refs = [stage_ref.at[k] for k in range(N_KV_HEADS)]
        _store_summaries(_compress_rows(x_refs, pe_ref, bd_ref, n_blocks, STAGE_PITCH), o_ref, n_blocks)


def _compress_paged(cache, page_table, pe, bd):
    b, n_pages = page_table.shape
    steps = n_pages // PAGES_PER_STEP
    n_blocks = n_pages * (PAGE_SIZE // BLK)

    def page_spec(p):
        return pl.BlockSpec((None, N_KV_HEADS, 2, HEAD_DIM, PAGE_SIZE),
                            lambda i, j, pt: (pt[i, j * PAGES_PER_STEP + p], 0, 0, 0, 0))

    grid_spec = pltpu.PrefetchScalarGridSpec(
        num_scalar_prefetch=1, grid=(b, steps),
        in_specs=[page_spec(p) for p in range(PAGES_PER_STEP)]
        + [pl.BlockSpec((BLK, KV_GROUP), lambda i, j, pt: (0, 0)),
           pl.BlockSpec((BLK, KV_GROUP, KV_GROUP), lambda i, j, pt: (0, 0, 0))],
        out_specs=pl.BlockSpec((None, n_blocks, KV_WIDTH), lambda i, j, pt: (i, 0, 0)),
        scratch_shapes=[pltpu.VMEM((N_KV_HEADS, n_blocks * STAGE_PITCH, KV_GROUP), F32)])
    return pl.pallas_call(
        functools.partial(_compress_paged_body, n_pages=n_pages), grid_spec=grid_spec,
        out_shape=jax.ShapeDtypeStruct((b, n_blocks, KV_WIDTH), BF16),
        compiler_params=_params(2), name="compress_paged")(page_table, *([cache] * PAGES_PER_STEP), pe, bd)


def _rel_bucket(dist):
    n = jnp.maximum(dist, 0)
    max_exact = N_BUCKETS // 2
    nf = jnp.maximum(n, 1).astype(F32)
    log_b = max_exact + (jnp.log(nf / max_exact) / math.log(MAX_DISTANCE / max_exact)
                         * (N_BUCKETS - max_exact)).astype(jnp.int32)
    return jnp.where(n < max_exact, n, jnp.minimum(log_b, N_BUCKETS - 1))


def _bias_of(rel_table, dist, visible):
    onehot = (_rel_bucket(dist)[..., None] == jnp.arange(N_BUCKETS, dtype=jnp.int32)).astype(F32)
    bias = jnp.einsum('rcb,bh->hrc', onehot, rel_table.astype(F32) * LOG2E, precision=lax.Precision.HIGHEST)
    return jnp.where(visible[None], bias, MASK_VALUE).reshape(-1, dist.shape[-1])


KIND_ZERO, KIND_DIAG, KIND_PREV, KIND_MASKED, KIND_OLDEST = range(5)


def _prompt_bias_tables(rel_table):
    r = jnp.arange(TILE, dtype=jnp.int32)[:, None]
    c = jnp.arange(TILE, dtype=jnp.int32)[None, :]
    true = jnp.ones((TILE, TILE), bool)
    far = _bias_of(rel_table, jnp.full((TILE, TILE), MAX_DISTANCE, jnp.int32), true)
    tiles = jnp.stack([
        jnp.zeros_like(far),
        _bias_of(rel_table, r - c, r >= c) - far,
        _bias_of(rel_table, TILE + r - c, true) - far,
        jnp.full_like(far, MASK_VALUE),
        jnp.where(jnp.tile(r < c, (N_ATT_HEADS, 1)), 0.0, MASK_VALUE),
    ])
    c_dist = r - (BLK - 1) + BLK * (BLK - 1 - c)
    cmp_tab = _bias_of(rel_table, c_dist, c_dist >= 0)
    return tiles, cmp_tab


def _selection_mask(score, n_rows):
    n_groups = score.shape[0] // SUBLANES
    groups = [score[SUBLANES * j:SUBLANES * (j + 1)] for j in range(n_groups)]
    ranks = [jnp.zeros(groups[0].shape, jnp.int32) for _ in range(n_groups)]
    row8 = lax.broadcasted_iota(jnp.int32, groups[0].shape, 0)
    for n in range(n_rows):
        other = jnp.broadcast_to(score[n:n + 1, :], groups[0].shape)
        for j in range(n_groups):
            lo = SUBLANES * j
            if lo > n:
                inc = jnp.where(other >= groups[j], 1, 0)
            elif lo + SUBLANES - 1 <= n:
                inc = jnp.where(other > groups[j], 1, 0)
            else:
                inc = jnp.where(row8 + lo > n, jnp.where(other >= groups[j], 1, 0),
                                jnp.where(other > groups[j], 1, 0))
            ranks[j] = ranks[j] + inc
    rank = jnp.concatenate(ranks, axis=0)
    return jnp.where(rank < N_SEL, jnp.where(score > -jnp.inf, 0.0, MASK_VALUE), MASK_VALUE)


PAIR = 2 * TILE
Q_TILES = PAIR // TILE
SEL_UNROLLS = (8, 4, 2, 1)
SEL_PAD_UNROLL = 8
_SEL_KINDS = {0: ((KIND_DIAG, KIND_MASKED), (KIND_PREV, KIND_DIAG)),
              1: ((KIND_ZERO, KIND_PREV), (KIND_ZERO, KIND_ZERO))}
_WIN_OLDEST = ((KIND_OLDEST, KIND_ZERO), (KIND_MASKED, KIND_OLDEST))


def _nsa_prompt_body(q_ref, gates_ref, kvc_ref, skv_ref, wkv_ref, tb_ref, ctab_ref, o_ref,
                     s_ref, sw_ref, mx_ref, mb_ref, acc_ref, part_ref, *, n_blocks):
    step = pl.program_id(2)
    head_rows = GQA * TILE
    rows = Q_TILES * head_rows
    lane = lax.broadcasted_iota(jnp.int32, (rows, TILE), 1)
    low = lane < HEAD_DIM
    key_lane = lax.broadcasted_iota(jnp.int32, (PAIR, TILE), 1)
    key_blk = lax.broadcasted_iota(jnp.int32, (PAIR, TILE), 0) // BLK
    n_pairs = skv_ref.shape[0] // PAIR

    qf = q_ref[...]
    halves = [qf[:, c * TILE:(c + 1) * TILE] for c in range(GQA // 2)]
    rolled = [pltpu.roll(h, HEAD_DIM, 1) for h in halves]
    q_all = jnp.concatenate([(halves if g % 2 == 0 else rolled)[g // 2][t * TILE:(t + 1) * TILE]
                             for t in range(Q_TILES) for g in range(GQA)], axis=0)
    lhs_plain = jnp.where(low, q_all, 0.0).astype(BF16)

    def ones_and_values(kv):
        kv_lane = lax.broadcasted_iota(jnp.int32, kv.shape, 1)
        return jnp.where(kv_lane < HEAD_DIM, jnp.ones_like(kv), kv)

    def normalise(acc):
        return acc / jnp.maximum(pltpu.roll(acc, HEAD_DIM, 1), 1e-30)

    def bias_of(kinds_per_tile, valid):
        parts = []
        for left, right in kinds_per_tile:
            left = jnp.where(valid, left, KIND_MASKED)
            right = jnp.where(valid, right, KIND_MASKED)
            parts.append(jnp.concatenate([tb_ref[left], tb_ref[right]], axis=1))
        return jnp.concatenate(parts, axis=0)

    def sel_kinds(j):
        behind = step - j
        return [tuple(jnp.where(behind == 0, _SEL_KINDS[0][t][side],
                                jnp.where(behind == 1, _SEL_KINDS[1][t][side], KIND_ZERO)) for side in range(2))
                for t in range(Q_TILES)]

    def keys_of(kv_ref, j):
        jc = jnp.clip(j, 0, n_pairs - 1)
        return kv_ref[pl.ds(pl.multiple_of(jc * PAIR, PAIR), PAIR), :]

    def score_pair(lhs, kv_ref, j, bias, block_columns):
        rhs = keys_of(kv_ref, j)
        if block_columns:
            onehot = jnp.where(key_lane - HEAD_DIM == (PAIR // BLK) * j + key_blk, 1.0, 0.0).astype(BF16)
            rhs = jnp.where(key_lane < HEAD_DIM, rhs, onehot)
        return lax.dot_general(lhs, rhs, NT_DIMS, preferred_element_type=F32) + bias

    def half_max(sc):
        return jnp.maximum(sc[:, :TILE], sc[:, TILE:])

    def weigh_pair(sc, mb, kv_ref, j):
        pr = jnp.exp2(sc - jnp.concatenate([mb, mb], axis=1))
        return jnp.dot(pr.astype(BF16), ones_and_values(keys_of(kv_ref, j)), preferred_element_type=F32)

    def row_max(mx):
        return jnp.broadcast_to(jnp.max(mx, axis=1, keepdims=True), (rows, TILE))

    n_win = WINDOW // PAIR + 1
    w_pairs = [step - (n_win - 1) + w for w in range(n_win)]
    w_kinds = [_WIN_OLDEST] + [[(KIND_ZERO, KIND_ZERO)] * Q_TILES] * (n_win - 3) + [_SEL_KINDS[1], _SEL_KINDS[0]]
    w_mx = None
    for w, j in enumerate(w_pairs):
        sc = score_pair(lhs_plain, wkv_ref, j, bias_of(w_kinds[w], j >= 0), False)
        sw_ref[w] = sc
        w_mx = half_max(sc) if w_mx is None else jnp.maximum(w_mx, half_max(sc))

    kvc = kvc_ref[...]
    bias = []
    for t in range(Q_TILES):
        i = Q_TILES * step + t
        shift = (2 * i + (TILE - (BLK - 1))) % TILE
        bias += [pltpu.roll(ctab_ref[g * TILE:(g + 1) * TILE, :], shift, 1) for g in range(GQA)]
    s = lax.dot_general(lhs_plain, kvc, NT_DIMS, preferred_element_type=F32)
    s = jnp.where(lane < n_blocks, s + jnp.concatenate(bias, axis=0), MASK_VALUE)
    m = jnp.maximum(jnp.max(s, axis=1, keepdims=True), M_INIT)
    p = jnp.exp2(s - m)
    c_acc = jnp.dot(p.astype(BF16), ones_and_values(kvc), preferred_element_type=F32)
    p = p / jnp.maximum(c_acc, 1e-30)
    o_cmp = normalise(c_acc)

    w_mb = row_max(jnp.maximum(w_mx, M_INIT))
    w_acc = None
    for w, j in enumerate(w_pairs):
        dacc = weigh_pair(sw_ref[w], w_mb, wkv_ref, j)
        w_acc = dacc if w_acc is None else w_acc + dacc
    o_win = normalise(w_acc)

    gate = jax.nn.sigmoid(gates_ref[...])

    def gate_col(t, g, br):
        c = g * N_BRANCH + br
        return gate[t * TILE:(t + 1) * TILE, c:c + 1]

    heads = [(t, g, slice(t * head_rows + g * TILE, t * head_rows + (g + 1) * TILE))
             for t in range(Q_TILES) for g in range(GQA)]
    for t, g, head in heads:
        part_ref[head, :] = gate_col(t, g, 0) * o_cmp[head] + gate_col(t, g, 2) * o_win[head]

    blk = lax.broadcasted_iota(jnp.int32, (n_blocks, PAIR), 0)
    query = lax.broadcasted_iota(jnp.int32, (n_blocks, PAIR), 1)
    cur = (PAIR // BLK) * step + query // BLK
    imp_t = []
    for t in range(Q_TILES):
        imp = p[t * head_rows:t * head_rows + TILE]
        for g in range(1, GQA):
            imp = imp + p[t * head_rows + g * TILE:t * head_rows + (g + 1) * TILE]
        imp_t.append(imp.T[0:n_blocks])
    forced = (blk == 0) | (blk == cur) | (blk == cur - 1)
    score = jnp.where(forced, jnp.inf, jnp.where(blk <= cur, jnp.concatenate(imp_t, axis=1), -jnp.inf))
    neg_t = _selection_mask(score, n_blocks)
    neg = []
    for t in range(Q_TILES):
        pieces = [jnp.zeros((HEAD_DIM, TILE), F32), neg_t[:, t * TILE:(t + 1) * TILE]]
        if n_blocks < TILE - HEAD_DIM:
            pieces.append(jnp.zeros((TILE - HEAD_DIM - n_blocks, TILE), F32))
        neg += [jnp.concatenate(pieces, axis=0).T] * GQA
    lhs_sel = jnp.where(low, q_all, jnp.concatenate(neg, axis=0)).astype(BF16)

    n_sel = step + 1
    spans, first = [], 0
    for unroll in SEL_UNROLLS:
        left = n_sel - first
        n_trips = jnp.maximum(left // unroll if unroll != SEL_PAD_UNROLL else (left + 1) // unroll, 0)
        spans.append((unroll, first, n_trips))
        first = first + n_trips * unroll
    mx_ref[...] = jnp.full(mx_ref.shape, M_INIT, F32)

    def score_trips(unroll, first, n_trips):
        def trip(t, carry):
            mx = None
            for u in range(unroll):
                j = first + t * unroll + u
                sc = score_pair(lhs_sel, skv_ref, j, bias_of(sel_kinds(j), j <= step), True)
                s_ref[j] = sc
                mx = half_max(sc) if mx is None else jnp.maximum(mx, half_max(sc))
            mx_ref[...] = jnp.maximum(mx_ref[...], mx)
            return carry
        lax.fori_loop(0, n_trips, trip, 0)

    for span in spans:
        score_trips(*span)
    mb_ref[...] = row_max(mx_ref[...])
    acc_ref[...] = jnp.zeros(acc_ref.shape, F32)

    def weigh_trips(unroll, first, n_trips):
        def trip(t, carry):
            mb = mb_ref[...]
            acc = None
            for u in range(unroll):
                j = first + t * unroll + u
                dacc = weigh_pair(s_ref[j], mb, skv_ref, j)
                acc = dacc if acc is None else acc + dacc
            acc_ref[...] += acc
            return carry
        lax.fori_loop(0, n_trips, trip, 0)

    for span in spans:
        weigh_trips(*span)
    o_sel = normalise(acc_ref[...])

    out_low = lax.broadcasted_iota(jnp.int32, (TILE, TILE), 1) < HEAD_DIM
    outs = {(t, g): part_ref[head, :] + gate_col(t, g, 1) * o_sel[head] for t, g, head in heads}
    for t in range(Q_TILES):
        for c in range(GQA // 2):
            o_ref[t * TILE:(t + 1) * TILE, c * TILE:(c + 1) * TILE] = jnp.where(
                out_low, pltpu.roll(outs[t, 2 * c], HEAD_DIM, 1), outs[t, 2 * c + 1])


def _nsa_prompt(q, gates, kvc, skv_b, wkv_b, tiles, cmp_tab):
    b, s, _ = q.shape
    n_blocks = s // BLK
    assert n_blocks <= TILE - HEAD_DIM and kvc.shape[1] == TILE and s % PAIR == 0 and WINDOW // PAIR >= 2
    gw = GQA * HEAD_DIM
    head_rows = GQA * TILE
    rows = Q_TILES * head_rows
    return pl.pallas_call(
        functools.partial(_nsa_prompt_body, n_blocks=n_blocks), grid=(b, N_KV_HEADS, s // PAIR),
        in_specs=[pl.BlockSpec((None, PAIR, gw), lambda bi, k, i: (bi, i, k)),
                  pl.BlockSpec((None, PAIR, TILE), lambda bi, k, i: (bi, i, k)),
                  pl.BlockSpec((None, TILE, KV_GROUP), lambda bi, k, i: (bi, 0, k)),
                  pl.BlockSpec((None, s, KV_GROUP), lambda bi, k, i: (bi, 0, k)),
                  pl.BlockSpec((None, s, KV_GROUP), lambda bi, k, i: (bi, 0, k)),
                  pl.BlockSpec((len(tiles), head_rows, TILE), lambda bi, k, i: (0, k, 0)),
                  pl.BlockSpec((head_rows, TILE), lambda bi, k, i: (k, 0))],
        out_specs=pl.BlockSpec((None, PAIR, gw), lambda bi, k, i: (bi, i, k)),
        out_shape=jax.ShapeDtypeStruct((b, s, ATT_WIDTH), F32),
        scratch_shapes=[pltpu.VMEM((s // PAIR, rows, PAIR), F32),
                        pltpu.VMEM((WINDOW // PAIR + 1, rows, PAIR), F32),
                        pltpu.VMEM((rows, TILE), F32), pltpu.VMEM((rows, TILE), F32),
                        pltpu.VMEM((rows, KV_GROUP), F32), pltpu.VMEM((rows, TILE), F32)],
        compiler_params=_params(3), name="nsa_prompt")(q, gates, kvc, skv_b, wkv_b, tiles, cmp_tab)


def _nsa_sample_body(pt_ref, *refs, n_pages, dec_seq, past_len):
    page_refs = refs[:PAGES_PER_STEP]
    (qk_ref, gates_ref, kc_ref, ctab_ref, gsum_ref, knew_ref, vnew_ref, cwin_ref, kwnew_ref, vwnew_ref,
     last_ref, new_ref, win0_ref, o_ref, lhs_ref, m_ref, l_ref, acc_ref, ow_ref, oc_ref) = refs[PAGES_PER_STEP:]
    j = pl.program_id(1)
    last_step = pl.num_programs(1) - 1
    blocks_per_page = PAGE_SIZE // BLK
    n_past = n_pages * blocks_per_page
    kd = N_KV_HEADS * HEAD_DIM
    n_lanes = kc_ref.shape[0]
    step_keys = PAGES_PER_STEP * PAGE_SIZE

    def gate_col(br):
        return jax.nn.sigmoid(gates_ref[:, br:br + 1])

    def head_major(ref, c, lo=0, n=None):
        n = ref.shape[-1] if n is None else n
        return ref[:, c, :, lo:lo + n].reshape(kd, n).astype(BF16)

    def lane_tiles(x):
        return [x[:, t * TILE:(t + 1) * TILE] for t in range(x.shape[1] // TILE)]

    def tile_max(tiles):
        m = tiles[0]
        for t in tiles[1:]:
            m = jnp.maximum(m, t)
        return jnp.broadcast_to(jnp.max(m, axis=1, keepdims=True), (TILE, TILE))

    def tile_sum(tiles):
        s = tiles[0]
        for t in tiles[1:]:
            s = s + t
        return s

    def online_update(s, values_t):
        tiles = lane_tiles(s)
        m_prev = m_ref[...]
        m_new = jnp.maximum(m_prev, tile_max(tiles))
        alpha = jnp.exp2(m_prev - m_new)
        p = [jnp.exp2(t - m_new) for t in tiles]
        l_ref[...] = alpha * l_ref[...] + tile_sum(p)
        pv = lax.dot_general(jnp.concatenate(p, axis=1).astype(BF16), values_t, NT_DIMS, preferred_element_type=F32)
        acc_ref[...] = jnp.concatenate([alpha] * (kd // TILE), axis=1) * acc_ref[...] + pv
        m_ref[...] = m_new

    @pl.when(j == 0)
    def _():
        kc = kc_ref[...]
        qf = qk_ref[...].astype(F32)
        gap = jnp.zeros((TILE, HEAD_DIM), F32)
        qc = jnp.concatenate([piece for k in range(N_KV_HEADS)
                              for piece in (qf[:, k * HEAD_DIM:(k + 1) * HEAD_DIM], gap)], axis=1).astype(BF16)
        s = lax.dot_general(qc, kc, NT_DIMS, preferred_element_type=F32) + ctab_ref[...]
        m = jnp.maximum(jnp.max(s, axis=1, keepdims=True), M_INIT)
        p = jnp.exp2(s - m)
        p = p / jnp.maximum(jnp.sum(p, axis=1, keepdims=True), 1e-30)
        oc_ref[...] = gate_col(0) * jnp.dot(p.astype(BF16), kc, preferred_element_type=F32)
        imp = _select_sum(p, gsum_ref[...], False)
        imp_t = jnp.concatenate([t.T for t in lane_tiles(imp)], axis=0)
        n_rows = -(-(n_past + 1) // SUBLANES) * SUBLANES
        blk = lax.broadcasted_iota(jnp.int32, (n_rows, TILE), 0)
        tok = lax.broadcasted_iota(jnp.int32, (n_rows, TILE), 1) % dec_seq
        cur = (past_len + tok) // BLK
        forced = (blk == 0) | (blk == cur) | (blk == cur - 1)
        score = jnp.where(forced, jnp.inf, jnp.where(blk <= cur, imp_t[0:n_rows], -jnp.inf))
        neg_t = jnp.concatenate([_selection_mask(score, n_past + 1), jnp.zeros((n_lanes - n_rows, TILE), F32)], axis=0)
        neg = jnp.concatenate([neg_t[t * TILE:(t + 1) * TILE].T for t in range(n_lanes // TILE)], axis=1)
        lhs_ref[:, 0:kd] = qk_ref[...]
        lhs_ref[:, kd:kd + n_lanes] = neg.astype(BF16)

        qk = qk_ref[...]
        n_win = cwin_ref.shape[-1] // TILE
        zero = jnp.zeros((TILE, TILE), F32)
        s_w = jnp.dot(qk, head_major(cwin_ref, 0), preferred_element_type=F32)
        s_w = s_w + jnp.concatenate([win0_ref[...]] + [zero] * (n_win - 2) + [last_ref[...]], axis=1)
        s_n = jnp.dot(qk, kwnew_ref[...], preferred_element_type=F32) + new_ref[...]
        tiles = lane_tiles(s_w) + [s_n]
        m = tile_max(tiles)
        p = [jnp.exp2(t - m) for t in tiles]
        l = jnp.sum(tile_sum(p), axis=1, keepdims=True)
        o_w = lax.dot_general(jnp.concatenate(p[:-1], axis=1).astype(BF16), head_major(cwin_ref, 1), NT_DIMS,
                              preferred_element_type=F32)
        o_w = o_w + lax.dot_general(p[-1].astype(BF16), vwnew_ref[...], NT_DIMS, preferred_element_type=F32)
        ow_ref[...] = gate_col(2) * (o_w / l)

        m_ref[...] = jnp.full(m_ref.shape, M_INIT, F32)
        l_ref[...] = jnp.zeros(l_ref.shape, F32)
        acc_ref[...] = jnp.zeros(acc_ref.shape, F32)

    keys_t = jnp.concatenate([head_major(r, 0) for r in page_refs], axis=1)
    values_t = jnp.concatenate([head_major(r, 1) for r in page_refs], axis=1)
    blk_row = lax.broadcasted_iota(jnp.int32, (n_lanes, step_keys), 0)
    key_blk = j * (PAGES_PER_STEP * blocks_per_page) + lax.broadcasted_iota(jnp.int32, (n_lanes, step_keys), 1) // BLK
    onehot = jnp.where(blk_row == key_blk, 1.0, 0.0).astype(BF16)
    s = jnp.dot(lhs_ref[...], jnp.concatenate([keys_t, onehot], axis=0), preferred_element_type=F32)
    tiles = lane_tiles(s)
    tiles[-1] = tiles[-1] + jnp.where(j == last_step, last_ref[...], 0.0)
    online_update(jnp.concatenate(tiles, axis=1), values_t)

    @pl.when(j == last_step)
    def _():
        new_blk = lax.broadcasted_iota(jnp.int32, (n_lanes, TILE), 0) == n_past
        rhs = jnp.concatenate([knew_ref[...], jnp.where(new_blk, 1.0, 0.0).astype(BF16)], axis=0)
        online_update(jnp.dot(lhs_ref[...], rhs, preferred_element_type=F32) + new_ref[...], vnew_ref[...])
        l = jnp.sum(l_ref[...], axis=1, keepdims=True)
        o_sw = gate_col(1) * (acc_ref[...] / l) + ow_ref[...]
        o_c = oc_ref[...]
        rows_per_head = TILE // N_KV_HEADS
        for k in range(N_KV_HEADS):
            rows = slice(k * rows_per_head, (k + 1) * rows_per_head)
            v_lo = k * KV_GROUP + HEAD_DIM
            o_ref[rows, :] = o_sw[rows, k * HEAD_DIM:(k + 1) * HEAD_DIM] + o_c[rows, v_lo:v_lo + HEAD_DIM]


def _nsa_sample(page_table, cache_slc, qk, gates_c, kc, ctab, gsum, k_new, v_new, cache_win, kw_new, vw_new,
                last_tab, new_tab, win0_tab, *, dec_seq, past_len):
    b, n_pages = page_table.shape
    steps = n_pages // PAGES_PER_STEP
    n_lanes = kc.shape[1]
    w_buf = cache_win.shape[-1]
    kd = N_KV_HEADS * HEAD_DIM

    def page_spec(p):
        return pl.BlockSpec((None, N_KV_HEADS, 2, HEAD_DIM, PAGE_SIZE),
                            lambda i, j, pt: (pt[i, j * PAGES_PER_STEP + p], 0, 0, 0, 0))

    def per_batch(*shape):
        zeros = (0,) * len(shape)
        return pl.BlockSpec((None,) + shape, lambda i, j, pt: (i,) + zeros)

    def const(*shape):
        zeros = (0,) * len(shape)
        return pl.BlockSpec(shape, lambda i, j, pt: zeros)

    grid_spec = pltpu.PrefetchScalarGridSpec(
        num_scalar_prefetch=1, grid=(b, steps),
        in_specs=[page_spec(p) for p in range(PAGES_PER_STEP)]
        + [per_batch(TILE, kd), per_batch(TILE, TILE), per_batch(n_lanes, KV_WIDTH),
           const(TILE, n_lanes), const(TILE, TILE), per_batch(kd, TILE), per_batch(kd, TILE),
           per_batch(N_KV_HEADS, 2, HEAD_DIM, w_buf), per_batch(kd, TILE), per_batch(kd, TILE),
           const(TILE, TILE), const(TILE, TILE), const(TILE, TILE)],
        out_specs=per_batch(TILE, HEAD_DIM),
        scratch_shapes=[pltpu.VMEM((TILE, kd + n_lanes), BF16), pltpu.VMEM((TILE, TILE), F32),
                        pltpu.VMEM((TILE, TILE), F32), pltpu.VMEM((TILE, kd), F32), pltpu.VMEM((TILE, kd), F32),
                        pltpu.VMEM((TILE, KV_WIDTH), F32)])
    return pl.pallas_call(
        functools.partial(_nsa_sample_body, n_pages=n_pages, dec_seq=dec_seq, past_len=past_len),
        grid_spec=grid_spec, out_shape=jax.ShapeDtypeStruct((b, TILE, HEAD_DIM), F32),
        compiler_params=_params(2), name="nsa_sample")(
            page_table, *([cache_slc] * PAGES_PER_STEP), qk, gates_c, kc, ctab, gsum, k_new, v_new, cache_win,
            kw_new, vw_new, last_tab, new_tab, win0_tab)


def _split3(x):
    hi = x.astype(BF16)
    r = x - hi.astype(F32)
    mid = r.astype(BF16)
    return hi, mid, (r - mid.astype(F32)).astype(BF16)


def _select_sum(x, onehot, x_on_left):
    out = None
    for part in _split3(x):
        term = (jnp.dot(part, onehot, preferred_element_type=F32) if x_on_left
                else jnp.dot(onehot, part, preferred_element_type=F32))
        out = term if out is None else out + term
    return out


def _softplus(x):
    return jnp.maximum(x, 0.0) + jnp.log(1.0 + jnp.exp(-jnp.abs(x)))


def _ssd_body(z_ref, xbc_ref, dt_ref, dtt_ref, convp_ref, h0_ref, cw_ref, cb_ref, dtb_ref, dtbt_ref, al_ref,
              alt_ref, dskip_ref, norm_ref, o_ref, hout_ref, convout_ref, xp_ref, *, cl):
    c = pl.program_id(1)
    pad = SUBLANES

    @pl.when(c == 0)
    def _():
        xp_ref[0:pad, :] = convp_ref[...]
        hout_ref[...] = h0_ref[...]

    xp_ref[pad:pad + cl, :] = xbc_ref[...]
    conv = cb_ref[...]
    for k in range(CONV_W):
        lo = pad - (CONV_W - 1) + k
        conv = conv + cw_ref[k:k + 1, :] * xp_ref[lo:lo + cl, :]
    u = conv * jax.nn.sigmoid(conv)
    tail = xp_ref[pad + cl - (CONV_W - 1):pad + cl, :]
    convout_ref[...] = tail
    xp_ref[pad - (CONV_W - 1):pad, :] = tail

    xs = u[:, :SSD_INNER]
    dt = _softplus(dt_ref[:, 0:SSD_HEADS] + dtb_ref[...])
    dt_t = _softplus(dtt_ref[0:SSD_HEADS, :] + dtbt_ref[...])
    la = dt * -jnp.exp(al_ref[...])
    la_t = dt_t * -jnp.exp(alt_ref[...])
    row = lax.broadcasted_iota(jnp.int32, (cl, cl), 0)
    col = lax.broadcasted_iota(jnp.int32, (cl, cl), 1)
    causal = row >= col
    cs = _select_sum(la, jnp.where(causal, 1.0, 0.0).astype(BF16), False)
    cs_t = _select_sum(la_t, jnp.where(row <= col, 1.0, 0.0).astype(BF16), True)

    def spread(width):
        lane_head = lax.broadcasted_iota(jnp.int32, (SSD_HEADS, SSD_HEADS * width), 1) // width
        return jnp.where(lane_head == lax.broadcasted_iota(jnp.int32, lane_head.shape, 0), 1.0, 0.0).astype(BF16)

    dt_x = _select_sum(dt, spread(SSD_HEAD_DIM), True)
    cs_x = _select_sum(cs, spread(SSD_HEAD_DIM), True)
    cs_wide = _select_sum(cs, spread(TILE), True)
    last_x = cs_x[cl - 1:cl, :]
    xdt = xs * dt_x
    xdt_b = xdt.astype(BF16)
    to_end = (xdt * jnp.exp(last_x - cs_x)).astype(BF16)
    grow = jnp.exp(cs_x)
    skip = dskip_ref[...] * xs

    heads_per_group = SSD_HEADS // SSD_GROUPS
    bms, cms, cbs = [], [], []
    for g in range(SSD_GROUPS):
        b_lo = SSD_INNER + g * SSD_STATE
        c_lo = SSD_INNER + SSD_GROUPS * SSD_STATE + g * SSD_STATE
        bms.append(u[:, b_lo:b_lo + SSD_STATE].astype(BF16))
        cms.append(u[:, c_lo:c_lo + SSD_STATE].astype(BF16))
        cbs.append(lax.dot_general(cms[g], bms[g], NT_DIMS, preferred_element_type=F32))
    first_head = lax.broadcasted_iota(jnp.int32, (cl, TILE), 1) < SSD_HEAD_DIM
    state_row = lax.broadcasted_iota(jnp.int32, (2 * SSD_HEAD_DIM, SSD_STATE), 0)
    ys = []
    for pair in range(SSD_HEADS // 2):
        g = 2 * pair // heads_per_group
        lanes = slice(pair * TILE, (pair + 1) * TILE)
        y_heads, keep = [], []
        for h in (2 * pair, 2 * pair + 1):
            diff = cs_wide[:, h * TILE:h * TILE + cl] - cs_t[h:h + 1, :]
            decay = jnp.where(causal, jnp.exp(jnp.where(causal, diff, 0.0)), 0.0)
            y_heads.append(jnp.dot((cbs[g] * decay).astype(BF16), xdt_b[:, lanes], preferred_element_type=F32))
            keep.append(jnp.exp(cs[cl - 1:cl, h:h + 1]))
        state_in = lax.dot_general(to_end[:, lanes], bms[g], TN_DIMS, preferred_element_type=F32)
        h_prev = jnp.concatenate([hout_ref[2 * pair], hout_ref[2 * pair + 1]], axis=0)
        y_off = lax.dot_general(cms[g], h_prev.astype(BF16), NT_DIMS, preferred_element_type=F32)
        h_new = h_prev * jnp.where(state_row < SSD_HEAD_DIM, keep[0], keep[1]) + state_in
        hout_ref[2 * pair] = h_new[0:SSD_HEAD_DIM]
        hout_ref[2 * pair + 1] = h_new[SSD_HEAD_DIM:]
        ys.append(jnp.where(first_head, y_heads[0], y_heads[1]) + y_off * grow[:, lanes] + skip[:, lanes])
    y = jnp.concatenate(ys, axis=1)
    z = z_ref[...]
    y = y * (z * jax.nn.sigmoid(z))
    gw = SSD_INNER // SSD_GROUPS
    normed = []
    for g in range(SSD_GROUPS):
        yg = y[:, g * gw:(g + 1) * gw]
        normed.append(yg * lax.rsqrt(jnp.mean(yg * yg, axis=-1, keepdims=True) + EPS))
    o_ref[...] = (jnp.concatenate(normed, axis=1) * norm_ref[...]).astype(o_ref.dtype)


def _ssd(z, xbc, dt, dt_t, conv_prev, h0, consts, cl):
    b, seq, _ = z.shape
    nc = seq // cl
    cw, cbias, dtb, dtb_t, al, al_t, dskip, norm = consts

    def rows(width):
        return pl.BlockSpec((None, cl, width), lambda i, c: (i, c, 0))

    state = pl.BlockSpec((None, SSD_HEADS, SSD_HEAD_DIM, SSD_STATE), lambda i, c: (i, 0, 0, 0))

    def const(x):
        zeros = (0,) * x.ndim
        return pl.BlockSpec(x.shape, lambda i, c: zeros)

    return pl.pallas_call(
        functools.partial(_ssd_body, cl=cl), grid=(b, nc),
        in_specs=[rows(SSD_INNER), rows(CONV_DIM), rows(TILE),
                  pl.BlockSpec((None, TILE, cl), lambda i, c: (i, 0, c)),
                  pl.BlockSpec((None, SUBLANES, CONV_DIM), lambda i, c: (i, 0, 0)), state,
                  const(cw), const(cbias), const(dtb), const(dtb_t), const(al), const(al_t), const(dskip),
                  const(norm)],
        out_specs=[rows(SSD_INNER), state, pl.BlockSpec((None, CONV_W - 1, CONV_DIM), lambda i, c: (i, 0, 0))],
        out_shape=[jax.ShapeDtypeStruct((b, seq, SSD_INNER), BF16),
                   jax.ShapeDtypeStruct((b, SSD_HEADS, SSD_HEAD_DIM, SSD_STATE), F32),
                   jax.ShapeDtypeStruct((b, CONV_W - 1, CONV_DIM), F32)],
        scratch_shapes=[pltpu.VMEM((SUBLANES + cl, CONV_DIM), F32)],
        compiler_params=_params(2), name="ssd")(z, xbc, dt, dt_t, conv_prev, h0, cw, cbias, dtb, dtb_t, al, al_t,
                                                 dskip, norm)


def _pack_inproj(w_in):
    splits = (ATT_WIDTH, KV_WIDTH, KV_WIDTH, KV_WIDTH, N_BRANCH * N_ATT_HEADS, SSD_INNER, CONV_DIM, SSD_HEADS)
    offs = [0]
    for s in splits:
        offs.append(offs[-1] + s)
    wq, wc, ws, ww, wg, wz, wx, wd = [w_in[:, offs[i]:offs[i + 1]] for i in range(len(splits))]
    d = w_in.shape[0]
    per_kv = GQA * N_BRANCH
    wg4 = jnp.pad(wg.reshape(d, N_KV_HEADS, per_kv), ((0, 0), (0, 0), (0, TILE - per_kv)))
    wd_pad = jnp.pad(wd, ((0, 0), (0, TILE - SSD_HEADS)))
    w = jnp.concatenate([wq * (HEAD_DIM ** -0.5 * LOG2E), wc, ws, ww, wg4.reshape(d, N_KV_HEADS * TILE), wz, wx,
                         wd_pad], axis=1)
    assert w.shape[1] == PROJ_WIDTH
    return w.astype(BF16)


def _pack_compress(cmp_pe, w_cmp):
    zero = jnp.zeros((BLK, HEAD_DIM, HEAD_DIM), w_cmp.dtype)
    bd = jnp.concatenate([jnp.concatenate([w_cmp[:, 0], zero], axis=2),
                          jnp.concatenate([zero, w_cmp[:, 1]], axis=2)], axis=1)
    pe = cmp_pe.reshape(BLK, KV_GROUP)
    return pe.astype(F32), bd.astype(BF16)


def _sample_tables(rel_table, dec_seq, past_len, n_lanes):
    col = jnp.arange(TILE, dtype=jnp.int32)
    head = (col // (GQA * dec_seq)) * GQA + (col // dec_seq) % GQA
    tok = (col % dec_seq)[:, None]
    per_col = (rel_table.astype(F32) * LOG2E)[:, head]

    def bias(dist):
        onehot = (_rel_bucket(dist)[..., None] == jnp.arange(N_BUCKETS, dtype=jnp.int32)).astype(F32)
        return jnp.einsum('ckb,bc->ck', onehot, per_col, precision=lax.Precision.HIGHEST)

    lane = jnp.arange(TILE, dtype=jnp.int32)[None, :]
    far = bias(jnp.full((TILE, 1), MAX_DISTANCE, jnp.int32))
    blk = jnp.arange(n_lanes, dtype=jnp.int32)[None, :]
    c_dist = past_len + tok - ((blk + 1) * BLK - 1)
    ctab = jnp.where(c_dist >= 0, bias(c_dist), MASK_VALUE)
    last_tab = bias(TILE + tok - lane) - far
    new_tab = jnp.where((lane <= tok) & (lane < dec_seq), bias(tok - lane) - far, MASK_VALUE)
    win0_tab = jnp.where(lane > tok, 0.0, MASK_VALUE)
    same = (col[:, None] // (GQA * dec_seq) == col[None, :] // (GQA * dec_seq)) & \
           (col[:, None] % dec_seq == col[None, :] % dec_seq)
    return ctab, last_tab, new_tab, win0_tab, same.astype(BF16)


def _layer_weights(ffn_pre, ffn_post, ffn_w_in, ffn_w_out):
    f = ffn_w_out.shape[0]
    return (ffn_pre[None, :], ffn_post[None, :], ffn_w_in[:, :f].astype(BF16), ffn_w_in[:, f:].astype(BF16),
            ffn_w_out.astype(BF16))


def kernel(x_prompt, x_sample, cache_cmp_kv, cache_slc_kv, cache_win_kv, state_ssm, state_conv, page_table,
           rel_table, ffn1_pre, ffn1_post, ffn1_w_in, ffn1_w_out, mix_pre, mix_post, w_in, w_out, att_out_norm,
           cmp_pe, w_cmp, conv_w, conv_b, dt_bias, a_log, d_skip, ssd_norm,
           ffn2_pre, ffn2_post, ffn2_w_in, ffn2_w_out):
    depth = ffn1_pre.shape[0]
    bp, seq, d = x_prompt.shape
    bs, dec_seq, _ = x_sample.shape
    n_pages = page_table.shape[1]
    past_len = n_pages * PAGE_SIZE
    w_buf = cache_win_kv.shape[2]
    kv_shape = (N_KV_HEADS, 2, HEAD_DIM)
    assert GQA * dec_seq * N_KV_HEADS == TILE and w_buf == WINDOW and seq % TILE == 0

    yp = x_prompt.reshape(bp * seq, d)
    ys = x_sample.reshape(bs * dec_seq, d)
    outs = [[] for _ in range(10)]
    for l in range(depth):
        ffn1 = _layer_weights(ffn1_pre[l], ffn1_post[l], ffn1_w_in[l], ffn1_w_out[l])
        ffn2 = _layer_weights(ffn2_pre[l], ffn2_post[l], ffn2_w_in[l], ffn2_w_out[l])
        w_proj = _pack_inproj(w_in[l])
        pe, bd = _pack_compress(cmp_pe[l], w_cmp[l])
        wo_att = w_out[l][:ATT_WIDTH].astype(BF16)
        wo_ssd = w_out[l][ATT_WIDTH:].astype(BF16)
        ssd_consts = (conv_w[l], conv_b[l][None, :], dt_bias[l][None, :], dt_bias[l][:, None], a_log[l][None, :],
                      a_log[l][:, None], jnp.repeat(d_skip[l], SSD_HEAD_DIM)[None, :], ssd_norm[l][None, :])
        mix_pre_l, mix_post_l, att_norm_l = mix_pre[l][None, :], mix_post[l][None, :], att_out_norm[l][None, :]

        yp = _ffn(yp, *ffn1)
        q, ckv, skv_b, wkv_b, gates, z, xbc, dt, dt_t, ckv_t, skv_t, wkv_t = _inproj(
            yp, mix_pre_l, w_proj, bp, True)
        kvc = _compress(ckv.reshape(bp, seq, KV_WIDTH), pe, bd, TILE)
        tiles, cmp_tab = _prompt_bias_tables(rel_table)
        o_att = _nsa_prompt(q.reshape(bp, seq, ATT_WIDTH), gates.reshape(bp, seq, N_KV_HEADS * TILE), kvc,
                            skv_b.reshape(bp, seq, KV_WIDTH), wkv_b.reshape(bp, seq, KV_WIDTH), tiles, cmp_tab)
        cl = min(CHUNK, seq)
        o_ssd, h_new, conv_new = _ssd(
            z.reshape(bp, seq, SSD_INNER), xbc.reshape(bp, seq, CONV_DIM), dt.reshape(bp, seq, TILE),
            dt_t, jnp.zeros((bp, SUBLANES, CONV_DIM), F32),
            jnp.zeros((bp, SSD_HEADS, SSD_HEAD_DIM, SSD_STATE), F32), ssd_consts, cl)
        yp = _mixout_ffn(yp, o_att.reshape(bp * seq, ATT_WIDTH), o_ssd.reshape(bp * seq, SSD_INNER), att_norm_l,
                         mix_post_l, wo_att, wo_ssd, *ffn2)

        def token_major(a_t):
            return a_t.reshape((bp,) + kv_shape + (a_t.shape[-1],)).transpose(0, 4, 1, 2, 3)

        outs[0].append(token_major(ckv_t))
        outs[1].append(token_major(skv_t))
        outs[2].append(token_major(wkv_t[:, :, -min(WINDOW, seq):]))
        outs[3].append(h_new)
        outs[4].append(conv_new)

        ys = _ffn(ys, *ffn1)
        q, ckv, _, _, gates, z, xbc, dt, dt_t, skv, wkv = _inproj(ys, mix_pre_l, w_proj, 1, False)
        native = (0, 2, 3, 4, 1)
        kc_past = _compress_paged(cache_cmp_kv[l].transpose(native), page_table, pe, bd)
        cmp_new = jnp.pad(ckv.reshape(bs, dec_seq, KV_WIDTH), ((0, 0), (0, BLK - dec_seq), (0, 0)))
        kc_new = _compress(cmp_new.reshape(1, bs * BLK, KV_WIDTH), pe, bd, bs).reshape(bs, 1, KV_WIDTH)
        n_past = kc_past.shape[1]
        n_lanes = -(-(n_past + 1) // TILE) * TILE
        kc = jnp.concatenate([kc_past, kc_new, jnp.zeros((bs, n_lanes - n_past - 1, KV_WIDTH), BF16)], axis=1)
        ctab, last_tab, new_tab, win0_tab, gsum = _sample_tables(rel_table, dec_seq, past_len, n_lanes)
        q_rows = q.reshape(bs, dec_seq, N_KV_HEADS, GQA, HEAD_DIM).transpose(0, 2, 3, 1, 4)
        own_head = jnp.eye(N_KV_HEADS, dtype=F32)[None, :, None, None, :, None]
        q_bd = q_rows[:, :, :, :, None, :] * own_head
        qk = q_bd.reshape(bs, TILE, N_KV_HEADS * HEAD_DIM).astype(BF16)
        g_rows = gates.reshape(bs, dec_seq, N_KV_HEADS, TILE)[..., :GQA * N_BRANCH]
        g_rows = g_rows.reshape(bs, dec_seq, N_KV_HEADS, GQA, N_BRANCH).transpose(0, 2, 3, 1, 4)
        gates_c = jnp.pad(g_rows.reshape(bs, TILE, N_BRANCH), ((0, 0), (0, 0), (0, TILE - N_BRANCH)))

        def new_t(a):
            a = a.reshape(bs, dec_seq, N_KV_HEADS, 2, HEAD_DIM).transpose(0, 3, 2, 4, 1)
            a = jnp.pad(a.reshape(bs, 2, N_KV_HEADS * HEAD_DIM, dec_seq), ((0, 0),) * 3 + ((0, TILE - dec_seq),))
            return a[:, 0].astype(BF16), a[:, 1].astype(BF16)

        k_new, v_new = new_t(skv)
        kw_new, vw_new = new_t(wkv)
        o_rows = _nsa_sample(page_table, cache_slc_kv[l].transpose(native), qk, gates_c, kc, ctab, gsum,
                             k_new, v_new, cache_win_kv[l].transpose(native), kw_new, vw_new, last_tab, new_tab,
                             win0_tab, dec_seq=dec_seq, past_len=past_len)
        o_att = o_rows.reshape(bs, N_KV_HEADS, GQA, dec_seq, HEAD_DIM).transpose(0, 3, 1, 2, 4)
        o_att = o_att.reshape(bs * dec_seq, ATT_WIDTH)
        conv_prev = jnp.pad(state_conv[l], ((0, 0), (SUBLANES - (CONV_W - 1), 0), (0, 0)))
        o_ssd, h_new, conv_new = _ssd(
            z.reshape(bs, dec_seq, SSD_INNER), xbc.reshape(bs, dec_seq, CONV_DIM), dt.reshape(bs, dec_seq, TILE),
            dt_t.reshape(TILE, bs, dec_seq).transpose(1, 0, 2), conv_prev, state_ssm[l], ssd_consts,
            min(CHUNK, dec_seq))
        ys = _mixout_ffn(ys, o_att, o_ssd.reshape(bs * dec_seq, SSD_INNER), att_norm_l, mix_post_l, wo_att,
                         wo_ssd, *ffn2)
        win_all = jnp.concatenate([cache_win_kv[l], wkv.reshape((bs, dec_seq) + kv_shape)], axis=1)
        outs[5].append(ckv.reshape((bs, dec_seq) + kv_shape))
        outs[6].append(skv.reshape((bs, dec_seq) + kv_shape))
        outs[7].append(win_all[:, -min(WINDOW, w_buf + dec_seq):])
        outs[8].append(h_new)
        outs[9].append(conv_new)

    return (yp.reshape(bp, seq, d), ys.reshape(bs, dec_seq, d)) + tuple(jnp.stack(o) for o in outs)
```
